```python
import math
import jax, jax.numpy as jnp
from jax import lax
import numpy as np

D_MODEL = 1024
BATCH = 8
SEQ = 2048
DEPTH = 1
DEC_BATCH = 128
DEC_SEQ = 4
PAST_LEN = 16384
PAGE_SIZE = 128

D_CONV = D_MODEL
CONV_W = 3
N_HEADS = 4
D_HEAD = D_MODEL // N_HEADS
D_MLSTM = N_HEADS * D_HEAD
D_FF = 4 * D_MODEL
CHUNK = 128
LN_EPS = 1e-5
ALPHA = (2.0 * DEPTH) ** 0.25
BETA = (8.0 * DEPTH) ** -0.25
F_BIAS_LO = 3.0
F_BIAS_HI = 6.0
D_IN = 3 * D_CONV + 4 * D_MLSTM + 2 * N_HEADS + 2 * D_MODEL

kernel_name = "hybrid_shortconv_mlstm_gated_merge_step"


def _split_points():
    sizes = (D_CONV, D_CONV, D_CONV, D_MLSTM, D_MLSTM, D_MLSTM, D_MLSTM,
             N_HEADS, N_HEADS, D_MODEL, D_MODEL)
    pts, acc = [], 0
    for s in sizes[:-1]:
        acc += s
        pts.append(acc)
    return pts


def layer_norm(x, g, b):
    xf = x.astype(jnp.float32)
    mu = jnp.mean(xf, axis=-1, keepdims=True)
    xc = xf - mu
    var = jnp.mean(xc * xc, axis=-1, keepdims=True)
    y = xc * lax.rsqrt(var + LN_EPS) * g.astype(jnp.float32) + b.astype(jnp.float32)
    return y.astype(x.dtype)


def short_conv(u, buf, w):
    T = u.shape[1]
    up = jnp.concatenate([buf.astype(u.dtype), u], axis=1)
    y = up[:, 0:T] * w[0]
    for j in range(1, CONV_W):
        y = y + up[:, j:j + T] * w[j]
    return y, up[:, -(CONV_W - 1):]


def mlstm_chunkwise(q, k, v, logi, logf, C0, n0, m0, chunk):
    Bsz, T, H, D = q.shape
    nc = T // chunk
    L = chunk

    def to_chunks(a):
        return a.reshape(Bsz, nc, L, H, D).transpose(1, 0, 3, 2, 4)

    def g_chunks(a):
        return a.reshape(Bsz, nc, L, H).transpose(1, 0, 3, 2)

    mask = jnp.tril(jnp.ones((L, L), dtype=bool))

    def step(carry, xs):
        C, n, m = carry
        qc, kc, vc, ic, fc = xs
        b = jnp.cumsum(fc, axis=-1)
        inter = b + m[..., None]
        Dm = b[..., :, None] - b[..., None, :] + ic[..., None, :]
        Dm = jnp.where(mask, Dm, -jnp.inf)
        m_t = jnp.maximum(inter, jnp.max(Dm, axis=-1))
        w_inter = jnp.exp(inter - m_t)
        S = jnp.einsum('bhtd,bhsd->bhts', qc, kc) * jnp.exp(Dm - m_t[..., None])
        num = (w_inter[..., None] * jnp.einsum('bhtd,bhdv->bhtv', qc, C)
               + jnp.einsum('bhts,bhsv->bhtv', S, vc))
        den = w_inter * jnp.einsum('bhtd,bhd->bht', qc, n) + jnp.sum(S, axis=-1)
        h = num / jnp.maximum(jnp.abs(den), jnp.exp(-m_t))[..., None]
        m_new = m_t[..., -1]
        decay = jnp.exp(b[..., -1] + m - m_new)
        ws = jnp.exp(ic + b[..., -1:] - b - m_new[..., None])
        C_new = decay[..., None, None] * C + jnp.einsum('bhs,bhsd,bhsv->bhdv', ws, kc, vc)
        n_new = decay[..., None] * n + jnp.einsum('bhs,bhsd->bhd', ws, kc)
        return (C_new, n_new, m_new), h

    (C, n, m), hs = lax.scan(step, (C0, n0, m0),
                             (to_chunks(q), to_chunks(k), to_chunks(v), g_chunks(logi), g_chunks(logf)))
    h = hs.transpose(1, 0, 3, 2, 4).reshape(Bsz, T, H, D)
    return h, C, n, m


def hybrid_layer(x, conv_buf, C0, n0, m0, chunk, w_in, b_gate, conv_w, w_conv_out, mh_g,
                 w_m_out, w_o, ln1_g, ln1_b, w_ff1, w_ff2, ln2_g, ln2_b):
    Bsz, T, _ = x.shape
    f32 = jnp.float32
    z = x @ w_in
    (bg, cg, hc, q, k, v, o, ig, fg, gc, gm) = jnp.split(z, _split_points(), axis=-1)
    conv, new_buf = short_conv(cg * hc, conv_buf, conv_w)
    y_conv = (bg * conv) @ w_conv_out
    qf = q.reshape(Bsz, T, N_HEADS, D_HEAD).astype(f32)
    kf = k.reshape(Bsz, T, N_HEADS, D_HEAD).astype(f32) * (D_HEAD ** -0.5)
    vf = v.reshape(Bsz, T, N_HEADS, D_HEAD).astype(f32)
    bgf = b_gate.astype(f32)
    logi = ig.astype(f32) + bgf[:N_HEADS]
    logf = jax.nn.log_sigmoid(fg.astype(f32) + bgf[N_HEADS:])
    h, C, n, m = mlstm_chunkwise(qf, kf, vf, logi, logf, C0.astype(f32), n0.astype(f32),
                                 m0.astype(f32), chunk)
    mu = jnp.mean(h, axis=-1, keepdims=True)
    hc_ = h - mu
    h = hc_ * lax.rsqrt(jnp.mean(hc_ * hc_, axis=-1, keepdims=True) + LN_EPS)
    h = h.reshape(Bsz, T, D_MLSTM) * mh_g.astype(f32) * jax.nn.sigmoid(o.astype(f32))
    y_m = h.astype(x.dtype) @ w_m_out
    merged = jax.nn.sigmoid(gc) * y_conv + jax.nn.sigmoid(gm) * y_m
    x1 = layer_norm(ALPHA * x + merged @ w_o, ln1_g, ln1_b)
    hid = jnp.square(jax.nn.relu(x1 @ w_ff1))
    x2 = layer_norm(ALPHA * x1 + hid @ w_ff2, ln2_g, ln2_b)
    dt = x.dtype
    return x2, new_buf.astype(dt), C.astype(dt), n.astype(dt), m.astype(dt)


def setup_inputs(seed: int = 0) -> dict:
    key = jax.random.key(seed)
    ks = jax.random.split(key, 24)
    nrm = lambda kk, shape, s: jax.random.normal(kk, shape, jnp.float32) * s
    f_bias = jnp.linspace(F_BIAS_LO, F_BIAS_HI, N_HEADS, dtype=jnp.float32)
    b_gate = jnp.concatenate([
        nrm(ks[9], (DEPTH, N_HEADS), 0.1),
        f_bias[None, :] + nrm(ks[10], (DEPTH, N_HEADS), 0.1)], axis=-1)
    return {
        "x_prompt": nrm(ks[0], (BATCH, SEQ, D_MODEL), 1.0),
        "x_sample": nrm(ks[1], (DEC_BATCH, DEC_SEQ, D_MODEL), 1.0),
        "state_conv": nrm(ks[2], (DEPTH, DEC_BATCH, CONV_W - 1, D_CONV), 1.0),
        "state_C": nrm(ks[3], (DEPTH, DEC_BATCH, N_HEADS, D_HEAD, D_HEAD), 0.1),
        "state_n": nrm(ks[4], (DEPTH, DEC_BATCH, N_HEADS, D_HEAD), 0.1),
        "state_m": 1.0 + nrm(ks[5], (DEPTH, DEC_BATCH, N_HEADS), 0.5),
        "w_in": nrm(ks[6], (DEPTH, D_MODEL, D_IN), D_MODEL ** -0.5),
        "b_gate": b_gate,
        "conv_w": nrm(ks[7], (DEPTH, CONV_W, D_CONV), CONV_W ** -0.5),
        "w_conv_out": nrm(ks[8], (DEPTH, D_CONV, D_MODEL), BETA * D_CONV ** -0.5),
        "mh_g": 1.0 + nrm(ks[11], (DEPTH, D_MLSTM), 0.02),
        "w_m_out": nrm(ks[12], (DEPTH, D_MLSTM, D_MODEL), BETA * D_MLSTM ** -0.5),
        "w_o": nrm(ks[13], (DEPTH, D_MODEL, D_MODEL), BETA * D_MODEL ** -0.5),
        "ln1_g": 1.0 + nrm(ks[14], (DEPTH, D_MODEL), 0.02),
        "ln1_b": nrm(ks[15], (DEPTH, D_MODEL), 0.02),
        "w_ff1": nrm(ks[16], (DEPTH, D_MODEL, D_FF), BETA * D_MODEL ** -0.5),
        "w_ff2": nrm(ks[17], (DEPTH, D_FF, D_MODEL), BETA * D_FF ** -0.5),
        "ln2_g": 1.0 + nrm(ks[18], (DEPTH, D_MODEL), 0.02),
        "ln2_b": nrm(ks[19], (DEPTH, D_MODEL), 0.02),
    }


def reference(x_prompt, x_sample, state_conv, state_C, state_n, state_m, w_in, b_gate, conv_w,
              w_conv_out, mh_g, w_m_out, w_o, ln1_g, ln1_b, w_ff1, w_ff2, ln2_g, ln2_b):
    dt = x_prompt.dtype
    chunk_p = CHUNK if SEQ % CHUNK == 0 else SEQ
    xp, xs = x_prompt, x_sample
    cp_l, cs_l, Cp_l, Cs_l, np_l, ns_l, mp_l, ms_l = [], [], [], [], [], [], [], []
    for l in range(DEPTH):
        params = (w_in[l], b_gate[l], conv_w[l], w_conv_out[l], mh_g[l], w_m_out[l], w_o[l],
                  ln1_g[l], ln1_b[l], w_ff1[l], w_ff2[l], ln2_g[l], ln2_b[l])
        xp, cp, Cp, np_, mp = hybrid_layer(
            xp, jnp.zeros((BATCH, CONV_W - 1, D_CONV), dt),
            jnp.zeros((BATCH, N_HEADS, D_HEAD, D_HEAD), jnp.float32),
            jnp.zeros((BATCH, N_HEADS, D_HEAD), jnp.float32),
            jnp.zeros((BATCH, N_HEADS), jnp.float32), chunk_p, *params)
        xs, cs, Cs, ns, ms = hybrid_layer(
            xs, state_conv[l], state_C[l], state_n[l], state_m[l], DEC_SEQ, *params)
        cp_l.append(cp); cs_l.append(cs); Cp_l.append(Cp); Cs_l.append(Cs)
        np_l.append(np_); ns_l.append(ns); mp_l.append(mp); ms_l.append(ms)
    return (xp, xs, jnp.stack(cp_l), jnp.stack(cs_l), jnp.stack(Cp_l), jnp.stack(Cs_l),
            jnp.stack(np_l), jnp.stack(ns_l), jnp.stack(mp_l), jnp.stack(ms_l))
```

```python
import functools

import jax
import jax.numpy as jnp
from jax import lax
from jax.experimental import pallas as pl
from jax.experimental.pallas import tpu as pltpu

F32 = jnp.float32
BF16 = jnp.bfloat16

LN_EPS = 1e-5
N_HEADS = 4
CHUNK = 128
CONV_W = 3
TOKEN_TILE = 512
SAMPLE_BATCH_BLOCK = 8
GATE_PAD = 128
VMEM_LIMIT = 56 * 1024 * 1024


def _dot(a, b):
    return jnp.dot(a, b, preferred_element_type=F32)


def _dot_nt(a, b):
    return lax.dot_general(a, b, (((1,), (1,)), ((), ())), preferred_element_type=F32)


def _dot_tn(a, b):
    return lax.dot_general(a, b, (((0,), (0,)), ((), ())), preferred_element_type=F32)


def _sigmoid(x):
    return 1.0 / (1.0 + jnp.exp(-x))


def _log_sigmoid(x):
    return jnp.minimum(x, 0.0) - jnp.log1p(jnp.exp(-jnp.abs(x)))


def _split_bf16(x):
    hi = x.astype(BF16)
    lo = (x - hi.astype(F32)).astype(BF16)
    return hi, lo


def _layer_norm(r, g, b):
    mu = jnp.mean(r, axis=-1, keepdims=True)
    xc = r - mu
    var = jnp.mean(xc * xc, axis=-1, keepdims=True)
    return xc * lax.rsqrt(var + LN_EPS) * g + b


def _head_norm(h):
    mu = jnp.mean(h, axis=-1, keepdims=True)
    hc = h - mu
    return hc * lax.rsqrt(jnp.mean(hc * hc, axis=-1, keepdims=True) + LN_EPS)


def _const_spec(shape):
    zeros = (0,) * len(shape)
    return pl.BlockSpec(shape, lambda *_: zeros, pipeline_mode=pl.Buffered(1))


def _params(n_axes):
    return pltpu.CompilerParams(dimension_semantics=("arbitrary",) * n_axes,
                                vmem_limit_bytes=VMEM_LIMIT)


def _conv_stripes(d):
    sw = 512 if d % 512 == 0 else d
    return [slice(s * sw, (s + 1) * sw) for s in range(d // sw)]


def _conv_prompt_kernel(x_ref, wbch_ref, cw_ref, wco_ref, wgc_ref, m1_ref, cs_ref, u_s, a_s):
    tm, d = a_s.shape
    @pl.when(pl.program_id(1) == 0)
    def _():
        u_s[0:8, :] = jnp.zeros((8, d), F32)

    xb = x_ref[0].astype(BF16)
    for cs in _conv_stripes(d):
        off = cs.start
        bg = _dot(xb, wbch_ref[:, off:cs.stop])
        cg = _dot(xb, wbch_ref[:, d + off:d + cs.stop])
        hc = _dot(xb, wbch_ref[:, 2 * d + off:2 * d + cs.stop])
        u = cg * hc
        u_s[8:8 + tm, cs] = u
        conv = (u_s[6:6 + tm, cs] * cw_ref[0:1, cs] + u_s[7:7 + tm, cs] * cw_ref[1:2, cs]
                + u * cw_ref[2:3, cs])
        a_s[:, cs] = (bg * conv).astype(BF16)
    cs_ref[0] = u_s[tm + 6:tm + 8, :]
    u_s[0:8, :] = u_s[tm:tm + 8, :]
    yc = _dot(a_s[...], wco_ref[...])
    m1_ref[0] = _sigmoid(_dot(xb, wgc_ref[...])) * yc


def _conv_sample_kernel(x_ref, st_ref, wbch_ref, cw_ref, wco_ref, wgc_ref, m1_ref, cs_ref, a_s,
                        *, n_steps):
    rows, d = a_s.shape
    nb = rows // n_steps
    xb = x_ref[...].astype(BF16)
    for cs in _conv_stripes(d):
        off = cs.start
        bg = _dot(xb, wbch_ref[:, off:cs.stop])
        cg = _dot(xb, wbch_ref[:, d + off:d + cs.stop])
        hc = _dot(xb, wbch_ref[:, 2 * d + off:2 * d + cs.stop])
        u = cg * hc
        up = [st_ref[j * nb:(j + 1) * nb, cs] for j in range(CONV_W - 1)]
        up += [u[t * nb:(t + 1) * nb] for t in range(n_steps)]
        for t in range(n_steps):
            conv = (up[t] * cw_ref[0:1, cs] + up[t + 1] * cw_ref[1:2, cs]
                    + up[t + 2] * cw_ref[2:3, cs])
            a_s[t * nb:(t + 1) * nb, cs] = (bg[t * nb:(t + 1) * nb] * conv).astype(BF16)
        for j in range(CONV_W - 1):
            cs_ref[j * nb:(j + 1) * nb, cs] = up[n_steps + j]
    yc = _dot(a_s[...], wco_ref[...])
    m1_ref[...] = _sigmoid(_dot(xb, wgc_ref[...])) * yc


def _conv_branch_prompt(x, wbch, cw, wco, wgc):
    bsz, t, d = x.shape
    tm = TOKEN_TILE
    return pl.pallas_call(
        _conv_prompt_kernel,
        grid=(bsz, t // tm),
        in_specs=[pl.BlockSpec((1, tm, d), lambda b, j: (b, j, 0)),
                  _const_spec(wbch.shape), _const_spec(cw.shape), _const_spec(wco.shape),
                  _const_spec(wgc.shape)],
        out_specs=[pl.BlockSpec((1, tm, d), lambda b, j: (b, j, 0)),
                   pl.BlockSpec((1, CONV_W - 1, d), lambda b, j: (b, 0, 0))],
        out_shape=[jax.ShapeDtypeStruct((bsz, t, d), F32),
                   jax.ShapeDtypeStruct((bsz, CONV_W - 1, d), F32)],
        scratch_shapes=[pltpu.VMEM((tm + 8, d), F32), pltpu.VMEM((tm, d), BF16)],
        compiler_params=_params(2),
        name="conv_branch_prompt",
    )(x, wbch, cw, wco, wgc)


def _conv_branch_sample(x_tm, st_tm, wbch, cw, wco, wgc, n_steps):
    rows, d = x_tm.shape
    return pl.pallas_call(
        functools.partial(_conv_sample_kernel, n_steps=n_steps),
        grid=(1,),
        in_specs=[_const_spec(x_tm.shape), _const_spec(st_tm.shape), _const_spec(wbch.shape),
                  _const_spec(cw.shape), _const_spec(wco.shape), _const_spec(wgc.shape)],
        out_specs=[pl.BlockSpec((rows, d), lambda i: (0, 0)),
                   pl.BlockSpec(st_tm.shape, lambda i: (0, 0))],
        out_shape=[jax.ShapeDtypeStruct((rows, d), F32),
                   jax.ShapeDtypeStruct(st_tm.shape, F32)],
        scratch_shapes=[pltpu.VMEM((rows, d), BF16)],
        compiler_params=_params(1),
        name="conv_branch_sample",
    )(x_tm, st_tm, wbch, cw, wco, wgc)


def _project_qkvo(xh, wqkvo_ref, mhg_ref, q_s, k_s, v_s, og_s, d, dh):
    q_s[...] = _dot(xh, wqkvo_ref[:, 0:d]).astype(BF16).astype(q_s.dtype)
    k_s[...] = (_dot(xh, wqkvo_ref[:, d:2 * d]) * (dh ** -0.5)).astype(BF16).astype(k_s.dtype)
    v_s[...] = _dot(xh, wqkvo_ref[:, 2 * d:3 * d]).astype(BF16).astype(v_s.dtype)
    og_s[...] = mhg_ref[...] * _sigmoid(_dot(xh, wqkvo_ref[:, 3 * d:4 * d]))


def _mlstm_prompt_kernel(x_ref, wqkvo_ref, wgt_ref, bgc_ref, mhg_ref,
                         hm_ref, c_ref, n_ref, m_ref, q_s, k_s, v_s, og_s, g_s):
    tm, d = q_s.shape
    dh = d // N_HEADS
    L = CHUNK

    @pl.when(pl.program_id(1) == 0)
    def _():
        c_ref[...] = jnp.zeros(c_ref.shape, F32)
        n_ref[...] = jnp.zeros(n_ref.shape, F32)
        m_ref[...] = jnp.zeros(m_ref.shape, F32)

    x = x_ref[0]
    xh, xl = _split_bf16(x)
    _project_qkvo(xh, wqkvo_ref, mhg_ref, q_s, k_s, v_s, og_s, d, dh)

    ga = _dot_nt(wgt_ref[0], xh)
    gb = _dot_nt(wgt_ref[1], xl)
    gt = ga[0:8] + ga[8:16] + gb[0:8] + bgc_ref[...]
    is_input_gate = lax.broadcasted_iota(jnp.int32, gt.shape, 0) < N_HEADS
    g_s[...] = jnp.where(is_input_gate, gt, _log_sigmoid(gt))

    ri = lax.broadcasted_iota(jnp.int32, (L, L), 0)
    ci = lax.broadcasted_iota(jnp.int32, (L, L), 1)
    eye = ri == ci
    causal = ri >= ci
    lane = lax.broadcasted_iota(jnp.int32, (8, L), 1)

    def chunk_body(c, carry):
        c0 = pl.multiple_of(c * L, L)
        rows = pl.ds(c0, L)
        slab = g_s[:, rows]
        csum = slab
        shift = 1
        while shift < L:
            csum = csum + jnp.where(lane >= shift, pltpu.roll(csum, shift, 1), 0.0)
            shift *= 2
        for h in range(N_HEADS):
            hc = slice(h * dh, (h + 1) * dh)
            q = q_s[rows, hc]
            k = k_s[rows, hc]
            v = v_s[rows, hc]
            b_row = csum[N_HEADS + h:N_HEADS + h + 1, :]
            a_row = slab[h:h + 1, :] - b_row
            b_col = jnp.sum(jnp.where(eye, b_row, 0.0), axis=1, keepdims=True)
            a_col = jnp.sum(jnp.where(eye, a_row, 0.0), axis=1, keepdims=True)
            m_prev = m_ref[0, h]
            dm = jnp.where(causal, b_col + a_row, -jnp.inf)
            inter = b_col + m_prev
            m_t = jnp.maximum(inter, jnp.max(dm, axis=1, keepdims=True))
            w_inter = jnp.exp(inter - m_t)
            s_mat = _dot_nt(q, k) * jnp.exp(dm - m_t)
            c_old = c_ref[0, h]
            num = w_inter * _dot(q, c_old.astype(BF16)) + _dot(s_mat.astype(BF16), v)
            qn = jnp.sum(q.astype(F32) * n_ref[0, h], axis=1, keepdims=True)
            den = w_inter * qn + jnp.sum(s_mat, axis=1, keepdims=True)
            hh = num / jnp.maximum(jnp.abs(den), jnp.exp(-m_t))
            hm_ref[rows, hc] = (_head_norm(hh) * og_s[rows, hc]).astype(BF16)
            m_new = m_t[L - 1:L, :]
            b_last = b_col[L - 1:L, :]
            decay = jnp.exp(b_last + m_prev - m_new)
            kw = k.astype(F32) * jnp.exp(a_col + b_last - m_new)
            c_ref[0, h] = decay * c_old + _dot_tn(kw.astype(BF16), v)
            n_ref[0, h] = decay * n_ref[0, h] + jnp.sum(kw, axis=0, keepdims=True)
            m_ref[0, h] = m_new
        return carry

    lax.fori_loop(0, tm // L, chunk_body, 0)


def _mlstm_prompt(x, wqkvo, wgt, bgc, mhg):
    bsz, t, d = x.shape
    dh = d // N_HEADS
    tm = TOKEN_TILE
    hm, c, n, m = pl.pallas_call(
        _mlstm_prompt_kernel,
        grid=(bsz, t // tm),
        in_specs=[pl.BlockSpec((1, tm, d), lambda b, j: (b, j, 0)),
                  _const_spec(wqkvo.shape), _const_spec(wgt.shape), _const_spec(bgc.shape),
                  _const_spec(mhg.shape)],
        out_specs=[pl.BlockSpec((tm, d), lambda b, j: (b * (t // tm) + j, 0)),
                   pl.BlockSpec((1, N_HEADS, dh, dh), lambda b, j: (b, 0, 0, 0)),
                   pl.BlockSpec((1, N_HEADS, 1, dh), lambda b, j: (b, 0, 0, 0)),
                   pl.BlockSpec((1, N_HEADS, 1, 1), lambda b, j: (b, 0, 0, 0))],
        out_shape=[jax.ShapeDtypeStruct((bsz * t, d), BF16),
                   jax.ShapeDtypeStruct((bsz, N_HEADS, dh, dh), F32),
                   jax.ShapeDtypeStruct((bsz, N_HEADS, 1, dh), F32),
                   jax.ShapeDtypeStruct((bsz, N_HEADS, 1, 1), F32)],
        scratch_shapes=[pltpu.VMEM((tm, d), BF16), pltpu.VMEM((tm, d), BF16),
                        pltpu.VMEM((tm, d), BF16), pltpu.VMEM((tm, d), F32),
                        pltpu.VMEM((8, tm), F32)],
        compiler_params=_params(2),
        name="mlstm_prompt",
    )(x, wqkvo, wgt, bgc, mhg)
    return hm, c, n.reshape(bsz, N_HEADS, dh), m.reshape(bsz, N_HEADS)


def _mlstm_sample_kernel(x_ref, wqkvo_ref, wgc_ref, bgr_ref, mhg_ref, c0_ref, n0_ref, m0_ref,
                         hm_ref, c_ref, n_ref, m_ref, q_s, k_s, v_s, og_s, g_s, *, n_steps):
    rows, d = q_s.shape
    dh = d // N_HEADS
    nb = rows // n_steps
    bb = SAMPLE_BATCH_BLOCK
    T = n_steps
    i = pl.program_id(0)
    h = pl.program_id(1)

    @pl.when(jnp.logical_and(i == 0, h == 0))
    def _():
        x = x_ref[...]
        xh, xl = _split_bf16(x)
        _project_qkvo(xh, wqkvo_ref, mhg_ref, q_s, k_s, v_s, og_s, d, dh)
        g = _dot(xh, wgc_ref[0]) + _dot(xl, wgc_ref[0]) + _dot(xh, wgc_ref[1]) + bgr_ref[...]
        is_input_gate = lax.broadcasted_iota(jnp.int32, g.shape, 1) < N_HEADS
        g_s[...] = jnp.where(is_input_gate, g, _log_sigmoid(g))

    r0 = pl.multiple_of(i * bb, bb)
    trow = [pl.ds(pl.multiple_of(t * nb + r0, bb), bb) for t in range(T)]
    hc = pl.ds(pl.multiple_of(h * dh, dh), dh)
    owner = lax.broadcasted_iota(jnp.int32, (T * bb, dh), 0) % bb
    gate_lane = lax.broadcasted_iota(jnp.int32, (bb, g_s.shape[1]), 1)

    def gate_column(t, col):
        return jnp.sum(jnp.where(gate_lane == col, g_s[trow[t], :], 0.0), axis=1, keepdims=True)

    qf = [q_s[trow[t], hc] for t in range(T)]
    kf = [k_s[trow[t], hc] for t in range(T)]
    vf = [v_s[trow[t], hc] for t in range(T)]
    li = [gate_column(t, h) for t in range(T)]
    lf = [gate_column(t, N_HEADS + h) for t in range(T)]
    m0 = m0_ref[0]
    n0 = n0_ref[...]
    b = [lf[0]]
    for t in range(1, T):
        b.append(b[t - 1] + lf[t])
    a = [li[t] - b[t] for t in range(T)]
    m_t, w_inter, s_w = [], [], []
    for t in range(T):
        inter = b[t] + m0
        mt = inter
        for s in range(t + 1):
            mt = jnp.maximum(mt, b[t] + a[s])
        m_t.append(mt)
        w_inter.append(jnp.exp(inter - mt))
        s_w.append([jnp.sum(qf[t] * kf[s], axis=1, keepdims=True) * jnp.exp(b[t] + a[s] - mt)
                    for s in range(t + 1)])
    m_new = m_t[T - 1]
    b_last = b[T - 1]
    decay = jnp.exp(b_last + m0 - m_new)
    kw = [kf[s] * jnp.exp(a[s] + b_last - m_new) for s in range(T)]
    n_new = decay * n0
    for s in range(T):
        n_new = n_new + kw[s]
    n_ref[...] = n_new
    m_ref[0] = m_new

    q_blk = jnp.concatenate(qf, axis=0).astype(BF16)
    kw_blk = jnp.concatenate(kw, axis=0).astype(BF16)
    v_blk = jnp.concatenate(vf, axis=0)
    qc = jnp.zeros((T * bb, dh), F32)
    for bi in range(bb):
        mine = owner == bi
        c_old = c0_ref[bi, 0]
        qc = jnp.where(mine, _dot(q_blk, c_old.astype(BF16)), qc)
        upd = _dot_tn(kw_blk, jnp.where(mine, v_blk, 0.0).astype(BF16))
        c_ref[bi, 0] = decay[bi:bi + 1, :] * c_old + upd

    for t in range(T):
        num = w_inter[t] * qc[t * bb:(t + 1) * bb]
        den = w_inter[t] * jnp.sum(qf[t] * n0, axis=1, keepdims=True)
        for s in range(t + 1):
            num = num + s_w[t][s] * vf[s]
            den = den + s_w[t][s]
        hh = num / jnp.maximum(jnp.abs(den), jnp.exp(-m_t[t]))
        hm_ref[trow[t], hc] = _head_norm(hh) * og_s[trow[t], hc]


def _mlstm_sample(x_tm, wqkvo, wgc, bgr, mhg, c0, n0, m0, n_steps):
    rows, d = x_tm.shape
    nb = rows // n_steps
    dh = d // N_HEADS
    bb = SAMPLE_BATCH_BLOCK
    cblock = pl.BlockSpec((bb, 1, dh, dh), lambda i, h: (i, h, 0, 0))
    nblock = pl.BlockSpec((bb, dh), lambda i, h: (i, h))
    mblock = pl.BlockSpec((1, bb, 1), lambda i, h: (h, i, 0))
    m0_hm = m0.T.reshape(N_HEADS, nb, 1)
    hm, c, n, m_hm = pl.pallas_call(
        functools.partial(_mlstm_sample_kernel, n_steps=n_steps),
        grid=(nb // bb, N_HEADS),
        in_specs=[_const_spec(x_tm.shape), _const_spec(wqkvo.shape), _const_spec(wgc.shape),
                  _const_spec(bgr.shape), _const_spec(mhg.shape), cblock, nblock, mblock],
        out_specs=[pl.BlockSpec((rows, d), lambda i, h: (0, 0)), cblock, nblock, mblock],
        out_shape=[jax.ShapeDtypeStruct((rows, d), F32),
                   jax.ShapeDtypeStruct(c0.shape, F32),
                   jax.ShapeDtypeStruct((nb, d), F32),
                   jax.ShapeDtypeStruct((N_HEADS, nb, 1), F32)],
        scratch_shapes=[pltpu.VMEM((rows, d), F32), pltpu.VMEM((rows, d), F32),
                        pltpu.VMEM((rows, d), F32), pltpu.VMEM((rows, d), F32),
                        pltpu.VMEM((rows, GATE_PAD), F32)],
        compiler_params=_params(2),
        name="mlstm_sample",
    )(x_tm, wqkvo, wgc, bgr, mhg, c0, n0, m0_hm)
    return hm, c, n, m_hm.reshape(N_HEADS, nb).T


def _merge_kernel(x_ref, m1_ref, hm_ref, wmo_ref, wgm_ref, wo_ref, g_ref, b_ref, o_ref, *, alpha):
    x = x_ref[...]
    xb = x.astype(BF16)
    ym = _dot(hm_ref[...].astype(BF16), wmo_ref[...])
    merged = m1_ref[...] + _sigmoid(_dot(xb, wgm_ref[...])) * ym
    r = alpha * x + _dot(merged.astype(BF16), wo_ref[...])
    o_ref[...] = _layer_norm(r, g_ref[...], b_ref[...])


def _merge(x, m1, hm, wmo, wgm, wo, g, b, alpha):
    n, d = x.shape
    tm = min(TOKEN_TILE, n)
    tile = pl.BlockSpec((tm, d), lambda i: (i, 0))
    return pl.pallas_call(
        functools.partial(_merge_kernel, alpha=alpha),
        grid=(n // tm,),
        in_specs=[tile, tile, tile, _const_spec(wmo.shape), _const_spec(wgm.shape),
                  _const_spec(wo.shape), _const_spec(g.shape), _const_spec(b.shape)],
        out_specs=tile,
        out_shape=jax.ShapeDtypeStruct((n, d), F32),
        compiler_params=_params(1),
        name="merge_ln1",
    )(x, m1, hm, wmo, wgm, wo, g, b)


def _ffn_kernel(x_ref, w1_ref, w2_ref, g_ref, b_ref, o_ref, hid_s, *, alpha):
    x = x_ref[...]
    xb = x.astype(BF16)
    dff = hid_s.shape[1]
    sw = 1024 if dff % 1024 == 0 else dff
    for s in range(dff // sw):
        cs = slice(s * sw, (s + 1) * sw)
        hid = jnp.maximum(_dot(xb, w1_ref[:, cs]), 0.0)
        hid_s[:, cs] = (hid * hid).astype(BF16)
    r = alpha * x + _dot(hid_s[...], w2_ref[...])
    o_ref[...] = _layer_norm(r, g_ref[...], b_ref[...])


def _ffn(x, w1, w2, g, b, alpha):
    n, d = x.shape
    tm = min(TOKEN_TILE, n)
    tile = pl.BlockSpec((tm, d), lambda i: (i, 0))
    return pl.pallas_call(
        functools.partial(_ffn_kernel, alpha=alpha),
        grid=(n // tm,),
        in_specs=[tile, _const_spec(w1.shape), _const_spec(w2.shape), _const_spec(g.shape),
                  _const_spec(b.shape)],
        out_specs=tile,
        out_shape=jax.ShapeDtypeStruct((n, d), F32),
        scratch_shapes=[pltpu.VMEM((tm, w1.shape[1]), BF16)],
        compiler_params=_params(1),
        name="ffn_ln2",
    )(x, w1, w2, g, b)


def _layer_weights(w_in, b_gate, conv_w, w_conv_out, mh_g, w_m_out, w_o, ln1_g, ln1_b,
                   w_ff1, w_ff2, ln2_g, ln2_b):
    d = w_in.shape[0]
    h2 = 2 * N_HEADS
    o_gate = 7 * d
    wg = w_in[:, o_gate:o_gate + h2]
    wg_hi, wg_lo = _split_bf16(wg)
    zeros_t = jnp.zeros((h2, d), BF16)
    wgt = jnp.stack([jnp.concatenate([wg_hi.T, wg_lo.T], axis=0),
                     jnp.concatenate([wg_hi.T, zeros_t], axis=0)])
    pad = ((0, 0), (0, GATE_PAD - h2))
    wgc = jnp.stack([jnp.pad(wg_hi, pad), jnp.pad(wg_lo, pad)])
    return dict(
        wbch=w_in[:, 0:3 * d].astype(BF16),
        wqkvo=w_in[:, 3 * d:7 * d].astype(BF16),
        wgt=wgt, wgc=wgc,
        bgc=b_gate.reshape(h2, 1).astype(F32),
        bgr=jnp.pad(b_gate.reshape(1, h2).astype(F32), pad),
        wgc_conv=w_in[:, o_gate + h2:o_gate + h2 + d].astype(BF16),
        wgm=w_in[:, o_gate + h2 + d:o_gate + h2 + 2 * d].astype(BF16),
        cw=conv_w.astype(F32),
        wco=w_conv_out.astype(BF16), mhg=mh_g.reshape(1, d).astype(F32),
        wmo=w_m_out.astype(BF16), wo=w_o.astype(BF16),
        ln1_g=ln1_g.reshape(1, d), ln1_b=ln1_b.reshape(1, d),
        w1=w_ff1.astype(BF16), w2=w_ff2.astype(BF16),
        ln2_g=ln2_g.reshape(1, d), ln2_b=ln2_b.reshape(1, d))


def _prompt_layer(x, p, alpha):
    bsz, t, d = x.shape
    m1, conv_buf = _conv_branch_prompt(x, p["wbch"], p["cw"], p["wco"], p["wgc_conv"])
    hm, c, n, m = _mlstm_prompt(x, p["wqkvo"], p["wgt"], p["bgc"], p["mhg"])
    xf = x.reshape(bsz * t, d)
    x1 = _merge(xf, m1.reshape(bsz * t, d), hm, p["wmo"], p["wgm"], p["wo"], p["ln1_g"],
                p["ln1_b"], alpha)
    x2 = _ffn(x1, p["w1"], p["w2"], p["ln2_g"], p["ln2_b"], alpha)
    return x2.reshape(bsz, t, d), conv_buf, c, n, m


def _sample_layer(x_tm, conv_tm, c0, n0, m0, p, alpha, n_steps):
    d = x_tm.shape[1]
    m1, conv_new = _conv_branch_sample(x_tm, conv_tm, p["wbch"], p["cw"], p["wco"], p["wgc_conv"],
                                       n_steps)
    hm, c, n, m = _mlstm_sample(x_tm, p["wqkvo"], p["wgc"], p["bgr"], p["mhg"], c0,
                                n0.reshape(n0.shape[0], d), m0, n_steps)
    x1 = _merge(x_tm, m1, hm, p["wmo"], p["wgm"], p["wo"], p["ln1_g"], p["ln1_b"], alpha)
    x2 = _ffn(x1, p["w1"], p["w2"], p["ln2_g"], p["ln2_b"], alpha)
    return x2, conv_new, c, n.reshape(n0.shape), m


def kernel(x_prompt, x_sample, state_conv, state_C, state_n, state_m, w_in, b_gate, conv_w,
           w_conv_out, mh_g, w_m_out, w_o, ln1_g, ln1_b, w_ff1, w_ff2, ln2_g, ln2_b):
    depth = w_in.shape[0]
    alpha = (2.0 * depth) ** 0.25
    bsz, t, d = x_prompt.shape
    sb, st, _ = x_sample.shape
    assert t % TOKEN_TILE == 0 and TOKEN_TILE % CHUNK == 0 and d % N_HEADS == 0
    assert sb % SAMPLE_BATCH_BLOCK == 0 and st >= CONV_W - 1

    xp = x_prompt
    xs = jnp.transpose(x_sample, (1, 0, 2)).reshape(st * sb, d)
    outs = [[] for _ in range(8)]
    for l in range(depth):
        p = _layer_weights(w_in[l], b_gate[l], conv_w[l], w_conv_out[l], mh_g[l], w_m_out[l],
                           w_o[l], ln1_g[l], ln1_b[l], w_ff1[l], w_ff2[l], ln2_g[l], ln2_b[l])
        xp, cp, c_p, n_p, m_p = _prompt_layer(xp, p, alpha)
        conv_tm = jnp.transpose(state_conv[l], (1, 0, 2)).reshape((CONV_W - 1) * sb, d)
        xs, cs_tm, c_s, n_s, m_s = _sample_layer(xs, conv_tm, state_C[l], state_n[l], state_m[l],
                                                 p, alpha, st)
        cs = jnp.transpose(cs_tm.reshape(CONV_W - 1, sb, d), (1, 0, 2))
        for acc, val in zip(outs, (cp, cs, c_p, c_s, n_p, n_s, m_p, m_s)):
            acc.append(val)
    ys = jnp.transpose(xs.reshape(st, sb, d), (1, 0, 2))
    return (xp, ys) + tuple(jnp.stack(acc) for acc in outs)
```

```python
import functools

import jax
import jax.numpy as jnp
from jax import lax
from jax.experimental import pallas as pl
from jax.experimental.pallas import tpu as pltpu

F32 = jnp.float32
BF16 = jnp.bfloat16

LN_EPS = 1e-5
N_HEADS = 4
CHUNK = 128
CONV_W = 3
TOKEN_TILE = 512
SAMPLE_BATCH_BLOCK = 8
LANES = 128
GATE_PAD = LANES
VMEM_LIMIT = 56 * 1024 * 1024


def _dot(a, b):
    return jnp.dot(a, b, preferred_element_type=F32)


def _dot_nt(a, b):
    return lax.dot_general(a, b, (((1,), (1,)), ((), ())), preferred_element_type=F32)


def _dot_tn(a, b):
    return lax.dot_general(a, b, (((0,), (0,)), ((), ())), preferred_element_type=F32)


def _sigmoid(x):
    return 1.0 / (1.0 + jnp.exp(-x))


def _log_sigmoid(x):
    return jnp.minimum(x, 0.0) - jnp.log1p(jnp.exp(-jnp.abs(x)))


def _split_bf16(x):
    hi = x.astype(BF16)
    lo = (x - hi.astype(F32)).astype(BF16)
    return hi, lo


def _layer_norm(r, g, b):
    mu = jnp.mean(r, axis=-1, keepdims=True)
    xc = r - mu
    var = jnp.mean(xc * xc, axis=-1, keepdims=True)
    return xc * lax.rsqrt(var + LN_EPS) * g + b


def _head_norm(h):
    mu = jnp.mean(h, axis=-1, keepdims=True)
    hc = h - mu
    return hc * lax.rsqrt(jnp.mean(hc * hc, axis=-1, keepdims=True) + LN_EPS)


def _const_spec(shape):
    zeros = (0,) * len(shape)
    return pl.BlockSpec(shape, lambda *_: zeros, pipeline_mode=pl.Buffered(1))


def _params(n_axes):
    return pltpu.CompilerParams(dimension_semantics=("arbitrary",) * n_axes,
                                vmem_limit_bytes=VMEM_LIMIT)


def _conv_stripes(d):
    sw = 512 if d % 512 == 0 else d
    return [slice(s * sw, (s + 1) * sw) for s in range(d // sw)]


def _conv_prompt_kernel(x_ref, wbch_ref, cw_ref, wco_ref, wgc_ref, m1_ref, cs_ref, u_s, a_s):
    tm, d = a_s.shape
    @pl.when(pl.program_id(1) == 0)
    def _():
        u_s[0:8, :] = jnp.zeros((8, d), F32)

    xb = x_ref[0].astype(BF16)
    for cs in _conv_stripes(d):
        off = cs.start
        bg = _dot(xb, wbch_ref[:, off:cs.stop])
        cg = _dot(xb, wbch_ref[:, d + off:d + cs.stop])
        hc = _dot(xb, wbch_ref[:, 2 * d + off:2 * d + cs.stop])
        u = cg * hc
        u_s[8:8 + tm, cs] = u
        conv = (u_s[6:6 + tm, cs] * cw_ref[0:1, cs] + u_s[7:7 + tm, cs] * cw_ref[1:2, cs]
                + u * cw_ref[2:3, cs])
        a_s[:, cs] = (bg * conv).astype(BF16)
    cs_ref[0] = u_s[tm + 6:tm + 8, :]
    u_s[0:8, :] = u_s[tm:tm + 8, :]
    yc = _dot(a_s[...], wco_ref[...])
    m1_ref[0] = _sigmoid(_dot(xb, wgc_ref[...])) * yc


def _conv_sample_kernel(x_ref, st_ref, wbch_ref, cw_ref, wco_ref, wgc_ref, m1_ref, cs_ref, a_s,
                        *, n_steps):
    rows, d = a_s.shape
    nb = rows // n_steps
    xb = x_ref[...].astype(BF16)
    for cs in _conv_stripes(d):
        off = cs.start
        bg = _dot(xb, wbch_ref[:, off:cs.stop])
        cg = _dot(xb, wbch_ref[:, d + off:d + cs.stop])
        hc = _dot(xb, wbch_ref[:, 2 * d + off:2 * d + cs.stop])
        u = cg * hc
        up = [st_ref[j * nb:(j + 1) * nb, cs] for j in range(CONV_W - 1)]
        up += [u[t * nb:(t + 1) * nb] for t in range(n_steps)]
        for t in range(n_steps):
            conv = (up[t] * cw_ref[0:1, cs] + up[t + 1] * cw_ref[1:2, cs]
                    + up[t + 2] * cw_ref[2:3, cs])
            a_s[t * nb:(t + 1) * nb, cs] = (bg[t * nb:(t + 1) * nb] * conv).astype(BF16)
        for j in range(CONV_W - 1):
            cs_ref[j * nb:(j + 1) * nb, cs] = up[n_steps + j]
    yc = _dot(a_s[...], wco_ref[...])
    m1_ref[...] = _sigmoid(_dot(xb, wgc_ref[...])) * yc


def _conv_branch_prompt(x, wbch, cw, wco, wgc):
    bsz, t, d = x.shape
    tm = TOKEN_TILE
    return pl.pallas_call(
        _conv_prompt_kernel,
        grid=(bsz, t // tm),
        in_specs=[pl.BlockSpec((1, tm, d), lambda b, j: (b, j, 0)),
                  _const_spec(wbch.shape), _const_spec(cw.shape), _const_spec(wco.shape),
                  _const_spec(wgc.shape)],
        out_specs=[pl.BlockSpec((1, tm, d), lambda b, j: (b, j, 0)),
                   pl.BlockSpec((1, CONV_W - 1, d), lambda b, j: (b, 0, 0))],
        out_shape=[jax.ShapeDtypeStruct((bsz, t, d), F32),
                   jax.ShapeDtypeStruct((bsz, CONV_W - 1, d), F32)],
        scratch_shapes=[pltpu.VMEM((tm + 8, d), F32), pltpu.VMEM((tm, d), BF16)],
        compiler_params=_params(2),
        name="conv_branch_prompt",
    )(x, wbch, cw, wco, wgc)


def _conv_branch_sample(x_tm, st_tm, wbch, cw, wco, wgc, n_steps):
    rows, d = x_tm.shape
    return pl.pallas_call(
        functools.partial(_conv_sample_kernel, n_steps=n_steps),
        grid=(1,),
        in_specs=[_const_spec(x_tm.shape), _const_spec(st_tm.shape), _const_spec(wbch.shape),
                  _const_spec(cw.shape), _const_spec(wco.shape), _const_spec(wgc.shape)],
        out_specs=[pl.BlockSpec((rows, d), lambda i: (0, 0)),
                   pl.BlockSpec(st_tm.shape, lambda i: (0, 0))],
        out_shape=[jax.ShapeDtypeStruct((rows, d), F32),
                   jax.ShapeDtypeStruct(st_tm.shape, F32)],
        scratch_shapes=[pltpu.VMEM((rows, d), BF16)],
        compiler_params=_params(1),
        name="conv_branch_sample",
    )(x_tm, st_tm, wbch, cw, wco, wgc)


def _project_qkvo(xh, wqkvo_ref, mhg_ref, q_s, k_s, v_s, og_s, d, dh):
    q_s[...] = _dot(xh, wqkvo_ref[:, 0:d]).astype(BF16).astype(q_s.dtype)
    k_s[...] = (_dot(xh, wqkvo_ref[:, d:2 * d]) * (dh ** -0.5)).astype(BF16).astype(k_s.dtype)
    v_s[...] = _dot(xh, wqkvo_ref[:, 2 * d:3 * d]).astype(BF16).astype(v_s.dtype)
    og_s[...] = mhg_ref[...] * _sigmoid(_dot(xh, wqkvo_ref[:, 3 * d:4 * d]))


def _rep(col, times):
    return col if times == 1 else jnp.concatenate([col] * times, axis=1)


def _mlstm_prompt_kernel(x_ref, wqkvo_ref, wgt_ref, bgc_ref, mhg_ref,
                         hm_ref, c_ref, n_ref, m_ref,
                         q_s, k_s, vx_s, og_s, e_s, colb_s, colm_s, cole_s, cst_s, mst_s):
    tm, d = q_s.shape
    dh = d // N_HEADS
    L = CHUNK
    LANES = colb_s.shape[-1]
    n_chunks = tm // L
    wide = (dh + LANES) // LANES

    @pl.when(pl.program_id(1) == 0)
    def _():
        cst_s[...] = jnp.zeros(cst_s.shape, F32)
        mst_s[...] = jnp.zeros(mst_s.shape, F32)

    x = x_ref[0]
    xh, xl = _split_bf16(x)

    ga = _dot_nt(wgt_ref[0], xh)
    gb = _dot_nt(wgt_ref[1], xl)
    gt = ga[0:8] + ga[8:16] + gb[0:8] + bgc_ref[...]
    is_input_gate = lax.broadcasted_iota(jnp.int32, gt.shape, 0) < N_HEADS
    g = jnp.where(is_input_gate, gt, _log_sigmoid(gt))

    pos = lax.broadcasted_iota(jnp.int32, g.shape, 1) % L
    csum = g
    shift = 1
    while shift < L:
        csum = csum + jnp.where(pos >= shift, pltpu.roll(csum, shift, 1), 0.0)
        shift *= 2
    ba = jnp.concatenate([csum[N_HEADS:], g[:N_HEADS] - csum[N_HEADS:]], axis=0)

    ri = lax.broadcasted_iota(jnp.int32, (L, L), 0)
    ci = lax.broadcasted_iota(jnp.int32, (L, L), 1)
    causal = ri >= ci

    for c in range(n_chunks):
        slab = ba[:, c * L:(c + 1) * L]
        cols = slab.T
        for h in range(N_HEADS):
            b_rep = jnp.broadcast_to(cols[:, h:h + 1], (L, LANES))
            a_rep = jnp.broadcast_to(cols[:, N_HEADS + h:N_HEADS + h + 1], (L, LANES))
            a_row = slab[N_HEADS + h:N_HEADS + h + 1, :]
            dm = jnp.where(causal, b_rep + a_row, -jnp.inf)
            m_loc = jnp.broadcast_to(jnp.max(dm, axis=1, keepdims=True), (L, LANES))
            e_s[c, h] = jnp.exp(dm - m_loc)
            colb_s[c, h] = b_rep
            colm_s[c, h] = m_loc
            cole_s[c, h] = jnp.exp(a_rep + b_rep[L - 1:L, :] - m_loc[L - 1:L, :])

    q_s[...] = _dot(xh, wqkvo_ref[:, 0:d]).astype(BF16)
    k_s[...] = (_dot(xh, wqkvo_ref[:, d:2 * d]) * (dh ** -0.5)).astype(BF16)
    v = _dot(xh, wqkvo_ref[:, 2 * d:3 * d]).astype(BF16)
    for h in range(N_HEADS):
        vx_s[h, :, 0:dh] = v[:, h * dh:(h + 1) * dh]
        vx_s[h, :, dh:] = jnp.ones((tm, LANES), BF16)
    og_s[...] = mhg_ref[...] * _sigmoid(_dot(xh, wqkvo_ref[:, 3 * d:4 * d]))

    def chunk_body(c, carry):
        rows = pl.ds(pl.multiple_of(c * L, L), L)
        heads = [slice(h * dh, (h + 1) * dh) for h in range(N_HEADS)]
        qk = [_dot_nt(q_s[rows, hc], k_s[rows, hc]) for hc in heads]
        states = [cst_s[h] for h in range(N_HEADS)]
        qc = [_dot(q_s[rows, hc], states[h].astype(BF16)) for h, hc in enumerate(heads)]
        ux = []
        for h, hc in enumerate(heads):
            kw = k_s[rows, hc].astype(F32) * _rep(cole_s[c, h], dh // LANES)
            ux.append(_dot_tn(kw.astype(BF16), vx_s[h, rows, :]))
        svx = []
        for h in range(N_HEADS):
            s_loc = qk[h] * e_s[c, h]
            svx.append(_dot(s_loc.astype(BF16), vx_s[h, rows, :]))
        for h, hc in enumerate(heads):
            m_prev = mst_s[h]
            m_loc = colm_s[c, h]
            inter = colb_s[c, h] + m_prev
            m_t = jnp.maximum(inter, m_loc)
            w_inter = _rep(jnp.exp(inter - m_t), wide)
            w_loc = _rep(jnp.exp(m_loc - m_t), wide)
            nd = w_inter * qc[h] + w_loc * svx[h]
            inv = 1.0 / jnp.maximum(jnp.abs(nd[:, dh:]), jnp.exp(-m_t))
            hh = nd[:, 0:dh] * _rep(inv, dh // LANES)
            hm_ref[rows, hc] = (_head_norm(hh) * og_s[rows, hc]).astype(BF16)
            m_new = m_t[L - 1:L, :]
            decay = _rep(jnp.exp(inter[L - 1:L, :] - m_new), wide)
            grow = _rep(jnp.exp(m_loc[L - 1:L, :] - m_new), wide)
            cst_s[h] = decay * states[h] + grow * ux[h]
            mst_s[h] = m_new
        return carry

    lax.fori_loop(0, n_chunks, chunk_body, 0)

    @pl.when(pl.program_id(1) == pl.num_programs(1) - 1)
    def _():
        for h in range(N_HEADS):
            state = cst_s[h]
            c_ref[0, h] = state[:, 0:dh]
            n_ref[0, h] = state[:, dh:].T[0:1, :]
            m_ref[0, h] = mst_s[h][:, 0:1]


def _mlstm_prompt(x, wqkvo, wgt, bgc, mhg):
    bsz, t, d = x.shape
    dh = d // N_HEADS
    tm = TOKEN_TILE
    nc = tm // CHUNK
    dx = dh + LANES
    hm, c, n, m = pl.pallas_call(
        _mlstm_prompt_kernel,
        grid=(bsz, t // tm),
        in_specs=[pl.BlockSpec((1, tm, d), lambda b, j: (b, j, 0)),
                  _const_spec(wqkvo.shape), _const_spec(wgt.shape), _const_spec(bgc.shape),
                  _const_spec(mhg.shape)],
        out_specs=[pl.BlockSpec((tm, d), lambda b, j: (b * (t // tm) + j, 0)),
                   pl.BlockSpec((1, N_HEADS, dh, dh), lambda b, j: (b, 0, 0, 0)),
                   pl.BlockSpec((1, N_HEADS, 1, dh), lambda b, j: (b, 0, 0, 0)),
                   pl.BlockSpec((1, N_HEADS, 1, 1), lambda b, j: (b, 0, 0, 0))],
        out_shape=[jax.ShapeDtypeStruct((bsz * t, d), BF16),
                   jax.ShapeDtypeStruct((bsz, N_HEADS, dh, dh), F32),
                   jax.ShapeDtypeStruct((bsz, N_HEADS, 1, dh), F32),
                   jax.ShapeDtypeStruct((bsz, N_HEADS, 1, 1), F32)],
        scratch_shapes=[
            pltpu.VMEM((tm, d), BF16),
            pltpu.VMEM((tm, d), BF16),
            pltpu.VMEM((N_HEADS, tm, dx), BF16),
            pltpu.VMEM((tm, d), F32),
            pltpu.VMEM((nc, N_HEADS, CHUNK, CHUNK), F32),
            pltpu.VMEM((nc, N_HEADS, CHUNK, LANES), F32),
            pltpu.VMEM((nc, N_HEADS, CHUNK, LANES), F32),
            pltpu.VMEM((nc, N_HEADS, CHUNK, LANES), F32),
            pltpu.VMEM((N_HEADS, dh, dx), F32),
            pltpu.VMEM((N_HEADS, 1, LANES), F32)],
        compiler_params=_params(2),
        name="mlstm_prompt",
    )(x, wqkvo, wgt, bgc, mhg)
    return hm, c, n.reshape(bsz, N_HEADS, dh), m.reshape(bsz, N_HEADS)


def _mlstm_sample_kernel(x_ref, wqkvo_ref, wgc_ref, bgr_ref, mhg_ref, c0_ref, n0_ref, m0_ref,
                         hm_ref, c_ref, n_ref, m_ref, q_s, k_s, v_s, og_s, g_s, *, n_steps):
    rows, d = q_s.shape
    dh = d // N_HEADS
    nb = rows // n_steps
    bb = SAMPLE_BATCH_BLOCK
    T = n_steps
    i = pl.program_id(0)
    h = pl.program_id(1)

    @pl.when(jnp.logical_and(i == 0, h == 0))
    def _():
        x = x_ref[...]
        xh, xl = _split_bf16(x)
        _project_qkvo(xh, wqkvo_ref, mhg_ref, q_s, k_s, v_s, og_s, d, dh)
        g = _dot(xh, wgc_ref[0]) + _dot(xl, wgc_ref[0]) + _dot(xh, wgc_ref[1]) + bgr_ref[...]
        is_input_gate = lax.broadcasted_iota(jnp.int32, g.shape, 1) < N_HEADS
        g_s[...] = jnp.where(is_input_gate, g, _log_sigmoid(g))

    r0 = pl.multiple_of(i * bb, bb)
    trow = [pl.ds(pl.multiple_of(t * nb + r0, bb), bb) for t in range(T)]
    hc = pl.ds(pl.multiple_of(h * dh, dh), dh)
    owner = lax.broadcasted_iota(jnp.int32, (T * bb, dh), 0) % bb
    gate_lane = lax.broadcasted_iota(jnp.int32, (bb, g_s.shape[1]), 1)

    def gate_column(t, col):
        return jnp.sum(jnp.where(gate_lane == col, g_s[trow[t], :], 0.0), axis=1, keepdims=True)

    qf = [q_s[trow[t], hc] for t in range(T)]
    kf = [k_s[trow[t], hc] for t in range(T)]
    vf = [v_s[trow[t], hc] for t in range(T)]
    li = [gate_column(t, h) for t in range(T)]
    lf = [gate_column(t, N_HEADS + h) for t in range(T)]
    m0 = m0_ref[0]
    n0 = n0_ref[...]
    b = [lf[0]]
    for t in range(1, T):
        b.append(b[t - 1] + lf[t])
    a = [li[t] - b[t] for t in range(T)]
    m_t, w_inter, s_w = [], [], []
    for t in range(T):
        inter = b[t] + m0
        mt = inter
        for s in range(t + 1):
            mt = jnp.maximum(mt, b[t] + a[s])
        m_t.append(mt)
        w_inter.append(jnp.exp(inter - mt))
        s_w.append([jnp.sum(qf[t] * kf[s], axis=1, keepdims=True) * jnp.exp(b[t] + a[s] - mt)
                    for s in range(t + 1)])
    m_new = m_t[T - 1]
    b_last = b[T - 1]
    decay = jnp.exp(b_last + m0 - m_new)
    kw = [kf[s] * jnp.exp(a[s] + b_last - m_new) for s in range(T)]
    n_new = decay * n0
    for s in range(T):
        n_new = n_new + kw[s]
    n_ref[...] = n_new
    m_ref[0] = m_new

    q_blk = jnp.concatenate(qf, axis=0).astype(BF16)
    kw_blk = jnp.concatenate(kw, axis=0).astype(BF16)
    v_blk = jnp.concatenate(vf, axis=0)
    qc = jnp.zeros((T * bb, dh), F32)
    for bi in range(bb):
        mine = owner == bi
        c_old = c0_ref[bi, 0]
        qc = jnp.where(mine, _dot(q_blk, c_old.astype(BF16)), qc)
        upd = _dot_tn(kw_blk, jnp.where(mine, v_blk, 0.0).astype(BF16))
        c_ref[bi, 0] = decay[bi:bi + 1, :] * c_old + upd

    for t in range(T):
        num = w_inter[t] * qc[t * bb:(t + 1) * bb]
        den = w_inter[t] * jnp.sum(qf[t] * n0, axis=1, keepdims=True)
        for s in range(t + 1):
            num = num + s_w[t][s] * vf[s]
            den = den + s_w[t][s]
        hh = num / jnp.maximum(jnp.abs(den), jnp.exp(-m_t[t]))
        hm_ref[trow[t], hc] = _head_norm(hh) * og_s[trow[t], hc]


def _mlstm_sample(x_tm, wqkvo, wgc, bgr, mhg, c0, n0, m0, n_steps):
    rows, d = x_tm.shape
    nb = rows // n_steps
    dh = d // N_HEADS
    bb = SAMPLE_BATCH_BLOCK
    cblock = pl.BlockSpec((bb, 1, dh, dh), lambda i, h: (i, h, 0, 0))
    nblock = pl.BlockSpec((bb, dh), lambda i, h: (i, h))
    mblock = pl.BlockSpec((1, bb, 1), lambda i, h: (h, i, 0))
    m0_hm = m0.T.reshape(N_HEADS, nb, 1)
    hm, c, n, m_hm = pl.pallas_call(
        functools.partial(_mlstm_sample_kernel, n_steps=n_steps),
        grid=(nb // bb, N_HEADS),
        in_specs=[_const_spec(x_tm.shape), _const_spec(wqkvo.shape), _const_spec(wgc.shape),
                  _const_spec(bgr.shape), _const_spec(mhg.shape), cblock, nblock, mblock],
        out_specs=[pl.BlockSpec((rows, d), lambda i, h: (0, 0)), cblock, nblock, mblock],
        out_shape=[jax.ShapeDtypeStruct((rows, d), F32),
                   jax.ShapeDtypeStruct(c0.shape, F32),
                   jax.ShapeDtypeStruct((nb, d), F32),
                   jax.ShapeDtypeStruct((N_HEADS, nb, 1), F32)],
        scratch_shapes=[pltpu.VMEM((rows, d), F32), pltpu.VMEM((rows, d), F32),
                        pltpu.VMEM((rows, d), F32), pltpu.VMEM((rows, d), F32),
                        pltpu.VMEM((rows, GATE_PAD), F32)],
        compiler_params=_params(2),
        name="mlstm_sample",
    )(x_tm, wqkvo, wgc, bgr, mhg, c0, n0, m0_hm)
    return hm, c, n, m_hm.reshape(N_HEADS, nb).T


def _merge_kernel(x_ref, m1_ref, hm_ref, wmo_ref, wgm_ref, wo_ref, g_ref, b_ref, o_ref, *, alpha):
    x = x_ref[...]
    xb = x.astype(BF16)
    ym = _dot(hm_ref[...].astype(BF16), wmo_ref[...])
    merged = m1_ref[...] + _sigmoid(_dot(xb, wgm_ref[...])) * ym
    r = alpha * x + _dot(merged.astype(BF16), wo_ref[...])
    o_ref[...] = _layer_norm(r, g_ref[...], b_ref[...])


def _merge(x, m1, hm, wmo, wgm, wo, g, b, alpha):
    n, d = x.shape
    tm = min(TOKEN_TILE, n)
    tile = pl.BlockSpec((tm, d), lambda i: (i, 0))
    return pl.pallas_call(
        functools.partial(_merge_kernel, alpha=alpha),
        grid=(n // tm,),
        in_specs=[tile, tile, tile, _const_spec(wmo.shape), _const_spec(wgm.shape),
                  _const_spec(wo.shape), _const_spec(g.shape), _const_spec(b.shape)],
        out_specs=tile,
        out_shape=jax.ShapeDtypeStruct((n, d), F32),
        compiler_params=_params(1),
        name="merge_ln1",
    )(x, m1, hm, wmo, wgm, wo, g, b)


def _ffn_kernel(x_ref, w1_ref, w2_ref, g_ref, b_ref, o_ref, hid_s, *, alpha):
    x = x_ref[...]
    xb = x.astype(BF16)
    dff = hid_s.shape[1]
    sw = 1024 if dff % 1024 == 0 else dff
    for s in range(dff // sw):
        cs = slice(s * sw, (s + 1) * sw)
        hid = jnp.maximum(_dot(xb, w1_ref[:, cs]), 0.0)
        hid_s[:, cs] = (hid * hid).astype(BF16)
    r = alpha * x + _dot(hid_s[...], w2_ref[...])
    o_ref[...] = _layer_norm(r, g_ref[...], b_ref[...])


def _ffn(x, w1, w2, g, b, alpha):
    n, d = x.shape
    tm = min(TOKEN_TILE, n)
    tile = pl.BlockSpec((tm, d), lambda i: (i, 0))
    return pl.pallas_call(
        functools.partial(_ffn_kernel, alpha=alpha),
        grid=(n // tm,),
        in_specs=[tile, _const_spec(w1.shape), _const_spec(w2.shape), _const_spec(g.shape),
                  _const_spec(b.shape)],
        out_specs=tile,
        out_shape=jax.ShapeDtypeStruct((n, d), F32),
        scratch_shapes=[pltpu.VMEM((tm, w1.shape[1]), BF16)],
        compiler_params=_params(1),
        name="ffn_ln2",
    )(x, w1, w2, g, b)


def _layer_weights(w_in, b_gate, conv_w, w_conv_out, mh_g, w_m_out, w_o, ln1_g, ln1_b,
                   w_ff1, w_ff2, ln2_g, ln2_b):
    d = w_in.shape[0]
    h2 = 2 * N_HEADS
    o_gate = 7 * d
    wg = w_in[:, o_gate:o_gate + h2]
    wg_hi, wg_lo = _split_bf16(wg)
    zeros_t = jnp.zeros((h2, d), BF16)
    wgt = jnp.stack([jnp.concatenate([wg_hi.T, wg_lo.T], axis=0),
                     jnp.concatenate([wg_hi.T, zeros_t], axis=0)])
    pad = ((0, 0), (0, GATE_PAD - h2))
    wgc = jnp.stack([jnp.pad(wg_hi, pad), jnp.pad(wg_lo, pad)])
    return dict(
        wbch=w_in[:, 0:3 * d].astype(BF16),
        wqkvo=w_in[:, 3 * d:7 * d].astype(BF16),
        wgt=wgt, wgc=wgc,
        bgc=b_gate.reshape(h2, 1).astype(F32),
        bgr=jnp.pad(b_gate.reshape(1, h2).astype(F32), pad),
        wgc_conv=w_in[:, o_gate + h2:o_gate + h2 + d].astype(BF16),
        wgm=w_in[:, o_gate + h2 + d:o_gate + h2 + 2 * d].astype(BF16),
        cw=conv_w.astype(F32),
        wco=w_conv_out.astype(BF16), mhg=mh_g.reshape(1, d).astype(F32),
        wmo=w_m_out.astype(BF16), wo=w_o.astype(BF16),
        ln1_g=ln1_g.reshape(1, d), ln1_b=ln1_b.reshape(1, d),
        w1=w_ff1.astype(BF16), w2=w_ff2.astype(BF16),
        ln2_g=ln2_g.reshape(1, d), ln2_b=ln2_b.reshape(1, d))


def _prompt_layer(x, p, alpha):
    bsz, t, d = x.shape
    m1, conv_buf = _conv_branch_prompt(x, p["wbch"], p["cw"], p["wco"], p["wgc_conv"])
    hm, c, n, m = _mlstm_prompt(x, p["wqkvo"], p["wgt"], p["bgc"], p["mhg"])
    xf = x.reshape(bsz * t, d)
    x1 = _merge(xf, m1.reshape(bsz * t, d), hm, p["wmo"], p["wgm"], p["wo"], p["ln1_g"],
                p["ln1_b"], alpha)
    x2 = _ffn(x1, p["w1"], p["w2"], p["ln2_g"], p["ln2_b"], alpha)
    return x2.reshape(bsz, t, d), conv_buf, c, n, m


def _sample_layer(x_tm, conv_tm, c0, n0, m0, p, alpha, n_steps):
    d = x_tm.shape[1]
    m1, conv_new = _conv_branch_sample(x_tm, conv_tm, p["wbch"], p["cw"], p["wco"], p["wgc_conv"],
                                       n_steps)
    hm, c, n, m = _mlstm_sample(x_tm, p["wqkvo"], p["wgc"], p["bgr"], p["mhg"], c0,
                                n0.reshape(n0.shape[0], d), m0, n_steps)
    x1 = _merge(x_tm, m1, hm, p["wmo"], p["wgm"], p["wo"], p["ln1_g"], p["ln1_b"], alpha)
    x2 = _ffn(x1, p["w1"], p["w2"], p["ln2_g"], p["ln2_b"], alpha)
    return x2, conv_new, c, n.reshape(n0.shape), m


def kernel(x_prompt, x_sample, state_conv, state_C, state_n, state_m, w_in, b_gate, conv_w,
           w_conv_out, mh_g, w_m_out, w_o, ln1_g, ln1_b, w_ff1, w_ff2, ln2_g, ln2_b):
    depth = w_in.shape[0]
    alpha = (2.0 * depth) ** 0.25
    bsz, t, d = x_prompt.shape
    sb, st, _ = x_sample.shape
    assert t % TOKEN_TILE == 0 and TOKEN_TILE % CHUNK == 0 and d % N_HEADS == 0
    assert sb % SAMPLE_BATCH_BLOCK == 0 and st >= CONV_W - 1

    xp = x_prompt
    xs = jnp.transpose(x_sample, (1, 0, 2)).reshape(st * sb, d)
    outs = [[] for _ in range(8)]
    for l in range(depth):
        p = _layer_weights(w_in[l], b_gate[l], conv_w[l], w_conv_out[l], mh_g[l], w_m_out[l],
                           w_o[l], ln1_g[l], ln1_b[l], w_ff1[l], w_ff2[l], ln2_g[l], ln2_b[l])
        xp, cp, c_p, n_p, m_p = _prompt_layer(xp, p, alpha)
        conv_tm = jnp.transpose(state_conv[l], (1, 0, 2)).reshape((CONV_W - 1) * sb, d)
        xs, cs_tm, c_s, n_s, m_s = _sample_layer(xs, conv_tm, state_C[l], state_n[l], state_m[l],
                                                 p, alpha, st)
        cs = jnp.transpose(cs_tm.reshape(CONV_W - 1, sb, d), (1, 0, 2))
        for acc, val in zip(outs, (cp, cs, c_p, c_s, n_p, n_s, m_p, m_s)):
            acc.append(val)
    ys = jnp.transpose(xs.reshape(st, sb, d), (1, 0, 2))
    return (xp, ys) + tuple(jnp.stack(acc) for acc in outs)
```

```python
import functools

import jax
import jax.numpy as jnp
from jax import lax
from jax.experimental import pallas as pl
from jax.experimental.pallas import tpu as pltpu

F32 = jnp.float32
BF16 = jnp.bfloat16

LN_EPS = 1e-5
N_HEADS = 4
CHUNK = 128
CONV_W = 3
TOKEN_TILE = 512
SAMPLE_BATCH_BLOCK = 8
LANES = 128
GATE_PAD = LANES
VMEM_LIMIT = 56 * 1024 * 1024


def _dot(a, b):
    return jnp.dot(a, b, preferred_element_type=F32)


def _dot_nt(a, b):
    return lax.dot_general(a, b, (((1,), (1,)), ((), ())), preferred_element_type=F32)


def _dot_tn(a, b):
    return lax.dot_general(a, b, (((0,), (0,)), ((), ())), preferred_element_type=F32)


def _sigmoid(x):
    return 1.0 / (1.0 + jnp.exp(-x))


def _log_sigmoid(x):
    return jnp.minimum(x, 0.0) - jnp.log1p(jnp.exp(-jnp.abs(x)))


def _split_bf16(x):
    hi = x.astype(BF16)
    lo = (x - hi.astype(F32)).astype(BF16)
    return hi, lo


def _layer_norm(r, g, b):
    mu = jnp.mean(r, axis=-1, keepdims=True)
    xc = r - mu
    var = jnp.mean(xc * xc, axis=-1, keepdims=True)
    return xc * lax.rsqrt(var + LN_EPS) * g + b


def _head_norm(h):
    mu = jnp.mean(h, axis=-1, keepdims=True)
    hc = h - mu
    return hc * lax.rsqrt(jnp.mean(hc * hc, axis=-1, keepdims=True) + LN_EPS)


def _const_spec(shape):
    zeros = (0,) * len(shape)
    return pl.BlockSpec(shape, lambda *_: zeros, pipeline_mode=pl.Buffered(1))


def _window_spec(block_shape, index):
    return pl.BlockSpec(block_shape, lambda *_: index, pipeline_mode=pl.Buffered(1))


def _bch_spec(d):
    return _window_spec((d, 3 * d), (0, 0))


def _qkvo_spec(d):
    return _window_spec((pl.Element(d), pl.Element(4 * d)), (0, 3 * d))


def _gate_spec(d, which):
    return _window_spec((d, d), (0, which))


def _params(n_axes):
    return pltpu.CompilerParams(dimension_semantics=("arbitrary",) * n_axes,
                                vmem_limit_bytes=VMEM_LIMIT)


def _conv_stripes(d):
    sw = 512 if d % 512 == 0 else d
    return [slice(s * sw, (s + 1) * sw) for s in range(d // sw)]


def _conv_prompt_kernel(x_ref, wbch_ref, cw_ref, wco_ref, wgc_ref, m1_ref, cs_ref, u_s, a_s):
    tm, d = a_s.shape
    @pl.when(pl.program_id(1) == 0)
    def _():
        u_s[0:8, :] = jnp.zeros((8, d), F32)

    xb = x_ref[0].astype(BF16)
    for cs in _conv_stripes(d):
        off = cs.start
        bg = _dot(xb, wbch_ref[:, off:cs.stop])
        cg = _dot(xb, wbch_ref[:, d + off:d + cs.stop])
        hc = _dot(xb, wbch_ref[:, 2 * d + off:2 * d + cs.stop])
        u = cg * hc
        u_s[8:8 + tm, cs] = u
        conv = (u_s[6:6 + tm, cs] * cw_ref[0:1, cs] + u_s[7:7 + tm, cs] * cw_ref[1:2, cs]
                + u * cw_ref[2:3, cs])
        a_s[:, cs] = (bg * conv).astype(BF16)
    cs_ref[0] = u_s[tm + 6:tm + 8, :]
    u_s[0:8, :] = u_s[tm:tm + 8, :]
    yc = _dot(a_s[...], wco_ref[...])
    m1_ref[0] = _sigmoid(_dot(xb, wgc_ref[...])) * yc


def _conv_sample_kernel(x_ref, st_ref, wbch_ref, cw_ref, wco_ref, wgc_ref, m1_ref, cs_ref, a_s,
                        *, n_steps):
    rows, d = a_s.shape
    nb = rows // n_steps
    xb = x_ref[...].astype(BF16)
    for cs in _conv_stripes(d):
        off = cs.start
        bg = _dot(xb, wbch_ref[:, off:cs.stop])
        cg = _dot(xb, wbch_ref[:, d + off:d + cs.stop])
        hc = _dot(xb, wbch_ref[:, 2 * d + off:2 * d + cs.stop])
        u = cg * hc
        up = [st_ref[j * nb:(j + 1) * nb, cs] for j in range(CONV_W - 1)]
        up += [u[t * nb:(t + 1) * nb] for t in range(n_steps)]
        for t in range(n_steps):
            conv = (up[t] * cw_ref[0:1, cs] + up[t + 1] * cw_ref[1:2, cs]
                    + up[t + 2] * cw_ref[2:3, cs])
            a_s[t * nb:(t + 1) * nb, cs] = (bg[t * nb:(t + 1) * nb] * conv).astype(BF16)
        for j in range(CONV_W - 1):
            cs_ref[j * nb:(j + 1) * nb, cs] = up[n_steps + j]
    yc = _dot(a_s[...], wco_ref[...])
    m1_ref[...] = _sigmoid(_dot(xb, wgc_ref[...])) * yc


def _conv_branch_prompt(x, wbch, cw, wco, wgc):
    bsz, t, d = x.shape
    tm = TOKEN_TILE
    return pl.pallas_call(
        _conv_prompt_kernel,
        grid=(bsz, t // tm),
        in_specs=[pl.BlockSpec((1, tm, d), lambda b, j: (b, j, 0)),
                  _bch_spec(d), _const_spec(cw.shape), _const_spec(wco.shape),
                  _gate_spec(d, 0)],
        out_specs=[pl.BlockSpec((1, tm, d), lambda b, j: (b, j, 0)),
                   pl.BlockSpec((1, CONV_W - 1, d), lambda b, j: (b, 0, 0))],
        out_shape=[jax.ShapeDtypeStruct((bsz, t, d), F32),
                   jax.ShapeDtypeStruct((bsz, CONV_W - 1, d), F32)],
        scratch_shapes=[pltpu.VMEM((tm + 8, d), F32), pltpu.VMEM((tm, d), BF16)],
        compiler_params=_params(2),
        name="conv_branch_prompt",
    )(x, wbch, cw, wco, wgc)


def _conv_branch_sample(x_tm, st_tm, wbch, cw, wco, wgc, n_steps):
    rows, d = x_tm.shape
    return pl.pallas_call(
        functools.partial(_conv_sample_kernel, n_steps=n_steps),
        grid=(1,),
        in_specs=[_const_spec(x_tm.shape), _const_spec(st_tm.shape), _bch_spec(d),
                  _const_spec(cw.shape), _const_spec(wco.shape), _gate_spec(d, 0)],
        out_specs=[pl.BlockSpec((rows, d), lambda i: (0, 0)),
                   pl.BlockSpec(st_tm.shape, lambda i: (0, 0))],
        out_shape=[jax.ShapeDtypeStruct((rows, d), F32),
                   jax.ShapeDtypeStruct(st_tm.shape, F32)],
        scratch_shapes=[pltpu.VMEM((rows, d), BF16)],
        compiler_params=_params(1),
        name="conv_branch_sample",
    )(x_tm, st_tm, wbch, cw, wco, wgc)


def _project_qkvo(xh, wqkvo_ref, mhg_ref, q_s, k_s, v_s, og_s, d, dh):
    q_s[...] = _dot(xh, wqkvo_ref[:, 0:d]).astype(BF16).astype(q_s.dtype)
    k_s[...] = (_dot(xh, wqkvo_ref[:, d:2 * d]) * (dh ** -0.5)).astype(BF16).astype(k_s.dtype)
    v_s[...] = _dot(xh, wqkvo_ref[:, 2 * d:3 * d]).astype(BF16).astype(v_s.dtype)
    og_s[...] = mhg_ref[...] * _sigmoid(_dot(xh, wqkvo_ref[:, 3 * d:4 * d]))


def _rep(col, times):
    return col if times == 1 else jnp.concatenate([col] * times, axis=1)


def _mlstm_prompt_kernel(x_ref, wqkvo_ref, wgt_ref, bgc_ref, mhg_ref,
                         hm_ref, c_ref, n_ref, m_ref,
                         q_s, k_s, vx_s, og_s, e_s, colb_s, colm_s, cole_s, cst_s, mst_s):
    tm, d = q_s.shape
    dh = d // N_HEADS
    L = CHUNK
    LANES = colb_s.shape[-1]
    n_chunks = tm // L
    wide = (dh + LANES) // LANES

    @pl.when(pl.program_id(1) == 0)
    def _():
        cst_s[...] = jnp.zeros(cst_s.shape, F32)
        mst_s[...] = jnp.zeros(mst_s.shape, F32)

    x = x_ref[0]
    xh, xl = _split_bf16(x)

    ga = _dot_nt(wgt_ref[0], xh)
    gb = _dot_nt(wgt_ref[1], xl)
    gt = ga[0:8] + ga[8:16] + gb[0:8] + bgc_ref[...]
    is_input_gate = lax.broadcasted_iota(jnp.int32, gt.shape, 0) < N_HEADS
    g = jnp.where(is_input_gate, gt, _log_sigmoid(gt))

    pos = lax.broadcasted_iota(jnp.int32, g.shape, 1) % L
    csum = g
    shift = 1
    while shift < L:
        csum = csum + jnp.where(pos >= shift, pltpu.roll(csum, shift, 1), 0.0)
        shift *= 2
    ba = jnp.concatenate([csum[N_HEADS:], g[:N_HEADS] - csum[N_HEADS:]], axis=0)

    ri = lax.broadcasted_iota(jnp.int32, (L, L), 0)
    ci = lax.broadcasted_iota(jnp.int32, (L, L), 1)
    causal = ri >= ci

    for c in range(n_chunks):
        slab = ba[:, c * L:(c + 1) * L]
        cols = slab.T
        for h in range(N_HEADS):
            b_rep = jnp.broadcast_to(cols[:, h:h + 1], (L, LANES))
            a_rep = jnp.broadcast_to(cols[:, N_HEADS + h:N_HEADS + h + 1], (L, LANES))
            a_row = slab[N_HEADS + h:N_HEADS + h + 1, :]
            dm = jnp.where(causal, b_rep + a_row, -jnp.inf)
            m_loc = jnp.broadcast_to(jnp.max(dm, axis=1, keepdims=True), (L, LANES))
            e_s[c, h] = jnp.exp(dm - m_loc)
            colb_s[c, h] = b_rep
            colm_s[c, h] = m_loc
            cole_s[c, h] = jnp.exp(a_rep + b_rep[L - 1:L, :] - m_loc[L - 1:L, :])

    q_s[...] = _dot(xh, wqkvo_ref[:, 0:d]).astype(BF16)
    k_s[...] = (_dot(xh, wqkvo_ref[:, d:2 * d]) * (dh ** -0.5)).astype(BF16)
    v = _dot(xh, wqkvo_ref[:, 2 * d:3 * d]).astype(BF16)
    for h in range(N_HEADS):
        vx_s[h, :, 0:dh] = v[:, h * dh:(h + 1) * dh]
        vx_s[h, :, dh:] = jnp.ones((tm, LANES), BF16)
    og_s[...] = mhg_ref[...] * _sigmoid(_dot(xh, wqkvo_ref[:, 3 * d:4 * d]))

    def chunk_body(c, carry):
        rows = pl.ds(pl.multiple_of(c * L, L), L)
        heads = [slice(h * dh, (h + 1) * dh) for h in range(N_HEADS)]
        qk = [_dot_nt(q_s[rows, hc], k_s[rows, hc]) for hc in heads]
        states = [cst_s[h] for h in range(N_HEADS)]
        qc = [_dot(q_s[rows, hc], states[h].astype(BF16)) for h, hc in enumerate(heads)]
        ux = []
        for h, hc in enumerate(heads):
            kw = k_s[rows, hc].astype(F32) * _rep(cole_s[c, h], dh // LANES)
            ux.append(_dot_tn(kw.astype(BF16), vx_s[h, rows, :]))
        svx = []
        for h in range(N_HEADS):
            s_loc = qk[h] * e_s[c, h]
            svx.append(_dot(s_loc.astype(BF16), vx_s[h, rows, :]))
        for h, hc in enumerate(heads):
            m_prev = mst_s[h]
            m_loc = colm_s[c, h]
            inter = colb_s[c, h] + m_prev
            m_t = jnp.maximum(inter, m_loc)
            w_inter = _rep(jnp.exp(inter - m_t), wide)
            w_loc = _rep(jnp.exp(m_loc - m_t), wide)
            nd = w_inter * qc[h] + w_loc * svx[h]
            inv = 1.0 / jnp.maximum(jnp.abs(nd[:, dh:]), jnp.exp(-m_t))
            hh = nd[:, 0:dh] * _rep(inv, dh // LANES)
            hm_ref[rows, hc] = (_head_norm(hh) * og_s[rows, hc]).astype(BF16)
            m_new = m_t[L - 1:L, :]
            decay = _rep(jnp.exp(inter[L - 1:L, :] - m_new), wide)
            grow = _rep(jnp.exp(m_loc[L - 1:L, :] - m_new), wide)
            cst_s[h] = decay * states[h] + grow * ux[h]
            mst_s[h] = m_new
        return carry

    lax.fori_loop(0, n_chunks, chunk_body, 0)

    @pl.when(pl.program_id(1) == pl.num_programs(1) - 1)
    def _():
        for h in range(N_HEADS):
            state = cst_s[h]
            c_ref[0, h] = state[:, 0:dh]
            n_ref[0, h] = state[:, dh:].T[0:1, :]
            m_ref[0, h] = mst_s[h][:, 0:1]


def _mlstm_prompt(x, wqkvo, wgt, bgc, mhg):
    bsz, t, d = x.shape
    dh = d // N_HEADS
    tm = TOKEN_TILE
    nc = tm // CHUNK
    dx = dh + LANES
    hm, c, n, m = pl.pallas_call(
        _mlstm_prompt_kernel,
        grid=(bsz, t // tm),
        in_specs=[pl.BlockSpec((1, tm, d), lambda b, j: (b, j, 0)),
                  _qkvo_spec(d), _const_spec(wgt.shape), _const_spec(bgc.shape),
                  _const_spec(mhg.shape)],
        out_specs=[pl.BlockSpec((tm, d), lambda b, j: (b * (t // tm) + j, 0)),
                   pl.BlockSpec((1, N_HEADS, dh, dh), lambda b, j: (b, 0, 0, 0)),
                   pl.BlockSpec((1, N_HEADS, 1, dh), lambda b, j: (b, 0, 0, 0)),
                   pl.BlockSpec((1, N_HEADS, 1, 1), lambda b, j: (b, 0, 0, 0))],
        out_shape=[jax.ShapeDtypeStruct((bsz * t, d), BF16),
                   jax.ShapeDtypeStruct((bsz, N_HEADS, dh, dh), F32),
                   jax.ShapeDtypeStruct((bsz, N_HEADS, 1, dh), F32),
                   jax.ShapeDtypeStruct((bsz, N_HEADS, 1, 1), F32)],
        scratch_shapes=[
            pltpu.VMEM((tm, d), BF16),
            pltpu.VMEM((tm, d), BF16),
            pltpu.VMEM((N_HEADS, tm, dx), BF16),
            pltpu.VMEM((tm, d), F32),
            pltpu.VMEM((nc, N_HEADS, CHUNK, CHUNK), F32),
            pltpu.VMEM((nc, N_HEADS, CHUNK, LANES), F32),
            pltpu.VMEM((nc, N_HEADS, CHUNK, LANES), F32),
            pltpu.VMEM((nc, N_HEADS, CHUNK, LANES), F32),
            pltpu.VMEM((N_HEADS, dh, dx), F32),
            pltpu.VMEM((N_HEADS, 1, LANES), F32)],
        compiler_params=_params(2),
        name="mlstm_prompt",
    )(x, wqkvo, wgt, bgc, mhg)
    return hm, c, n.reshape(bsz, N_HEADS, dh), m.reshape(bsz, N_HEADS)


def _mlstm_sample_kernel(x_ref, wqkvo_ref, wgc_ref, bgr_ref, mhg_ref, c0_ref, n0_ref, m0_ref,
                         hm_ref, c_ref, n_ref, m_ref, q_s, k_s, v_s, og_s, g_s, *, n_steps):
    rows, d = q_s.shape
    dh = d // N_HEADS
    nb = rows // n_steps
    bb = SAMPLE_BATCH_BLOCK
    T = n_steps
    i = pl.program_id(0)
    h = pl.program_id(1)

    @pl.when(jnp.logical_and(i == 0, h == 0))
    def _():
        x = x_ref[...]
        xh, xl = _split_bf16(x)
        _project_qkvo(xh, wqkvo_ref, mhg_ref, q_s, k_s, v_s, og_s, d, dh)
        g = _dot(xh, wgc_ref[0]) + _dot(xl, wgc_ref[0]) + _dot(xh, wgc_ref[1]) + bgr_ref[...]
        is_input_gate = lax.broadcasted_iota(jnp.int32, g.shape, 1) < N_HEADS
        g_s[...] = jnp.where(is_input_gate, g, _log_sigmoid(g))

    r0 = pl.multiple_of(i * bb, bb)
    trow = [pl.ds(pl.multiple_of(t * nb + r0, bb), bb) for t in range(T)]
    hc = pl.ds(pl.multiple_of(h * dh, dh), dh)
    owner = lax.broadcasted_iota(jnp.int32, (T * bb, dh), 0) % bb
    gate_lane = lax.broadcasted_iota(jnp.int32, (bb, g_s.shape[1]), 1)

    def gate_column(t, col):
        return jnp.sum(jnp.where(gate_lane == col, g_s[trow[t], :], 0.0), axis=1, keepdims=True)

    qf = [q_s[trow[t], hc] for t in range(T)]
    kf = [k_s[trow[t], hc] for t in range(T)]
    vf = [v_s[trow[t], hc] for t in range(T)]
    li = [gate_column(t, h) for t in range(T)]
    lf = [gate_column(t, N_HEADS + h) for t in range(T)]
    m0 = m0_ref[0]
    n0 = n0_ref[...]
    b = [lf[0]]
    for t in range(1, T):
        b.append(b[t - 1] + lf[t])
    a = [li[t] - b[t] for t in range(T)]
    m_t, w_inter, s_w = [], [], []
    for t in range(T):
        inter = b[t] + m0
        mt = inter
        for s in range(t + 1):
            mt = jnp.maximum(mt, b[t] + a[s])
        m_t.append(mt)
        w_inter.append(jnp.exp(inter - mt))
        s_w.append([jnp.sum(qf[t] * kf[s], axis=1, keepdims=True) * jnp.exp(b[t] + a[s] - mt)
                    for s in range(t + 1)])
    m_new = m_t[T - 1]
    b_last = b[T - 1]
    decay = jnp.exp(b_last + m0 - m_new)
    kw = [kf[s] * jnp.exp(a[s] + b_last - m_new) for s in range(T)]
    n_new = decay * n0
    for s in range(T):
        n_new = n_new + kw[s]
    n_ref[...] = n_new
    m_ref[0] = m_new

    q_blk = jnp.concatenate(qf, axis=0).astype(BF16)
    kw_blk = jnp.concatenate(kw, axis=0).astype(BF16)
    v_blk = jnp.concatenate(vf, axis=0)
    qc = jnp.zeros((T * bb, dh), F32)
    for bi in range(bb):
        mine = owner == bi
        c_old = c0_ref[bi, 0]
        qc = jnp.where(mine, _dot(q_blk, c_old.astype(BF16)), qc)
        upd = _dot_tn(kw_blk, jnp.where(mine, v_blk, 0.0).astype(BF16))
        c_ref[bi, 0] = decay[bi:bi + 1, :] * c_old + upd

    for t in range(T):
        num = w_inter[t] * qc[t * bb:(t + 1) * bb]
        den = w_inter[t] * jnp.sum(qf[t] * n0, axis=1, keepdims=True)
        for s in range(t + 1):
            num = num + s_w[t][s] * vf[s]
            den = den + s_w[t][s]
        hh = num / jnp.maximum(jnp.abs(den), jnp.exp(-m_t[t]))
        hm_ref[trow[t], hc] = _head_norm(hh) * og_s[trow[t], hc]


def _mlstm_sample(x_tm, wqkvo, wgc, bgr, mhg, c0, n0, m0, n_steps):
    rows, d = x_tm.shape
    nb = rows // n_steps
    dh = d // N_HEADS
    bb = SAMPLE_BATCH_BLOCK
    cblock = pl.BlockSpec((bb, 1, dh, dh), lambda i, h: (i, h, 0, 0))
    nblock = pl.BlockSpec((bb, dh), lambda i, h: (i, h))
    mblock = pl.BlockSpec((1, bb, 1), lambda i, h: (h, i, 0))
    m0_hm = m0.T.reshape(N_HEADS, nb, 1)
    hm, c, n, m_hm = pl.pallas_call(
        functools.partial(_mlstm_sample_kernel, n_steps=n_steps),
        grid=(nb // bb, N_HEADS),
        in_specs=[_const_spec(x_tm.shape), _qkvo_spec(d), _const_spec(wgc.shape),
                  _const_spec(bgr.shape), _const_spec(mhg.shape), cblock, nblock, mblock],
        out_specs=[pl.BlockSpec((rows, d), lambda i, h: (0, 0)), cblock, nblock, mblock],
        out_shape=[jax.ShapeDtypeStruct((rows, d), F32),
                   jax.ShapeDtypeStruct(c0.shape, F32),
                   jax.ShapeDtypeStruct((nb, d), F32),
                   jax.ShapeDtypeStruct((N_HEADS, nb, 1), F32)],
        scratch_shapes=[pltpu.VMEM((rows, d), F32), pltpu.VMEM((rows, d), F32),
                        pltpu.VMEM((rows, d), F32), pltpu.VMEM((rows, d), F32),
                        pltpu.VMEM((rows, GATE_PAD), F32)],
        compiler_params=_params(2),
        name="mlstm_sample",
    )(x_tm, wqkvo, wgc, bgr, mhg, c0, n0, m0_hm)
    return hm, c, n, m_hm.reshape(N_HEADS, nb).T


def _merge_kernel(x_ref, m1_ref, hm_ref, wmo_ref, wgm_ref, wo_ref, g_ref, b_ref, o_ref, *, alpha):
    x = x_ref[...]
    xb = x.astype(BF16)
    ym = _dot(hm_ref[...].astype(BF16), wmo_ref[...])
    merged = m1_ref[...] + _sigmoid(_dot(xb, wgm_ref[...])) * ym
    r = alpha * x + _dot(merged.astype(BF16), wo_ref[...])
    o_ref[...] = _layer_norm(r, g_ref[...], b_ref[...])


def _merge(x, m1, hm, wmo, wgm, wo, g, b, alpha):
    n, d = x.shape
    tm = min(TOKEN_TILE, n)
    tile = pl.BlockSpec((tm, d), lambda i: (i, 0))
    return pl.pallas_call(
        functools.partial(_merge_kernel, alpha=alpha),
        grid=(n // tm,),
        in_specs=[tile, tile, tile, _const_spec(wmo.shape), _gate_spec(d, 1),
                  _const_spec(wo.shape), _const_spec(g.shape), _const_spec(b.shape)],
        out_specs=tile,
        out_shape=jax.ShapeDtypeStruct((n, d), F32),
        compiler_params=_params(1),
        name="merge_ln1",
    )(x, m1, hm, wmo, wgm, wo, g, b)


def _ffn_kernel(x_ref, w1_ref, w2_ref, g_ref, b_ref, o_ref, hid_s, *, alpha):
    x = x_ref[...]
    xb = x.astype(BF16)
    dff = hid_s.shape[1]
    sw = 1024 if dff % 1024 == 0 else dff
    for s in range(dff // sw):
        cs = slice(s * sw, (s + 1) * sw)
        hid = jnp.maximum(_dot(xb, w1_ref[:, cs]), 0.0)
        hid_s[:, cs] = (hid * hid).astype(BF16)
    r = alpha * x + _dot(hid_s[...], w2_ref[...])
    o_ref[...] = _layer_norm(r, g_ref[...], b_ref[...])


def _ffn(x, w1, w2, g, b, alpha):
    n, d = x.shape
    tm = min(TOKEN_TILE, n)
    tile = pl.BlockSpec((tm, d), lambda i: (i, 0))
    return pl.pallas_call(
        functools.partial(_ffn_kernel, alpha=alpha),
        grid=(n // tm,),
        in_specs=[tile, _const_spec(w1.shape), _const_spec(w2.shape), _const_spec(g.shape),
                  _const_spec(b.shape)],
        out_specs=tile,
        out_shape=jax.ShapeDtypeStruct((n, d), F32),
        scratch_shapes=[pltpu.VMEM((tm, w1.shape[1]), BF16)],
        compiler_params=_params(1),
        name="ffn_ln2",
    )(x, w1, w2, g, b)


def _w_in_prep_kernel(a_ref, g_ref, b_ref, wa_ref, wg_ref, wb_ref, *, n_a):
    j = pl.program_id(0)

    @pl.when(j == 0)
    def _():
        wg_ref[...] = g_ref[...]

    @pl.when(j < n_a)
    def _():
        wa_ref[...] = a_ref[...].T.astype(BF16)

    @pl.when(j >= n_a)
    def _():
        wb_ref[...] = b_ref[...].T.astype(BF16)


def _prepare_w_in(w_in_t):
    d = w_in_t.shape[1]
    n_a, n_b = 7, 2
    h2 = 2 * N_HEADS
    off_b = n_a * d + h2
    return pl.pallas_call(
        functools.partial(_w_in_prep_kernel, n_a=n_a),
        grid=(n_a + n_b,),
        in_specs=[pl.BlockSpec((d, d), lambda j: (jnp.minimum(j, n_a - 1), 0)),
                  pl.BlockSpec((pl.Element(h2), pl.Element(d)), lambda j: (n_a * d, 0)),
                  pl.BlockSpec((pl.Element(d), pl.Element(d)),
                               lambda j: (pl.multiple_of(off_b + jnp.maximum(j - n_a, 0) * d, 8),
                                          0))],
        out_specs=[pl.BlockSpec((d, d), lambda j: (0, jnp.minimum(j, n_a - 1))),
                   pl.BlockSpec((h2, d), lambda j: (0, 0)),
                   pl.BlockSpec((d, d), lambda j: (0, jnp.maximum(j - n_a, 0)))],
        out_shape=[jax.ShapeDtypeStruct((d, n_a * d), BF16),
                   jax.ShapeDtypeStruct((h2, d), F32),
                   jax.ShapeDtypeStruct((d, n_b * d), BF16)],
        compiler_params=_params(1),
        name="w_in_prep",
    )(w_in_t, w_in_t, w_in_t)


def _layer_weights(w_in, b_gate, conv_w, w_conv_out, mh_g, w_m_out, w_o, ln1_g, ln1_b,
                   w_ff1, w_ff2, ln2_g, ln2_b):
    d = w_in.shape[0]
    h2 = 2 * N_HEADS
    w_in_t = jnp.swapaxes(w_in, 0, 1)
    w_a, wg_t, w_b = _prepare_w_in(w_in_t)
    wgt_hi, wgt_lo = _split_bf16(wg_t)
    wgt = jnp.stack([jnp.concatenate([wgt_hi, wgt_lo], axis=0),
                     jnp.concatenate([wgt_hi, jnp.zeros_like(wgt_hi)], axis=0)])
    pad = ((0, 0), (0, GATE_PAD - h2))
    wgc = jnp.stack([jnp.pad(wgt_hi.T, pad), jnp.pad(wgt_lo.T, pad)])
    return dict(
        w_a=w_a, w_b=w_b,
        wgt=wgt, wgc=wgc,
        bgc=b_gate.reshape(h2, 1).astype(F32),
        bgr=jnp.pad(b_gate.reshape(1, h2).astype(F32), pad),
        cw=conv_w.astype(F32),
        wco=w_conv_out.astype(BF16), mhg=mh_g.reshape(1, d).astype(F32),
        wmo=w_m_out.astype(BF16), wo=w_o.astype(BF16),
        ln1_g=ln1_g.reshape(1, d), ln1_b=ln1_b.reshape(1, d),
        w1=w_ff1.astype(BF16), w2=w_ff2.astype(BF16),
        ln2_g=ln2_g.reshape(1, d), ln2_b=ln2_b.reshape(1, d))


def _prompt_layer(x, p, alpha):
    bsz, t, d = x.shape
    m1, conv_buf = _conv_branch_prompt(x, p["w_a"], p["cw"], p["wco"], p["w_b"])
    hm, c, n, m = _mlstm_prompt(x, p["w_a"], p["wgt"], p["bgc"], p["mhg"])
    xf = x.reshape(bsz * t, d)
    x1 = _merge(xf, m1.reshape(bsz * t, d), hm, p["wmo"], p["w_b"], p["wo"], p["ln1_g"],
                p["ln1_b"], alpha)
    x2 = _ffn(x1, p["w1"], p["w2"], p["ln2_g"], p["ln2_b"], alpha)
    return x2.reshape(bsz, t, d), conv_buf, c, n, m


def _sample_layer(x_tm, conv_tm, c0, n0, m0, p, alpha, n_steps):
    d = x_tm.shape[1]
    m1, conv_new = _conv_branch_sample(x_tm, conv_tm, p["w_a"], p["cw"], p["wco"], p["w_b"],
                                       n_steps)
    hm, c, n, m = _mlstm_sample(x_tm, p["w_a"], p["wgc"], p["bgr"], p["mhg"], c0,
                                n0.reshape(n0.shape[0], d), m0, n_steps)
    x1 = _merge(x_tm, m1, hm, p["wmo"], p["w_b"], p["wo"], p["ln1_g"], p["ln1_b"], alpha)
    x2 = _ffn(x1, p["w1"], p["w2"], p["ln2_g"], p["ln2_b"], alpha)
    return x2, conv_new, c, n.reshape(n0.shape), m


def kernel(x_prompt, x_sample, state_conv, state_C, state_n, state_m, w_in, b_gate, conv_w,
           w_conv_out, mh_g, w_m_out, w_o, ln1_g, ln1_b, w_ff1, w_ff2, ln2_g, ln2_b):
    depth = w_in.shape[0]
    alpha = (2.0 * depth) ** 0.25
    bsz, t, d = x_prompt.shape
    sb, st, _ = x_sample.shape
    assert t % TOKEN_TILE == 0 and TOKEN_TILE % CHUNK == 0 and d % N_HEADS == 0
    assert sb % SAMPLE_BATCH_BLOCK == 0 and st >= CONV_W - 1

    xp = x_prompt
    xs = jnp.transpose(x_sample, (1, 0, 2)).reshape(st * sb, d)
    outs = [[] for _ in range(8)]
    for l in range(depth):
        p = _layer_weights(w_in[l], b_gate[l], conv_w[l], w_conv_out[l], mh_g[l], w_m_out[l],
                           w_o[l], ln1_g[l], ln1_b[l], w_ff1[l], w_ff2[l], ln2_g[l], ln2_b[l])
        xp, cp, c_p, n_p, m_p = _prompt_layer(xp, p, alpha)
        conv_tm = jnp.transpose(state_conv[l], (1, 0, 2)).reshape((CONV_W - 1) * sb, d)
        xs, cs_tm, c_s, n_s, m_s = _sample_layer(xs, conv_tm, state_C[l], state_n[l], state_m[l],
                                                 p, alpha, st)
        cs = jnp.transpose(cs_tm.reshape(CONV_W - 1, sb, d), (1, 0, 2))
        for acc, val in zip(outs, (cp, cs, c_p, c_s, n_p, n_s, m_p, m_s)):
            acc.append(val)
    ys = jnp.transpose(xs.reshape(st, sb, d), (1, 0, 2))
    return (xp, ys) + tuple(jnp.stack(acc) for acc in outs)
```

```python
import functools

import jax
import jax.numpy as jnp
from jax import lax
from jax.experimental import pallas as pl
from jax.experimental.pallas import tpu as pltpu

F32 = jnp.float32
BF16 = jnp.bfloat16

LN_EPS = 1e-5
N_HEADS = 4
CHUNK = 128
CONV_W = 3
TOKEN_TILE = 1024
SUB_TILE = 512
MLSTM_TILE = 512
SAMPLE_BATCH_BLOCK = 8
LANES = 128
GATE_PAD = LANES
VMEM_LIMIT = 56 * 1024 * 1024


def _dot(a, b):
    return jnp.dot(a, b, preferred_element_type=F32)


def _dot_nt(a, b):
    return lax.dot_general(a, b, (((1,), (1,)), ((), ())), preferred_element_type=F32)


def _dot_tn(a, b):
    return lax.dot_general(a, b, (((0,), (0,)), ((), ())), preferred_element_type=F32)


def _sigmoid(x):
    return 1.0 / (1.0 + jnp.exp(-x))


def _log_sigmoid(x):
    return jnp.minimum(x, 0.0) - jnp.log1p(jnp.exp(-jnp.abs(x)))


def _split_bf16(x):
    hi = x.astype(BF16)
    lo = (x - hi.astype(F32)).astype(BF16)
    return hi, lo


def _layer_norm(r, g, b):
    mu = jnp.mean(r, axis=-1, keepdims=True)
    xc = r - mu
    var = jnp.mean(xc * xc, axis=-1, keepdims=True)
    return xc * lax.rsqrt(var + LN_EPS) * g + b


def _head_norm(h):
    mu = jnp.mean(h, axis=-1, keepdims=True)
    hc = h - mu
    return hc * lax.rsqrt(jnp.mean(hc * hc, axis=-1, keepdims=True) + LN_EPS)


def _const_spec(shape):
    zeros = (0,) * len(shape)
    return pl.BlockSpec(shape, lambda *_: zeros, pipeline_mode=pl.Buffered(1))


def _window_spec(block_shape, index):
    return pl.BlockSpec(block_shape, lambda *_: index, pipeline_mode=pl.Buffered(1))


def _bch_spec(d):
    return _window_spec((d, 3 * d), (0, 0))


def _qkvo_spec(d):
    return _window_spec((pl.Element(d), pl.Element(4 * d)), (0, 3 * d))


def _gate_spec(d, which):
    return _window_spec((d, d), (0, which))


def _params(n_axes):
    return pltpu.CompilerParams(dimension_semantics=("arbitrary",) * n_axes,
                                vmem_limit_bytes=VMEM_LIMIT)


def _conv_stripes(d):
    sw = 512 if d % 512 == 0 else d
    return [slice(s * sw, (s + 1) * sw) for s in range(d // sw)]


def _conv_prompt_kernel(x_ref, wbch_ref, cw_ref, wco_ref, wgc_ref, m1_ref, cs_ref, u_s, a_s):
    tm, d = a_s.shape
    @pl.when(pl.program_id(1) == 0)
    def _():
        u_s[0:8, :] = jnp.zeros((8, d), F32)

    for r0 in range(0, tm, SUB_TILE):
        sub = min(SUB_TILE, tm - r0)
        xb = x_ref[0, r0:r0 + sub, :].astype(BF16)
        for cs in _conv_stripes(d):
            off = cs.start
            bg = _dot(xb, wbch_ref[:, off:cs.stop])
            cg = _dot(xb, wbch_ref[:, d + off:d + cs.stop])
            hc = _dot(xb, wbch_ref[:, 2 * d + off:2 * d + cs.stop])
            u = cg * hc
            u_s[8 + r0:8 + r0 + sub, cs] = u
            conv = (u_s[6 + r0:6 + r0 + sub, cs] * cw_ref[0:1, cs]
                    + u_s[7 + r0:7 + r0 + sub, cs] * cw_ref[1:2, cs] + u * cw_ref[2:3, cs])
            a_s[r0:r0 + sub, cs] = (bg * conv).astype(BF16)
        gate = _dot(xb, wgc_ref[...])
        yc = _dot(a_s[r0:r0 + sub, :], wco_ref[...])
        m1_ref[0, r0:r0 + sub, :] = _sigmoid(gate) * yc
    cs_ref[0] = u_s[tm + 6:tm + 8, :]
    u_s[0:8, :] = u_s[tm:tm + 8, :]


def _conv_sample_kernel(x_ref, st_ref, wbch_ref, cw_ref, wco_ref, wgc_ref, m1_ref, cs_ref, a_s,
                        *, n_steps):
    rows, d = a_s.shape
    nb = rows // n_steps
    xb = x_ref[...].astype(BF16)
    for cs in _conv_stripes(d):
        off = cs.start
        bg = _dot(xb, wbch_ref[:, off:cs.stop])
        cg = _dot(xb, wbch_ref[:, d + off:d + cs.stop])
        hc = _dot(xb, wbch_ref[:, 2 * d + off:2 * d + cs.stop])
        u = cg * hc
        up = [st_ref[j * nb:(j + 1) * nb, cs] for j in range(CONV_W - 1)]
        up += [u[t * nb:(t + 1) * nb] for t in range(n_steps)]
        for t in range(n_steps):
            conv = (up[t] * cw_ref[0:1, cs] + up[t + 1] * cw_ref[1:2, cs]
                    + up[t + 2] * cw_ref[2:3, cs])
            a_s[t * nb:(t + 1) * nb, cs] = (bg[t * nb:(t + 1) * nb] * conv).astype(BF16)
        for j in range(CONV_W - 1):
            cs_ref[j * nb:(j + 1) * nb, cs] = up[n_steps + j]
    yc = _dot(a_s[...], wco_ref[...])
    m1_ref[...] = _sigmoid(_dot(xb, wgc_ref[...])) * yc


def _conv_branch_prompt(x, wbch, cw, wco, wgc):
    bsz, t, d = x.shape
    tm = TOKEN_TILE
    return pl.pallas_call(
        _conv_prompt_kernel,
        grid=(bsz, t // tm),
        in_specs=[pl.BlockSpec((1, tm, d), lambda b, j: (b, j, 0)),
                  _bch_spec(d), _const_spec(cw.shape), _const_spec(wco.shape),
                  _gate_spec(d, 0)],
        out_specs=[pl.BlockSpec((1, tm, d), lambda b, j: (b, j, 0)),
                   pl.BlockSpec((1, CONV_W - 1, d), lambda b, j: (b, 0, 0))],
        out_shape=[jax.ShapeDtypeStruct((bsz, t, d), F32),
                   jax.ShapeDtypeStruct((bsz, CONV_W - 1, d), F32)],
        scratch_shapes=[pltpu.VMEM((tm + 8, d), F32), pltpu.VMEM((tm, d), BF16)],
        compiler_params=_params(2),
        name="conv_branch_prompt",
    )(x, wbch, cw, wco, wgc)


def _conv_branch_sample(x_tm, st_tm, wbch, cw, wco, wgc, n_steps):
    rows, d = x_tm.shape
    return pl.pallas_call(
        functools.partial(_conv_sample_kernel, n_steps=n_steps),
        grid=(1,),
        in_specs=[_const_spec(x_tm.shape), _const_spec(st_tm.shape), _bch_spec(d),
                  _const_spec(cw.shape), _const_spec(wco.shape), _gate_spec(d, 0)],
        out_specs=[pl.BlockSpec((rows, d), lambda i: (0, 0)),
                   pl.BlockSpec(st_tm.shape, lambda i: (0, 0))],
        out_shape=[jax.ShapeDtypeStruct((rows, d), F32),
                   jax.ShapeDtypeStruct(st_tm.shape, F32)],
        scratch_shapes=[pltpu.VMEM((rows, d), BF16)],
        compiler_params=_params(1),
        name="conv_branch_sample",
    )(x_tm, st_tm, wbch, cw, wco, wgc)


def _project_qkvo(xh, wqkvo_ref, mhg_ref, q_s, k_s, v_s, og_s, d, dh):
    q_s[...] = _dot(xh, wqkvo_ref[:, 0:d]).astype(BF16).astype(q_s.dtype)
    k_s[...] = (_dot(xh, wqkvo_ref[:, d:2 * d]) * (dh ** -0.5)).astype(BF16).astype(k_s.dtype)
    v_s[...] = _dot(xh, wqkvo_ref[:, 2 * d:3 * d]).astype(BF16).astype(v_s.dtype)
    og_s[...] = mhg_ref[...] * _sigmoid(_dot(xh, wqkvo_ref[:, 3 * d:4 * d]))


def _rep(col, times):
    return col if times == 1 else jnp.concatenate([col] * times, axis=1)


def _mlstm_prompt_kernel(x_ref, wqkvo_ref, wgt_ref, bgc_ref, mhg_ref,
                         hm_ref, c_ref, n_ref, m_ref,
                         q_s, k_s, vx_s, og_s, e_s, colb_s, colm_s, cole_s, cst_s, mst_s):
    tm, d = q_s.shape
    dh = d // N_HEADS
    L = CHUNK
    LANES = colb_s.shape[-1]
    n_chunks = tm // L
    wide = (dh + LANES) // LANES

    @pl.when(pl.program_id(1) == 0)
    def _():
        cst_s[...] = jnp.zeros(cst_s.shape, F32)
        mst_s[...] = jnp.zeros(mst_s.shape, F32)

    x = x_ref[0]
    xh, xl = _split_bf16(x)

    ga = _dot_nt(wgt_ref[0], xh)
    gb = _dot_nt(wgt_ref[1], xl)
    gt = ga[0:8] + ga[8:16] + gb[0:8] + bgc_ref[...]
    is_input_gate = lax.broadcasted_iota(jnp.int32, gt.shape, 0) < N_HEADS
    g = jnp.where(is_input_gate, gt, _log_sigmoid(gt))

    pos = lax.broadcasted_iota(jnp.int32, g.shape, 1) % L
    csum = g
    shift = 1
    while shift < L:
        csum = csum + jnp.where(pos >= shift, pltpu.roll(csum, shift, 1), 0.0)
        shift *= 2
    ba = jnp.concatenate([csum[N_HEADS:], g[:N_HEADS] - csum[N_HEADS:]], axis=0)

    ri = lax.broadcasted_iota(jnp.int32, (L, L), 0)
    ci = lax.broadcasted_iota(jnp.int32, (L, L), 1)
    causal = ri >= ci

    for c in range(n_chunks):
        slab = ba[:, c * L:(c + 1) * L]
        cols = slab.T
        for h in range(N_HEADS):
            b_rep = jnp.broadcast_to(cols[:, h:h + 1], (L, LANES))
            a_rep = jnp.broadcast_to(cols[:, N_HEADS + h:N_HEADS + h + 1], (L, LANES))
            a_row = slab[N_HEADS + h:N_HEADS + h + 1, :]
            dm = jnp.where(causal, b_rep + a_row, -jnp.inf)
            m_loc = jnp.broadcast_to(jnp.max(dm, axis=1, keepdims=True), (L, LANES))
            e_s[c, h] = jnp.exp(dm - m_loc)
            colb_s[c, h] = b_rep
            colm_s[c, h] = m_loc
            cole_s[c, h] = jnp.exp(a_rep + b_rep[L - 1:L, :] - m_loc[L - 1:L, :])

    q_s[...] = _dot(xh, wqkvo_ref[:, 0:d]).astype(BF16)
    k_s[...] = (_dot(xh, wqkvo_ref[:, d:2 * d]) * (dh ** -0.5)).astype(BF16)
    v = _dot(xh, wqkvo_ref[:, 2 * d:3 * d]).astype(BF16)
    for h in range(N_HEADS):
        vx_s[h, :, 0:dh] = v[:, h * dh:(h + 1) * dh]
        vx_s[h, :, dh:] = jnp.ones((tm, LANES), BF16)
    og_s[...] = mhg_ref[...] * _sigmoid(_dot(xh, wqkvo_ref[:, 3 * d:4 * d]))

    def chunk_body(c, carry):
        rows = pl.ds(pl.multiple_of(c * L, L), L)
        heads = [slice(h * dh, (h + 1) * dh) for h in range(N_HEADS)]
        qk = [_dot_nt(q_s[rows, hc], k_s[rows, hc]) for hc in heads]
        states = [cst_s[h] for h in range(N_HEADS)]
        qc = [_dot(q_s[rows, hc], states[h].astype(BF16)) for h, hc in enumerate(heads)]
        ux = []
        for h, hc in enumerate(heads):
            kw = k_s[rows, hc].astype(F32) * _rep(cole_s[c, h], dh // LANES)
            ux.append(_dot_tn(kw.astype(BF16), vx_s[h, rows, :]))
        svx = []
        for h in range(N_HEADS):
            s_loc = qk[h] * e_s[c, h]
            svx.append(_dot(s_loc.astype(BF16), vx_s[h, rows, :]))
        for h, hc in enumerate(heads):
            m_prev = mst_s[h]
            m_loc = colm_s[c, h]
            inter = colb_s[c, h] + m_prev
            m_t = jnp.maximum(inter, m_loc)
            w_inter = _rep(jnp.exp(inter - m_t), wide)
            w_loc = _rep(jnp.exp(m_loc - m_t), wide)
            nd = w_inter * qc[h] + w_loc * svx[h]
            inv = 1.0 / jnp.maximum(jnp.abs(nd[:, dh:]), jnp.exp(-m_t))
            hh = nd[:, 0:dh] * _rep(inv, dh // LANES)
            hm_ref[rows, hc] = (_head_norm(hh) * og_s[rows, hc]).astype(BF16)
            m_new = m_t[L - 1:L, :]
            decay = _rep(jnp.exp(inter[L - 1:L, :] - m_new), wide)
            grow = _rep(jnp.exp(m_loc[L - 1:L, :] - m_new), wide)
            cst_s[h] = decay * states[h] + grow * ux[h]
            mst_s[h] = m_new
        return carry

    lax.fori_loop(0, n_chunks, chunk_body, 0)

    @pl.when(pl.program_id(1) == pl.num_programs(1) - 1)
    def _():
        for h in range(N_HEADS):
            state = cst_s[h]
            c_ref[0, h] = state[:, 0:dh]
            n_ref[0, h] = state[:, dh:].T[0:1, :]
            m_ref[0, h] = mst_s[h][:, 0:1]


def _mlstm_prompt(x, wqkvo, wgt, bgc, mhg):
    bsz, t, d = x.shape
    dh = d // N_HEADS
    tm = MLSTM_TILE
    nc = tm // CHUNK
    dx = dh + LANES
    hm, c, n, m = pl.pallas_call(
        _mlstm_prompt_kernel,
        grid=(bsz, t // tm),
        in_specs=[pl.BlockSpec((1, tm, d), lambda b, j: (b, j, 0)),
                  _qkvo_spec(d), _const_spec(wgt.shape), _const_spec(bgc.shape),
                  _const_spec(mhg.shape)],
        out_specs=[pl.BlockSpec((tm, d), lambda b, j: (b * (t // tm) + j, 0)),
                   pl.BlockSpec((1, N_HEADS, dh, dh), lambda b, j: (b, 0, 0, 0)),
                   pl.BlockSpec((1, N_HEADS, 1, dh), lambda b, j: (b, 0, 0, 0)),
                   pl.BlockSpec((1, N_HEADS, 1, 1), lambda b, j: (b, 0, 0, 0))],
        out_shape=[jax.ShapeDtypeStruct((bsz * t, d), BF16),
                   jax.ShapeDtypeStruct((bsz, N_HEADS, dh, dh), F32),
                   jax.ShapeDtypeStruct((bsz, N_HEADS, 1, dh), F32),
                   jax.ShapeDtypeStruct((bsz, N_HEADS, 1, 1), F32)],
        scratch_shapes=[
            pltpu.VMEM((tm, d), BF16),
            pltpu.VMEM((tm, d), BF16),
            pltpu.VMEM((N_HEADS, tm, dx), BF16),
            pltpu.VMEM((tm, d), F32),
            pltpu.VMEM((nc, N_HEADS, CHUNK, CHUNK), F32),
            pltpu.VMEM((nc, N_HEADS, CHUNK, LANES), F32),
            pltpu.VMEM((nc, N_HEADS, CHUNK, LANES), F32),
            pltpu.VMEM((nc, N_HEADS, CHUNK, LANES), F32),
            pltpu.VMEM((N_HEADS, dh, dx), F32),
            pltpu.VMEM((N_HEADS, 1, LANES), F32)],
        compiler_params=_params(2),
        name="mlstm_prompt",
    )(x, wqkvo, wgt, bgc, mhg)
    return hm, c, n.reshape(bsz, N_HEADS, dh), m.reshape(bsz, N_HEADS)


def _mlstm_sample_kernel(x_ref, wqkvo_ref, wgc_ref, bgr_ref, mhg_ref, c0_ref, n0_ref, m0_ref,
                         hm_ref, c_ref, n_ref, m_ref, q_s, k_s, v_s, og_s, g_s, *, n_steps):
    rows, d = q_s.shape
    dh = d // N_HEADS
    nb = rows // n_steps
    bb = SAMPLE_BATCH_BLOCK
    T = n_steps
    i = pl.program_id(0)
    h = pl.program_id(1)

    @pl.when(jnp.logical_and(i == 0, h == 0))
    def _():
        x = x_ref[...]
        xh, xl = _split_bf16(x)
        _project_qkvo(xh, wqkvo_ref, mhg_ref, q_s, k_s, v_s, og_s, d, dh)
        g = _dot(xh, wgc_ref[0]) + _dot(xl, wgc_ref[0]) + _dot(xh, wgc_ref[1]) + bgr_ref[...]
        is_input_gate = lax.broadcasted_iota(jnp.int32, g.shape, 1) < N_HEADS
        g_s[...] = jnp.where(is_input_gate, g, _log_sigmoid(g))

    r0 = pl.multiple_of(i * bb, bb)
    trow = [pl.ds(pl.multiple_of(t * nb + r0, bb), bb) for t in range(T)]
    hc = pl.ds(pl.multiple_of(h * dh, dh), dh)
    owner = lax.broadcasted_iota(jnp.int32, (T * bb, dh), 0) % bb
    gate_lane = lax.broadcasted_iota(jnp.int32, (bb, g_s.shape[1]), 1)

    def gate_column(t, col):
        return jnp.sum(jnp.where(gate_lane == col, g_s[trow[t], :], 0.0), axis=1, keepdims=True)

    qf = [q_s[trow[t], hc] for t in range(T)]
    kf = [k_s[trow[t], hc] for t in range(T)]
    vf = [v_s[trow[t], hc] for t in range(T)]
    li = [gate_column(t, h) for t in range(T)]
    lf = [gate_column(t, N_HEADS + h) for t in range(T)]
    m0 = m0_ref[0]
    n0 = n0_ref[...]
    b = [lf[0]]
    for t in range(1, T):
        b.append(b[t - 1] + lf[t])
    a = [li[t] - b[t] for t in range(T)]
    m_t, w_inter, s_w = [], [], []
    for t in range(T):
        inter = b[t] + m0
        mt = inter
        for s in range(t + 1):
            mt = jnp.maximum(mt, b[t] + a[s])
        m_t.append(mt)
        w_inter.append(jnp.exp(inter - mt))
        s_w.append([jnp.sum(qf[t] * kf[s], axis=1, keepdims=True) * jnp.exp(b[t] + a[s] - mt)
                    for s in range(t + 1)])
    m_new = m_t[T - 1]
    b_last = b[T - 1]
    decay = jnp.exp(b_last + m0 - m_new)
    kw = [kf[s] * jnp.exp(a[s] + b_last - m_new) for s in range(T)]
    n_new = decay * n0
    for s in range(T):
        n_new = n_new + kw[s]
    n_ref[...] = n_new
    m_ref[0] = m_new

    q_blk = jnp.concatenate(qf, axis=0).astype(BF16)
    kw_blk = jnp.concatenate(kw, axis=0).astype(BF16)
    v_blk = jnp.concatenate(vf, axis=0)
    qc = jnp.zeros((T * bb, dh), F32)
    for bi in range(bb):
        mine = owner == bi
        c_old = c0_ref[bi, 0]
        qc = jnp.where(mine, _dot(q_blk, c_old.astype(BF16)), qc)
        upd = _dot_tn(kw_blk, jnp.where(mine, v_blk, 0.0).astype(BF16))
        c_ref[bi, 0] = decay[bi:bi + 1, :] * c_old + upd

    for t in range(T):
        num = w_inter[t] * qc[t * bb:(t + 1) * bb]
        den = w_inter[t] * jnp.sum(qf[t] * n0, axis=1, keepdims=True)
        for s in range(t + 1):
            num = num + s_w[t][s] * vf[s]
            den = den + s_w[t][s]
        hh = num / jnp.maximum(jnp.abs(den), jnp.exp(-m_t[t]))
        hm_ref[trow[t], hc] = _head_norm(hh) * og_s[trow[t], hc]


def _mlstm_sample(x_tm, wqkvo, wgc, bgr, mhg, c0, n0, m0, n_steps):
    rows, d = x_tm.shape
    nb = rows // n_steps
    dh = d // N_HEADS
    bb = SAMPLE_BATCH_BLOCK
    cblock = pl.BlockSpec((bb, 1, dh, dh), lambda i, h: (i, h, 0, 0))
    nblock = pl.BlockSpec((bb, dh), lambda i, h: (i, h))
    mblock = pl.BlockSpec((1, bb, 1), lambda i, h: (h, i, 0))
    m0_hm = m0.T.reshape(N_HEADS, nb, 1)
    hm, c, n, m_hm = pl.pallas_call(
        functools.partial(_mlstm_sample_kernel, n_steps=n_steps),
        grid=(nb // bb, N_HEADS),
        in_specs=[_const_spec(x_tm.shape), _qkvo_spec(d), _const_spec(wgc.shape),
                  _const_spec(bgr.shape), _const_spec(mhg.shape), cblock, nblock, mblock],
        out_specs=[pl.BlockSpec((rows, d), lambda i, h: (0, 0)), cblock, nblock, mblock],
        out_shape=[jax.ShapeDtypeStruct((rows, d), F32),
                   jax.ShapeDtypeStruct(c0.shape, F32),
                   jax.ShapeDtypeStruct((nb, d), F32),
                   jax.ShapeDtypeStruct((N_HEADS, nb, 1), F32)],
        scratch_shapes=[pltpu.VMEM((rows, d), F32), pltpu.VMEM((rows, d), F32),
                        pltpu.VMEM((rows, d), F32), pltpu.VMEM((rows, d), F32),
                        pltpu.VMEM((rows, GATE_PAD), F32)],
        compiler_params=_params(2),
        name="mlstm_sample",
    )(x_tm, wqkvo, wgc, bgr, mhg, c0, n0, m0_hm)
    return hm, c, n, m_hm.reshape(N_HEADS, nb).T


def _merge_kernel(x_ref, m1_ref, hm_ref, wmo_ref, wgm_ref, wo_ref, g_ref, b_ref, o_ref, *, alpha):
    tm = x_ref.shape[0]
    for r0 in range(0, tm, SUB_TILE):
        rows = slice(r0, min(r0 + SUB_TILE, tm))
        x = x_ref[rows, :]
        ym = _dot(hm_ref[rows, :].astype(BF16), wmo_ref[...])
        gate = _dot(x.astype(BF16), wgm_ref[...])
        merged = m1_ref[rows, :] + _sigmoid(gate) * ym
        r = alpha * x + _dot(merged.astype(BF16), wo_ref[...])
        o_ref[rows, :] = _layer_norm(r, g_ref[...], b_ref[...])


def _merge(x, m1, hm, wmo, wgm, wo, g, b, alpha):
    n, d = x.shape
    tm = min(TOKEN_TILE, n)
    tile = pl.BlockSpec((tm, d), lambda i: (i, 0))
    return pl.pallas_call(
        functools.partial(_merge_kernel, alpha=alpha),
        grid=(n // tm,),
        in_specs=[tile, tile, tile, _const_spec(wmo.shape), _gate_spec(d, 1),
                  _const_spec(wo.shape), _const_spec(g.shape), _const_spec(b.shape)],
        out_specs=tile,
        out_shape=jax.ShapeDtypeStruct((n, d), F32),
        compiler_params=_params(1),
        name="merge_ln1",
    )(x, m1, hm, wmo, wgm, wo, g, b)


def _ffn_kernel(x_ref, w1_ref, w2_ref, g_ref, b_ref, o_ref, hid_s, *, alpha):
    tm, dff = hid_s.shape
    sw = 1024 if dff % 1024 == 0 else dff
    for r0 in range(0, tm, SUB_TILE):
        rows = slice(r0, min(r0 + SUB_TILE, tm))
        x = x_ref[rows, :]
        xb = x.astype(BF16)
        for s in range(dff // sw):
            cs = slice(s * sw, (s + 1) * sw)
            hid = jnp.maximum(_dot(xb, w1_ref[:, cs]), 0.0)
            hid_s[rows, cs] = (hid * hid).astype(BF16)
        r = alpha * x + _dot(hid_s[rows, :], w2_ref[...])
        o_ref[rows, :] = _layer_norm(r, g_ref[...], b_ref[...])


def _ffn(x, w1, w2, g, b, alpha):
    n, d = x.shape
    tm = min(TOKEN_TILE, n)
    tile = pl.BlockSpec((tm, d), lambda i: (i, 0))
    return pl.pallas_call(
        functools.partial(_ffn_kernel, alpha=alpha),
        grid=(n // tm,),
        in_specs=[tile, _const_spec(w1.shape), _const_spec(w2.shape), _const_spec(g.shape),
                  _const_spec(b.shape)],
        out_specs=tile,
        out_shape=jax.ShapeDtypeStruct((n, d), F32),
        scratch_shapes=[pltpu.VMEM((tm, w1.shape[1]), BF16)],
        compiler_params=_params(1),
        name="ffn_ln2",
    )(x, w1, w2, g, b)


def _w_in_prep_kernel(a_ref, g_ref, b_ref, wa_ref, wg_ref, wb_ref, *, n_a):
    j = pl.program_id(0)

    @pl.when(j == 0)
    def _():
        wg_ref[...] = g_ref[...]

    @pl.when(j < n_a)
    def _():
        wa_ref[...] = a_ref[...].T.astype(BF16)

    @pl.when(j >= n_a)
    def _():
        wb_ref[...] = b_ref[...].T.astype(BF16)


def _prepare_w_in(w_in_t):
    d = w_in_t.shape[1]
    n_a, n_b = 7, 2
    h2 = 2 * N_HEADS
    off_b = n_a * d + h2
    return pl.pallas_call(
        functools.partial(_w_in_prep_kernel, n_a=n_a),
        grid=(n_a + n_b,),
        in_specs=[pl.BlockSpec((d, d), lambda j: (jnp.minimum(j, n_a - 1), 0)),
                  pl.BlockSpec((pl.Element(h2), pl.Element(d)), lambda j: (n_a * d, 0)),
                  pl.BlockSpec((pl.Element(d), pl.Element(d)),
                               lambda j: (pl.multiple_of(off_b + jnp.maximum(j - n_a, 0) * d, 8),
                                          0))],
        out_specs=[pl.BlockSpec((d, d), lambda j: (0, jnp.minimum(j, n_a - 1))),
                   pl.BlockSpec((h2, d), lambda j: (0, 0)),
                   pl.BlockSpec((d, d), lambda j: (0, jnp.maximum(j - n_a, 0)))],
        out_shape=[jax.ShapeDtypeStruct((d, n_a * d), BF16),
                   jax.ShapeDtypeStruct((h2, d), F32),
                   jax.ShapeDtypeStruct((d, n_b * d), BF16)],
        compiler_params=_params(1),
        name="w_in_prep",
    )(w_in_t, w_in_t, w_in_t)


def _layer_weights(w_in, b_gate, conv_w, w_conv_out, mh_g, w_m_out, w_o, ln1_g, ln1_b,
                   w_ff1, w_ff2, ln2_g, ln2_b):
    d = w_in.shape[0]
    h2 = 2 * N_HEADS
    w_in_t = jnp.swapaxes(w_in, 0, 1)
    w_a, wg_t, w_b = _prepare_w_in(w_in_t)
    wgt_hi, wgt_lo = _split_bf16(wg_t)
    wgt = jnp.stack([jnp.concatenate([wgt_hi, wgt_lo], axis=0),
                     jnp.concatenate([wgt_hi, jnp.zeros_like(wgt_hi)], axis=0)])
    pad = ((0, 0), (0, GATE_PAD - h2))
    wgc = jnp.stack([jnp.pad(wgt_hi.T, pad), jnp.pad(wgt_lo.T, pad)])
    return dict(
        w_a=w_a, w_b=w_b,
        wgt=wgt, wgc=wgc,
        bgc=b_gate.reshape(h2, 1).astype(F32),
        bgr=jnp.pad(b_gate.reshape(1, h2).astype(F32), pad),
        cw=conv_w.astype(F32),
        wco=w_conv_out.astype(BF16), mhg=mh_g.reshape(1, d).astype(F32),
        wmo=w_m_out.astype(BF16), wo=w_o.astype(BF16),
        ln1_g=ln1_g.reshape(1, d), ln1_b=ln1_b.reshape(1, d),
        w1=w_ff1.astype(BF16), w2=w_ff2.astype(BF16),
        ln2_g=ln2_g.reshape(1, d), ln2_b=ln2_b.reshape(1, d))


def _prompt_layer(x, p, alpha):
    bsz, t, d = x.shape
    m1, conv_buf = _conv_branch_prompt(x, p["w_a"], p["cw"], p["wco"], p["w_b"])
    hm, c, n, m = _mlstm_prompt(x, p["w_a"], p["wgt"], p["bgc"], p["mhg"])
    xf = x.reshape(bsz * t, d)
    x1 = _merge(xf, m1.reshape(bsz * t, d), hm, p["wmo"], p["w_b"], p["wo"], p["ln1_g"],
                p["ln1_b"], alpha)
    x2 = _ffn(x1, p["w1"], p["w2"], p["ln2_g"], p["ln2_b"], alpha)
    return x2.reshape(bsz, t, d), conv_buf, c, n, m


def _sample_layer(x_tm, conv_tm, c0, n0, m0, p, alpha, n_steps):
    d = x_tm.shape[1]
    m1, conv_new = _conv_branch_sample(x_tm, conv_tm, p["w_a"], p["cw"], p["wco"], p["w_b"],
                                       n_steps)
    hm, c, n, m = _mlstm_sample(x_tm, p["w_a"], p["wgc"], p["bgr"], p["mhg"], c0,
                                n0.reshape(n0.shape[0], d), m0, n_steps)
    x1 = _merge(x_tm, m1, hm, p["wmo"], p["w_b"], p["wo"], p["ln1_g"], p["ln1_b"], alpha)
    x2 = _ffn(x1, p["w1"], p["w2"], p["ln2_g"], p["ln2_b"], alpha)
    return x2, conv_new, c, n.reshape(n0.shape), m


def kernel(x_prompt, x_sample, state_conv, state_C, state_n, state_m, w_in, b_gate, conv_w,
           w_conv_out, mh_g, w_m_out, w_o, ln1_g, ln1_b, w_ff1, w_ff2, ln2_g, ln2_b):
    depth = w_in.shape[0]
    alpha = (2.0 * depth) ** 0.25
    bsz, t, d = x_prompt.shape
    sb, st, _ = x_sample.shape
    assert t % TOKEN_TILE == 0 and t % MLSTM_TILE == 0 and MLSTM_TILE % CHUNK == 0
    assert d % N_HEADS == 0
    assert sb % SAMPLE_BATCH_BLOCK == 0 and st >= CONV_W - 1

    xp = x_prompt
    xs = jnp.transpose(x_sample, (1, 0, 2)).reshape(st * sb, d)
    outs = [[] for _ in range(8)]
    for l in range(depth):
        p = _layer_weights(w_in[l], b_gate[l], conv_w[l], w_conv_out[l], mh_g[l], w_m_out[l],
                           w_o[l], ln1_g[l], ln1_b[l], w_ff1[l], w_ff2[l], ln2_g[l], ln2_b[l])
        xp, cp, c_p, n_p, m_p = _prompt_layer(xp, p, alpha)
        conv_tm = jnp.transpose(state_conv[l], (1, 0, 2)).reshape((CONV_W - 1) * sb, d)
        xs, cs_tm, c_s, n_s, m_s = _sample_layer(xs, conv_tm, state_C[l], state_n[l], state_m[l],
                                                 p, alpha, st)
        cs = jnp.transpose(cs_tm.reshape(CONV_W - 1, sb, d), (1, 0, 2))
        for acc, val in zip(outs, (cp, cs, c_p, c_s, n_p, n_s, m_p, m_s)):
            acc.append(val)
    ys = jnp.transpose(xs.reshape(st, sb, d), (1, 0, 2))
    return (xp, ys) + tuple(jnp.stack(acc) for acc in outs)
```

```python
import functools

import jax
import jax.numpy as jnp
from jax import lax
from jax.experimental import pallas as pl
from jax.experimental.pallas import tpu as pltpu

F32 = jnp.float32
BF16 = jnp.bfloat16

LN_EPS = 1e-5
N_HEADS = 4
CHUNK = 128
CONV_W = 3
TOKEN_TILE = 1024
SUB_TILE = 512
MLSTM_TILE = 512
SAMPLE_BATCH_BLOCK = 8
LANES = 128
GATE_PAD = LANES
VMEM_LIMIT = 56 * 1024 * 1024


def _dot(a, b):
    return jnp.dot(a, b, preferred_element_type=F32)


def _dot_nt(a, b):
    return lax.dot_general(a, b, (((1,), (1,)), ((), ())), preferred_element_type=F32)


def _dot_tn(a, b):
    return lax.dot_general(a, b, (((0,), (0,)), ((), ())), preferred_element_type=F32)


def _sigmoid(x):
    return 1.0 / (1.0 + jnp.exp(-x))


def _log_sigmoid(x):
    return jnp.minimum(x, 0.0) - jnp.log1p(jnp.exp(-jnp.abs(x)))


def _split_bf16(x):
    hi = x.astype(BF16)
    lo = (x - hi.astype(F32)).astype(BF16)
    return hi, lo


def _layer_norm(r, g, b):
    mu = jnp.mean(r, axis=-1, keepdims=True)
    xc = r - mu
    var = jnp.mean(xc * xc, axis=-1, keepdims=True)
    return xc * lax.rsqrt(var + LN_EPS) * g + b


def _head_norm(h):
    mu = jnp.mean(h, axis=-1, keepdims=True)
    hc = h - mu
    return hc * lax.rsqrt(jnp.mean(hc * hc, axis=-1, keepdims=True) + LN_EPS)


def _const_spec(shape):
    zeros = (0,) * len(shape)
    return pl.BlockSpec(shape, lambda *_: zeros, pipeline_mode=pl.Buffered(1))


def _window_spec(block_shape, index):
    return pl.BlockSpec(block_shape, lambda *_: index, pipeline_mode=pl.Buffered(1))


def _bch_spec(d):
    return _window_spec((d, 3 * d), (0, 0))


def _qkvo_spec(d):
    return _window_spec((pl.Element(d), pl.Element(4 * d)), (0, 3 * d))


def _gate_spec(d, which):
    return _window_spec((d, d), (0, which))


def _params(n_axes):
    return pltpu.CompilerParams(dimension_semantics=("arbitrary",) * n_axes,
                                vmem_limit_bytes=VMEM_LIMIT)


def _conv_stripes(d):
    sw = 512 if d % 512 == 0 else d
    return [slice(s * sw, (s + 1) * sw) for s in range(d // sw)]


def _conv_prompt_kernel(x_ref, wbch_ref, cw_ref, wco_ref, wgc_ref, m1_ref, cs_ref, u_s, a_s):
    tm, d = a_s.shape
    @pl.when(pl.program_id(1) == 0)
    def _():
        u_s[0:8, :] = jnp.zeros((8, d), F32)

    for r0 in range(0, tm, SUB_TILE):
        sub = min(SUB_TILE, tm - r0)
        xb = x_ref[0, r0:r0 + sub, :].astype(BF16)
        for cs in _conv_stripes(d):
            off = cs.start
            bg = _dot(xb, wbch_ref[:, off:cs.stop])
            cg = _dot(xb, wbch_ref[:, d + off:d + cs.stop])
            hc = _dot(xb, wbch_ref[:, 2 * d + off:2 * d + cs.stop])
            u = cg * hc
            u_s[8 + r0:8 + r0 + sub, cs] = u
            conv = (u_s[6 + r0:6 + r0 + sub, cs] * cw_ref[0:1, cs]
                    + u_s[7 + r0:7 + r0 + sub, cs] * cw_ref[1:2, cs] + u * cw_ref[2:3, cs])
            a_s[r0:r0 + sub, cs] = (bg * conv).astype(BF16)
        gate = _dot(xb, wgc_ref[...])
        yc = _dot(a_s[r0:r0 + sub, :], wco_ref[...])
        m1_ref[0, r0:r0 + sub, :] = _sigmoid(gate) * yc
    cs_ref[0] = u_s[tm + 6:tm + 8, :]
    u_s[0:8, :] = u_s[tm:tm + 8, :]


def _conv_sample_kernel(x_ref, st_ref, wbch_ref, cw_ref, wco_ref, wgc_ref, m1_ref, cs_ref, a_s,
                        *, n_steps):
    rows, d = a_s.shape
    nb = rows // n_steps
    xb = x_ref[...].astype(BF16)
    for cs in _conv_stripes(d):
        off = cs.start
        bg = _dot(xb, wbch_ref[:, off:cs.stop])
        cg = _dot(xb, wbch_ref[:, d + off:d + cs.stop])
        hc = _dot(xb, wbch_ref[:, 2 * d + off:2 * d + cs.stop])
        u = cg * hc
        up = [st_ref[j * nb:(j + 1) * nb, cs] for j in range(CONV_W - 1)]
        up += [u[t * nb:(t + 1) * nb] for t in range(n_steps)]
        for t in range(n_steps):
            conv = (up[t] * cw_ref[0:1, cs] + up[t + 1] * cw_ref[1:2, cs]
                    + up[t + 2] * cw_ref[2:3, cs])
            a_s[t * nb:(t + 1) * nb, cs] = (bg[t * nb:(t + 1) * nb] * conv).astype(BF16)
        for j in range(CONV_W - 1):
            cs_ref[j * nb:(j + 1) * nb, cs] = up[n_steps + j]
    yc = _dot(a_s[...], wco_ref[...])
    m1_ref[...] = _sigmoid(_dot(xb, wgc_ref[...])) * yc


def _conv_branch_prompt(x, wbch, cw, wco, wgc):
    bsz, t, d = x.shape
    tm = TOKEN_TILE
    return pl.pallas_call(
        _conv_prompt_kernel,
        grid=(bsz, t // tm),
        in_specs=[pl.BlockSpec((1, tm, d), lambda b, j: (b, j, 0)),
                  _bch_spec(d), _const_spec(cw.shape), _const_spec(wco.shape),
                  _gate_spec(d, 0)],
        out_specs=[pl.BlockSpec((1, tm, d), lambda b, j: (b, j, 0)),
                   pl.BlockSpec((1, CONV_W - 1, d), lambda b, j: (b, 0, 0))],
        out_shape=[jax.ShapeDtypeStruct((bsz, t, d), F32),
                   jax.ShapeDtypeStruct((bsz, CONV_W - 1, d), F32)],
        scratch_shapes=[pltpu.VMEM((tm + 8, d), F32), pltpu.VMEM((tm, d), BF16)],
        compiler_params=_params(2),
        name="conv_branch_prompt",
    )(x, wbch, cw, wco, wgc)


def _conv_branch_sample(x_tm, st_tm, wbch, cw, wco, wgc, n_steps):
    rows, d = x_tm.shape
    return pl.pallas_call(
        functools.partial(_conv_sample_kernel, n_steps=n_steps),
        grid=(1,),
        in_specs=[_const_spec(x_tm.shape), _const_spec(st_tm.shape), _bch_spec(d),
                  _const_spec(cw.shape), _const_spec(wco.shape), _gate_spec(d, 0)],
        out_specs=[pl.BlockSpec((rows, d), lambda i: (0, 0)),
                   pl.BlockSpec(st_tm.shape, lambda i: (0, 0))],
        out_shape=[jax.ShapeDtypeStruct((rows, d), F32),
                   jax.ShapeDtypeStruct(st_tm.shape, F32)],
        scratch_shapes=[pltpu.VMEM((rows, d), BF16)],
        compiler_params=_params(1),
        name="conv_branch_sample",
    )(x_tm, st_tm, wbch, cw, wco, wgc)


def _project_qkvo(xh, wqkvo_ref, mhg_ref, q_s, k_s, v_s, og_s, d, dh):
    q_s[...] = _dot(xh, wqkvo_ref[:, 0:d]).astype(BF16).astype(q_s.dtype)
    k_s[...] = (_dot(xh, wqkvo_ref[:, d:2 * d]) * (dh ** -0.5)).astype(BF16).astype(k_s.dtype)
    v_s[...] = _dot(xh, wqkvo_ref[:, 2 * d:3 * d]).astype(BF16).astype(v_s.dtype)
    og_s[...] = mhg_ref[...] * _sigmoid(_dot(xh, wqkvo_ref[:, 3 * d:4 * d]))


def _rep(col, times):
    return col if times == 1 else jnp.concatenate([col] * times, axis=1)


def _mlstm_prompt_step(x_ref, wqkvo_ref, wgt_ref, bgc_ref, mhg_ref, hm_ref,
                       q_s, k_s, vx_s, og_s, e_s, colb_s, colm_s, cole_s, cst_s, mst_s,
                       *, cur, prev):
    _, tm, d = q_s.shape
    dh = d // N_HEADS
    L = CHUNK
    LANES = colb_s.shape[-1]
    n_chunks = tm // L
    wide = (dh + LANES) // LANES

    x = x_ref[...]
    xh, xl = _split_bf16(x)

    ga = _dot_nt(wgt_ref[0], xh)
    gb = _dot_nt(wgt_ref[1], xl)
    gt = ga[0:8] + ga[8:16] + gb[0:8] + bgc_ref[...]
    is_input_gate = lax.broadcasted_iota(jnp.int32, gt.shape, 0) < N_HEADS
    g = jnp.where(is_input_gate, gt, _log_sigmoid(gt))

    pos = lax.broadcasted_iota(jnp.int32, g.shape, 1) % L
    csum = g
    shift = 1
    while shift < L:
        csum = csum + jnp.where(pos >= shift, pltpu.roll(csum, shift, 1), 0.0)
        shift *= 2
    ba = jnp.concatenate([csum[N_HEADS:], g[:N_HEADS] - csum[N_HEADS:]], axis=0)

    ri = lax.broadcasted_iota(jnp.int32, (L, L), 0)
    ci = lax.broadcasted_iota(jnp.int32, (L, L), 1)
    causal = ri >= ci

    for c in range(n_chunks):
        slab = ba[:, c * L:(c + 1) * L]
        cols = slab.T
        for h in range(N_HEADS):
            b_rep = jnp.broadcast_to(cols[:, h:h + 1], (L, LANES))
            a_rep = jnp.broadcast_to(cols[:, N_HEADS + h:N_HEADS + h + 1], (L, LANES))
            a_row = slab[N_HEADS + h:N_HEADS + h + 1, :]
            dm = jnp.where(causal, b_rep + a_row, -jnp.inf)
            m_loc = jnp.broadcast_to(jnp.max(dm, axis=1, keepdims=True), (L, LANES))
            e_s[cur, c, h] = jnp.exp(dm - m_loc)
            colb_s[cur, c, h] = b_rep
            colm_s[cur, c, h] = m_loc
            cole_s[cur, c, h] = jnp.exp(a_rep + b_rep[L - 1:L, :] - m_loc[L - 1:L, :])

    def project(part):
        w = wqkvo_ref[:, part * d:(part + 1) * d]
        if part == 0:
            q_s[cur] = _dot(xh, w).astype(BF16)
        elif part == 1:
            k_s[cur] = (_dot(xh, w) * (dh ** -0.5)).astype(BF16)
        elif part == 2:
            v = _dot(xh, w).astype(BF16)
            for h in range(N_HEADS):
                vx_s[cur, h, :, 0:dh] = v[:, h * dh:(h + 1) * dh]
                vx_s[cur, h, :, dh:] = jnp.ones((tm, LANES), BF16)
        else:
            og_s[cur] = mhg_ref[...] * _sigmoid(_dot(xh, w))

    def recur(c):
        rows = slice(c * L, (c + 1) * L)
        heads = [slice(h * dh, (h + 1) * dh) for h in range(N_HEADS)]
        qk = [_dot_nt(q_s[prev, rows, hc], k_s[prev, rows, hc]) for hc in heads]
        states = [cst_s[h] for h in range(N_HEADS)]
        qc = [_dot(q_s[prev, rows, hc], states[h].astype(BF16)) for h, hc in enumerate(heads)]
        ux = []
        for h, hc in enumerate(heads):
            kw = k_s[prev, rows, hc].astype(F32) * _rep(cole_s[prev, c, h], dh // LANES)
            ux.append(_dot_tn(kw.astype(BF16), vx_s[prev, h, rows, :]))
        svx = []
        for h in range(N_HEADS):
            s_loc = qk[h] * e_s[prev, c, h]
            svx.append(_dot(s_loc.astype(BF16), vx_s[prev, h, rows, :]))
        for h, hc in enumerate(heads):
            m_prev = mst_s[h]
            m_loc = colm_s[prev, c, h]
            inter = colb_s[prev, c, h] + m_prev
            m_t = jnp.maximum(inter, m_loc)
            w_inter = _rep(jnp.exp(inter - m_t), wide)
            w_loc = _rep(jnp.exp(m_loc - m_t), wide)
            nd = w_inter * qc[h] + w_loc * svx[h]
            inv = 1.0 / jnp.maximum(jnp.abs(nd[:, dh:]), jnp.exp(-m_t))
            hh = nd[:, 0:dh] * _rep(inv, dh // LANES)
            hm_ref[rows, hc] = (_head_norm(hh) * og_s[prev, rows, hc]).astype(BF16)
            m_new = m_t[L - 1:L, :]
            decay = _rep(jnp.exp(inter[L - 1:L, :] - m_new), wide)
            grow = _rep(jnp.exp(m_loc[L - 1:L, :] - m_new), wide)
            cst_s[h] = decay * states[h] + grow * ux[h]
            mst_s[h] = m_new

    n_parts = 4
    for i in range(max(n_chunks, n_parts)):
        if i < n_chunks:
            recur(i)
        if i < n_parts:
            project(i)


def _mlstm_prompt_kernel(x_ref, wqkvo_ref, wgt_ref, bgc_ref, mhg_ref,
                         hm_ref, c_ref, n_ref, m_ref,
                         q_s, k_s, vx_s, og_s, e_s, colb_s, colm_s, cole_s, cst_s, mst_s,
                         *, tiles_per_seq):
    dh = cst_s.shape[1]
    g_step = pl.program_id(0)

    @pl.when(g_step == 0)
    def _():
        for ref in (q_s, k_s, vx_s, og_s, e_s, colb_s, colm_s, cole_s):
            ref[1] = jnp.zeros(ref.shape[1:], ref.dtype)

    @pl.when(jnp.logical_or(g_step == 0, (g_step - 1) % tiles_per_seq == 0))
    def _():
        cst_s[...] = jnp.zeros(cst_s.shape, F32)
        mst_s[...] = jnp.zeros(mst_s.shape, F32)

    for parity in (0, 1):
        @pl.when(g_step % 2 == parity)
        def _():
            _mlstm_prompt_step(x_ref, wqkvo_ref, wgt_ref, bgc_ref, mhg_ref, hm_ref,
                               q_s, k_s, vx_s, og_s, e_s, colb_s, colm_s, cole_s, cst_s, mst_s,
                               cur=parity, prev=1 - parity)

    @pl.when(jnp.logical_and(g_step >= 1, (g_step - 1) % tiles_per_seq == tiles_per_seq - 1))
    def _():
        for h in range(N_HEADS):
            state = cst_s[h]
            c_ref[0, h] = state[:, 0:dh]
            n_ref[0, h] = state[:, dh:].T[0:1, :]
            m_ref[0, h] = mst_s[h][:, 0:1]


def _mlstm_prompt(x, wqkvo, wgt, bgc, mhg):
    bsz, t, d = x.shape
    dh = d // N_HEADS
    tm = MLSTM_TILE
    nc = tm // CHUNK
    dx = dh + LANES
    tps = t // tm
    n_tiles = bsz * tps
    seq_block = lambda g: (jnp.maximum(g - 1, 0) // tps, 0, 0, 0)
    hm, c, n, m = pl.pallas_call(
        functools.partial(_mlstm_prompt_kernel, tiles_per_seq=tps),
        grid=(n_tiles + 1,),
        in_specs=[pl.BlockSpec((tm, d), lambda g: (jnp.minimum(g, n_tiles - 1), 0)),
                  _qkvo_spec(d), _const_spec(wgt.shape), _const_spec(bgc.shape),
                  _const_spec(mhg.shape)],
        out_specs=[pl.BlockSpec((tm, d), lambda g: (jnp.maximum(g - 1, 0), 0)),
                   pl.BlockSpec((1, N_HEADS, dh, dh), seq_block),
                   pl.BlockSpec((1, N_HEADS, 1, dh), seq_block),
                   pl.BlockSpec((1, N_HEADS, 1, 1), seq_block)],
        out_shape=[jax.ShapeDtypeStruct((bsz * t, d), BF16),
                   jax.ShapeDtypeStruct((bsz, N_HEADS, dh, dh), F32),
                   jax.ShapeDtypeStruct((bsz, N_HEADS, 1, dh), F32),
                   jax.ShapeDtypeStruct((bsz, N_HEADS, 1, 1), F32)],
        scratch_shapes=[
            pltpu.VMEM((2, tm, d), BF16),
            pltpu.VMEM((2, tm, d), BF16),
            pltpu.VMEM((2, N_HEADS, tm, dx), BF16),
            pltpu.VMEM((2, tm, d), F32),
            pltpu.VMEM((2, nc, N_HEADS, CHUNK, CHUNK), F32),
            pltpu.VMEM((2, nc, N_HEADS, CHUNK, LANES), F32),
            pltpu.VMEM((2, nc, N_HEADS, CHUNK, LANES), F32),
            pltpu.VMEM((2, nc, N_HEADS, CHUNK, LANES), F32),
            pltpu.VMEM((N_HEADS, dh, dx), F32),
            pltpu.VMEM((N_HEADS, 1, LANES), F32)],
        compiler_params=_params(1),
        name="mlstm_prompt",
    )(x.reshape(bsz * t, d), wqkvo, wgt, bgc, mhg)
    return hm, c, n.reshape(bsz, N_HEADS, dh), m.reshape(bsz, N_HEADS)


def _mlstm_sample_kernel(x_ref, wqkvo_ref, wgc_ref, bgr_ref, mhg_ref, c0_ref, n0_ref, m0_ref,
                         hm_ref, c_ref, n_ref, m_ref, q_s, k_s, v_s, og_s, g_s, *, n_steps):
    rows, d = q_s.shape
    dh = d // N_HEADS
    nb = rows // n_steps
    bb = SAMPLE_BATCH_BLOCK
    T = n_steps
    i = pl.program_id(0)
    h = pl.program_id(1)

    @pl.when(jnp.logical_and(i == 0, h == 0))
    def _():
        x = x_ref[...]
        xh, xl = _split_bf16(x)
        _project_qkvo(xh, wqkvo_ref, mhg_ref, q_s, k_s, v_s, og_s, d, dh)
        g = _dot(xh, wgc_ref[0]) + _dot(xl, wgc_ref[0]) + _dot(xh, wgc_ref[1]) + bgr_ref[...]
        is_input_gate = lax.broadcasted_iota(jnp.int32, g.shape, 1) < N_HEADS
        g_s[...] = jnp.where(is_input_gate, g, _log_sigmoid(g))

    r0 = pl.multiple_of(i * bb, bb)
    trow = [pl.ds(pl.multiple_of(t * nb + r0, bb), bb) for t in range(T)]
    hc = pl.ds(pl.multiple_of(h * dh, dh), dh)
    owner = lax.broadcasted_iota(jnp.int32, (T * bb, dh), 0) % bb
    gate_lane = lax.broadcasted_iota(jnp.int32, (bb, g_s.shape[1]), 1)

    def gate_column(t, col):
        return jnp.sum(jnp.where(gate_lane == col, g_s[trow[t], :], 0.0), axis=1, keepdims=True)

    qf = [q_s[trow[t], hc] for t in range(T)]
    kf = [k_s[trow[t], hc] for t in range(T)]
    vf = [v_s[trow[t], hc] for t in range(T)]
    li = [gate_column(t, h) for t in range(T)]
    lf = [gate_column(t, N_HEADS + h) for t in range(T)]
    m0 = m0_ref[0]
    n0 = n0_ref[...]
    b = [lf[0]]
    for t in range(1, T):
        b.append(b[t - 1] + lf[t])
    a = [li[t] - b[t] for t in range(T)]
    m_t, w_inter, s_w = [], [], []
    for t in range(T):
        inter = b[t] + m0
        mt = inter
        for s in range(t + 1):
            mt = jnp.maximum(mt, b[t] + a[s])
        m_t.append(mt)
        w_inter.append(jnp.exp(inter - mt))
        s_w.append([jnp.sum(qf[t] * kf[s], axis=1, keepdims=True) * jnp.exp(b[t] + a[s] - mt)
                    for s in range(t + 1)])
    m_new = m_t[T - 1]
    b_last = b[T - 1]
    decay = jnp.exp(b_last + m0 - m_new)
    kw = [kf[s] * jnp.exp(a[s] + b_last - m_new) for s in range(T)]
    n_new = decay * n0
    for s in range(T):
        n_new = n_new + kw[s]
    n_ref[...] = n_new
    m_ref[0] = m_new

    q_blk = jnp.concatenate(qf, axis=0).astype(BF16)
    kw_blk = jnp.concatenate(kw, axis=0).astype(BF16)
    v_blk = jnp.concatenate(vf, axis=0)
    qc = jnp.zeros((T * bb, dh), F32)
    for bi in range(bb):
        mine = owner == bi
        c_old = c0_ref[bi, 0]
        qc = jnp.where(mine, _dot(q_blk, c_old.astype(BF16)), qc)
        upd = _dot_tn(kw_blk, jnp.where(mine, v_blk, 0.0).astype(BF16))
        c_ref[bi, 0] = decay[bi:bi + 1, :] * c_old + upd

    for t in range(T):
        num = w_inter[t] * qc[t * bb:(t + 1) * bb]
        den = w_inter[t] * jnp.sum(qf[t] * n0, axis=1, keepdims=True)
        for s in range(t + 1):
            num = num + s_w[t][s] * vf[s]
            den = den + s_w[t][s]
        hh = num / jnp.maximum(jnp.abs(den), jnp.exp(-m_t[t]))
        hm_ref[trow[t], hc] = _head_norm(hh) * og_s[trow[t], hc]


def _mlstm_sample(x_tm, wqkvo, wgc, bgr, mhg, c0, n0, m0, n_steps):
    rows, d = x_tm.shape
    nb = rows // n_steps
    dh = d // N_HEADS
    bb = SAMPLE_BATCH_BLOCK
    cblock = pl.BlockSpec((bb, 1, dh, dh), lambda i, h: (i, h, 0, 0))
    nblock = pl.BlockSpec((bb, dh), lambda i, h: (i, h))
    mblock = pl.BlockSpec((1, bb, 1), lambda i, h: (h, i, 0))
    m0_hm = m0.T.reshape(N_HEADS, nb, 1)
    hm, c, n, m_hm = pl.pallas_call(
        functools.partial(_mlstm_sample_kernel, n_steps=n_steps),
        grid=(nb // bb, N_HEADS),
        in_specs=[_const_spec(x_tm.shape), _qkvo_spec(d), _const_spec(wgc.shape),
                  _const_spec(bgr.shape), _const_spec(mhg.shape), cblock, nblock, mblock],
        out_specs=[pl.BlockSpec((rows, d), lambda i, h: (0, 0)), cblock, nblock, mblock],
        out_shape=[jax.ShapeDtypeStruct((rows, d), F32),
                   jax.ShapeDtypeStruct(c0.shape, F32),
                   jax.ShapeDtypeStruct((nb, d), F32),
                   jax.ShapeDtypeStruct((N_HEADS, nb, 1), F32)],
        scratch_shapes=[pltpu.VMEM((rows, d), F32), pltpu.VMEM((rows, d), F32),
                        pltpu.VMEM((rows, d), F32), pltpu.VMEM((rows, d), F32),
                        pltpu.VMEM((rows, GATE_PAD), F32)],
        compiler_params=_params(2),
        name="mlstm_sample",
    )(x_tm, wqkvo, wgc, bgr, mhg, c0, n0, m0_hm)
    return hm, c, n, m_hm.reshape(N_HEADS, nb).T


def _merge_kernel(x_ref, m1_ref, hm_ref, wmo_ref, wgm_ref, wo_ref, g_ref, b_ref, o_ref, *, alpha):
    tm = x_ref.shape[0]
    for r0 in range(0, tm, SUB_TILE):
        rows = slice(r0, min(r0 + SUB_TILE, tm))
        x = x_ref[rows, :]
        ym = _dot(hm_ref[rows, :].astype(BF16), wmo_ref[...])
        gate = _dot(x.astype(BF16), wgm_ref[...])
        merged = m1_ref[rows, :] + _sigmoid(gate) * ym
        r = alpha * x + _dot(merged.astype(BF16), wo_ref[...])
        o_ref[rows, :] = _layer_norm(r, g_ref[...], b_ref[...])


def _merge(x, m1, hm, wmo, wgm, wo, g, b, alpha):
    n, d = x.shape
    tm = min(TOKEN_TILE, n)
    tile = pl.BlockSpec((tm, d), lambda i: (i, 0))
    return pl.pallas_call(
        functools.partial(_merge_kernel, alpha=alpha),
        grid=(n // tm,),
        in_specs=[tile, tile, tile, _const_spec(wmo.shape), _gate_spec(d, 1),
                  _const_spec(wo.shape), _const_spec(g.shape), _const_spec(b.shape)],
        out_specs=tile,
        out_shape=jax.ShapeDtypeStruct((n, d), F32),
        compiler_params=_params(1),
        name="merge_ln1",
    )(x, m1, hm, wmo, wgm, wo, g, b)


def _ffn_kernel(x_ref, w1_ref, w2_ref, g_ref, b_ref, o_ref, hid_s, *, alpha):
    tm, dff = hid_s.shape
    sw = 1024 if dff % 1024 == 0 else dff
    for r0 in range(0, tm, SUB_TILE):
        rows = slice(r0, min(r0 + SUB_TILE, tm))
        x = x_ref[rows, :]
        xb = x.astype(BF16)
        for s in range(dff // sw):
            cs = slice(s * sw, (s + 1) * sw)
            hid = jnp.maximum(_dot(xb, w1_ref[:, cs]), 0.0)
            hid_s[rows, cs] = (hid * hid).astype(BF16)
        r = alpha * x + _dot(hid_s[rows, :], w2_ref[...])
        o_ref[rows, :] = _layer_norm(r, g_ref[...], b_ref[...])


def _ffn(x, w1, w2, g, b, alpha):
    n, d = x.shape
    tm = min(TOKEN_TILE, n)
    tile = pl.BlockSpec((tm, d), lambda i: (i, 0))
    return pl.pallas_call(
        functools.partial(_ffn_kernel, alpha=alpha),
        grid=(n // tm,),
        in_specs=[tile, _const_spec(w1.shape), _const_spec(w2.shape), _const_spec(g.shape),
                  _const_spec(b.shape)],
        out_specs=tile,
        out_shape=jax.ShapeDtypeStruct((n, d), F32),
        scratch_shapes=[pltpu.VMEM((tm, w1.shape[1]), BF16)],
        compiler_params=_params(1),
        name="ffn_ln2",
    )(x, w1, w2, g, b)


def _w_in_prep_kernel(a_ref, g_ref, b_ref, wa_ref, wg_ref, wb_ref, *, n_a):
    j = pl.program_id(0)

    @pl.when(j == 0)
    def _():
        wg_ref[...] = g_ref[...]

    @pl.when(j < n_a)
    def _():
        wa_ref[...] = a_ref[...].T.astype(BF16)

    @pl.when(j >= n_a)
    def _():
        wb_ref[...] = b_ref[...].T.astype(BF16)


def _prepare_w_in(w_in_t):
    d = w_in_t.shape[1]
    n_a, n_b = 7, 2
    h2 = 2 * N_HEADS
    off_b = n_a * d + h2
    return pl.pallas_call(
        functools.partial(_w_in_prep_kernel, n_a=n_a),
        grid=(n_a + n_b,),
        in_specs=[pl.BlockSpec((d, d), lambda j: (jnp.minimum(j, n_a - 1), 0)),
                  pl.BlockSpec((pl.Element(h2), pl.Element(d)), lambda j: (n_a * d, 0)),
                  pl.BlockSpec((pl.Element(d), pl.Element(d)),
                               lambda j: (pl.multiple_of(off_b + jnp.maximum(j - n_a, 0) * d, 8),
                                          0))],
        out_specs=[pl.BlockSpec((d, d), lambda j: (0, jnp.minimum(j, n_a - 1))),
                   pl.BlockSpec((h2, d), lambda j: (0, 0)),
                   pl.BlockSpec((d, d), lambda j: (0, jnp.maximum(j - n_a, 0)))],
        out_shape=[jax.ShapeDtypeStruct((d, n_a * d), BF16),
                   jax.ShapeDtypeStruct((h2, d), F32),
                   jax.ShapeDtypeStruct((d, n_b * d), BF16)],
        compiler_params=_params(1),
        name="w_in_prep",
    )(w_in_t, w_in_t, w_in_t)


def _layer_weights(w_in, b_gate, conv_w, w_conv_out, mh_g, w_m_out, w_o, ln1_g, ln1_b,
                   w_ff1, w_ff2, ln2_g, ln2_b):
    d = w_in.shape[0]
    h2 = 2 * N_HEADS
    w_in_t = jnp.swapaxes(w_in, 0, 1)
    w_a, wg_t, w_b = _prepare_w_in(w_in_t)
    wgt_hi, wgt_lo = _split_bf16(wg_t)
    wgt = jnp.stack([jnp.concatenate([wgt_hi, wgt_lo], axis=0),
                     jnp.concatenate([wgt_hi, jnp.zeros_like(wgt_hi)], axis=0)])
    pad = ((0, 0), (0, GATE_PAD - h2))
    wgc = jnp.stack([jnp.pad(wgt_hi.T, pad), jnp.pad(wgt_lo.T, pad)])
    return dict(
        w_a=w_a, w_b=w_b,
        wgt=wgt, wgc=wgc,
        bgc=b_gate.reshape(h2, 1).astype(F32),
        bgr=jnp.pad(b_gate.reshape(1, h2).astype(F32), pad),
        cw=conv_w.astype(F32),
        wco=w_conv_out.astype(BF16), mhg=mh_g.reshape(1, d).astype(F32),
        wmo=w_m_out.astype(BF16), wo=w_o.astype(BF16),
        ln1_g=ln1_g.reshape(1, d), ln1_b=ln1_b.reshape(1, d),
        w1=w_ff1.astype(BF16), w2=w_ff2.astype(BF16),
        ln2_g=ln2_g.reshape(1, d), ln2_b=ln2_b.reshape(1, d))


def _prompt_layer(x, p, alpha):
    bsz, t, d = x.shape
    m1, conv_buf = _conv_branch_prompt(x, p["w_a"], p["cw"], p["wco"], p["w_b"])
    hm, c, n, m = _mlstm_prompt(x, p["w_a"], p["wgt"], p["bgc"], p["mhg"])
    xf = x.reshape(bsz * t, d)
    x1 = _merge(xf, m1.reshape(bsz * t, d), hm, p["wmo"], p["w_b"], p["wo"], p["ln1_g"],
                p["ln1_b"], alpha)
    x2 = _ffn(x1, p["w1"], p["w2"], p["ln2_g"], p["ln2_b"], alpha)
    return x2.reshape(bsz, t, d), conv_buf, c, n, m


def _sample_layer(x_tm, conv_tm, c0, n0, m0, p, alpha, n_steps):
    d = x_tm.shape[1]
    m1, conv_new = _conv_branch_sample(x_tm, conv_tm, p["w_a"], p["cw"], p["wco"], p["w_b"],
                                       n_steps)
    hm, c, n, m = _mlstm_sample(x_tm, p["w_a"], p["wgc"], p["bgr"], p["mhg"], c0,
                                n0.reshape(n0.shape[0], d), m0, n_steps)
    x1 = _merge(x_tm, m1, hm, p["wmo"], p["w_b"], p["wo"], p["ln1_g"], p["ln1_b"], alpha)
    x2 = _ffn(x1, p["w1"], p["w2"], p["ln2_g"], p["ln2_b"], alpha)
    return x2, conv_new, c, n.reshape(n0.shape), m


def kernel(x_prompt, x_sample, state_conv, state_C, state_n, state_m, w_in, b_gate, conv_w,
           w_conv_out, mh_g, w_m_out, w_o, ln1_g, ln1_b, w_ff1, w_ff2, ln2_g, ln2_b):
    depth = w_in.shape[0]
    alpha = (2.0 * depth) ** 0.25
    bsz, t, d = x_prompt.shape
    sb, st, _ = x_sample.shape
    assert t % TOKEN_TILE == 0 and t % MLSTM_TILE == 0 and MLSTM_TILE % CHUNK == 0
    assert d % N_HEADS == 0
    assert sb % SAMPLE_BATCH_BLOCK == 0 and st >= CONV_W - 1

    xp = x_prompt
    xs = jnp.transpose(x_sample, (1, 0, 2)).reshape(st * sb, d)
    outs = [[] for _ in range(8)]
    for l in range(depth):
        p = _layer_weights(w_in[l], b_gate[l], conv_w[l], w_conv_out[l], mh_g[l], w_m_out[l],
                           w_o[l], ln1_g[l], ln1_b[l], w_ff1[l], w_ff2[l], ln2_g[l], ln2_b[l])
        xp, cp, c_p, n_p, m_p = _prompt_layer(xp, p, alpha)
        conv_tm = jnp.transpose(state_conv[l], (1, 0, 2)).reshape((CONV_W - 1) * sb, d)
        xs, cs_tm, c_s, n_s, m_s = _sample_layer(xs, conv_tm, state_C[l], state_n[l], state_m[l],
                                                 p, alpha, st)
        cs = jnp.transpose(cs_tm.reshape(CONV_W - 1, sb, d), (1, 0, 2))
        for acc, val in zip(outs, (cp, cs, c_p, c_s, n_p, n_s, m_p, m_s)):
            acc.append(val)
    ys = jnp.transpose(xs.reshape(st, sb, d), (1, 0, 2))
    return (xp, ys) + tuple(jnp.stack(acc) for acc in outs)
```

```python
import functools

import jax
import jax.numpy as jnp
from jax import lax
from jax.experimental import pallas as pl
from jax.experimental.pallas import tpu as pltpu

F32 = jnp.float32
BF16 = jnp.bfloat16

LN_EPS = 1e-5
N_HEADS = 4
CHUNK = 128
CONV_W = 3
TOKEN_TILE = 1024
SUB_TILE = 512
MLSTM_TILE = 512
SAMPLE_BATCH_BLOCK = 8
LANES = 128
GATE_PAD = LANES
VMEM_LIMIT = 56 * 1024 * 1024


def _dot(a, b):
    return jnp.dot(a, b, preferred_element_type=F32)


def _dot_nt(a, b):
    return lax.dot_general(a, b, (((1,), (1,)), ((), ())), preferred_element_type=F32)


def _dot_tn(a, b):
    return lax.dot_general(a, b, (((0,), (0,)), ((), ())), preferred_element_type=F32)


def _sigmoid(x):
    return 1.0 / (1.0 + jnp.exp(-x))


def _log_sigmoid(x):
    return jnp.minimum(x, 0.0) - jnp.log1p(jnp.exp(-jnp.abs(x)))


def _split_bf16(x):
    hi = x.astype(BF16)
    lo = (x - hi.astype(F32)).astype(BF16)
    return hi, lo


def _layer_norm(r, g, b):
    mu = jnp.mean(r, axis=-1, keepdims=True)
    xc = r - mu
    var = jnp.mean(xc * xc, axis=-1, keepdims=True)
    return xc * lax.rsqrt(var + LN_EPS) * g + b


def _head_norm(h):
    mu = jnp.mean(h, axis=-1, keepdims=True)
    hc = h - mu
    return hc * lax.rsqrt(jnp.mean(hc * hc, axis=-1, keepdims=True) + LN_EPS)


def _const_spec(shape):
    zeros = (0,) * len(shape)
    return pl.BlockSpec(shape, lambda *_: zeros, pipeline_mode=pl.Buffered(1))


def _window_spec(block_shape, index):
    return pl.BlockSpec(block_shape, lambda *_: index, pipeline_mode=pl.Buffered(1))


def _bch_spec(d):
    return _window_spec((d, 3 * d), (0, 0))


def _qkvo_spec(d):
    return _window_spec((pl.Element(d), pl.Element(4 * d)), (0, 3 * d))


def _gate_spec(d, which):
    return _window_spec((d, d), (0, which))


def _params(n_axes):
    return pltpu.CompilerParams(dimension_semantics=("arbitrary",) * n_axes,
                                vmem_limit_bytes=VMEM_LIMIT)


def _conv_stripes(d):
    sw = 512 if d % 512 == 0 else d
    return [slice(s * sw, (s + 1) * sw) for s in range(d // sw)]


def _conv_prompt_kernel(x_ref, wbch_ref, cw_ref, wco_ref, wgc_ref, m1_ref, cs_ref, u_s, a_s):
    tm, d = a_s.shape
    @pl.when(pl.program_id(1) == 0)
    def _():
        u_s[0:8, :] = jnp.zeros((8, d), F32)

    for r0 in range(0, tm, SUB_TILE):
        sub = min(SUB_TILE, tm - r0)
        xb = x_ref[0, r0:r0 + sub, :].astype(BF16)
        for cs in _conv_stripes(d):
            off = cs.start
            bg = _dot(xb, wbch_ref[:, off:cs.stop])
            cg = _dot(xb, wbch_ref[:, d + off:d + cs.stop])
            hc = _dot(xb, wbch_ref[:, 2 * d + off:2 * d + cs.stop])
            u = cg * hc
            u_s[8 + r0:8 + r0 + sub, cs] = u
            conv = (u_s[6 + r0:6 + r0 + sub, cs] * cw_ref[0:1, cs]
                    + u_s[7 + r0:7 + r0 + sub, cs] * cw_ref[1:2, cs] + u * cw_ref[2:3, cs])
            a_s[r0:r0 + sub, cs] = (bg * conv).astype(BF16)
        gate = _dot(xb, wgc_ref[...])
        yc = _dot(a_s[r0:r0 + sub, :], wco_ref[...])
        m1_ref[0, r0:r0 + sub, :] = _sigmoid(gate) * yc
    cs_ref[0] = u_s[tm + 6:tm + 8, :]
    u_s[0:8, :] = u_s[tm:tm + 8, :]


def _conv_sample_kernel(x_ref, st_ref, wbch_ref, cw_ref, wco_ref, wgc_ref, m1_ref, cs_ref, a_s,
                        *, n_steps):
    rows, d = a_s.shape
    nb = rows // n_steps
    xb = x_ref[...].astype(BF16)
    for cs in _conv_stripes(d):
        off = cs.start
        bg = _dot(xb, wbch_ref[:, off:cs.stop])
        cg = _dot(xb, wbch_ref[:, d + off:d + cs.stop])
        hc = _dot(xb, wbch_ref[:, 2 * d + off:2 * d + cs.stop])
        u = cg * hc
        up = [st_ref[j * nb:(j + 1) * nb, cs] for j in range(CONV_W - 1)]
        up += [u[t * nb:(t + 1) * nb] for t in range(n_steps)]
        for t in range(n_steps):
            conv = (up[t] * cw_ref[0:1, cs] + up[t + 1] * cw_ref[1:2, cs]
                    + up[t + 2] * cw_ref[2:3, cs])
            a_s[t * nb:(t + 1) * nb, cs] = (bg[t * nb:(t + 1) * nb] * conv).astype(BF16)
        for j in range(CONV_W - 1):
            cs_ref[j * nb:(j + 1) * nb, cs] = up[n_steps + j]
    yc = _dot(a_s[...], wco_ref[...])
    m1_ref[...] = _sigmoid(_dot(xb, wgc_ref[...])) * yc


def _conv_branch_prompt(x, wbch, cw, wco, wgc):
    bsz, t, d = x.shape
    tm = TOKEN_TILE
    return pl.pallas_call(
        _conv_prompt_kernel,
        grid=(bsz, t // tm),
        in_specs=[pl.BlockSpec((1, tm, d), lambda b, j: (b, j, 0)),
                  _bch_spec(d), _const_spec(cw.shape), _const_spec(wco.shape),
                  _gate_spec(d, 0)],
        out_specs=[pl.BlockSpec((1, tm, d), lambda b, j: (b, j, 0)),
                   pl.BlockSpec((1, CONV_W - 1, d), lambda b, j: (b, 0, 0))],
        out_shape=[jax.ShapeDtypeStruct((bsz, t, d), F32),
                   jax.ShapeDtypeStruct((bsz, CONV_W - 1, d), F32)],
        scratch_shapes=[pltpu.VMEM((tm + 8, d), F32), pltpu.VMEM((tm, d), BF16)],
        compiler_params=_params(2),
        name="conv_branch_prompt",
    )(x, wbch, cw, wco, wgc)


def _conv_branch_sample(x_tm, st_tm, wbch, cw, wco, wgc, n_steps):
    rows, d = x_tm.shape
    return pl.pallas_call(
        functools.partial(_conv_sample_kernel, n_steps=n_steps),
        grid=(1,),
        in_specs=[_const_spec(x_tm.shape), _const_spec(st_tm.shape), _bch_spec(d),
                  _const_spec(cw.shape), _const_spec(wco.shape), _gate_spec(d, 0)],
        out_specs=[pl.BlockSpec((rows, d), lambda i: (0, 0)),
                   pl.BlockSpec(st_tm.shape, lambda i: (0, 0))],
        out_shape=[jax.ShapeDtypeStruct((rows, d), F32),
                   jax.ShapeDtypeStruct(st_tm.shape, F32)],
        scratch_shapes=[pltpu.VMEM((rows, d), BF16)],
        compiler_params=_params(1),
        name="conv_branch_sample",
    )(x_tm, st_tm, wbch, cw, wco, wgc)


def _project_qkvo(xh, wqkvo_ref, mhg_ref, q_s, k_s, v_s, og_s, d, dh):
    q_s[...] = _dot(xh, wqkvo_ref[:, 0:d]).astype(BF16).astype(q_s.dtype)
    k_s[...] = (_dot(xh, wqkvo_ref[:, d:2 * d]) * (dh ** -0.5)).astype(BF16).astype(k_s.dtype)
    v_s[...] = _dot(xh, wqkvo_ref[:, 2 * d:3 * d]).astype(BF16).astype(v_s.dtype)
    og_s[...] = mhg_ref[...] * _sigmoid(_dot(xh, wqkvo_ref[:, 3 * d:4 * d]))


def _rep(col, times):
    return col if times == 1 else jnp.concatenate([col] * times, axis=1)


def _mlstm_prompt_step(x_ref, wqkvo_ref, wgt_ref, bgc_ref, mhg_ref, hm_ref,
                       q_s, k_s, vx_s, og_s, e_s, colb_s, colm_s, cole_s, cst_s, mst_s,
                       *, cur, prev):
    _, tm, d = q_s.shape
    dh = d // N_HEADS
    L = CHUNK
    LANES = colb_s.shape[-1]
    n_chunks = tm // L
    wide = (dh + LANES) // LANES

    x = x_ref[...]
    xh, xl = _split_bf16(x)

    ga = _dot_nt(wgt_ref[0], xh)
    gb = _dot_nt(wgt_ref[1], xl)
    gt = ga[0:8] + ga[8:16] + gb[0:8] + bgc_ref[...]
    is_input_gate = lax.broadcasted_iota(jnp.int32, gt.shape, 0) < N_HEADS
    g = jnp.where(is_input_gate, gt, _log_sigmoid(gt))

    pos = lax.broadcasted_iota(jnp.int32, g.shape, 1) % L
    csum = g
    shift = 1
    while shift < L:
        csum = csum + jnp.where(pos >= shift, pltpu.roll(csum, shift, 1), 0.0)
        shift *= 2
    ba = jnp.concatenate([csum[N_HEADS:], g[:N_HEADS] - csum[N_HEADS:]], axis=0)

    ri = lax.broadcasted_iota(jnp.int32, (L, L), 0)
    ci = lax.broadcasted_iota(jnp.int32, (L, L), 1)
    causal = ri >= ci

    for c in range(n_chunks):
        slab = ba[:, c * L:(c + 1) * L]
        cols = slab.T
        for h in range(N_HEADS):
            b_rep = jnp.broadcast_to(cols[:, h:h + 1], (L, LANES))
            a_rep = jnp.broadcast_to(cols[:, N_HEADS + h:N_HEADS + h + 1], (L, LANES))
            a_row = slab[N_HEADS + h:N_HEADS + h + 1, :]
            dm = jnp.where(causal, b_rep + a_row, -jnp.inf)
            m_loc = jnp.broadcast_to(jnp.max(dm, axis=1, keepdims=True), (L, LANES))
            e_s[cur, c, h] = jnp.exp(dm - m_loc)
            colb_s[cur, c, h] = b_rep
            colm_s[cur, c, h] = m_loc
            cole_s[cur, c, h] = jnp.exp(a_rep + b_rep[L - 1:L, :] - m_loc[L - 1:L, :])

    def project(part):
        w = wqkvo_ref[:, part * d:(part + 1) * d]
        if part == 0:
            q_s[cur] = _dot(xh, w).astype(BF16)
        elif part == 1:
            k_s[cur] = (_dot(xh, w) * (dh ** -0.5)).astype(BF16)
        elif part == 2:
            v = _dot(xh, w).astype(BF16)
            for h in range(N_HEADS):
                vx_s[cur, h, :, 0:dh] = v[:, h * dh:(h + 1) * dh]
                vx_s[cur, h, :, dh:] = jnp.ones((tm, LANES), BF16)
        else:
            og_s[cur] = mhg_ref[...] * _sigmoid(_dot(xh, w))

    def recur(c):
        rows = slice(c * L, (c + 1) * L)
        heads = [slice(h * dh, (h + 1) * dh) for h in range(N_HEADS)]
        qk = [_dot_nt(q_s[prev, rows, hc], k_s[prev, rows, hc]) for hc in heads]
        states = [cst_s[h] for h in range(N_HEADS)]
        qc = [_dot(q_s[prev, rows, hc], states[h].astype(BF16)) for h, hc in enumerate(heads)]
        ux = []
        for h, hc in enumerate(heads):
            kw = k_s[prev, rows, hc].astype(F32) * _rep(cole_s[prev, c, h], dh // LANES)
            ux.append(_dot_tn(kw.astype(BF16), vx_s[prev, h, rows, :]))
        svx = []
        for h in range(N_HEADS):
            s_loc = qk[h] * e_s[prev, c, h]
            svx.append(_dot(s_loc.astype(BF16), vx_s[prev, h, rows, :]))
        for h, hc in enumerate(heads):
            m_prev = mst_s[h]
            m_loc = colm_s[prev, c, h]
            inter = colb_s[prev, c, h] + m_prev
            m_t = jnp.maximum(inter, m_loc)
            w_inter = _rep(jnp.exp(inter - m_t), wide)
            w_loc = _rep(jnp.exp(m_loc - m_t), wide)
            nd = w_inter * qc[h] + w_loc * svx[h]
            inv = 1.0 / jnp.maximum(jnp.abs(nd[:, dh:]), jnp.exp(-m_t))
            hh = nd[:, 0:dh] * _rep(inv, dh // LANES)
            hm_ref[rows, hc] = (_head_norm(hh) * og_s[prev, rows, hc]).astype(BF16)
            m_new = m_t[L - 1:L, :]
            decay = _rep(jnp.exp(inter[L - 1:L, :] - m_new), wide)
            grow = _rep(jnp.exp(m_loc[L - 1:L, :] - m_new), wide)
            cst_s[h] = decay * states[h] + grow * ux[h]
            mst_s[h] = m_new

    n_parts = 4
    for i in range(max(n_chunks, n_parts)):
        if i < n_chunks:
            recur(i)
        if i < n_parts:
            project(i)


def _mlstm_prompt_kernel(x_ref, wqkvo_ref, wgt_ref, bgc_ref, mhg_ref,
                         hm_ref, c_ref, n_ref, m_ref,
                         q_s, k_s, vx_s, og_s, e_s, colb_s, colm_s, cole_s, cst_s, mst_s,
                         *, tiles_per_seq):
    dh = cst_s.shape[1]
    g_step = pl.program_id(0)

    @pl.when(g_step == 0)
    def _():
        for ref in (q_s, k_s, vx_s, og_s, e_s, colb_s, colm_s, cole_s):
            ref[1] = jnp.zeros(ref.shape[1:], ref.dtype)

    @pl.when(jnp.logical_or(g_step == 0, (g_step - 1) % tiles_per_seq == 0))
    def _():
        cst_s[...] = jnp.zeros(cst_s.shape, F32)
        mst_s[...] = jnp.zeros(mst_s.shape, F32)

    for parity in (0, 1):
        @pl.when(g_step % 2 == parity)
        def _():
            _mlstm_prompt_step(x_ref, wqkvo_ref, wgt_ref, bgc_ref, mhg_ref, hm_ref,
                               q_s, k_s, vx_s, og_s, e_s, colb_s, colm_s, cole_s, cst_s, mst_s,
                               cur=parity, prev=1 - parity)

    @pl.when(jnp.logical_and(g_step >= 1, (g_step - 1) % tiles_per_seq == tiles_per_seq - 1))
    def _():
        for h in range(N_HEADS):
            state = cst_s[h]
            c_ref[0, h] = state[:, 0:dh]
            n_ref[0, h] = state[:, dh:].T[0:1, :]
            m_ref[0, h] = mst_s[h][:, 0:1]


def _mlstm_prompt(x, wqkvo, wgt, bgc, mhg):
    bsz, t, d = x.shape
    dh = d // N_HEADS
    tm = MLSTM_TILE
    nc = tm // CHUNK
    dx = dh + LANES
    tps = t // tm
    n_tiles = bsz * tps
    seq_block = lambda g: (jnp.maximum(g - 1, 0) // tps, 0, 0, 0)
    hm, c, n, m = pl.pallas_call(
        functools.partial(_mlstm_prompt_kernel, tiles_per_seq=tps),
        grid=(n_tiles + 1,),
        in_specs=[pl.BlockSpec((tm, d), lambda g: (jnp.minimum(g, n_tiles - 1), 0)),
                  _qkvo_spec(d), _const_spec(wgt.shape), _const_spec(bgc.shape),
                  _const_spec(mhg.shape)],
        out_specs=[pl.BlockSpec((tm, d), lambda g: (jnp.maximum(g - 1, 0), 0)),
                   pl.BlockSpec((1, N_HEADS, dh, dh), seq_block),
                   pl.BlockSpec((1, N_HEADS, 1, dh), seq_block),
                   pl.BlockSpec((1, N_HEADS, 1, 1), seq_block)],
        out_shape=[jax.ShapeDtypeStruct((bsz * t, d), BF16),
                   jax.ShapeDtypeStruct((bsz, N_HEADS, dh, dh), F32),
                   jax.ShapeDtypeStruct((bsz, N_HEADS, 1, dh), F32),
                   jax.ShapeDtypeStruct((bsz, N_HEADS, 1, 1), F32)],
        scratch_shapes=[
            pltpu.VMEM((2, tm, d), BF16),
            pltpu.VMEM((2, tm, d), BF16),
            pltpu.VMEM((2, N_HEADS, tm, dx), BF16),
            pltpu.VMEM((2, tm, d), F32),
            pltpu.VMEM((2, nc, N_HEADS, CHUNK, CHUNK), F32),
            pltpu.VMEM((2, nc, N_HEADS, CHUNK, LANES), F32),
            pltpu.VMEM((2, nc, N_HEADS, CHUNK, LANES), F32),
            pltpu.VMEM((2, nc, N_HEADS, CHUNK, LANES), F32),
            pltpu.VMEM((N_HEADS, dh, dx), F32),
            pltpu.VMEM((N_HEADS, 1, LANES), F32)],
        compiler_params=_params(1),
        name="mlstm_prompt",
    )(x.reshape(bsz * t, d), wqkvo, wgt, bgc, mhg)
    return hm, c, n.reshape(bsz, N_HEADS, dh), m.reshape(bsz, N_HEADS)


def _mlstm_sample_kernel(x_ref, wqkvo_ref, wgc_ref, bgr_ref, mhg_ref, c0_ref, n0_ref, m0_ref,
                         hm_ref, c_ref, n_ref, m_ref,
                         q_s, kw_s, v_s, og_s, dec_s, wa_s, wb_s, qc_s, *, n_steps):
    rows, d = q_s.shape
    dh = d // N_HEADS
    nb = rows // n_steps
    bb = SAMPLE_BATCH_BLOCK
    T = n_steps
    i = pl.program_id(0)
    h = pl.program_id(1)
    slab = [slice(t * nb, (t + 1) * nb) for t in range(T)]

    @pl.when(jnp.logical_and(i == 0, h == 0))
    def _():
        x = x_ref[...]
        xh, xl = _split_bf16(x)
        _project_qkvo(xh, wqkvo_ref, mhg_ref, q_s, kw_s, v_s, og_s, d, dh)
        g = _dot(xh, wgc_ref[0]) + _dot(xl, wgc_ref[0]) + _dot(xh, wgc_ref[1]) + bgr_ref[...]
        is_input_gate = lax.broadcasted_iota(jnp.int32, g.shape, 1) < N_HEADS
        g = jnp.where(is_input_gate, g, _log_sigmoid(g))
        for hd in range(N_HEADS):
            hc = slice(hd * dh, (hd + 1) * dh)
            qf = [q_s[slab[t], hc] for t in range(T)]
            kf = [kw_s[slab[t], hc] for t in range(T)]
            vf = [v_s[slab[t], hc] for t in range(T)]
            li = [g[slab[t], hd:hd + 1] for t in range(T)]
            lf = [g[slab[t], N_HEADS + hd:N_HEADS + hd + 1] for t in range(T)]
            m0 = m0_ref[:, hd:hd + 1]
            n0 = n0_ref[:, hc]
            b = [lf[0]]
            for t in range(1, T):
                b.append(b[t - 1] + lf[t])
            a = [li[t] - b[t] for t in range(T)]
            m_new = None
            for t in range(T):
                inter = b[t] + m0
                m_t = inter
                for s in range(t + 1):
                    m_t = jnp.maximum(m_t, b[t] + a[s])
                w_inter = jnp.exp(inter - m_t)
                num = jnp.zeros((nb, dh), F32)
                den = w_inter * jnp.sum(qf[t] * n0, axis=1, keepdims=True)
                for s in range(t + 1):
                    s_w = (jnp.sum(qf[t] * kf[s], axis=1, keepdims=True)
                           * jnp.exp(b[t] + a[s] - m_t))
                    num = num + s_w * vf[s]
                    den = den + s_w
                inv = 1.0 / jnp.maximum(jnp.abs(den), jnp.exp(-m_t))
                wa_s[slab[t], hc] = jnp.broadcast_to(w_inter * inv, (nb, dh))
                wb_s[slab[t], hc] = num * inv
                m_new = m_t
            b_last = b[T - 1]
            decay = jnp.exp(b_last + m0 - m_new)
            n_new = decay * n0
            for s in range(T):
                kw = kf[s] * jnp.exp(a[s] + b_last - m_new)
                kw_s[slab[s], hc] = kw
                n_new = n_new + kw
            n_ref[:, hc] = n_new
            m_ref[:, hd:hd + 1] = m_new
            dec_s[:, hc] = jnp.broadcast_to(decay, (nb, dh))

    r0 = pl.multiple_of(i * bb, bb)
    trow = [pl.ds(pl.multiple_of(t * nb + r0, bb), bb) for t in range(T)]
    hcd = pl.ds(pl.multiple_of(h * dh, dh), dh)
    owner = lax.broadcasted_iota(jnp.int32, (T * bb, dh), 0) % bb
    q_blk = jnp.concatenate([q_s[trow[t], hcd] for t in range(T)], axis=0).astype(BF16)
    kw_blk = jnp.concatenate([kw_s[trow[t], hcd] for t in range(T)], axis=0).astype(BF16)
    v_blk = jnp.concatenate([v_s[trow[t], hcd] for t in range(T)], axis=0)
    c_olds = [c0_ref[bi, 0] for bi in range(bb)]
    reads = [_dot(q_blk, c_olds[bi].astype(BF16)) for bi in range(bb)]
    qc = jnp.zeros((T * bb, dh), F32)
    for bi in range(bb):
        qc = jnp.where(owner == bi, reads[bi], qc)
    for t in range(T):
        qc_s[trow[t], hcd] = qc[t * bb:(t + 1) * bb]
    for bi in range(bb):
        upd = _dot_tn(kw_blk, jnp.where(owner == bi, v_blk, 0.0).astype(BF16))
        c_ref[bi, 0] = dec_s[pl.ds(r0 + bi, 1), hcd] * c_olds[bi] + upd

    @pl.when(jnp.logical_and(i == pl.num_programs(0) - 1, h == N_HEADS - 1))
    def _():
        for hd in range(N_HEADS):
            hc = slice(hd * dh, (hd + 1) * dh)
            for t in range(T):
                hh = wa_s[slab[t], hc] * qc_s[slab[t], hc] + wb_s[slab[t], hc]
                hm_ref[slab[t], hc] = _head_norm(hh) * og_s[slab[t], hc]


def _mlstm_sample(x_tm, wqkvo, wgc, bgr, mhg, c0, n0, m0, n_steps):
    rows, d = x_tm.shape
    nb = rows // n_steps
    dh = d // N_HEADS
    bb = SAMPLE_BATCH_BLOCK
    cblock = pl.BlockSpec((bb, 1, dh, dh), lambda i, h: (i, h, 0, 0))
    whole = lambda shape: pl.BlockSpec(shape, lambda i, h: (0,) * len(shape))
    return pl.pallas_call(
        functools.partial(_mlstm_sample_kernel, n_steps=n_steps),
        grid=(nb // bb, N_HEADS),
        in_specs=[_const_spec(x_tm.shape), _qkvo_spec(d), _const_spec(wgc.shape),
                  _const_spec(bgr.shape), _const_spec(mhg.shape), cblock,
                  _const_spec(n0.shape), _const_spec(m0.shape)],
        out_specs=[whole((rows, d)), cblock, whole((nb, d)), whole((nb, N_HEADS))],
        out_shape=[jax.ShapeDtypeStruct((rows, d), F32),
                   jax.ShapeDtypeStruct(c0.shape, F32),
                   jax.ShapeDtypeStruct((nb, d), F32),
                   jax.ShapeDtypeStruct((nb, N_HEADS), F32)],
        scratch_shapes=[pltpu.VMEM((rows, d), F32),
                        pltpu.VMEM((rows, d), F32),
                        pltpu.VMEM((rows, d), F32),
                        pltpu.VMEM((rows, d), F32),
                        pltpu.VMEM((nb, d), F32),
                        pltpu.VMEM((rows, d), F32),
                        pltpu.VMEM((rows, d), F32),
                        pltpu.VMEM((rows, d), F32)],
        compiler_params=_params(2),
        name="mlstm_sample",
    )(x_tm, wqkvo, wgc, bgr, mhg, c0, n0, m0)


def _merge_kernel(x_ref, m1_ref, hm_ref, wmo_ref, wgm_ref, wo_ref, g_ref, b_ref, o_ref, *, alpha):
    tm = x_ref.shape[0]
    for r0 in range(0, tm, SUB_TILE):
        rows = slice(r0, min(r0 + SUB_TILE, tm))
        x = x_ref[rows, :]
        ym = _dot(hm_ref[rows, :].astype(BF16), wmo_ref[...])
        gate = _dot(x.astype(BF16), wgm_ref[...])
        merged = m1_ref[rows, :] + _sigmoid(gate) * ym
        r = alpha * x + _dot(merged.astype(BF16), wo_ref[...])
        o_ref[rows, :] = _layer_norm(r, g_ref[...], b_ref[...])


def _merge(x, m1, hm, wmo, wgm, wo, g, b, alpha):
    n, d = x.shape
    tm = min(TOKEN_TILE, n)
    tile = pl.BlockSpec((tm, d), lambda i: (i, 0))
    return pl.pallas_call(
        functools.partial(_merge_kernel, alpha=alpha),
        grid=(n // tm,),
        in_specs=[tile, tile, tile, _const_spec(wmo.shape), _gate_spec(d, 1),
                  _const_spec(wo.shape), _const_spec(g.shape), _const_spec(b.shape)],
        out_specs=tile,
        out_shape=jax.ShapeDtypeStruct((n, d), F32),
        compiler_params=_params(1),
        name="merge_ln1",
    )(x, m1, hm, wmo, wgm, wo, g, b)


def _ffn_kernel(x_ref, w1_ref, w2_ref, g_ref, b_ref, o_ref, hid_s, *, alpha):
    tm, dff = hid_s.shape
    sw = 1024 if dff % 1024 == 0 else dff
    for r0 in range(0, tm, SUB_TILE):
        rows = slice(r0, min(r0 + SUB_TILE, tm))
        x = x_ref[rows, :]
        xb = x.astype(BF16)
        for s in range(dff // sw):
            cs = slice(s * sw, (s + 1) * sw)
            hid = jnp.maximum(_dot(xb, w1_ref[:, cs]), 0.0)
            hid_s[rows, cs] = (hid * hid).astype(BF16)
        r = alpha * x + _dot(hid_s[rows, :], w2_ref[...])
        o_ref[rows, :] = _layer_norm(r, g_ref[...], b_ref[...])


def _ffn(x, w1, w2, g, b, alpha):
    n, d = x.shape
    tm = min(TOKEN_TILE, n)
    tile = pl.BlockSpec((tm, d), lambda i: (i, 0))
    return pl.pallas_call(
        functools.partial(_ffn_kernel, alpha=alpha),
        grid=(n // tm,),
        in_specs=[tile, _const_spec(w1.shape), _const_spec(w2.shape), _const_spec(g.shape),
                  _const_spec(b.shape)],
        out_specs=tile,
        out_shape=jax.ShapeDtypeStruct((n, d), F32),
        scratch_shapes=[pltpu.VMEM((tm, w1.shape[1]), BF16)],
        compiler_params=_params(1),
        name="ffn_ln2",
    )(x, w1, w2, g, b)


def _w_in_prep_kernel(a_ref, g_ref, b_ref, wa_ref, wg_ref, wb_ref, *, n_a):
    j = pl.program_id(0)

    @pl.when(j == 0)
    def _():
        wg_ref[...] = g_ref[...]

    @pl.when(j < n_a)
    def _():
        wa_ref[...] = a_ref[...].T.astype(BF16)

    @pl.when(j >= n_a)
    def _():
        wb_ref[...] = b_ref[...].T.astype(BF16)


def _prepare_w_in(w_in_t):
    d = w_in_t.shape[1]
    n_a, n_b = 7, 2
    h2 = 2 * N_HEADS
    off_b = n_a * d + h2
    return pl.pallas_call(
        functools.partial(_w_in_prep_kernel, n_a=n_a),
        grid=(n_a + n_b,),
        in_specs=[pl.BlockSpec((d, d), lambda j: (jnp.minimum(j, n_a - 1), 0)),
                  pl.BlockSpec((pl.Element(h2), pl.Element(d)), lambda j: (n_a * d, 0)),
                  pl.BlockSpec((pl.Element(d), pl.Element(d)),
                               lambda j: (pl.multiple_of(off_b + jnp.maximum(j - n_a, 0) * d, 8),
                                          0))],
        out_specs=[pl.BlockSpec((d, d), lambda j: (0, jnp.minimum(j, n_a - 1))),
                   pl.BlockSpec((h2, d), lambda j: (0, 0)),
                   pl.BlockSpec((d, d), lambda j: (0, jnp.maximum(j - n_a, 0)))],
        out_shape=[jax.ShapeDtypeStruct((d, n_a * d), BF16),
                   jax.ShapeDtypeStruct((h2, d), F32),
                   jax.ShapeDtypeStruct((d, n_b * d), BF16)],
        compiler_params=_params(1),
        name="w_in_prep",
    )(w_in_t, w_in_t, w_in_t)


def _layer_weights(w_in, b_gate, conv_w, w_conv_out, mh_g, w_m_out, w_o, ln1_g, ln1_b,
                   w_ff1, w_ff2, ln2_g, ln2_b):
    d = w_in.shape[0]
    h2 = 2 * N_HEADS
    w_in_t = jnp.swapaxes(w_in, 0, 1)
    w_a, wg_t, w_b = _prepare_w_in(w_in_t)
    wgt_hi, wgt_lo = _split_bf16(wg_t)
    wgt = jnp.stack([jnp.concatenate([wgt_hi, wgt_lo], axis=0),
                     jnp.concatenate([wgt_hi, jnp.zeros_like(wgt_hi)], axis=0)])
    pad = ((0, 0), (0, GATE_PAD - h2))
    wgc = jnp.stack([jnp.pad(wgt_hi.T, pad), jnp.pad(wgt_lo.T, pad)])
    return dict(
        w_a=w_a, w_b=w_b,
        wgt=wgt, wgc=wgc,
        bgc=b_gate.reshape(h2, 1).astype(F32),
        bgr=jnp.pad(b_gate.reshape(1, h2).astype(F32), pad),
        cw=conv_w.astype(F32),
        wco=w_conv_out.astype(BF16), mhg=mh_g.reshape(1, d).astype(F32),
        wmo=w_m_out.astype(BF16), wo=w_o.astype(BF16),
        ln1_g=ln1_g.reshape(1, d), ln1_b=ln1_b.reshape(1, d),
        w1=w_ff1.astype(BF16), w2=w_ff2.astype(BF16),
        ln2_g=ln2_g.reshape(1, d), ln2_b=ln2_b.reshape(1, d))


def _prompt_layer(x, p, alpha):
    bsz, t, d = x.shape
    m1, conv_buf = _conv_branch_prompt(x, p["w_a"], p["cw"], p["wco"], p["w_b"])
    hm, c, n, m = _mlstm_prompt(x, p["w_a"], p["wgt"], p["bgc"], p["mhg"])
    xf = x.reshape(bsz * t, d)
    x1 = _merge(xf, m1.reshape(bsz * t, d), hm, p["wmo"], p["w_b"], p["wo"], p["ln1_g"],
                p["ln1_b"], alpha)
    x2 = _ffn(x1, p["w1"], p["w2"], p["ln2_g"], p["ln2_b"], alpha)
    return x2.reshape(bsz, t, d), conv_buf, c, n, m


def _sample_layer(x_tm, conv_tm, c0, n0, m0, p, alpha, n_steps):
    d = x_tm.shape[1]
    m1, conv_new = _conv_branch_sample(x_tm, conv_tm, p["w_a"], p["cw"], p["wco"], p["w_b"],
                                       n_steps)
    hm, c, n, m = _mlstm_sample(x_tm, p["w_a"], p["wgc"], p["bgr"], p["mhg"], c0,
                                n0.reshape(n0.shape[0], d), m0, n_steps)
    x1 = _merge(x_tm, m1, hm, p["wmo"], p["w_b"], p["wo"], p["ln1_g"], p["ln1_b"], alpha)
    x2 = _ffn(x1, p["w1"], p["w2"], p["ln2_g"], p["ln2_b"], alpha)
    return x2, conv_new, c, n.reshape(n0.shape), m


def kernel(x_prompt, x_sample, state_conv, state_C, state_n, state_m, w_in, b_gate, conv_w,
           w_conv_out, mh_g, w_m_out, w_o, ln1_g, ln1_b, w_ff1, w_ff2, ln2_g, ln2_b):
    depth = w_in.shape[0]
    alpha = (2.0 * depth) ** 0.25
    bsz, t, d = x_prompt.shape
    sb, st, _ = x_sample.shape
    assert t % TOKEN_TILE == 0 and t % MLSTM_TILE == 0 and MLSTM_TILE % CHUNK == 0
    assert d % N_HEADS == 0
    assert sb % SAMPLE_BATCH_BLOCK == 0 and st >= CONV_W - 1

    xp = x_prompt
    xs = jnp.transpose(x_sample, (1, 0, 2)).reshape(st * sb, d)
    outs = [[] for _ in range(8)]
    for l in range(depth):
        p = _layer_weights(w_in[l], b_gate[l], conv_w[l], w_conv_out[l], mh_g[l], w_m_out[l],
                           w_o[l], ln1_g[l], ln1_b[l], w_ff1[l], w_ff2[l], ln2_g[l], ln2_b[l])
        xp, cp, c_p, n_p, m_p = _prompt_layer(xp, p, alpha)
        conv_tm = jnp.transpose(state_conv[l], (1, 0, 2)).reshape((CONV_W - 1) * sb, d)
        xs, cs_tm, c_s, n_s, m_s = _sample_layer(xs, conv_tm, state_C[l], state_n[l], state_m[l],
                                                 p, alpha, st)
        cs = jnp.transpose(cs_tm.reshape(CONV_W - 1, sb, d), (1, 0, 2))
        for acc, val in zip(outs, (cp, cs, c_p, c_s, n_p, n_s, m_p, m_s)):
            acc.append(val)
    ys = jnp.transpose(xs.reshape(st, sb, d), (1, 0, 2))
    return (xp, ys) + tuple(jnp.stack(acc) for acc in outs)
```

```python
import functools

import jax
import jax.numpy as jnp
from jax import lax
from jax.experimental import pallas as pl
from jax.experimental.pallas import tpu as pltpu

F32 = jnp.float32
BF16 = jnp.bfloat16

LN_EPS = 1e-5
N_HEADS = 4
CHUNK = 128
CONV_W = 3
TOKEN_TILE = 1024
SUB_TILE = 512
MLSTM_TILE = 512
SAMPLE_BATCH_BLOCK = 16
LANES = 128
GATE_PAD = LANES
VMEM_LIMIT = 56 * 1024 * 1024


def _dot(a, b):
    return jnp.dot(a, b, preferred_element_type=F32)


def _dot_nt(a, b):
    return lax.dot_general(a, b, (((1,), (1,)), ((), ())), preferred_element_type=F32)


def _dot_tn(a, b):
    return lax.dot_general(a, b, (((0,), (0,)), ((), ())), preferred_element_type=F32)


def _sigmoid(x):
    return 1.0 / (1.0 + jnp.exp(-x))


def _log_sigmoid(x):
    return jnp.minimum(x, 0.0) - jnp.log1p(jnp.exp(-jnp.abs(x)))


def _split_bf16(x):
    hi = x.astype(BF16)
    lo = (x - hi.astype(F32)).astype(BF16)
    return hi, lo


def _layer_norm(r, g, b):
    mu = jnp.mean(r, axis=-1, keepdims=True)
    xc = r - mu
    var = jnp.mean(xc * xc, axis=-1, keepdims=True)
    return xc * lax.rsqrt(var + LN_EPS) * g + b


def _head_norm(h):
    mu = jnp.mean(h, axis=-1, keepdims=True)
    hc = h - mu
    return hc * lax.rsqrt(jnp.mean(hc * hc, axis=-1, keepdims=True) + LN_EPS)


def _const_spec(shape):
    zeros = (0,) * len(shape)
    return pl.BlockSpec(shape, lambda *_: zeros, pipeline_mode=pl.Buffered(1))


def _window_spec(block_shape, index):
    return pl.BlockSpec(block_shape, lambda *_: index, pipeline_mode=pl.Buffered(1))


def _bch_spec(d):
    return _window_spec((d, 3 * d), (0, 0))


def _qkvo_spec(d):
    return _window_spec((pl.Element(d), pl.Element(4 * d)), (0, 3 * d))


def _gate_spec(d, which):
    return _window_spec((d, d), (0, which))


def _params(n_axes):
    return pltpu.CompilerParams(dimension_semantics=("arbitrary",) * n_axes,
                                vmem_limit_bytes=VMEM_LIMIT)


def _conv_stripes(d):
    sw = 512 if d % 512 == 0 else d
    return [slice(s * sw, (s + 1) * sw) for s in range(d // sw)]


def _conv_prompt_kernel(x_ref, wbch_ref, cw_ref, wco_ref, wgc_ref, m1_ref, cs_ref, u_s, a_s):
    tm, d = a_s.shape
    @pl.when(pl.program_id(1) == 0)
    def _():
        u_s[0:8, :] = jnp.zeros((8, d), F32)

    for r0 in range(0, tm, SUB_TILE):
        sub = min(SUB_TILE, tm - r0)
        xb = x_ref[0, r0:r0 + sub, :].astype(BF16)
        for cs in _conv_stripes(d):
            off = cs.start
            bg = _dot(xb, wbch_ref[:, off:cs.stop])
            cg = _dot(xb, wbch_ref[:, d + off:d + cs.stop])
            hc = _dot(xb, wbch_ref[:, 2 * d + off:2 * d + cs.stop])
            u = cg * hc
            u_s[8 + r0:8 + r0 + sub, cs] = u
            conv = (u_s[6 + r0:6 + r0 + sub, cs] * cw_ref[0:1, cs]
                    + u_s[7 + r0:7 + r0 + sub, cs] * cw_ref[1:2, cs] + u * cw_ref[2:3, cs])
            a_s[r0:r0 + sub, cs] = (bg * conv).astype(BF16)
        gate = _dot(xb, wgc_ref[...])
        yc = _dot(a_s[r0:r0 + sub, :], wco_ref[...])
        m1_ref[0, r0:r0 + sub, :] = _sigmoid(gate) * yc
    cs_ref[0] = u_s[tm + 6:tm + 8, :]
    u_s[0:8, :] = u_s[tm:tm + 8, :]


def _conv_sample_kernel(x_ref, st_ref, wbch_ref, cw_ref, wco_ref, wgc_ref, m1_ref, cs_ref, a_s,
                        *, n_steps):
    rows, d = a_s.shape
    nb = rows // n_steps
    xb = x_ref[...].astype(BF16)
    for cs in _conv_stripes(d):
        off = cs.start
        bg = _dot(xb, wbch_ref[:, off:cs.stop])
        cg = _dot(xb, wbch_ref[:, d + off:d + cs.stop])
        hc = _dot(xb, wbch_ref[:, 2 * d + off:2 * d + cs.stop])
        u = cg * hc
        up = [st_ref[j * nb:(j + 1) * nb, cs] for j in range(CONV_W - 1)]
        up += [u[t * nb:(t + 1) * nb] for t in range(n_steps)]
        for t in range(n_steps):
            conv = (up[t] * cw_ref[0:1, cs] + up[t + 1] * cw_ref[1:2, cs]
                    + up[t + 2] * cw_ref[2:3, cs])
            a_s[t * nb:(t + 1) * nb, cs] = (bg[t * nb:(t + 1) * nb] * conv).astype(BF16)
        for j in range(CONV_W - 1):
            cs_ref[j * nb:(j + 1) * nb, cs] = up[n_steps + j]
    yc = _dot(a_s[...], wco_ref[...])
    m1_ref[...] = _sigmoid(_dot(xb, wgc_ref[...])) * yc


def _conv_branch_prompt(x, wbch, cw, wco, wgc):
    bsz, t, d = x.shape
    tm = TOKEN_TILE
    return pl.pallas_call(
        _conv_prompt_kernel,
        grid=(bsz, t // tm),
        in_specs=[pl.BlockSpec((1, tm, d), lambda b, j: (b, j, 0)),
                  _bch_spec(d), _const_spec(cw.shape), _const_spec(wco.shape),
                  _gate_spec(d, 0)],
        out_specs=[pl.BlockSpec((1, tm, d), lambda b, j: (b, j, 0)),
                   pl.BlockSpec((1, CONV_W - 1, d), lambda b, j: (b, 0, 0))],
        out_shape=[jax.ShapeDtypeStruct((bsz, t, d), F32),
                   jax.ShapeDtypeStruct((bsz, CONV_W - 1, d), F32)],
        scratch_shapes=[pltpu.VMEM((tm + 8, d), F32), pltpu.VMEM((tm, d), BF16)],
        compiler_params=_params(2),
        name="conv_branch_prompt",
    )(x, wbch, cw, wco, wgc)


def _conv_branch_sample(x_tm, st_tm, wbch, cw, wco, wgc, n_steps):
    rows, d = x_tm.shape
    return pl.pallas_call(
        functools.partial(_conv_sample_kernel, n_steps=n_steps),
        grid=(1,),
        in_specs=[_const_spec(x_tm.shape), _const_spec(st_tm.shape), _bch_spec(d),
                  _const_spec(cw.shape), _const_spec(wco.shape), _gate_spec(d, 0)],
        out_specs=[pl.BlockSpec((rows, d), lambda i: (0, 0)),
                   pl.BlockSpec(st_tm.shape, lambda i: (0, 0))],
        out_shape=[jax.ShapeDtypeStruct((rows, d), F32),
                   jax.ShapeDtypeStruct(st_tm.shape, F32)],
        scratch_shapes=[pltpu.VMEM((rows, d), BF16)],
        compiler_params=_params(1),
        name="conv_branch_sample",
    )(x_tm, st_tm, wbch, cw, wco, wgc)


def _project_qkvo(xh, wqkvo_ref, mhg_ref, q_s, k_s, v_s, og_s, d, dh):
    q_s[...] = _dot(xh, wqkvo_ref[:, 0:d]).astype(BF16).astype(q_s.dtype)
    k_s[...] = (_dot(xh, wqkvo_ref[:, d:2 * d]) * (dh ** -0.5)).astype(BF16).astype(k_s.dtype)
    v_s[...] = _dot(xh, wqkvo_ref[:, 2 * d:3 * d]).astype(BF16).astype(v_s.dtype)
    og_s[...] = mhg_ref[...] * _sigmoid(_dot(xh, wqkvo_ref[:, 3 * d:4 * d]))


def _rep(col, times):
    return col if times == 1 else jnp.concatenate([col] * times, axis=1)


def _mlstm_prompt_step(x_ref, wqkvo_ref, wgt_ref, bgc_ref, mhg_ref, hm_ref,
                       q_s, k_s, vx_s, og_s, e_s, colb_s, colm_s, cole_s, cst_s, mst_s,
                       *, cur, prev):
    _, tm, d = q_s.shape
    dh = d // N_HEADS
    L = CHUNK
    LANES = colb_s.shape[-1]
    n_chunks = tm // L
    wide = (dh + LANES) // LANES

    x = x_ref[...]
    xh, xl = _split_bf16(x)

    ga = _dot_nt(wgt_ref[0], xh)
    gb = _dot_nt(wgt_ref[1], xl)
    gt = ga[0:8] + ga[8:16] + gb[0:8] + bgc_ref[...]
    is_input_gate = lax.broadcasted_iota(jnp.int32, gt.shape, 0) < N_HEADS
    g = jnp.where(is_input_gate, gt, _log_sigmoid(gt))

    pos = lax.broadcasted_iota(jnp.int32, g.shape, 1) % L
    csum = g
    shift = 1
    while shift < L:
        csum = csum + jnp.where(pos >= shift, pltpu.roll(csum, shift, 1), 0.0)
        shift *= 2
    ba = jnp.concatenate([csum[N_HEADS:], g[:N_HEADS] - csum[N_HEADS:]], axis=0)

    ri = lax.broadcasted_iota(jnp.int32, (L, L), 0)
    ci = lax.broadcasted_iota(jnp.int32, (L, L), 1)
    causal = ri >= ci

    for c in range(n_chunks):
        slab = ba[:, c * L:(c + 1) * L]
        cols = slab.T
        for h in range(N_HEADS):
            b_rep = jnp.broadcast_to(cols[:, h:h + 1], (L, LANES))
            a_rep = jnp.broadcast_to(cols[:, N_HEADS + h:N_HEADS + h + 1], (L, LANES))
            a_row = slab[N_HEADS + h:N_HEADS + h + 1, :]
            dm = jnp.where(causal, b_rep + a_row, -jnp.inf)
            m_loc = jnp.broadcast_to(jnp.max(dm, axis=1, keepdims=True), (L, LANES))
            e_s[cur, c, h] = jnp.exp(dm - m_loc)
            colb_s[cur, c, h] = b_rep
            colm_s[cur, c, h] = m_loc
            cole_s[cur, c, h] = jnp.exp(a_rep + b_rep[L - 1:L, :] - m_loc[L - 1:L, :])

    def project(part):
        w = wqkvo_ref[:, part * d:(part + 1) * d]
        if part == 0:
            q_s[cur] = _dot(xh, w).astype(BF16)
        elif part == 1:
            k_s[cur] = (_dot(xh, w) * (dh ** -0.5)).astype(BF16)
        elif part == 2:
            v = _dot(xh, w).astype(BF16)
            for h in range(N_HEADS):
                vx_s[cur, h, :, 0:dh] = v[:, h * dh:(h + 1) * dh]
                vx_s[cur, h, :, dh:] = jnp.ones((tm, LANES), BF16)
        else:
            og_s[cur] = mhg_ref[...] * _sigmoid(_dot(xh, w))

    def recur(c):
        rows = slice(c * L, (c + 1) * L)
        heads = [slice(h * dh, (h + 1) * dh) for h in range(N_HEADS)]
        qk = [_dot_nt(q_s[prev, rows, hc], k_s[prev, rows, hc]) for hc in heads]
        states = [cst_s[h] for h in range(N_HEADS)]
        qc = [_dot(q_s[prev, rows, hc], states[h].astype(BF16)) for h, hc in enumerate(heads)]
        ux = []
        for h, hc in enumerate(heads):
            kw = k_s[prev, rows, hc].astype(F32) * _rep(cole_s[prev, c, h], dh // LANES)
            ux.append(_dot_tn(kw.astype(BF16), vx_s[prev, h, rows, :]))
        svx = []
        for h in range(N_HEADS):
            s_loc = qk[h] * e_s[prev, c, h]
            svx.append(_dot(s_loc.astype(BF16), vx_s[prev, h, rows, :]))
        for h, hc in enumerate(heads):
            m_prev = mst_s[h]
            m_loc = colm_s[prev, c, h]
            inter = colb_s[prev, c, h] + m_prev
            m_t = jnp.maximum(inter, m_loc)
            w_inter = _rep(jnp.exp(inter - m_t), wide)
            w_loc = _rep(jnp.exp(m_loc - m_t), wide)
            nd = w_inter * qc[h] + w_loc * svx[h]
            inv = 1.0 / jnp.maximum(jnp.abs(nd[:, dh:]), jnp.exp(-m_t))
            hh = nd[:, 0:dh] * _rep(inv, dh // LANES)
            hm_ref[rows, hc] = (_head_norm(hh) * og_s[prev, rows, hc]).astype(BF16)
            m_new = m_t[L - 1:L, :]
            decay = _rep(jnp.exp(inter[L - 1:L, :] - m_new), wide)
            grow = _rep(jnp.exp(m_loc[L - 1:L, :] - m_new), wide)
            cst_s[h] = decay * states[h] + grow * ux[h]
            mst_s[h] = m_new

    n_parts = 4
    for i in range(max(n_chunks, n_parts)):
        if i < n_chunks:
            recur(i)
        if i < n_parts:
            project(i)


def _mlstm_prompt_kernel(x_ref, wqkvo_ref, wgt_ref, bgc_ref, mhg_ref,
                         hm_ref, c_ref, n_ref, m_ref,
                         q_s, k_s, vx_s, og_s, e_s, colb_s, colm_s, cole_s, cst_s, mst_s,
                         *, tiles_per_seq):
    dh = cst_s.shape[1]
    g_step = pl.program_id(0)

    @pl.when(g_step == 0)
    def _():
        for ref in (q_s, k_s, vx_s, og_s, e_s, colb_s, colm_s, cole_s):
            ref[1] = jnp.zeros(ref.shape[1:], ref.dtype)

    @pl.when(jnp.logical_or(g_step == 0, (g_step - 1) % tiles_per_seq == 0))
    def _():
        cst_s[...] = jnp.zeros(cst_s.shape, F32)
        mst_s[...] = jnp.zeros(mst_s.shape, F32)

    for parity in (0, 1):
        @pl.when(g_step % 2 == parity)
        def _():
            _mlstm_prompt_step(x_ref, wqkvo_ref, wgt_ref, bgc_ref, mhg_ref, hm_ref,
                               q_s, k_s, vx_s, og_s, e_s, colb_s, colm_s, cole_s, cst_s, mst_s,
                               cur=parity, prev=1 - parity)

    @pl.when(jnp.logical_and(g_step >= 1, (g_step - 1) % tiles_per_seq == tiles_per_seq - 1))
    def _():
        for h in range(N_HEADS):
            state = cst_s[h]
            c_ref[0, h] = state[:, 0:dh]
            n_ref[0, h] = state[:, dh:].T[0:1, :]
            m_ref[0, h] = mst_s[h][:, 0:1]


def _mlstm_prompt(x, wqkvo, wgt, bgc, mhg):
    bsz, t, d = x.shape
    dh = d // N_HEADS
    tm = MLSTM_TILE
    nc = tm // CHUNK
    dx = dh + LANES
    tps = t // tm
    n_tiles = bsz * tps
    seq_block = lambda g: (jnp.maximum(g - 1, 0) // tps, 0, 0, 0)
    hm, c, n, m = pl.pallas_call(
        functools.partial(_mlstm_prompt_kernel, tiles_per_seq=tps),
        grid=(n_tiles + 1,),
        in_specs=[pl.BlockSpec((tm, d), lambda g: (jnp.minimum(g, n_tiles - 1), 0)),
                  _qkvo_spec(d), _const_spec(wgt.shape), _const_spec(bgc.shape),
                  _const_spec(mhg.shape)],
        out_specs=[pl.BlockSpec((tm, d), lambda g: (jnp.maximum(g - 1, 0), 0)),
                   pl.BlockSpec((1, N_HEADS, dh, dh), seq_block),
                   pl.BlockSpec((1, N_HEADS, 1, dh), seq_block),
                   pl.BlockSpec((1, N_HEADS, 1, 1), seq_block)],
        out_shape=[jax.ShapeDtypeStruct((bsz * t, d), BF16),
                   jax.ShapeDtypeStruct((bsz, N_HEADS, dh, dh), F32),
                   jax.ShapeDtypeStruct((bsz, N_HEADS, 1, dh), F32),
                   jax.ShapeDtypeStruct((bsz, N_HEADS, 1, 1), F32)],
        scratch_shapes=[
            pltpu.VMEM((2, tm, d), BF16),
            pltpu.VMEM((2, tm, d), BF16),
            pltpu.VMEM((2, N_HEADS, tm, dx), BF16),
            pltpu.VMEM((2, tm, d), F32),
            pltpu.VMEM((2, nc, N_HEADS, CHUNK, CHUNK), F32),
            pltpu.VMEM((2, nc, N_HEADS, CHUNK, LANES), F32),
            pltpu.VMEM((2, nc, N_HEADS, CHUNK, LANES), F32),
            pltpu.VMEM((2, nc, N_HEADS, CHUNK, LANES), F32),
            pltpu.VMEM((N_HEADS, dh, dx), F32),
            pltpu.VMEM((N_HEADS, 1, LANES), F32)],
        compiler_params=_params(1),
        name="mlstm_prompt",
    )(x.reshape(bsz * t, d), wqkvo, wgt, bgc, mhg)
    return hm, c, n.reshape(bsz, N_HEADS, dh), m.reshape(bsz, N_HEADS)


def _mlstm_sample_kernel(x_ref, wqkvo_ref, wgc_ref, bgr_ref, mhg_ref, c0_ref, n0_ref, m0_ref,
                         hm_ref, c_ref, n_ref, m_ref,
                         q_s, kw_s, v_s, og_s, dec_s, wa_s, wb_s, qc_s, *, n_steps):
    rows, d = q_s.shape
    dh = d // N_HEADS
    nb = rows // n_steps
    bb = SAMPLE_BATCH_BLOCK
    T = n_steps
    i = pl.program_id(0)
    h = pl.program_id(1)
    slab = [slice(t * nb, (t + 1) * nb) for t in range(T)]

    @pl.when(jnp.logical_and(i == 0, h == 0))
    def _():
        x = x_ref[...]
        xh, xl = _split_bf16(x)
        _project_qkvo(xh, wqkvo_ref, mhg_ref, q_s, kw_s, v_s, og_s, d, dh)
        g = _dot(xh, wgc_ref[0]) + _dot(xl, wgc_ref[0]) + _dot(xh, wgc_ref[1]) + bgr_ref[...]
        is_input_gate = lax.broadcasted_iota(jnp.int32, g.shape, 1) < N_HEADS
        g = jnp.where(is_input_gate, g, _log_sigmoid(g))
        for hd in range(N_HEADS):
            hc = slice(hd * dh, (hd + 1) * dh)
            qf = [q_s[slab[t], hc] for t in range(T)]
            kf = [kw_s[slab[t], hc] for t in range(T)]
            vf = [v_s[slab[t], hc] for t in range(T)]
            li = [g[slab[t], hd:hd + 1] for t in range(T)]
            lf = [g[slab[t], N_HEADS + hd:N_HEADS + hd + 1] for t in range(T)]
            m0 = m0_ref[:, hd:hd + 1]
            n0 = n0_ref[:, hc]
            b = [lf[0]]
            for t in range(1, T):
                b.append(b[t - 1] + lf[t])
            a = [li[t] - b[t] for t in range(T)]
            m_new = None
            for t in range(T):
                inter = b[t] + m0
                m_t = inter
                for s in range(t + 1):
                    m_t = jnp.maximum(m_t, b[t] + a[s])
                w_inter = jnp.exp(inter - m_t)
                num = jnp.zeros((nb, dh), F32)
                den = w_inter * jnp.sum(qf[t] * n0, axis=1, keepdims=True)
                for s in range(t + 1):
                    s_w = (jnp.sum(qf[t] * kf[s], axis=1, keepdims=True)
                           * jnp.exp(b[t] + a[s] - m_t))
                    num = num + s_w * vf[s]
                    den = den + s_w
                inv = 1.0 / jnp.maximum(jnp.abs(den), jnp.exp(-m_t))
                wa_s[slab[t], hc] = jnp.broadcast_to(w_inter * inv, (nb, dh))
                wb_s[slab[t], hc] = num * inv
                m_new = m_t
            b_last = b[T - 1]
            decay = jnp.exp(b_last + m0 - m_new)
            n_new = decay * n0
            for s in range(T):
                kw = kf[s] * jnp.exp(a[s] + b_last - m_new)
                kw_s[slab[s], hc] = kw
                n_new = n_new + kw
            n_ref[:, hc] = n_new
            m_ref[:, hd:hd + 1] = m_new
            dec_s[:, hc] = jnp.broadcast_to(decay, (nb, dh))

    r0 = pl.multiple_of(i * bb, bb)
    trow = [pl.ds(pl.multiple_of(t * nb + r0, bb), bb) for t in range(T)]
    hcd = pl.ds(pl.multiple_of(h * dh, dh), dh)
    owner = lax.broadcasted_iota(jnp.int32, (T * bb, dh), 0) % bb
    q_blk = jnp.concatenate([q_s[trow[t], hcd] for t in range(T)], axis=0).astype(BF16)
    kw_blk = jnp.concatenate([kw_s[trow[t], hcd] for t in range(T)], axis=0).astype(BF16)
    v_blk = jnp.concatenate([v_s[trow[t], hcd] for t in range(T)], axis=0)
    c_olds = [c0_ref[bi, 0] for bi in range(bb)]
    reads = [_dot(q_blk, c_olds[bi].astype(BF16)) for bi in range(bb)]
    qc = jnp.zeros((T * bb, dh), F32)
    for bi in range(bb):
        qc = jnp.where(owner == bi, reads[bi], qc)
    for t in range(T):
        qc_s[trow[t], hcd] = qc[t * bb:(t + 1) * bb]
    for bi in range(bb):
        upd = _dot_tn(kw_blk, jnp.where(owner == bi, v_blk, 0.0).astype(BF16))
        c_ref[bi, 0] = dec_s[pl.ds(r0 + bi, 1), hcd] * c_olds[bi] + upd

    @pl.when(jnp.logical_and(i == pl.num_programs(0) - 1, h == N_HEADS - 1))
    def _():
        for hd in range(N_HEADS):
            hc = slice(hd * dh, (hd + 1) * dh)
            for t in range(T):
                hh = wa_s[slab[t], hc] * qc_s[slab[t], hc] + wb_s[slab[t], hc]
                hm_ref[slab[t], hc] = _head_norm(hh) * og_s[slab[t], hc]


def _mlstm_sample(x_tm, wqkvo, wgc, bgr, mhg, c0, n0, m0, n_steps):
    rows, d = x_tm.shape
    nb = rows // n_steps
    dh = d // N_HEADS
    bb = SAMPLE_BATCH_BLOCK
    cblock = pl.BlockSpec((bb, 1, dh, dh), lambda i, h: (i, h, 0, 0))
    whole = lambda shape: pl.BlockSpec(shape, lambda i, h: (0,) * len(shape))
    return pl.pallas_call(
        functools.partial(_mlstm_sample_kernel, n_steps=n_steps),
        grid=(nb // bb, N_HEADS),
        in_specs=[_const_spec(x_tm.shape), _qkvo_spec(d), _const_spec(wgc.shape),
                  _const_spec(bgr.shape), _const_spec(mhg.shape), cblock,
                  _const_spec(n0.shape), _const_spec(m0.shape)],
        out_specs=[whole((rows, d)), cblock, whole((nb, d)), whole((nb, N_HEADS))],
        out_shape=[jax.ShapeDtypeStruct((rows, d), F32),
                   jax.ShapeDtypeStruct(c0.shape, F32),
                   jax.ShapeDtypeStruct((nb, d), F32),
                   jax.ShapeDtypeStruct((nb, N_HEADS), F32)],
        scratch_shapes=[pltpu.VMEM((rows, d), F32),
                        pltpu.VMEM((rows, d), F32),
                        pltpu.VMEM((rows, d), F32),
                        pltpu.VMEM((rows, d), F32),
                        pltpu.VMEM((nb, d), F32),
                        pltpu.VMEM((rows, d), F32),
                        pltpu.VMEM((rows, d), F32),
                        pltpu.VMEM((rows, d), F32)],
        compiler_params=_params(2),
        name="mlstm_sample",
    )(x_tm, wqkvo, wgc, bgr, mhg, c0, n0, m0)


def _merge_kernel(x_ref, m1_ref, hm_ref, wmo_ref, wgm_ref, wo_ref, g_ref, b_ref, o_ref, *, alpha):
    tm = x_ref.shape[0]
    for r0 in range(0, tm, SUB_TILE):
        rows = slice(r0, min(r0 + SUB_TILE, tm))
        x = x_ref[rows, :]
        ym = _dot(hm_ref[rows, :].astype(BF16), wmo_ref[...])
        gate = _dot(x.astype(BF16), wgm_ref[...])
        merged = m1_ref[rows, :] + _sigmoid(gate) * ym
        r = alpha * x + _dot(merged.astype(BF16), wo_ref[...])
        o_ref[rows, :] = _layer_norm(r, g_ref[...], b_ref[...])


def _merge(x, m1, hm, wmo, wgm, wo, g, b, alpha):
    n, d = x.shape
    tm = min(TOKEN_TILE, n)
    tile = pl.BlockSpec((tm, d), lambda i: (i, 0))
    return pl.pallas_call(
        functools.partial(_merge_kernel, alpha=alpha),
        grid=(n // tm,),
        in_specs=[tile, tile, tile, _const_spec(wmo.shape), _gate_spec(d, 1),
                  _const_spec(wo.shape), _const_spec(g.shape), _const_spec(b.shape)],
        out_specs=tile,
        out_shape=jax.ShapeDtypeStruct((n, d), F32),
        compiler_params=_params(1),
        name="merge_ln1",
    )(x, m1, hm, wmo, wgm, wo, g, b)


def _ffn_kernel(x_ref, w1_ref, w2_ref, g_ref, b_ref, o_ref, hid_s, *, alpha):
    tm, dff = hid_s.shape
    sw = 1024 if dff % 1024 == 0 else dff
    for r0 in range(0, tm, SUB_TILE):
        rows = slice(r0, min(r0 + SUB_TILE, tm))
        x = x_ref[rows, :]
        xb = x.astype(BF16)
        for s in range(dff // sw):
            cs = slice(s * sw, (s + 1) * sw)
            hid = jnp.maximum(_dot(xb, w1_ref[:, cs]), 0.0)
            hid_s[rows, cs] = (hid * hid).astype(BF16)
        r = alpha * x + _dot(hid_s[rows, :], w2_ref[...])
        o_ref[rows, :] = _layer_norm(r, g_ref[...], b_ref[...])


def _ffn(x, w1, w2, g, b, alpha):
    n, d = x.shape
    tm = min(TOKEN_TILE, n)
    tile = pl.BlockSpec((tm, d), lambda i: (i, 0))
    return pl.pallas_call(
        functools.partial(_ffn_kernel, alpha=alpha),
        grid=(n // tm,),
        in_specs=[tile, _const_spec(w1.shape), _const_spec(w2.shape), _const_spec(g.shape),
                  _const_spec(b.shape)],
        out_specs=tile,
        out_shape=jax.ShapeDtypeStruct((n, d), F32),
        scratch_shapes=[pltpu.VMEM((tm, w1.shape[1]), BF16)],
        compiler_params=_params(1),
        name="ffn_ln2",
    )(x, w1, w2, g, b)


def _w_in_prep_kernel(a_ref, g_ref, b_ref, wa_ref, wg_ref, wb_ref, *, n_a):
    j = pl.program_id(0)

    @pl.when(j == 0)
    def _():
        wg_ref[...] = g_ref[...]

    @pl.when(j < n_a)
    def _():
        wa_ref[...] = a_ref[...].T.astype(BF16)

    @pl.when(j >= n_a)
    def _():
        wb_ref[...] = b_ref[...].T.astype(BF16)


def _prepare_w_in(w_in_t):
    d = w_in_t.shape[1]
    n_a, n_b = 7, 2
    h2 = 2 * N_HEADS
    off_b = n_a * d + h2
    return pl.pallas_call(
        functools.partial(_w_in_prep_kernel, n_a=n_a),
        grid=(n_a + n_b,),
        in_specs=[pl.BlockSpec((d, d), lambda j: (jnp.minimum(j, n_a - 1), 0)),
                  pl.BlockSpec((pl.Element(h2), pl.Element(d)), lambda j: (n_a * d, 0)),
                  pl.BlockSpec((pl.Element(d), pl.Element(d)),
                               lambda j: (pl.multiple_of(off_b + jnp.maximum(j - n_a, 0) * d, 8),
                                          0))],
        out_specs=[pl.BlockSpec((d, d), lambda j: (0, jnp.minimum(j, n_a - 1))),
                   pl.BlockSpec((h2, d), lambda j: (0, 0)),
                   pl.BlockSpec((d, d), lambda j: (0, jnp.maximum(j - n_a, 0)))],
        out_shape=[jax.ShapeDtypeStruct((d, n_a * d), BF16),
                   jax.ShapeDtypeStruct((h2, d), F32),
                   jax.ShapeDtypeStruct((d, n_b * d), BF16)],
        compiler_params=_params(1),
        name="w_in_prep",
    )(w_in_t, w_in_t, w_in_t)


def _layer_weights(w_in, b_gate, conv_w, w_conv_out, mh_g, w_m_out, w_o, ln1_g, ln1_b,
                   w_ff1, w_ff2, ln2_g, ln2_b):
    d = w_in.shape[0]
    h2 = 2 * N_HEADS
    w_in_t = jnp.swapaxes(w_in, 0, 1)
    w_a, wg_t, w_b = _prepare_w_in(w_in_t)
    wgt_hi, wgt_lo = _split_bf16(wg_t)
    wgt = jnp.stack([jnp.concatenate([wgt_hi, wgt_lo], axis=0),
                     jnp.concatenate([wgt_hi, jnp.zeros_like(wgt_hi)], axis=0)])
    pad = ((0, 0), (0, GATE_PAD - h2))
    wgc = jnp.stack([jnp.pad(wgt_hi.T, pad), jnp.pad(wgt_lo.T, pad)])
    return dict(
        w_a=w_a, w_b=w_b,
        wgt=wgt, wgc=wgc,
        bgc=b_gate.reshape(h2, 1).astype(F32),
        bgr=jnp.pad(b_gate.reshape(1, h2).astype(F32), pad),
        cw=conv_w.astype(F32),
        wco=w_conv_out.astype(BF16), mhg=mh_g.reshape(1, d).astype(F32),
        wmo=w_m_out.astype(BF16), wo=w_o.astype(BF16),
        ln1_g=ln1_g.reshape(1, d), ln1_b=ln1_b.reshape(1, d),
        w1=w_ff1.astype(BF16), w2=w_ff2.astype(BF16),
        ln2_g=ln2_g.reshape(1, d), ln2_b=ln2_b.reshape(1, d))


def _prompt_layer(x, p, alpha):
    bsz, t, d = x.shape
    m1, conv_buf = _conv_branch_prompt(x, p["w_a"], p["cw"], p["wco"], p["w_b"])
    hm, c, n, m = _mlstm_prompt(x, p["w_a"], p["wgt"], p["bgc"], p["mhg"])
    xf = x.reshape(bsz * t, d)
    x1 = _merge(xf, m1.reshape(bsz * t, d), hm, p["wmo"], p["w_b"], p["wo"], p["ln1_g"],
                p["ln1_b"], alpha)
    x2 = _ffn(x1, p["w1"], p["w2"], p["ln2_g"], p["ln2_b"], alpha)
    return x2.reshape(bsz, t, d), conv_buf, c, n, m


def _sample_layer(x_tm, conv_tm, c0, n0, m0, p, alpha, n_steps):
    d = x_tm.shape[1]
    m1, conv_new = _conv_branch_sample(x_tm, conv_tm, p["w_a"], p["cw"], p["wco"], p["w_b"],
                                       n_steps)
    hm, c, n, m = _mlstm_sample(x_tm, p["w_a"], p["wgc"], p["bgr"], p["mhg"], c0,
                                n0.reshape(n0.shape[0], d), m0, n_steps)
    x1 = _merge(x_tm, m1, hm, p["wmo"], p["w_b"], p["wo"], p["ln1_g"], p["ln1_b"], alpha)
    x2 = _ffn(x1, p["w1"], p["w2"], p["ln2_g"], p["ln2_b"], alpha)
    return x2, conv_new, c, n.reshape(n0.shape), m


def kernel(x_prompt, x_sample, state_conv, state_C, state_n, state_m, w_in, b_gate, conv_w,
           w_conv_out, mh_g, w_m_out, w_o, ln1_g, ln1_b, w_ff1, w_ff2, ln2_g, ln2_b):
    depth = w_in.shape[0]
    alpha = (2.0 * depth) ** 0.25
    bsz, t, d = x_prompt.shape
    sb, st, _ = x_sample.shape
    assert t % TOKEN_TILE == 0 and t % MLSTM_TILE == 0 and MLSTM_TILE % CHUNK == 0
    assert d % N_HEADS == 0
    assert sb % SAMPLE_BATCH_BLOCK == 0 and st >= CONV_W - 1

    xp = x_prompt
    xs = jnp.transpose(x_sample, (1, 0, 2)).reshape(st * sb, d)
    outs = [[] for _ in range(8)]
    for l in range(depth):
        p = _layer_weights(w_in[l], b_gate[l], conv_w[l], w_conv_out[l], mh_g[l], w_m_out[l],
                           w_o[l], ln1_g[l], ln1_b[l], w_ff1[l], w_ff2[l], ln2_g[l], ln2_b[l])
        xp, cp, c_p, n_p, m_p = _prompt_layer(xp, p, alpha)
        conv_tm = jnp.transpose(state_conv[l], (1, 0, 2)).reshape((CONV_W - 1) * sb, d)
        xs, cs_tm, c_s, n_s, m_s = _sample_layer(xs, conv_tm, state_C[l], state_n[l], state_m[l],
                                                 p, alpha, st)
        cs = jnp.transpose(cs_tm.reshape(CONV_W - 1, sb, d), (1, 0, 2))
        for acc, val in zip(outs, (cp, cs, c_p, c_s, n_p, n_s, m_p, m_s)):
            acc.append(val)
    ys = jnp.transpose(xs.reshape(st, sb, d), (1, 0, 2))
    return (xp, ys) + tuple(jnp.stack(acc) for acc in outs)
```

```python
import functools

import jax
import jax.numpy as jnp
from jax import lax
from jax.experimental import pallas as pl
from jax.experimental.pallas import tpu as pltpu

F32 = jnp.float32
BF16 = jnp.bfloat16

LN_EPS = 1e-5
N_HEADS = 4
CHUNK = 128
CONV_W = 3
TOKEN_TILE = 1024
SUB_TILE = 512
MLSTM_TILE = 512
SAMPLE_BATCH_BLOCK = 16
LANES = 128
GATE_PAD = LANES
VMEM_LIMIT = 56 * 1024 * 1024


def _dot(a, b):
    return jnp.dot(a, b, preferred_element_type=F32)


def _dot_nt(a, b):
    return lax.dot_general(a, b, (((1,), (1,)), ((), ())), preferred_element_type=F32)


def _dot_tn(a, b):
    return lax.dot_general(a, b, (((0,), (0,)), ((), ())), preferred_element_type=F32)


def _sigmoid(x):
    return 1.0 / (1.0 + jnp.exp(-x))


def _log_sigmoid(x):
    return jnp.minimum(x, 0.0) - jnp.log1p(jnp.exp(-jnp.abs(x)))


def _split_bf16(x):
    hi = x.astype(BF16)
    lo = (x - hi.astype(F32)).astype(BF16)
    return hi, lo


def _layer_norm(r, g, b):
    mu = jnp.mean(r, axis=-1, keepdims=True)
    xc = r - mu
    var = jnp.mean(xc * xc, axis=-1, keepdims=True)
    return xc * lax.rsqrt(var + LN_EPS) * g + b


def _head_norm(h):
    mu = jnp.mean(h, axis=-1, keepdims=True)
    hc = h - mu
    return hc * lax.rsqrt(jnp.mean(hc * hc, axis=-1, keepdims=True) + LN_EPS)


def _const_spec(shape):
    zeros = (0,) * len(shape)
    return pl.BlockSpec(shape, lambda *_: zeros, pipeline_mode=pl.Buffered(1))


def _window_spec(block_shape, index):
    return pl.BlockSpec(block_shape, lambda *_: index, pipeline_mode=pl.Buffered(1))


def _bch_spec(d):
    return _window_spec((d, 3 * d), (0, 0))


def _qkvo_spec(d):
    return _window_spec((pl.Element(d), pl.Element(4 * d)), (0, 3 * d))


def _gate_spec(d, which):
    return _window_spec((d, d), (0, which))


def _params(n_axes):
    return pltpu.CompilerParams(dimension_semantics=("arbitrary",) * n_axes,
                                vmem_limit_bytes=VMEM_LIMIT)


def _conv_stripes(d):
    sw = 512 if d % 512 == 0 else d
    return [slice(s * sw, (s + 1) * sw) for s in range(d // sw)]


def _conv_prompt_kernel(x_ref, wbch_ref, cw_ref, wco_ref, wgc_ref, m1_ref, cs_ref, u_s, a_s):
    tm, d = a_s.shape
    @pl.when(pl.program_id(1) == 0)
    def _():
        u_s[0:8, :] = jnp.zeros((8, d), F32)

    for r0 in range(0, tm, SUB_TILE):
        sub = min(SUB_TILE, tm - r0)
        xb = x_ref[0, r0:r0 + sub, :].astype(BF16)
        for cs in _conv_stripes(d):
            off = cs.start
            bg = _dot(xb, wbch_ref[:, off:cs.stop])
            cg = _dot(xb, wbch_ref[:, d + off:d + cs.stop])
            hc = _dot(xb, wbch_ref[:, 2 * d + off:2 * d + cs.stop])
            u = cg * hc
            u_s[8 + r0:8 + r0 + sub, cs] = u
            conv = (u_s[6 + r0:6 + r0 + sub, cs] * cw_ref[0:1, cs]
                    + u_s[7 + r0:7 + r0 + sub, cs] * cw_ref[1:2, cs] + u * cw_ref[2:3, cs])
            a_s[r0:r0 + sub, cs] = (bg * conv).astype(BF16)
        gate = _dot(xb, wgc_ref[...])
        yc = _dot(a_s[r0:r0 + sub, :], wco_ref[...])
        m1_ref[0, r0:r0 + sub, :] = _sigmoid(gate) * yc
    cs_ref[0] = u_s[tm + 6:tm + 8, :]
    u_s[0:8, :] = u_s[tm:tm + 8, :]


def _conv_sample_kernel(x_ref, st_ref, wbch_ref, cw_ref, wco_ref, wgc_ref, m1_ref, cs_ref, a_s,
                        *, n_steps):
    rows, d = a_s.shape
    nb = rows // n_steps
    xb = x_ref[...].astype(BF16)
    for cs in _conv_stripes(d):
        off = cs.start
        bg = _dot(xb, wbch_ref[:, off:cs.stop])
        cg = _dot(xb, wbch_ref[:, d + off:d + cs.stop])
        hc = _dot(xb, wbch_ref[:, 2 * d + off:2 * d + cs.stop])
        u = cg * hc
        up = [st_ref[j * nb:(j + 1) * nb, cs] for j in range(CONV_W - 1)]
        up += [u[t * nb:(t + 1) * nb] for t in range(n_steps)]
        for t in range(n_steps):
            conv = (up[t] * cw_ref[0:1, cs] + up[t + 1] * cw_ref[1:2, cs]
                    + up[t + 2] * cw_ref[2:3, cs])
            a_s[t * nb:(t + 1) * nb, cs] = (bg[t * nb:(t + 1) * nb] * conv).astype(BF16)
        for j in range(CONV_W - 1):
            cs_ref[j * nb:(j + 1) * nb, cs] = up[n_steps + j]
    yc = _dot(a_s[...], wco_ref[...])
    m1_ref[...] = _sigmoid(_dot(xb, wgc_ref[...])) * yc


def _conv_branch_prompt(x, wbch, cw, wco, wgc):
    bsz, t, d = x.shape
    tm = TOKEN_TILE
    return pl.pallas_call(
        _conv_prompt_kernel,
        grid=(bsz, t // tm),
        in_specs=[pl.BlockSpec((1, tm, d), lambda b, j: (b, j, 0)),
                  _bch_spec(d), _const_spec(cw.shape), _const_spec(wco.shape),
                  _gate_spec(d, 0)],
        out_specs=[pl.BlockSpec((1, tm, d), lambda b, j: (b, j, 0)),
                   pl.BlockSpec((1, CONV_W - 1, d), lambda b, j: (b, 0, 0))],
        out_shape=[jax.ShapeDtypeStruct((bsz, t, d), F32),
                   jax.ShapeDtypeStruct((bsz, CONV_W - 1, d), F32)],
        scratch_shapes=[pltpu.VMEM((tm + 8, d), F32), pltpu.VMEM((tm, d), BF16)],
        compiler_params=_params(2),
        name="conv_branch_prompt",
    )(x, wbch, cw, wco, wgc)


def _conv_branch_sample(x_tm, st_tm, wbch, cw, wco, wgc, n_steps):
    rows, d = x_tm.shape
    return pl.pallas_call(
        functools.partial(_conv_sample_kernel, n_steps=n_steps),
        grid=(1,),
        in_specs=[_const_spec(x_tm.shape), _const_spec(st_tm.shape), _bch_spec(d),
                  _const_spec(cw.shape), _const_spec(wco.shape), _gate_spec(d, 0)],
        out_specs=[pl.BlockSpec((rows, d), lambda i: (0, 0)),
                   pl.BlockSpec(st_tm.shape, lambda i: (0, 0))],
        out_shape=[jax.ShapeDtypeStruct((rows, d), F32),
                   jax.ShapeDtypeStruct(st_tm.shape, F32)],
        scratch_shapes=[pltpu.VMEM((rows, d), BF16)],
        compiler_params=_params(1),
        name="conv_branch_sample",
    )(x_tm, st_tm, wbch, cw, wco, wgc)


def _project_qkvo(xh, wqkvo_ref, mhg_ref, q_s, k_s, v_s, og_s, d, dh):
    q_s[...] = _dot(xh, wqkvo_ref[:, 0:d]).astype(BF16).astype(q_s.dtype)
    k_s[...] = (_dot(xh, wqkvo_ref[:, d:2 * d]) * (dh ** -0.5)).astype(BF16).astype(k_s.dtype)
    v_s[...] = _dot(xh, wqkvo_ref[:, 2 * d:3 * d]).astype(BF16).astype(v_s.dtype)
    og_s[...] = mhg_ref[...] * _sigmoid(_dot(xh, wqkvo_ref[:, 3 * d:4 * d]))


def _rep(col, times):
    return col if times == 1 else jnp.concatenate([col] * times, axis=1)


def _mlstm_prompt_step(x_ref, wqkvo_ref, wgt_ref, bgc_ref, mhg_ref, hm_ref,
                       q_s, k_s, vx_s, og_s, e_s, colb_s, colm_s, cole_s, cst_s, mst_s,
                       *, cur, prev):
    _, tm, d = q_s.shape
    dh = d // N_HEADS
    L = CHUNK
    LANES = colb_s.shape[-1]
    n_chunks = tm // L
    wide = (dh + LANES) // LANES

    x = x_ref[...]
    xh, xl = _split_bf16(x)

    ga = _dot_nt(wgt_ref[0], xh)
    gb = _dot_nt(wgt_ref[1], xl)
    gt = ga[0:8] + ga[8:16] + gb[0:8] + bgc_ref[...]
    is_input_gate = lax.broadcasted_iota(jnp.int32, gt.shape, 0) < N_HEADS
    g = jnp.where(is_input_gate, gt, _log_sigmoid(gt))

    pos = lax.broadcasted_iota(jnp.int32, g.shape, 1) % L
    csum = g
    shift = 1
    while shift < L:
        csum = csum + jnp.where(pos >= shift, pltpu.roll(csum, shift, 1), 0.0)
        shift *= 2
    ba = jnp.concatenate([csum[N_HEADS:], g[:N_HEADS] - csum[N_HEADS:]], axis=0)

    ri = lax.broadcasted_iota(jnp.int32, (L, L), 0)
    ci = lax.broadcasted_iota(jnp.int32, (L, L), 1)
    causal = ri >= ci

    for c in range(n_chunks):
        slab = ba[:, c * L:(c + 1) * L]
        cols = slab.T
        for h in range(N_HEADS):
            b_rep = jnp.broadcast_to(cols[:, h:h + 1], (L, LANES))
            a_rep = jnp.broadcast_to(cols[:, N_HEADS + h:N_HEADS + h + 1], (L, LANES))
            a_row = slab[N_HEADS + h:N_HEADS + h + 1, :]
            dm = jnp.where(causal, b_rep + a_row, -jnp.inf)
            m_loc = jnp.broadcast_to(jnp.max(dm, axis=1, keepdims=True), (L, LANES))
            e_s[cur, c, h] = jnp.exp(dm - m_loc)
            colb_s[cur, c, h] = b_rep
            colm_s[cur, c, h] = m_loc
            cole_s[cur, c, h] = jnp.exp(a_rep + b_rep[L - 1:L, :] - m_loc[L - 1:L, :])

    def project(part):
        w = wqkvo_ref[:, part * d:(part + 1) * d]
        if part == 0:
            q_s[cur] = _dot(xh, w).astype(BF16)
        elif part == 1:
            k_s[cur] = (_dot(xh, w) * (dh ** -0.5)).astype(BF16)
        elif part == 2:
            v = _dot(xh, w).astype(BF16)
            for h in range(N_HEADS):
                vx_s[cur, h, :, 0:dh] = v[:, h * dh:(h + 1) * dh]
                vx_s[cur, h, :, dh:] = jnp.ones((tm, LANES), BF16)
        else:
            og_s[cur] = mhg_ref[...] * _sigmoid(_dot(xh, w))

    def recur(c):
        rows = slice(c * L, (c + 1) * L)
        heads = [slice(h * dh, (h + 1) * dh) for h in range(N_HEADS)]
        qk = [_dot_nt(q_s[prev, rows, hc], k_s[prev, rows, hc]) for hc in heads]
        states = [cst_s[h] for h in range(N_HEADS)]
        qc = [_dot(q_s[prev, rows, hc], states[h].astype(BF16)) for h, hc in enumerate(heads)]
        ux = []
        for h, hc in enumerate(heads):
            kw = k_s[prev, rows, hc].astype(F32) * _rep(cole_s[prev, c, h], dh // LANES)
            ux.append(_dot_tn(kw.astype(BF16), vx_s[prev, h, rows, :]))
        svx = []
        for h in range(N_HEADS):
            s_loc = qk[h] * e_s[prev, c, h]
            svx.append(_dot(s_loc.astype(BF16), vx_s[prev, h, rows, :]))
        for h, hc in enumerate(heads):
            m_prev = mst_s[h]
            m_loc = colm_s[prev, c, h]
            inter = colb_s[prev, c, h] + m_prev
            m_t = jnp.maximum(inter, m_loc)
            w_inter = _rep(jnp.exp(inter - m_t), wide)
            w_loc = _rep(jnp.exp(m_loc - m_t), wide)
            nd = w_inter * qc[h] + w_loc * svx[h]
            inv = 1.0 / jnp.maximum(jnp.abs(nd[:, dh:]), jnp.exp(-m_t))
            hh = nd[:, 0:dh] * _rep(inv, dh // LANES)
            hm_ref[rows, hc] = (_head_norm(hh) * og_s[prev, rows, hc]).astype(BF16)
            m_new = m_t[L - 1:L, :]
            decay = _rep(jnp.exp(inter[L - 1:L, :] - m_new), wide)
            grow = _rep(jnp.exp(m_loc[L - 1:L, :] - m_new), wide)
            cst_s[h] = decay * states[h] + grow * ux[h]
            mst_s[h] = m_new

    n_parts = 4
    for i in range(max(n_chunks, n_parts)):
        if i < n_chunks:
            recur(i)
        if i < n_parts:
            project(i)


def _mlstm_prompt_kernel(x_ref, wqkvo_ref, wgt_ref, bgc_ref, mhg_ref,
                         hm_ref, c_ref, n_ref, m_ref,
                         q_s, k_s, vx_s, og_s, e_s, colb_s, colm_s, cole_s, cst_s, mst_s,
                         *, tiles_per_seq):
    dh = cst_s.shape[1]
    g_step = pl.program_id(0)

    @pl.when(g_step == 0)
    def _():
        for ref in (q_s, k_s, vx_s, og_s, e_s, colb_s, colm_s, cole_s):
            ref[1] = jnp.zeros(ref.shape[1:], ref.dtype)

    @pl.when(jnp.logical_or(g_step == 0, (g_step - 1) % tiles_per_seq == 0))
    def _():
        cst_s[...] = jnp.zeros(cst_s.shape, F32)
        mst_s[...] = jnp.zeros(mst_s.shape, F32)

    for parity in (0, 1):
        @pl.when(g_step % 2 == parity)
        def _():
            _mlstm_prompt_step(x_ref, wqkvo_ref, wgt_ref, bgc_ref, mhg_ref, hm_ref,
                               q_s, k_s, vx_s, og_s, e_s, colb_s, colm_s, cole_s, cst_s, mst_s,
                               cur=parity, prev=1 - parity)

    @pl.when(jnp.logical_and(g_step >= 1, (g_step - 1) % tiles_per_seq == tiles_per_seq - 1))
    def _():
        for h in range(N_HEADS):
            state = cst_s[h]
            c_ref[0, h] = state[:, 0:dh]
            n_ref[0, h] = state[:, dh:].T[0:1, :]
            m_ref[0, h] = mst_s[h][:, 0:1]


def _mlstm_prompt(x, wqkvo, wgt, bgc, mhg):
    bsz, t, d = x.shape
    dh = d // N_HEADS
    tm = MLSTM_TILE
    nc = tm // CHUNK
    dx = dh + LANES
    tps = t // tm
    n_tiles = bsz * tps
    seq_block = lambda g: (jnp.maximum(g - 1, 0) // tps, 0, 0, 0)
    hm, c, n, m = pl.pallas_call(
        functools.partial(_mlstm_prompt_kernel, tiles_per_seq=tps),
        grid=(n_tiles + 1,),
        in_specs=[pl.BlockSpec((tm, d), lambda g: (jnp.minimum(g, n_tiles - 1), 0)),
                  _qkvo_spec(d), _const_spec(wgt.shape), _const_spec(bgc.shape),
                  _const_spec(mhg.shape)],
        out_specs=[pl.BlockSpec((tm, d), lambda g: (jnp.maximum(g - 1, 0), 0)),
                   pl.BlockSpec((1, N_HEADS, dh, dh), seq_block),
                   pl.BlockSpec((1, N_HEADS, 1, dh), seq_block),
                   pl.BlockSpec((1, N_HEADS, 1, 1), seq_block)],
        out_shape=[jax.ShapeDtypeStruct((bsz * t, d), BF16),
                   jax.ShapeDtypeStruct((bsz, N_HEADS, dh, dh), F32),
                   jax.ShapeDtypeStruct((bsz, N_HEADS, 1, dh), F32),
                   jax.ShapeDtypeStruct((bsz, N_HEADS, 1, 1), F32)],
        scratch_shapes=[
            pltpu.VMEM((2, tm, d), BF16),
            pltpu.VMEM((2, tm, d), BF16),
            pltpu.VMEM((2, N_HEADS, tm, dx), BF16),
            pltpu.VMEM((2, tm, d), F32),
            pltpu.VMEM((2, nc, N_HEADS, CHUNK, CHUNK), F32),
            pltpu.VMEM((2, nc, N_HEADS, CHUNK, LANES), F32),
            pltpu.VMEM((2, nc, N_HEADS, CHUNK, LANES), F32),
            pltpu.VMEM((2, nc, N_HEADS, CHUNK, LANES), F32),
            pltpu.VMEM((N_HEADS, dh, dx), F32),
            pltpu.VMEM((N_HEADS, 1, LANES), F32)],
        compiler_params=_params(1),
        name="mlstm_prompt",
    )(x.reshape(bsz * t, d), wqkvo, wgt, bgc, mhg)
    return hm, c, n.reshape(bsz, N_HEADS, dh), m.reshape(bsz, N_HEADS)


def _mlstm_sample_pre_kernel(x_ref, wqkvo_ref, wgc_ref, bgr_ref, mhg_ref, n0_ref, m0_ref,
                             q_ref, kw_ref, v_ref, og_ref, dec_ref, wa_ref, wb_ref, n_ref, m_ref,
                             *, n_steps):
    rows, d = q_ref.shape
    dh = d // N_HEADS
    nb = rows // n_steps
    T = n_steps
    slab = [slice(t * nb, (t + 1) * nb) for t in range(T)]

    x = x_ref[...]
    xh, xl = _split_bf16(x)
    _project_qkvo(xh, wqkvo_ref, mhg_ref, q_ref, kw_ref, v_ref, og_ref, d, dh)
    g = _dot(xh, wgc_ref[0]) + _dot(xl, wgc_ref[0]) + _dot(xh, wgc_ref[1]) + bgr_ref[...]
    is_input_gate = lax.broadcasted_iota(jnp.int32, g.shape, 1) < N_HEADS
    g = jnp.where(is_input_gate, g, _log_sigmoid(g))
    for hd in range(N_HEADS):
        hc = slice(hd * dh, (hd + 1) * dh)
        qf = [q_ref[slab[t], hc] for t in range(T)]
        kf = [kw_ref[slab[t], hc] for t in range(T)]
        vf = [v_ref[slab[t], hc] for t in range(T)]
        li = [g[slab[t], hd:hd + 1] for t in range(T)]
        lf = [g[slab[t], N_HEADS + hd:N_HEADS + hd + 1] for t in range(T)]
        m0 = m0_ref[:, hd:hd + 1]
        n0 = n0_ref[:, hc]
        b = [lf[0]]
        for t in range(1, T):
            b.append(b[t - 1] + lf[t])
        a = [li[t] - b[t] for t in range(T)]
        m_new = None
        for t in range(T):
            inter = b[t] + m0
            m_t = inter
            for s in range(t + 1):
                m_t = jnp.maximum(m_t, b[t] + a[s])
            w_inter = jnp.exp(inter - m_t)
            num = jnp.zeros((nb, dh), F32)
            den = w_inter * jnp.sum(qf[t] * n0, axis=1, keepdims=True)
            for s in range(t + 1):
                s_w = (jnp.sum(qf[t] * kf[s], axis=1, keepdims=True)
                       * jnp.exp(b[t] + a[s] - m_t))
                num = num + s_w * vf[s]
                den = den + s_w
            inv = 1.0 / jnp.maximum(jnp.abs(den), jnp.exp(-m_t))
            wa_ref[slab[t], hc] = jnp.broadcast_to(w_inter * inv, (nb, dh))
            wb_ref[slab[t], hc] = num * inv
            m_new = m_t
        b_last = b[T - 1]
        decay = jnp.exp(b_last + m0 - m_new)
        n_new = decay * n0
        for s in range(T):
            kw = kf[s] * jnp.exp(a[s] + b_last - m_new)
            kw_ref[slab[s], hc] = kw
            n_new = n_new + kw
        n_ref[:, hc] = n_new
        m_ref[:, hd:hd + 1] = m_new
        dec_ref[:, hc] = jnp.broadcast_to(decay, (nb, dh))


def _mlstm_sample_pre(x_tm, wqkvo, wgc, bgr, mhg, n0, m0, n_steps):
    rows, d = x_tm.shape
    nb = rows // n_steps
    whole = lambda shape: pl.BlockSpec(shape, lambda i: (0,) * len(shape))
    big = jax.ShapeDtypeStruct((rows, d), F32)
    small = jax.ShapeDtypeStruct((nb, d), F32)
    return pl.pallas_call(
        functools.partial(_mlstm_sample_pre_kernel, n_steps=n_steps),
        grid=(1,),
        in_specs=[_const_spec(x_tm.shape), _qkvo_spec(d), _const_spec(wgc.shape),
                  _const_spec(bgr.shape), _const_spec(mhg.shape), _const_spec(n0.shape),
                  _const_spec(m0.shape)],
        out_specs=[whole((rows, d))] * 4 + [whole((nb, d))] + [whole((rows, d))] * 2
                  + [whole((nb, d)), whole((nb, N_HEADS))],
        out_shape=[big, big, big, big, small, big, big, small,
                   jax.ShapeDtypeStruct((nb, N_HEADS), F32)],
        compiler_params=_params(1),
        name="mlstm_sample_pre",
    )(x_tm, wqkvo, wgc, bgr, mhg, n0, m0)


def _stream_matrix_memory(c0_ref, q_ref, kw_ref, v_ref, dec_ref, c_ref, qc_ref):
    T, bb, dh = q_ref.shape
    owner = lax.broadcasted_iota(jnp.int32, (T * bb, dh), 0) % bb
    q_blk = jnp.concatenate([q_ref[t] for t in range(T)], axis=0).astype(BF16)
    kw_blk = jnp.concatenate([kw_ref[t] for t in range(T)], axis=0).astype(BF16)
    v_blk = jnp.concatenate([v_ref[t] for t in range(T)], axis=0)
    c_olds = [c0_ref[bi, 0] for bi in range(bb)]
    reads = [_dot(q_blk, c_olds[bi].astype(BF16)) for bi in range(bb)]
    qc = jnp.zeros((T * bb, dh), F32)
    for bi in range(bb):
        qc = jnp.where(owner == bi, reads[bi], qc)
    for t in range(T):
        qc_ref[t] = qc[t * bb:(t + 1) * bb]
    for bi in range(bb):
        upd = _dot_tn(kw_blk, jnp.where(owner == bi, v_blk, 0.0).astype(BF16))
        c_ref[bi, 0] = dec_ref[bi:bi + 1, :] * c_olds[bi] + upd


def _mlstm_sample_post_kernel(wa_ref, wb_ref, qc_ref, og_ref, hm_ref):
    d = hm_ref.shape[1]
    dh = d // N_HEADS
    for hd in range(N_HEADS):
        hc = slice(hd * dh, (hd + 1) * dh)
        hh = wa_ref[:, hc] * qc_ref[:, hc] + wb_ref[:, hc]
        hm_ref[:, hc] = _head_norm(hh) * og_ref[:, hc]


def _mlstm_sample_post(wa, wb, qc, og):
    spec = _const_spec(wa.shape)
    return pl.pallas_call(
        _mlstm_sample_post_kernel,
        grid=(1,),
        in_specs=[spec, spec, spec, spec],
        out_specs=pl.BlockSpec(wa.shape, lambda i: (0, 0)),
        out_shape=jax.ShapeDtypeStruct(wa.shape, F32),
        compiler_params=_params(1),
        name="mlstm_sample_post",
    )(wa, wb, qc, og)


def _merge_kernel(x_ref, m1_ref, hm_ref, wmo_ref, wgm_ref, wo_ref, g_ref, b_ref, o_ref, *, alpha):
    tm = x_ref.shape[0]
    for r0 in range(0, tm, SUB_TILE):
        rows = slice(r0, min(r0 + SUB_TILE, tm))
        x = x_ref[rows, :]
        ym = _dot(hm_ref[rows, :].astype(BF16), wmo_ref[...])
        gate = _dot(x.astype(BF16), wgm_ref[...])
        merged = m1_ref[rows, :] + _sigmoid(gate) * ym
        r = alpha * x + _dot(merged.astype(BF16), wo_ref[...])
        o_ref[rows, :] = _layer_norm(r, g_ref[...], b_ref[...])


def _merge(x, m1, hm, wmo, wgm, wo, g, b, alpha):
    n, d = x.shape
    tm = min(TOKEN_TILE, n)
    tile = pl.BlockSpec((tm, d), lambda i: (i, 0))
    return pl.pallas_call(
        functools.partial(_merge_kernel, alpha=alpha),
        grid=(n // tm,),
        in_specs=[tile, tile, tile, _const_spec(wmo.shape), _gate_spec(d, 1),
                  _const_spec(wo.shape), _const_spec(g.shape), _const_spec(b.shape)],
        out_specs=tile,
        out_shape=jax.ShapeDtypeStruct((n, d), F32),
        compiler_params=_params(1),
        name="merge_ln1",
    )(x, m1, hm, wmo, wgm, wo, g, b)


def _ffn_kernel(x_ref, w1_ref, w2_ref, g_ref, b_ref, *rest, alpha, stream):
    if stream:
        c0_ref, q_ref, kw_ref, v_ref, dec_ref, o_ref, c_ref, qc_ref, hid_s = rest
        _stream_matrix_memory(c0_ref, q_ref, kw_ref, v_ref, dec_ref, c_ref, qc_ref)
    else:
        o_ref, hid_s = rest
    tm, dff = hid_s.shape
    sw = 1024 if dff % 1024 == 0 else dff
    for r0 in range(0, tm, SUB_TILE):
        rows = slice(r0, min(r0 + SUB_TILE, tm))
        x = x_ref[rows, :]
        xb = x.astype(BF16)
        for s in range(dff // sw):
            cs = slice(s * sw, (s + 1) * sw)
            hid = jnp.maximum(_dot(xb, w1_ref[:, cs]), 0.0)
            hid_s[rows, cs] = (hid * hid).astype(BF16)
        r = alpha * x + _dot(hid_s[rows, :], w2_ref[...])
        o_ref[rows, :] = _layer_norm(r, g_ref[...], b_ref[...])


def _ffn(x, w1, w2, g, b, alpha):
    n, d = x.shape
    tm = min(TOKEN_TILE, n)
    tile = pl.BlockSpec((tm, d), lambda i: (i, 0))
    return pl.pallas_call(
        functools.partial(_ffn_kernel, alpha=alpha, stream=False),
        grid=(n // tm,),
        in_specs=[tile, _const_spec(w1.shape), _const_spec(w2.shape), _const_spec(g.shape),
                  _const_spec(b.shape)],
        out_specs=tile,
        out_shape=jax.ShapeDtypeStruct((n, d), F32),
        scratch_shapes=[pltpu.VMEM((tm, w1.shape[1]), BF16)],
        compiler_params=_params(1),
        name="ffn_ln2",
    )(x, w1, w2, g, b)


def _ffn_with_stream(x, w1, w2, g, b, alpha, c0, q, kw, v, dec, n_steps):
    n, d = x.shape
    rows = q.shape[0]
    nb = rows // n_steps
    dh = d // N_HEADS
    bb = SAMPLE_BATCH_BLOCK
    n_grid = (nb // bb) * N_HEADS
    tm = n // n_grid
    assert tm * n_grid == n and tm % 8 == 0 and nb % bb == 0
    tile = pl.BlockSpec((tm, d), lambda s: (s, 0))
    cblock = pl.BlockSpec((bb, 1, dh, dh), lambda s: (s // N_HEADS, s % N_HEADS, 0, 0))
    tblock = pl.BlockSpec((n_steps, bb, dh), lambda s: (0, s // N_HEADS, s % N_HEADS))
    dblock = pl.BlockSpec((bb, dh), lambda s: (s // N_HEADS, s % N_HEADS))
    as_tbd = lambda a: a.reshape(n_steps, nb, d)
    x2, c_new, qc = pl.pallas_call(
        functools.partial(_ffn_kernel, alpha=alpha, stream=True),
        grid=(n_grid,),
        in_specs=[tile, _const_spec(w1.shape), _const_spec(w2.shape), _const_spec(g.shape),
                  _const_spec(b.shape), cblock, tblock, tblock, tblock, dblock],
        out_specs=[tile, cblock, tblock],
        out_shape=[jax.ShapeDtypeStruct((n, d), F32), jax.ShapeDtypeStruct(c0.shape, F32),
                   jax.ShapeDtypeStruct((n_steps, nb, d), F32)],
        scratch_shapes=[pltpu.VMEM((tm, w1.shape[1]), BF16)],
        compiler_params=_params(1),
        name="ffn_ln2_stream",
    )(x, w1, w2, g, b, c0, as_tbd(q), as_tbd(kw), as_tbd(v), dec)
    return x2, c_new, qc.reshape(rows, d)


def _w_in_prep_kernel(a_ref, g_ref, b_ref, wa_ref, wg_ref, wb_ref, *, n_a):
    j = pl.program_id(0)

    @pl.when(j == 0)
    def _():
        wg_ref[...] = g_ref[...]

    @pl.when(j < n_a)
    def _():
        wa_ref[...] = a_ref[...].T.astype(BF16)

    @pl.when(j >= n_a)
    def _():
        wb_ref[...] = b_ref[...].T.astype(BF16)


def _prepare_w_in(w_in_t):
    d = w_in_t.shape[1]
    n_a, n_b = 7, 2
    h2 = 2 * N_HEADS
    off_b = n_a * d + h2
    return pl.pallas_call(
        functools.partial(_w_in_prep_kernel, n_a=n_a),
        grid=(n_a + n_b,),
        in_specs=[pl.BlockSpec((d, d), lambda j: (jnp.minimum(j, n_a - 1), 0)),
                  pl.BlockSpec((pl.Element(h2), pl.Element(d)), lambda j: (n_a * d, 0)),
                  pl.BlockSpec((pl.Element(d), pl.Element(d)),
                               lambda j: (pl.multiple_of(off_b + jnp.maximum(j - n_a, 0) * d, 8),
                                          0))],
        out_specs=[pl.BlockSpec((d, d), lambda j: (0, jnp.minimum(j, n_a - 1))),
                   pl.BlockSpec((h2, d), lambda j: (0, 0)),
                   pl.BlockSpec((d, d), lambda j: (0, jnp.maximum(j - n_a, 0)))],
        out_shape=[jax.ShapeDtypeStruct((d, n_a * d), BF16),
                   jax.ShapeDtypeStruct((h2, d), F32),
                   jax.ShapeDtypeStruct((d, n_b * d), BF16)],
        compiler_params=_params(1),
        name="w_in_prep",
    )(w_in_t, w_in_t, w_in_t)


def _layer_weights(w_in, b_gate, conv_w, w_conv_out, mh_g, w_m_out, w_o, ln1_g, ln1_b,
                   w_ff1, w_ff2, ln2_g, ln2_b):
    d = w_in.shape[0]
    h2 = 2 * N_HEADS
    w_in_t = jnp.swapaxes(w_in, 0, 1)
    w_a, wg_t, w_b = _prepare_w_in(w_in_t)
    wgt_hi, wgt_lo = _split_bf16(wg_t)
    wgt = jnp.stack([jnp.concatenate([wgt_hi, wgt_lo], axis=0),
                     jnp.concatenate([wgt_hi, jnp.zeros_like(wgt_hi)], axis=0)])
    pad = ((0, 0), (0, GATE_PAD - h2))
    wgc = jnp.stack([jnp.pad(wgt_hi.T, pad), jnp.pad(wgt_lo.T, pad)])
    return dict(
        w_a=w_a, w_b=w_b,
        wgt=wgt, wgc=wgc,
        bgc=b_gate.reshape(h2, 1).astype(F32),
        bgr=jnp.pad(b_gate.reshape(1, h2).astype(F32), pad),
        cw=conv_w.astype(F32),
        wco=w_conv_out.astype(BF16), mhg=mh_g.reshape(1, d).astype(F32),
        wmo=w_m_out.astype(BF16), wo=w_o.astype(BF16),
        ln1_g=ln1_g.reshape(1, d), ln1_b=ln1_b.reshape(1, d),
        w1=w_ff1.astype(BF16), w2=w_ff2.astype(BF16),
        ln2_g=ln2_g.reshape(1, d), ln2_b=ln2_b.reshape(1, d))


def _layer(x, xs_tm, conv_tm, c0, n0, m0, p, alpha, n_steps):
    bsz, t, d = x.shape
    m1, conv_p = _conv_branch_prompt(x, p["w_a"], p["cw"], p["wco"], p["w_b"])
    hm, c_p, n_p, m_p = _mlstm_prompt(x, p["w_a"], p["wgt"], p["bgc"], p["mhg"])
    xf = x.reshape(bsz * t, d)
    x1 = _merge(xf, m1.reshape(bsz * t, d), hm, p["wmo"], p["w_b"], p["wo"], p["ln1_g"],
                p["ln1_b"], alpha)
    m1_s, conv_s = _conv_branch_sample(xs_tm, conv_tm, p["w_a"], p["cw"], p["wco"], p["w_b"],
                                       n_steps)
    q, kw, v, og, dec, wa, wb, n_s, m_s = _mlstm_sample_pre(
        xs_tm, p["w_a"], p["wgc"], p["bgr"], p["mhg"], n0.reshape(n0.shape[0], d), m0, n_steps)
    x2, c_s, qc = _ffn_with_stream(x1, p["w1"], p["w2"], p["ln2_g"], p["ln2_b"], alpha,
                                   c0, q, kw, v, dec, n_steps)
    hm_s = _mlstm_sample_post(wa, wb, qc, og)
    x1_s = _merge(xs_tm, m1_s, hm_s, p["wmo"], p["w_b"], p["wo"], p["ln1_g"], p["ln1_b"], alpha)
    x2_s = _ffn(x1_s, p["w1"], p["w2"], p["ln2_g"], p["ln2_b"], alpha)
    return (x2.reshape(bsz, t, d), conv_p, c_p, n_p, m_p,
            x2_s, conv_s, c_s, n_s.reshape(n0.shape), m_s)


def kernel(x_prompt, x_sample, state_conv, state_C, state_n, state_m, w_in, b_gate, conv_w,
           w_conv_out, mh_g, w_m_out, w_o, ln1_g, ln1_b, w_ff1, w_ff2, ln2_g, ln2_b):
    depth = w_in.shape[0]
    alpha = (2.0 * depth) ** 0.25
    bsz, t, d = x_prompt.shape
    sb, st, _ = x_sample.shape
    assert t % TOKEN_TILE == 0 and t % MLSTM_TILE == 0 and MLSTM_TILE % CHUNK == 0
    assert d % N_HEADS == 0
    assert sb % SAMPLE_BATCH_BLOCK == 0 and st >= CONV_W - 1

    xp = x_prompt
    xs = jnp.transpose(x_sample, (1, 0, 2)).reshape(st * sb, d)
    outs = [[] for _ in range(8)]
    for l in range(depth):
        p = _layer_weights(w_in[l], b_gate[l], conv_w[l], w_conv_out[l], mh_g[l], w_m_out[l],
                           w_o[l], ln1_g[l], ln1_b[l], w_ff1[l], w_ff2[l], ln2_g[l], ln2_b[l])
        conv_tm = jnp.transpose(state_conv[l], (1, 0, 2)).reshape((CONV_W - 1) * sb, d)
        xp, cp, c_p, n_p, m_p, xs, cs_tm, c_s, n_s, m_s = _layer(
            xp, xs, conv_tm, state_C[l], state_n[l], state_m[l], p, alpha, st)
        cs = jnp.transpose(cs_tm.reshape(CONV_W - 1, sb, d), (1, 0, 2))
        for acc, val in zip(outs, (cp, cs, c_p, c_s, n_p, n_s, m_p, m_s)):
            acc.append(val)
    ys = jnp.transpose(xs.reshape(st, sb, d), (1, 0, 2))
    return (xp, ys) + tuple(jnp.stack(acc) for acc in outs)
```

```python
import functools

import jax
import jax.numpy as jnp
from jax import lax
from jax.experimental import pallas as pl
from jax.experimental.pallas import tpu as pltpu

F32 = jnp.float32
BF16 = jnp.bfloat16

LN_EPS = 1e-5
N_HEADS = 4
CHUNK = 128
CONV_W = 3
TOKEN_TILE = 1024
SUB_TILE = 512
MLSTM_TILE = 512
SAMPLE_BATCH_BLOCK = 16
LANES = 128
GATE_PAD = LANES
VMEM_LIMIT = 56 * 1024 * 1024


def _dot(a, b):
    return jnp.dot(a, b, preferred_element_type=F32)


def _dot_nt(a, b):
    return lax.dot_general(a, b, (((1,), (1,)), ((), ())), preferred_element_type=F32)


def _dot_tn(a, b):
    return lax.dot_general(a, b, (((0,), (0,)), ((), ())), preferred_element_type=F32)


def _sigmoid(x):
    return 1.0 / (1.0 + jnp.exp(-x))


def _log_sigmoid(x):
    return jnp.minimum(x, 0.0) - jnp.log1p(jnp.exp(-jnp.abs(x)))


def _split_bf16(x):
    hi = x.astype(BF16)
    lo = (x - hi.astype(F32)).astype(BF16)
    return hi, lo


def _layer_norm(r, g, b):
    mu = jnp.mean(r, axis=-1, keepdims=True)
    xc = r - mu
    var = jnp.mean(xc * xc, axis=-1, keepdims=True)
    return xc * lax.rsqrt(var + LN_EPS) * g + b


def _head_norm(h):
    mu = jnp.mean(h, axis=-1, keepdims=True)
    hc = h - mu
    return hc * lax.rsqrt(jnp.mean(hc * hc, axis=-1, keepdims=True) + LN_EPS)


def _const_spec(shape):
    zeros = (0,) * len(shape)
    return pl.BlockSpec(shape, lambda *_: zeros, pipeline_mode=pl.Buffered(1))


def _window_spec(block_shape, index):
    return pl.BlockSpec(block_shape, lambda *_: index, pipeline_mode=pl.Buffered(1))


def _bch_spec(d):
    return _window_spec((d, 3 * d), (0, 0))


def _qkvo_spec(d):
    return _window_spec((pl.Element(d), pl.Element(4 * d)), (0, 3 * d))


def _gate_spec(d, which):
    return _window_spec((d, d), (0, which))


def _params(n_axes):
    return pltpu.CompilerParams(dimension_semantics=("arbitrary",) * n_axes,
                                vmem_limit_bytes=VMEM_LIMIT)


def _conv_stripes(d):
    sw = 512 if d % 512 == 0 else d
    return [slice(s * sw, (s + 1) * sw) for s in range(d // sw)]


def _conv_merge_prompt_kernel(x_ref, hm_ref, wbch_ref, cw_ref, wco_ref, wgc_ref, wmo_ref, wgm_ref,
                              wo_ref, g_ref, b_ref, o_ref, cs_ref, u_s, a_s, *, alpha):
    tm, d = a_s.shape
    @pl.when(pl.program_id(1) == 0)
    def _():
        u_s[0:8, :] = jnp.zeros((8, d), F32)

    for r0 in range(0, tm, SUB_TILE):
        sub = min(SUB_TILE, tm - r0)
        x = x_ref[0, r0:r0 + sub, :]
        xb = x.astype(BF16)
        for cs in _conv_stripes(d):
            off = cs.start
            bg = _dot(xb, wbch_ref[:, off:cs.stop])
            cg = _dot(xb, wbch_ref[:, d + off:d + cs.stop])
            hc = _dot(xb, wbch_ref[:, 2 * d + off:2 * d + cs.stop])
            u = cg * hc
            u_s[8 + r0:8 + r0 + sub, cs] = u
            conv = (u_s[6 + r0:6 + r0 + sub, cs] * cw_ref[0:1, cs]
                    + u_s[7 + r0:7 + r0 + sub, cs] * cw_ref[1:2, cs] + u * cw_ref[2:3, cs])
            a_s[r0:r0 + sub, cs] = (bg * conv).astype(BF16)
        gate_c = _dot(xb, wgc_ref[...])
        gate_m = _dot(xb, wgm_ref[...])
        ym = _dot(hm_ref[r0:r0 + sub, :], wmo_ref[...])
        yc = _dot(a_s[r0:r0 + sub, :], wco_ref[...])
        merged = _sigmoid(gate_c) * yc + _sigmoid(gate_m) * ym
        r = alpha * x + _dot(merged.astype(BF16), wo_ref[...])
        o_ref[r0:r0 + sub, :] = _layer_norm(r, g_ref[...], b_ref[...])
    cs_ref[0] = u_s[tm + 6:tm + 8, :]
    u_s[0:8, :] = u_s[tm:tm + 8, :]


def _conv_sample_kernel(x_ref, st_ref, wbch_ref, cw_ref, wco_ref, wgc_ref, m1_ref, cs_ref, a_s,
                        *, n_steps):
    rows, d = a_s.shape
    nb = rows // n_steps
    xb = x_ref[...].astype(BF16)
    for cs in _conv_stripes(d):
        off = cs.start
        bg = _dot(xb, wbch_ref[:, off:cs.stop])
        cg = _dot(xb, wbch_ref[:, d + off:d + cs.stop])
        hc = _dot(xb, wbch_ref[:, 2 * d + off:2 * d + cs.stop])
        u = cg * hc
        up = [st_ref[j * nb:(j + 1) * nb, cs] for j in range(CONV_W - 1)]
        up += [u[t * nb:(t + 1) * nb] for t in range(n_steps)]
        for t in range(n_steps):
            conv = (up[t] * cw_ref[0:1, cs] + up[t + 1] * cw_ref[1:2, cs]
                    + up[t + 2] * cw_ref[2:3, cs])
            a_s[t * nb:(t + 1) * nb, cs] = (bg[t * nb:(t + 1) * nb] * conv).astype(BF16)
        for j in range(CONV_W - 1):
            cs_ref[j * nb:(j + 1) * nb, cs] = up[n_steps + j]
    yc = _dot(a_s[...], wco_ref[...])
    m1_ref[...] = _sigmoid(_dot(xb, wgc_ref[...])) * yc


def _conv_merge_prompt(x, hm, w_a, cw, wco, w_b, wmo, wo, g, b, alpha):
    bsz, t, d = x.shape
    tm = TOKEN_TILE
    flat = pl.BlockSpec((tm, d), lambda bi, j: (bi * (t // tm) + j, 0))
    return pl.pallas_call(
        functools.partial(_conv_merge_prompt_kernel, alpha=alpha),
        grid=(bsz, t // tm),
        in_specs=[pl.BlockSpec((1, tm, d), lambda bi, j: (bi, j, 0)), flat,
                  _bch_spec(d), _const_spec(cw.shape), _const_spec(wco.shape),
                  _gate_spec(d, 0), _const_spec(wmo.shape), _gate_spec(d, 1),
                  _const_spec(wo.shape), _const_spec(g.shape), _const_spec(b.shape)],
        out_specs=[flat, pl.BlockSpec((1, CONV_W - 1, d), lambda bi, j: (bi, 0, 0))],
        out_shape=[jax.ShapeDtypeStruct((bsz * t, d), F32),
                   jax.ShapeDtypeStruct((bsz, CONV_W - 1, d), F32)],
        scratch_shapes=[pltpu.VMEM((tm + 8, d), F32), pltpu.VMEM((tm, d), BF16)],
        compiler_params=_params(2),
        name="conv_merge_prompt",
    )(x, hm, w_a, cw, wco, w_b, wmo, w_b, wo, g, b)


def _conv_branch_sample(x_tm, st_tm, wbch, cw, wco, wgc, n_steps):
    rows, d = x_tm.shape
    return pl.pallas_call(
        functools.partial(_conv_sample_kernel, n_steps=n_steps),
        grid=(1,),
        in_specs=[_const_spec(x_tm.shape), _const_spec(st_tm.shape), _bch_spec(d),
                  _const_spec(cw.shape), _const_spec(wco.shape), _gate_spec(d, 0)],
        out_specs=[pl.BlockSpec((rows, d), lambda i: (0, 0)),
                   pl.BlockSpec(st_tm.shape, lambda i: (0, 0))],
        out_shape=[jax.ShapeDtypeStruct((rows, d), F32),
                   jax.ShapeDtypeStruct(st_tm.shape, F32)],
        scratch_shapes=[pltpu.VMEM((rows, d), BF16)],
        compiler_params=_params(1),
        name="conv_branch_sample",
    )(x_tm, st_tm, wbch, cw, wco, wgc)


def _project_qkvo(xh, wqkvo_ref, mhg_ref, q_s, k_s, v_s, og_s, d, dh):
    q_s[...] = _dot(xh, wqkvo_ref[:, 0:d]).astype(BF16).astype(q_s.dtype)
    k_s[...] = (_dot(xh, wqkvo_ref[:, d:2 * d]) * (dh ** -0.5)).astype(BF16).astype(k_s.dtype)
    v_s[...] = _dot(xh, wqkvo_ref[:, 2 * d:3 * d]).astype(BF16).astype(v_s.dtype)
    og_s[...] = mhg_ref[...] * _sigmoid(_dot(xh, wqkvo_ref[:, 3 * d:4 * d]))


def _rep(col, times):
    return col if times == 1 else jnp.concatenate([col] * times, axis=1)


def _mlstm_prompt_step(x_ref, wqkvo_ref, wgt_ref, bgc_ref, mhg_ref, hm_ref,
                       q_s, k_s, vx_s, og_s, e_s, colb_s, colm_s, cole_s, cst_s, mst_s,
                       *, cur, prev):
    _, tm, d = q_s.shape
    dh = d // N_HEADS
    L = CHUNK
    LANES = colb_s.shape[-1]
    n_chunks = tm // L
    wide = (dh + LANES) // LANES

    x = x_ref[...]
    xh, xl = _split_bf16(x)

    ga = _dot_nt(wgt_ref[0], xh)
    gb = _dot_nt(wgt_ref[1], xl)
    gt = ga[0:8] + ga[8:16] + gb[0:8] + bgc_ref[...]
    is_input_gate = lax.broadcasted_iota(jnp.int32, gt.shape, 0) < N_HEADS
    g = jnp.where(is_input_gate, gt, _log_sigmoid(gt))

    pos = lax.broadcasted_iota(jnp.int32, g.shape, 1) % L
    csum = g
    shift = 1
    while shift < L:
        csum = csum + jnp.where(pos >= shift, pltpu.roll(csum, shift, 1), 0.0)
        shift *= 2
    ba = jnp.concatenate([csum[N_HEADS:], g[:N_HEADS] - csum[N_HEADS:]], axis=0)

    ri = lax.broadcasted_iota(jnp.int32, (L, L), 0)
    ci = lax.broadcasted_iota(jnp.int32, (L, L), 1)
    causal = ri >= ci

    for c in range(n_chunks):
        slab = ba[:, c * L:(c + 1) * L]
        cols = slab.T
        for h in range(N_HEADS):
            b_rep = jnp.broadcast_to(cols[:, h:h + 1], (L, LANES))
            a_rep = jnp.broadcast_to(cols[:, N_HEADS + h:N_HEADS + h + 1], (L, LANES))
            a_row = slab[N_HEADS + h:N_HEADS + h + 1, :]
            dm = jnp.where(causal, b_rep + a_row, -jnp.inf)
            m_loc = jnp.broadcast_to(jnp.max(dm, axis=1, keepdims=True), (L, LANES))
            e_s[cur, c, h] = jnp.exp(dm - m_loc)
            colb_s[cur, c, h] = b_rep
            colm_s[cur, c, h] = m_loc
            cole_s[cur, c, h] = jnp.exp(a_rep + b_rep[L - 1:L, :] - m_loc[L - 1:L, :])

    def project(part):
        w = wqkvo_ref[:, part * d:(part + 1) * d]
        if part == 0:
            q_s[cur] = _dot(xh, w).astype(BF16)
        elif part == 1:
            k_s[cur] = (_dot(xh, w) * (dh ** -0.5)).astype(BF16)
        elif part == 2:
            v = _dot(xh, w).astype(BF16)
            for h in range(N_HEADS):
                vx_s[cur, h, :, 0:dh] = v[:, h * dh:(h + 1) * dh]
                vx_s[cur, h, :, dh:] = jnp.ones((tm, LANES), BF16)
        else:
            og_s[cur] = mhg_ref[...] * _sigmoid(_dot(xh, w))

    def recur(c):
        rows = slice(c * L, (c + 1) * L)
        heads = [slice(h * dh, (h + 1) * dh) for h in range(N_HEADS)]
        qk = [_dot_nt(q_s[prev, rows, hc], k_s[prev, rows, hc]) for hc in heads]
        states = [cst_s[h] for h in range(N_HEADS)]
        qc = [_dot(q_s[prev, rows, hc], states[h].astype(BF16)) for h, hc in enumerate(heads)]
        ux = []
        for h, hc in enumerate(heads):
            kw = k_s[prev, rows, hc].astype(F32) * _rep(cole_s[prev, c, h], dh // LANES)
            ux.append(_dot_tn(kw.astype(BF16), vx_s[prev, h, rows, :]))
        svx = []
        for h in range(N_HEADS):
            s_loc = qk[h] * e_s[prev, c, h]
            svx.append(_dot(s_loc.astype(BF16), vx_s[prev, h, rows, :]))
        for h, hc in enumerate(heads):
            m_prev = mst_s[h]
            m_loc = colm_s[prev, c, h]
            inter = colb_s[prev, c, h] + m_prev
            m_t = jnp.maximum(inter, m_loc)
            w_inter = _rep(jnp.exp(inter - m_t), wide)
            w_loc = _rep(jnp.exp(m_loc - m_t), wide)
            nd = w_inter * qc[h] + w_loc * svx[h]
            inv = 1.0 / jnp.maximum(jnp.abs(nd[:, dh:]), jnp.exp(-m_t))
            hh = nd[:, 0:dh] * _rep(inv, dh // LANES)
            hm_ref[rows, hc] = (_head_norm(hh) * og_s[prev, rows, hc]).astype(BF16)
            m_new = m_t[L - 1:L, :]
            decay = _rep(jnp.exp(inter[L - 1:L, :] - m_new), wide)
            grow = _rep(jnp.exp(m_loc[L - 1:L, :] - m_new), wide)
            cst_s[h] = decay * states[h] + grow * ux[h]
            mst_s[h] = m_new

    n_parts = 4
    for i in range(max(n_chunks, n_parts)):
        if i < n_chunks:
            recur(i)
        if i < n_parts:
            project(i)


def _mlstm_prompt_kernel(x_ref, wqkvo_ref, wgt_ref, bgc_ref, mhg_ref,
                         hm_ref, c_ref, n_ref, m_ref,
                         q_s, k_s, vx_s, og_s, e_s, colb_s, colm_s, cole_s, cst_s, mst_s,
                         *, tiles_per_seq):
    dh = cst_s.shape[1]
    g_step = pl.program_id(0)

    @pl.when(g_step == 0)
    def _():
        for ref in (q_s, k_s, vx_s, og_s, e_s, colb_s, colm_s, cole_s):
            ref[1] = jnp.zeros(ref.shape[1:], ref.dtype)

    @pl.when(jnp.logical_or(g_step == 0, (g_step - 1) % tiles_per_seq == 0))
    def _():
        cst_s[...] = jnp.zeros(cst_s.shape, F32)
        mst_s[...] = jnp.zeros(mst_s.shape, F32)

    for parity in (0, 1):
        @pl.when(g_step % 2 == parity)
        def _():
            _mlstm_prompt_step(x_ref, wqkvo_ref, wgt_ref, bgc_ref, mhg_ref, hm_ref,
                               q_s, k_s, vx_s, og_s, e_s, colb_s, colm_s, cole_s, cst_s, mst_s,
                               cur=parity, prev=1 - parity)

    @pl.when(jnp.logical_and(g_step >= 1, (g_step - 1) % tiles_per_seq == tiles_per_seq - 1))
    def _():
        for h in range(N_HEADS):
            state = cst_s[h]
            c_ref[0, h] = state[:, 0:dh]
            n_ref[0, h] = state[:, dh:].T[0:1, :]
            m_ref[0, h] = mst_s[h][:, 0:1]


def _mlstm_prompt(x, wqkvo, wgt, bgc, mhg):
    bsz, t, d = x.shape
    dh = d // N_HEADS
    tm = MLSTM_TILE
    nc = tm // CHUNK
    dx = dh + LANES
    tps = t // tm
    n_tiles = bsz * tps
    seq_block = lambda g: (jnp.maximum(g - 1, 0) // tps, 0, 0, 0)
    hm, c, n, m = pl.pallas_call(
        functools.partial(_mlstm_prompt_kernel, tiles_per_seq=tps),
        grid=(n_tiles + 1,),
        in_specs=[pl.BlockSpec((tm, d), lambda g: (jnp.minimum(g, n_tiles - 1), 0)),
                  _qkvo_spec(d), _const_spec(wgt.shape), _const_spec(bgc.shape),
                  _const_spec(mhg.shape)],
        out_specs=[pl.BlockSpec((tm, d), lambda g: (jnp.maximum(g - 1, 0), 0)),
                   pl.BlockSpec((1, N_HEADS, dh, dh), seq_block),
                   pl.BlockSpec((1, N_HEADS, 1, dh), seq_block),
                   pl.BlockSpec((1, N_HEADS, 1, 1), seq_block)],
        out_shape=[jax.ShapeDtypeStruct((bsz * t, d), BF16),
                   jax.ShapeDtypeStruct((bsz, N_HEADS, dh, dh), F32),
                   jax.ShapeDtypeStruct((bsz, N_HEADS, 1, dh), F32),
                   jax.ShapeDtypeStruct((bsz, N_HEADS, 1, 1), F32)],
        scratch_shapes=[
            pltpu.VMEM((2, tm, d), BF16),
            pltpu.VMEM((2, tm, d), BF16),
            pltpu.VMEM((2, N_HEADS, tm, dx), BF16),
            pltpu.VMEM((2, tm, d), F32),
            pltpu.VMEM((2, nc, N_HEADS, CHUNK, CHUNK), F32),
            pltpu.VMEM((2, nc, N_HEADS, CHUNK, LANES), F32),
            pltpu.VMEM((2, nc, N_HEADS, CHUNK, LANES), F32),
            pltpu.VMEM((2, nc, N_HEADS, CHUNK, LANES), F32),
            pltpu.VMEM((N_HEADS, dh, dx), F32),
            pltpu.VMEM((N_HEADS, 1, LANES), F32)],
        compiler_params=_params(1),
        name="mlstm_prompt",
    )(x.reshape(bsz * t, d), wqkvo, wgt, bgc, mhg)
    return hm, c, n.reshape(bsz, N_HEADS, dh), m.reshape(bsz, N_HEADS)


def _mlstm_sample_pre_kernel(x_ref, wqkvo_ref, wgc_ref, bgr_ref, mhg_ref, n0_ref, m0_ref,
                             q_ref, kw_ref, v_ref, og_ref, dec_ref, wa_ref, wb_ref, n_ref, m_ref,
                             *, n_steps):
    rows, d = q_ref.shape
    dh = d // N_HEADS
    nb = rows // n_steps
    T = n_steps
    slab = [slice(t * nb, (t + 1) * nb) for t in range(T)]

    x = x_ref[...]
    xh, xl = _split_bf16(x)
    _project_qkvo(xh, wqkvo_ref, mhg_ref, q_ref, kw_ref, v_ref, og_ref, d, dh)
    g = _dot(xh, wgc_ref[0]) + _dot(xl, wgc_ref[0]) + _dot(xh, wgc_ref[1]) + bgr_ref[...]
    is_input_gate = lax.broadcasted_iota(jnp.int32, g.shape, 1) < N_HEADS
    g = jnp.where(is_input_gate, g, _log_sigmoid(g))
    for hd in range(N_HEADS):
        hc = slice(hd * dh, (hd + 1) * dh)
        qf = [q_ref[slab[t], hc] for t in range(T)]
        kf = [kw_ref[slab[t], hc] for t in range(T)]
        vf = [v_ref[slab[t], hc] for t in range(T)]
        li = [g[slab[t], hd:hd + 1] for t in range(T)]
        lf = [g[slab[t], N_HEADS + hd:N_HEADS + hd + 1] for t in range(T)]
        m0 = m0_ref[:, hd:hd + 1]
        n0 = n0_ref[:, hc]
        b = [lf[0]]
        for t in range(1, T):
            b.append(b[t - 1] + lf[t])
        a = [li[t] - b[t] for t in range(T)]
        m_new = None
        for t in range(T):
            inter = b[t] + m0
            m_t = inter
            for s in range(t + 1):
                m_t = jnp.maximum(m_t, b[t] + a[s])
            w_inter = jnp.exp(inter - m_t)
            num = jnp.zeros((nb, dh), F32)
            den = w_inter * jnp.sum(qf[t] * n0, axis=1, keepdims=True)
            for s in range(t + 1):
                s_w = (jnp.sum(qf[t] * kf[s], axis=1, keepdims=True)
                       * jnp.exp(b[t] + a[s] - m_t))
                num = num + s_w * vf[s]
                den = den + s_w
            inv = 1.0 / jnp.maximum(jnp.abs(den), jnp.exp(-m_t))
            wa_ref[slab[t], hc] = jnp.broadcast_to(w_inter * inv, (nb, dh))
            wb_ref[slab[t], hc] = num * inv
            m_new = m_t
        b_last = b[T - 1]
        decay = jnp.exp(b_last + m0 - m_new)
        n_new = decay * n0
        for s in range(T):
            kw = kf[s] * jnp.exp(a[s] + b_last - m_new)
            kw_ref[slab[s], hc] = kw
            n_new = n_new + kw
        n_ref[:, hc] = n_new
        m_ref[:, hd:hd + 1] = m_new
        dec_ref[:, hc] = jnp.broadcast_to(decay, (nb, dh))


def _mlstm_sample_pre(x_tm, wqkvo, wgc, bgr, mhg, n0, m0, n_steps):
    rows, d = x_tm.shape
    nb = rows // n_steps
    whole = lambda shape: pl.BlockSpec(shape, lambda i: (0,) * len(shape))
    big = jax.ShapeDtypeStruct((rows, d), F32)
    small = jax.ShapeDtypeStruct((nb, d), F32)
    return pl.pallas_call(
        functools.partial(_mlstm_sample_pre_kernel, n_steps=n_steps),
        grid=(1,),
        in_specs=[_const_spec(x_tm.shape), _qkvo_spec(d), _const_spec(wgc.shape),
                  _const_spec(bgr.shape), _const_spec(mhg.shape), _const_spec(n0.shape),
                  _const_spec(m0.shape)],
        out_specs=[whole((rows, d))] * 4 + [whole((nb, d))] + [whole((rows, d))] * 2
                  + [whole((nb, d)), whole((nb, N_HEADS))],
        out_shape=[big, big, big, big, small, big, big, small,
                   jax.ShapeDtypeStruct((nb, N_HEADS), F32)],
        compiler_params=_params(1),
        name="mlstm_sample_pre",
    )(x_tm, wqkvo, wgc, bgr, mhg, n0, m0)


def _stream_matrix_memory(c0_ref, q_ref, kw_ref, v_ref, dec_ref, c_ref, qc_ref):
    T, bb, dh = q_ref.shape
    owner = lax.broadcasted_iota(jnp.int32, (T * bb, dh), 0) % bb
    q_blk = jnp.concatenate([q_ref[t] for t in range(T)], axis=0).astype(BF16)
    kw_blk = jnp.concatenate([kw_ref[t] for t in range(T)], axis=0).astype(BF16)
    v_blk = jnp.concatenate([v_ref[t] for t in range(T)], axis=0)
    c_olds = [c0_ref[bi, 0] for bi in range(bb)]
    reads = [_dot(q_blk, c_olds[bi].astype(BF16)) for bi in range(bb)]
    qc = jnp.zeros((T * bb, dh), F32)
    for bi in range(bb):
        qc = jnp.where(owner == bi, reads[bi], qc)
    for t in range(T):
        qc_ref[t] = qc[t * bb:(t + 1) * bb]
    for bi in range(bb):
        upd = _dot_tn(kw_blk, jnp.where(owner == bi, v_blk, 0.0).astype(BF16))
        c_ref[bi, 0] = dec_ref[bi:bi + 1, :] * c_olds[bi] + upd


def _mlstm_sample_post_kernel(wa_ref, wb_ref, qc_ref, og_ref, hm_ref):
    d = hm_ref.shape[1]
    dh = d // N_HEADS
    for hd in range(N_HEADS):
        hc = slice(hd * dh, (hd + 1) * dh)
        hh = wa_ref[:, hc] * qc_ref[:, hc] + wb_ref[:, hc]
        hm_ref[:, hc] = _head_norm(hh) * og_ref[:, hc]


def _mlstm_sample_post(wa, wb, qc, og):
    spec = _const_spec(wa.shape)
    return pl.pallas_call(
        _mlstm_sample_post_kernel,
        grid=(1,),
        in_specs=[spec, spec, spec, spec],
        out_specs=pl.BlockSpec(wa.shape, lambda i: (0, 0)),
        out_shape=jax.ShapeDtypeStruct(wa.shape, F32),
        compiler_params=_params(1),
        name="mlstm_sample_post",
    )(wa, wb, qc, og)


def _merge_kernel(x_ref, m1_ref, hm_ref, wmo_ref, wgm_ref, wo_ref, g_ref, b_ref, o_ref, *, alpha):
    tm = x_ref.shape[0]
    for r0 in range(0, tm, SUB_TILE):
        rows = slice(r0, min(r0 + SUB_TILE, tm))
        x = x_ref[rows, :]
        ym = _dot(hm_ref[rows, :].astype(BF16), wmo_ref[...])
        gate = _dot(x.astype(BF16), wgm_ref[...])
        merged = m1_ref[rows, :] + _sigmoid(gate) * ym
        r = alpha * x + _dot(merged.astype(BF16), wo_ref[...])
        o_ref[rows, :] = _layer_norm(r, g_ref[...], b_ref[...])


def _merge(x, m1, hm, wmo, wgm, wo, g, b, alpha):
    n, d = x.shape
    tm = min(TOKEN_TILE, n)
    tile = pl.BlockSpec((tm, d), lambda i: (i, 0))
    return pl.pallas_call(
        functools.partial(_merge_kernel, alpha=alpha),
        grid=(n // tm,),
        in_specs=[tile, tile, tile, _const_spec(wmo.shape), _gate_spec(d, 1),
                  _const_spec(wo.shape), _const_spec(g.shape), _const_spec(b.shape)],
        out_specs=tile,
        out_shape=jax.ShapeDtypeStruct((n, d), F32),
        compiler_params=_params(1),
        name="merge_ln1",
    )(x, m1, hm, wmo, wgm, wo, g, b)


def _ffn_kernel(x_ref, w1_ref, w2_ref, g_ref, b_ref, *rest, alpha, stream):
    if stream:
        c0_ref, q_ref, kw_ref, v_ref, dec_ref, o_ref, c_ref, qc_ref, hid_s = rest
        _stream_matrix_memory(c0_ref, q_ref, kw_ref, v_ref, dec_ref, c_ref, qc_ref)
    else:
        o_ref, hid_s = rest
    tm, dff = hid_s.shape
    sw = 1024 if dff % 1024 == 0 else dff
    for r0 in range(0, tm, SUB_TILE):
        rows = slice(r0, min(r0 + SUB_TILE, tm))
        x = x_ref[rows, :]
        xb = x.astype(BF16)
        for s in range(dff // sw):
            cs = slice(s * sw, (s + 1) * sw)
            hid = jnp.maximum(_dot(xb, w1_ref[:, cs]), 0.0)
            hid_s[rows, cs] = (hid * hid).astype(BF16)
        r = alpha * x + _dot(hid_s[rows, :], w2_ref[...])
        o_ref[rows, :] = _layer_norm(r, g_ref[...], b_ref[...])


def _ffn(x, w1, w2, g, b, alpha):
    n, d = x.shape
    tm = min(TOKEN_TILE, n)
    tile = pl.BlockSpec((tm, d), lambda i: (i, 0))
    return pl.pallas_call(
        functools.partial(_ffn_kernel, alpha=alpha, stream=False),
        grid=(n // tm,),
        in_specs=[tile, _const_spec(w1.shape), _const_spec(w2.shape), _const_spec(g.shape),
                  _const_spec(b.shape)],
        out_specs=tile,
        out_shape=jax.ShapeDtypeStruct((n, d), F32),
        scratch_shapes=[pltpu.VMEM((tm, w1.shape[1]), BF16)],
        compiler_params=_params(1),
        name="ffn_ln2",
    )(x, w1, w2, g, b)


def _ffn_with_stream(x, w1, w2, g, b, alpha, c0, q, kw, v, dec, n_steps):
    n, d = x.shape
    rows = q.shape[0]
    nb = rows // n_steps
    dh = d // N_HEADS
    bb = SAMPLE_BATCH_BLOCK
    n_grid = (nb // bb) * N_HEADS
    tm = n // n_grid
    assert tm * n_grid == n and tm % 8 == 0 and nb % bb == 0
    tile = pl.BlockSpec((tm, d), lambda s: (s, 0))
    cblock = pl.BlockSpec((bb, 1, dh, dh), lambda s: (s // N_HEADS, s % N_HEADS, 0, 0))
    tblock = pl.BlockSpec((n_steps, bb, dh), lambda s: (0, s // N_HEADS, s % N_HEADS))
    dblock = pl.BlockSpec((bb, dh), lambda s: (s // N_HEADS, s % N_HEADS))
    as_tbd = lambda a: a.reshape(n_steps, nb, d)
    x2, c_new, qc = pl.pallas_call(
        functools.partial(_ffn_kernel, alpha=alpha, stream=True),
        grid=(n_grid,),
        in_specs=[tile, _const_spec(w1.shape), _const_spec(w2.shape), _const_spec(g.shape),
                  _const_spec(b.shape), cblock, tblock, tblock, tblock, dblock],
        out_specs=[tile, cblock, tblock],
        out_shape=[jax.ShapeDtypeStruct((n, d), F32), jax.ShapeDtypeStruct(c0.shape, F32),
                   jax.ShapeDtypeStruct((n_steps, nb, d), F32)],
        scratch_shapes=[pltpu.VMEM((tm, w1.shape[1]), BF16)],
        compiler_params=_params(1),
        name="ffn_ln2_stream",
    )(x, w1, w2, g, b, c0, as_tbd(q), as_tbd(kw), as_tbd(v), dec)
    return x2, c_new, qc.reshape(rows, d)


def _w_in_prep_kernel(a_ref, g_ref, b_ref, wa_ref, wg_ref, wb_ref, *, n_a):
    j = pl.program_id(0)

    @pl.when(j == 0)
    def _():
        wg_ref[...] = g_ref[...]

    @pl.when(j < n_a)
    def _():
        wa_ref[...] = a_ref[...].T.astype(BF16)

    @pl.when(j >= n_a)
    def _():
        wb_ref[...] = b_ref[...].T.astype(BF16)


def _prepare_w_in(w_in_t):
    d = w_in_t.shape[1]
    n_a, n_b = 7, 2
    h2 = 2 * N_HEADS
    off_b = n_a * d + h2
    return pl.pallas_call(
        functools.partial(_w_in_prep_kernel, n_a=n_a),
        grid=(n_a + n_b,),
        in_specs=[pl.BlockSpec((d, d), lambda j: (jnp.minimum(j, n_a - 1), 0)),
                  pl.BlockSpec((pl.Element(h2), pl.Element(d)), lambda j: (n_a * d, 0)),
                  pl.BlockSpec((pl.Element(d), pl.Element(d)),
                               lambda j: (pl.multiple_of(off_b + jnp.maximum(j - n_a, 0) * d, 8),
                                          0))],
        out_specs=[pl.BlockSpec((d, d), lambda j: (0, jnp.minimum(j, n_a - 1))),
                   pl.BlockSpec((h2, d), lambda j: (0, 0)),
                   pl.BlockSpec((d, d), lambda j: (0, jnp.maximum(j - n_a, 0)))],
        out_shape=[jax.ShapeDtypeStruct((d, n_a * d), BF16),
                   jax.ShapeDtypeStruct((h2, d), F32),
                   jax.ShapeDtypeStruct((d, n_b * d), BF16)],
        compiler_params=_params(1),
        name="w_in_prep",
    )(w_in_t, w_in_t, w_in_t)


def _layer_weights(w_in, b_gate, conv_w, w_conv_out, mh_g, w_m_out, w_o, ln1_g, ln1_b,
                   w_ff1, w_ff2, ln2_g, ln2_b):
    d = w_in.shape[0]
    h2 = 2 * N_HEADS
    w_in_t = jnp.swapaxes(w_in, 0, 1)
    w_a, wg_t, w_b = _prepare_w_in(w_in_t)
    wgt_hi, wgt_lo = _split_bf16(wg_t)
    wgt = jnp.stack([jnp.concatenate([wgt_hi, wgt_lo], axis=0),
                     jnp.concatenate([wgt_hi, jnp.zeros_like(wgt_hi)], axis=0)])
    pad = ((0, 0), (0, GATE_PAD - h2))
    wgc = jnp.stack([jnp.pad(wgt_hi.T, pad), jnp.pad(wgt_lo.T, pad)])
    return dict(
        w_a=w_a, w_b=w_b,
        wgt=wgt, wgc=wgc,
        bgc=b_gate.reshape(h2, 1).astype(F32),
        bgr=jnp.pad(b_gate.reshape(1, h2).astype(F32), pad),
        cw=conv_w.astype(F32),
        wco=w_conv_out.astype(BF16), mhg=mh_g.reshape(1, d).astype(F32),
        wmo=w_m_out.astype(BF16), wo=w_o.astype(BF16),
        ln1_g=ln1_g.reshape(1, d), ln1_b=ln1_b.reshape(1, d),
        w1=w_ff1.astype(BF16), w2=w_ff2.astype(BF16),
        ln2_g=ln2_g.reshape(1, d), ln2_b=ln2_b.reshape(1, d))


def _layer(x, xs_tm, conv_tm, c0, n0, m0, p, alpha, n_steps):
    bsz, t, d = x.shape
    hm, c_p, n_p, m_p = _mlstm_prompt(x, p["w_a"], p["wgt"], p["bgc"], p["mhg"])
    x1, conv_p = _conv_merge_prompt(x, hm, p["w_a"], p["cw"], p["wco"], p["w_b"], p["wmo"],
                                    p["wo"], p["ln1_g"], p["ln1_b"], alpha)
    m1_s, conv_s = _conv_branch_sample(xs_tm, conv_tm, p["w_a"], p["cw"], p["wco"], p["w_b"],
                                       n_steps)
    q, kw, v, og, dec, wa, wb, n_s, m_s = _mlstm_sample_pre(
        xs_tm, p["w_a"], p["wgc"], p["bgr"], p["mhg"], n0.reshape(n0.shape[0], d), m0, n_steps)
    x2, c_s, qc = _ffn_with_stream(x1, p["w1"], p["w2"], p["ln2_g"], p["ln2_b"], alpha,
                                   c0, q, kw, v, dec, n_steps)
    hm_s = _mlstm_sample_post(wa, wb, qc, og)
    x1_s = _merge(xs_tm, m1_s, hm_s, p["wmo"], p["w_b"], p["wo"], p["ln1_g"], p["ln1_b"], alpha)
    x2_s = _ffn(x1_s, p["w1"], p["w2"], p["ln2_g"], p["ln2_b"], alpha)
    return (x2.reshape(bsz, t, d), conv_p, c_p, n_p, m_p,
            x2_s, conv_s, c_s, n_s.reshape(n0.shape), m_s)


def kernel(x_prompt, x_sample, state_conv, state_C, state_n, state_m, w_in, b_gate, conv_w,
           w_conv_out, mh_g, w_m_out, w_o, ln1_g, ln1_b, w_ff1, w_ff2, ln2_g, ln2_b):
    depth = w_in.shape[0]
    alpha = (2.0 * depth) ** 0.25
    bsz, t, d = x_prompt.shape
    sb, st, _ = x_sample.shape
    assert t % TOKEN_TILE == 0 and t % MLSTM_TILE == 0 and MLSTM_TILE % CHUNK == 0
    assert d % N_HEADS == 0
    assert sb % SAMPLE_BATCH_BLOCK == 0 and st >= CONV_W - 1

    xp = x_prompt
    xs = jnp.transpose(x_sample, (1, 0, 2)).reshape(st * sb, d)
    outs = [[] for _ in range(8)]
    for l in range(depth):
        p = _layer_weights(w_in[l], b_gate[l], conv_w[l], w_conv_out[l], mh_g[l], w_m_out[l],
                           w_o[l], ln1_g[l], ln1_b[l], w_ff1[l], w_ff2[l], ln2_g[l], ln2_b[l])
        conv_tm = jnp.transpose(state_conv[l], (1, 0, 2)).reshape((CONV_W - 1) * sb, d)
        xp, cp, c_p, n_p, m_p, xs, cs_tm, c_s, n_s, m_s = _layer(
            xp, xs, conv_tm, state_C[l], state_n[l], state_m[l], p, alpha, st)
        cs = jnp.transpose(cs_tm.reshape(CONV_W - 1, sb, d), (1, 0, 2))
        for acc, val in zip(outs, (cp, cs, c_p, c_s, n_p, n_s, m_p, m_s)):
            acc.append(val)
    ys = jnp.transpose(xs.reshape(st, sb, d), (1, 0, 2))
    return (xp, ys) + tuple(jnp.stack(acc) for acc in outs)
```

```python
import functools

import jax
import jax.numpy as jnp
from jax import lax
from jax.experimental import pallas as pl
from jax.experimental.pallas import tpu as pltpu

F32 = jnp.float32
BF16 = jnp.bfloat16

LN_EPS = 1e-5
N_HEADS = 4
CHUNK = 128
CONV_W = 3
TOKEN_TILE = 1024
SUB_TILE = 512
MLSTM_TILE = 512
SAMPLE_BATCH_BLOCK = 16
LANES = 128
GATE_PAD = LANES
VMEM_LIMIT = 56 * 1024 * 1024


def _dot(a, b):
    return jnp.dot(a, b, preferred_element_type=F32)


def _dot_nt(a, b):
    return lax.dot_general(a, b, (((1,), (1,)), ((), ())), preferred_element_type=F32)


def _dot_tn(a, b):
    return lax.dot_general(a, b, (((0,), (0,)), ((), ())), preferred_element_type=F32)


def _sigmoid(x):
    return 1.0 / (1.0 + jnp.exp(-x))


def _log_sigmoid(x):
    return jnp.minimum(x, 0.0) - jnp.log1p(jnp.exp(-jnp.abs(x)))


def _split_bf16(x):
    hi = x.astype(BF16)
    lo = (x - hi.astype(F32)).astype(BF16)
    return hi, lo


def _layer_norm(r, g, b):
    mu = jnp.mean(r, axis=-1, keepdims=True)
    xc = r - mu
    var = jnp.mean(xc * xc, axis=-1, keepdims=True)
    return xc * lax.rsqrt(var + LN_EPS) * g + b


def _head_norm(h):
    mu = jnp.mean(h, axis=-1, keepdims=True)
    hc = h - mu
    return hc * lax.rsqrt(jnp.mean(hc * hc, axis=-1, keepdims=True) + LN_EPS)


def _const_spec(shape):
    zeros = (0,) * len(shape)
    return pl.BlockSpec(shape, lambda *_: zeros, pipeline_mode=pl.Buffered(1))


def _window_spec(block_shape, index):
    return pl.BlockSpec(block_shape, lambda *_: index, pipeline_mode=pl.Buffered(1))


def _bch_spec(d):
    return _window_spec((d, 3 * d), (0, 0))


def _qkvo_spec(d):
    return _window_spec((pl.Element(d), pl.Element(4 * d)), (0, 3 * d))


def _gate_spec(d, which):
    return _window_spec((d, d), (0, which))


def _params(n_axes):
    return pltpu.CompilerParams(dimension_semantics=("arbitrary",) * n_axes,
                                vmem_limit_bytes=VMEM_LIMIT)


def _conv_stripes(d):
    sw = 512 if d % 512 == 0 else d
    return [slice(s * sw, (s + 1) * sw) for s in range(d // sw)]


def _conv_merge_prompt_kernel(x_ref, hm_ref, wbch_ref, cw_ref, wco_ref, wgc_ref, wmo_ref, wgm_ref,
                              wo_ref, g_ref, b_ref, o_ref, cs_ref, u_s, a_s, *, alpha):
    tm, d = a_s.shape
    @pl.when(pl.program_id(1) == 0)
    def _():
        u_s[0:8, :] = jnp.zeros((8, d), F32)

    for r0 in range(0, tm, SUB_TILE):
        sub = min(SUB_TILE, tm - r0)
        x = x_ref[0, r0:r0 + sub, :]
        xb = x.astype(BF16)
        for cs in _conv_stripes(d):
            off = cs.start
            bg = _dot(xb, wbch_ref[:, off:cs.stop])
            cg = _dot(xb, wbch_ref[:, d + off:d + cs.stop])
            hc = _dot(xb, wbch_ref[:, 2 * d + off:2 * d + cs.stop])
            u = cg * hc
            u_s[8 + r0:8 + r0 + sub, cs] = u
            conv = (u_s[6 + r0:6 + r0 + sub, cs] * cw_ref[0:1, cs]
                    + u_s[7 + r0:7 + r0 + sub, cs] * cw_ref[1:2, cs] + u * cw_ref[2:3, cs])
            a_s[r0:r0 + sub, cs] = (bg * conv).astype(BF16)
        gate_c = _dot(xb, wgc_ref[...])
        gate_m = _dot(xb, wgm_ref[...])
        ym = _dot(hm_ref[r0:r0 + sub, :], wmo_ref[...])
        yc = _dot(a_s[r0:r0 + sub, :], wco_ref[...])
        merged = _sigmoid(gate_c) * yc + _sigmoid(gate_m) * ym
        r = alpha * x + _dot(merged.astype(BF16), wo_ref[...])
        o_ref[r0:r0 + sub, :] = _layer_norm(r, g_ref[...], b_ref[...])
    cs_ref[0] = u_s[tm + 6:tm + 8, :]
    u_s[0:8, :] = u_s[tm:tm + 8, :]


def _conv_sample_kernel(x_ref, st_ref, wbch_ref, cw_ref, wco_ref, wgc_ref, m1_ref, cs_ref, a_s,
                        *, n_steps):
    rows, d = a_s.shape
    nb = rows // n_steps
    xb = x_ref[...].astype(BF16)
    for cs in _conv_stripes(d):
        off = cs.start
        bg = _dot(xb, wbch_ref[:, off:cs.stop])
        cg = _dot(xb, wbch_ref[:, d + off:d + cs.stop])
        hc = _dot(xb, wbch_ref[:, 2 * d + off:2 * d + cs.stop])
        u = cg * hc
        up = [st_ref[j * nb:(j + 1) * nb, cs] for j in range(CONV_W - 1)]
        up += [u[t * nb:(t + 1) * nb] for t in range(n_steps)]
        for t in range(n_steps):
            conv = (up[t] * cw_ref[0:1, cs] + up[t + 1] * cw_ref[1:2, cs]
                    + up[t + 2] * cw_ref[2:3, cs])
            a_s[t * nb:(t + 1) * nb, cs] = (bg[t * nb:(t + 1) * nb] * conv).astype(BF16)
        for j in range(CONV_W - 1):
            cs_ref[j * nb:(j + 1) * nb, cs] = up[n_steps + j]
    yc = _dot(a_s[...], wco_ref[...])
    m1_ref[...] = _sigmoid(_dot(xb, wgc_ref[...])) * yc


def _conv_merge_prompt(x, hm, w_a, cw, wco, w_b, wmo, wo, g, b, alpha):
    bsz, t, d = x.shape
    tm = TOKEN_TILE
    flat = pl.BlockSpec((tm, d), lambda bi, j: (bi * (t // tm) + j, 0))
    return pl.pallas_call(
        functools.partial(_conv_merge_prompt_kernel, alpha=alpha),
        grid=(bsz, t // tm),
        in_specs=[pl.BlockSpec((1, tm, d), lambda bi, j: (bi, j, 0)), flat,
                  _bch_spec(d), _const_spec(cw.shape), _const_spec(wco.shape),
                  _gate_spec(d, 0), _const_spec(wmo.shape), _gate_spec(d, 1),
                  _const_spec(wo.shape), _const_spec(g.shape), _const_spec(b.shape)],
        out_specs=[flat, pl.BlockSpec((1, CONV_W - 1, d), lambda bi, j: (bi, 0, 0))],
        out_shape=[jax.ShapeDtypeStruct((bsz * t, d), F32),
                   jax.ShapeDtypeStruct((bsz, CONV_W - 1, d), F32)],
        scratch_shapes=[pltpu.VMEM((tm + 8, d), F32), pltpu.VMEM((tm, d), BF16)],
        compiler_params=_params(2),
        name="conv_merge_prompt",
    )(x, hm, w_a, cw, wco, w_b, wmo, w_b, wo, g, b)


def _conv_branch_sample(x_tm, st_tm, wbch, cw, wco, wgc, n_steps):
    rows, d = x_tm.shape
    return pl.pallas_call(
        functools.partial(_conv_sample_kernel, n_steps=n_steps),
        grid=(1,),
        in_specs=[_const_spec(x_tm.shape), _const_spec(st_tm.shape), _bch_spec(d),
                  _const_spec(cw.shape), _const_spec(wco.shape), _gate_spec(d, 0)],
        out_specs=[pl.BlockSpec((rows, d), lambda i: (0, 0)),
                   pl.BlockSpec(st_tm.shape, lambda i: (0, 0))],
        out_shape=[jax.ShapeDtypeStruct((rows, d), F32),
                   jax.ShapeDtypeStruct(st_tm.shape, F32)],
        scratch_shapes=[pltpu.VMEM((rows, d), BF16)],
        compiler_params=_params(1),
        name="conv_branch_sample",
    )(x_tm, st_tm, wbch, cw, wco, wgc)


def _project_qkvo(xh, wqkvo_ref, mhg_ref, q_s, k_s, v_s, og_s, d, dh):
    q_s[...] = _dot(xh, wqkvo_ref[:, 0:d]).astype(BF16).astype(q_s.dtype)
    k_s[...] = (_dot(xh, wqkvo_ref[:, d:2 * d]) * (dh ** -0.5)).astype(BF16).astype(k_s.dtype)
    v_s[...] = _dot(xh, wqkvo_ref[:, 2 * d:3 * d]).astype(BF16).astype(v_s.dtype)
    og_s[...] = mhg_ref[...] * _sigmoid(_dot(xh, wqkvo_ref[:, 3 * d:4 * d]))


def _rep(col, times):
    return col if times == 1 else jnp.concatenate([col] * times, axis=1)


def _mlstm_prompt_step(x_ref, wqkvo_ref, wgt_ref, bgc_ref, mhg_ref, hm_ref,
                       q_s, k_s, vx_s, og_s, e_s, colb_s, colm_s, cole_s, cst_s, mst_s,
                       *, cur, prev):
    _, tm, d = q_s.shape
    dh = d // N_HEADS
    L = CHUNK
    LANES = colb_s.shape[-1]
    n_chunks = tm // L
    wide = (dh + LANES) // LANES

    x = x_ref[...]
    xh, xl = _split_bf16(x)

    ga = _dot_nt(wgt_ref[0], xh)
    gb = _dot_nt(wgt_ref[1], xl)
    gt = ga[0:8] + ga[8:16] + gb[0:8] + bgc_ref[...]
    is_input_gate = lax.broadcasted_iota(jnp.int32, gt.shape, 0) < N_HEADS
    g = jnp.where(is_input_gate, gt, _log_sigmoid(gt))

    pos = lax.broadcasted_iota(jnp.int32, g.shape, 1) % L
    csum = g
    shift = 1
    while shift < L:
        csum = csum + jnp.where(pos >= shift, pltpu.roll(csum, shift, 1), 0.0)
        shift *= 2
    ba = jnp.concatenate([csum[N_HEADS:], g[:N_HEADS] - csum[N_HEADS:]], axis=0)

    ri = lax.broadcasted_iota(jnp.int32, (L, L), 0)
    ci = lax.broadcasted_iota(jnp.int32, (L, L), 1)
    causal = ri >= ci

    for c in range(n_chunks):
        slab = ba[:, c * L:(c + 1) * L]
        cols = slab.T
        for h in range(N_HEADS):
            b_rep = jnp.broadcast_to(cols[:, h:h + 1], (L, LANES))
            a_rep = jnp.broadcast_to(cols[:, N_HEADS + h:N_HEADS + h + 1], (L, LANES))
            a_row = slab[N_HEADS + h:N_HEADS + h + 1, :]
            dm = jnp.where(causal, b_rep + a_row, -jnp.inf)
            m_loc = jnp.broadcast_to(jnp.max(dm, axis=1, keepdims=True), (L, LANES))
            e_s[cur, c, h] = jnp.exp(dm - m_loc)
            colb_s[cur, c, h] = b_rep
            colm_s[cur, c, h] = m_loc
            cole_s[cur, c, h] = jnp.exp(a_rep + b_rep[L - 1:L, :] - m_loc[L - 1:L, :])

    def project(part):
        w = wqkvo_ref[:, part * d:(part + 1) * d]
        if part == 0:
            q_s[cur] = _dot(xh, w).astype(BF16)
        elif part == 1:
            k_s[cur] = (_dot(xh, w) * (dh ** -0.5)).astype(BF16)
        elif part == 2:
            v = _dot(xh, w).astype(BF16)
            for h in range(N_HEADS):
                vx_s[cur, h, :, 0:dh] = v[:, h * dh:(h + 1) * dh]
                vx_s[cur, h, :, dh:] = jnp.ones((tm, LANES), BF16)
        else:
            og_s[cur] = mhg_ref[...] * _sigmoid(_dot(xh, w))

    def recur(c):
        rows = slice(c * L, (c + 1) * L)
        heads = [slice(h * dh, (h + 1) * dh) for h in range(N_HEADS)]
        qk = [_dot_nt(q_s[prev, rows, hc], k_s[prev, rows, hc]) for hc in heads]
        states = [cst_s[h] for h in range(N_HEADS)]
        qc = [_dot(q_s[prev, rows, hc], states[h].astype(BF16)) for h, hc in enumerate(heads)]
        ux = []
        for h, hc in enumerate(heads):
            kw = k_s[prev, rows, hc].astype(F32) * _rep(cole_s[prev, c, h], dh // LANES)
            ux.append(_dot_tn(kw.astype(BF16), vx_s[prev, h, rows, :]))
        svx = []
        for h in range(N_HEADS):
            s_loc = qk[h] * e_s[prev, c, h]
            svx.append(_dot(s_loc.astype(BF16), vx_s[prev, h, rows, :]))
        for h, hc in enumerate(heads):
            m_prev = mst_s[h]
            m_loc = colm_s[prev, c, h]
            inter = colb_s[prev, c, h] + m_prev
            m_t = jnp.maximum(inter, m_loc)
            w_inter = _rep(jnp.exp(inter - m_t), wide)
            w_loc = _rep(jnp.exp(m_loc - m_t), wide)
            nd = w_inter * qc[h] + w_loc * svx[h]
            inv = 1.0 / jnp.maximum(jnp.abs(nd[:, dh:]), jnp.exp(-m_t))
            hh = nd[:, 0:dh] * _rep(inv, dh // LANES)
            hm_ref[rows, hc] = (_head_norm(hh) * og_s[prev, rows, hc]).astype(BF16)
            m_new = m_t[L - 1:L, :]
            decay = _rep(jnp.exp(inter[L - 1:L, :] - m_new), wide)
            grow = _rep(jnp.exp(m_loc[L - 1:L, :] - m_new), wide)
            cst_s[h] = decay * states[h] + grow * ux[h]
            mst_s[h] = m_new

    n_parts = 4
    for i in range(max(n_chunks, n_parts)):
        if i < n_chunks:
            recur(i)
        if i < n_parts:
            project(i)


def _mlstm_prompt_kernel(*refs, tiles_per_seq, n_cast):
    x_ref, wqkvo_ref, wgt_ref, bgc_ref, mhg_ref = refs[:5]
    cast_src = refs[5:5 + n_cast]
    hm_ref, c_ref, n_ref, m_ref = refs[5 + n_cast:9 + n_cast]
    cast_dst = refs[9 + n_cast:9 + 2 * n_cast]
    q_s, k_s, vx_s, og_s, e_s, colb_s, colm_s, cole_s, cst_s, mst_s = refs[9 + 2 * n_cast:]
    dh = cst_s.shape[1]
    g_step = pl.program_id(0)
    for src, dst in zip(cast_src, cast_dst):
        dst[...] = src[...].astype(BF16)

    @pl.when(g_step == 0)
    def _():
        for ref in (q_s, k_s, vx_s, og_s, e_s, colb_s, colm_s, cole_s):
            ref[1] = jnp.zeros(ref.shape[1:], ref.dtype)

    @pl.when(jnp.logical_or(g_step == 0, (g_step - 1) % tiles_per_seq == 0))
    def _():
        cst_s[...] = jnp.zeros(cst_s.shape, F32)
        mst_s[...] = jnp.zeros(mst_s.shape, F32)

    for parity in (0, 1):
        @pl.when(g_step % 2 == parity)
        def _():
            _mlstm_prompt_step(x_ref, wqkvo_ref, wgt_ref, bgc_ref, mhg_ref, hm_ref,
                               q_s, k_s, vx_s, og_s, e_s, colb_s, colm_s, cole_s, cst_s, mst_s,
                               cur=parity, prev=1 - parity)

    @pl.when(jnp.logical_and(g_step >= 1, (g_step - 1) % tiles_per_seq == tiles_per_seq - 1))
    def _():
        for h in range(N_HEADS):
            state = cst_s[h]
            c_ref[0, h] = state[:, 0:dh]
            n_ref[0, h] = state[:, dh:].T[0:1, :]
            m_ref[0, h] = mst_s[h][:, 0:1]


def _mlstm_prompt(x, wqkvo, wgt, bgc, mhg, f32_weights):
    bsz, t, d = x.shape
    dh = d // N_HEADS
    tm = MLSTM_TILE
    nc = tm // CHUNK
    dx = dh + LANES
    tps = t // tm
    n_tiles = bsz * tps
    seq_block = lambda g: (jnp.maximum(g - 1, 0) // tps, 0, 0, 0)
    cast_specs = []
    for w in f32_weights:
        rows_per_step = w.shape[0] // n_tiles
        assert rows_per_step * n_tiles == w.shape[0] and rows_per_step % 16 == 0
        cast_specs.append(pl.BlockSpec((rows_per_step, w.shape[1]),
                                       lambda g: (jnp.minimum(g, n_tiles - 1), 0)))
    n_cast = len(f32_weights)
    hm, c, n, m, *casted = pl.pallas_call(
        functools.partial(_mlstm_prompt_kernel, tiles_per_seq=tps, n_cast=n_cast),
        grid=(n_tiles + 1,),
        in_specs=[pl.BlockSpec((tm, d), lambda g: (jnp.minimum(g, n_tiles - 1), 0)),
                  _qkvo_spec(d), _const_spec(wgt.shape), _const_spec(bgc.shape),
                  _const_spec(mhg.shape)] + cast_specs,
        out_specs=[pl.BlockSpec((tm, d), lambda g: (jnp.maximum(g - 1, 0), 0)),
                   pl.BlockSpec((1, N_HEADS, dh, dh), seq_block),
                   pl.BlockSpec((1, N_HEADS, 1, dh), seq_block),
                   pl.BlockSpec((1, N_HEADS, 1, 1), seq_block)] + cast_specs,
        out_shape=[jax.ShapeDtypeStruct((bsz * t, d), BF16),
                   jax.ShapeDtypeStruct((bsz, N_HEADS, dh, dh), F32),
                   jax.ShapeDtypeStruct((bsz, N_HEADS, 1, dh), F32),
                   jax.ShapeDtypeStruct((bsz, N_HEADS, 1, 1), F32)]
                  + [jax.ShapeDtypeStruct(w.shape, BF16) for w in f32_weights],
        scratch_shapes=[
            pltpu.VMEM((2, tm, d), BF16),
            pltpu.VMEM((2, tm, d), BF16),
            pltpu.VMEM((2, N_HEADS, tm, dx), BF16),
            pltpu.VMEM((2, tm, d), F32),
            pltpu.VMEM((2, nc, N_HEADS, CHUNK, CHUNK), F32),
            pltpu.VMEM((2, nc, N_HEADS, CHUNK, LANES), F32),
            pltpu.VMEM((2, nc, N_HEADS, CHUNK, LANES), F32),
            pltpu.VMEM((2, nc, N_HEADS, CHUNK, LANES), F32),
            pltpu.VMEM((N_HEADS, dh, dx), F32),
            pltpu.VMEM((N_HEADS, 1, LANES), F32)],
        compiler_params=_params(1),
        name="mlstm_prompt",
    )(x.reshape(bsz * t, d), wqkvo, wgt, bgc, mhg, *f32_weights)
    return hm, c, n.reshape(bsz, N_HEADS, dh), m.reshape(bsz, N_HEADS), casted


def _mlstm_sample_pre_kernel(x_ref, wqkvo_ref, wgc_ref, bgr_ref, mhg_ref, n0_ref, m0_ref,
                             q_ref, kw_ref, v_ref, og_ref, dec_ref, wa_ref, wb_ref, n_ref, m_ref,
                             *, n_steps):
    rows, d = q_ref.shape
    dh = d // N_HEADS
    nb = rows // n_steps
    T = n_steps
    slab = [slice(t * nb, (t + 1) * nb) for t in range(T)]

    x = x_ref[...]
    xh, xl = _split_bf16(x)
    _project_qkvo(xh, wqkvo_ref, mhg_ref, q_ref, kw_ref, v_ref, og_ref, d, dh)
    g = _dot(xh, wgc_ref[0]) + _dot(xl, wgc_ref[0]) + _dot(xh, wgc_ref[1]) + bgr_ref[...]
    is_input_gate = lax.broadcasted_iota(jnp.int32, g.shape, 1) < N_HEADS
    g = jnp.where(is_input_gate, g, _log_sigmoid(g))
    for hd in range(N_HEADS):
        hc = slice(hd * dh, (hd + 1) * dh)
        qf = [q_ref[slab[t], hc] for t in range(T)]
        kf = [kw_ref[slab[t], hc] for t in range(T)]
        vf = [v_ref[slab[t], hc] for t in range(T)]
        li = [g[slab[t], hd:hd + 1] for t in range(T)]
        lf = [g[slab[t], N_HEADS + hd:N_HEADS + hd + 1] for t in range(T)]
        m0 = m0_ref[:, hd:hd + 1]
        n0 = n0_ref[:, hc]
        b = [lf[0]]
        for t in range(1, T):
            b.append(b[t - 1] + lf[t])
        a = [li[t] - b[t] for t in range(T)]
        m_new = None
        for t in range(T):
            inter = b[t] + m0
            m_t = inter
            for s in range(t + 1):
                m_t = jnp.maximum(m_t, b[t] + a[s])
            w_inter = jnp.exp(inter - m_t)
            num = jnp.zeros((nb, dh), F32)
            den = w_inter * jnp.sum(qf[t] * n0, axis=1, keepdims=True)
            for s in range(t + 1):
                s_w = (jnp.sum(qf[t] * kf[s], axis=1, keepdims=True)
                       * jnp.exp(b[t] + a[s] - m_t))
                num = num + s_w * vf[s]
                den = den + s_w
            inv = 1.0 / jnp.maximum(jnp.abs(den), jnp.exp(-m_t))
            wa_ref[slab[t], hc] = jnp.broadcast_to(w_inter * inv, (nb, dh))
            wb_ref[slab[t], hc] = num * inv
            m_new = m_t
        b_last = b[T - 1]
        decay = jnp.exp(b_last + m0 - m_new)
        n_new = decay * n0
        for s in range(T):
            kw = kf[s] * jnp.exp(a[s] + b_last - m_new)
            kw_ref[slab[s], hc] = kw
            n_new = n_new + kw
        n_ref[:, hc] = n_new
        m_ref[:, hd:hd + 1] = m_new
        dec_ref[:, hc] = jnp.broadcast_to(decay, (nb, dh))


def _mlstm_sample_pre(x_tm, wqkvo, wgc, bgr, mhg, n0, m0, n_steps):
    rows, d = x_tm.shape
    nb = rows // n_steps
    whole = lambda shape: pl.BlockSpec(shape, lambda i: (0,) * len(shape))
    big = jax.ShapeDtypeStruct((rows, d), F32)
    small = jax.ShapeDtypeStruct((nb, d), F32)
    return pl.pallas_call(
        functools.partial(_mlstm_sample_pre_kernel, n_steps=n_steps),
        grid=(1,),
        in_specs=[_const_spec(x_tm.shape), _qkvo_spec(d), _const_spec(wgc.shape),
                  _const_spec(bgr.shape), _const_spec(mhg.shape), _const_spec(n0.shape),
                  _const_spec(m0.shape)],
        out_specs=[whole((rows, d))] * 4 + [whole((nb, d))] + [whole((rows, d))] * 2
                  + [whole((nb, d)), whole((nb, N_HEADS))],
        out_shape=[big, big, big, big, small, big, big, small,
                   jax.ShapeDtypeStruct((nb, N_HEADS), F32)],
        compiler_params=_params(1),
        name="mlstm_sample_pre",
    )(x_tm, wqkvo, wgc, bgr, mhg, n0, m0)


def _stream_matrix_memory(c0_ref, q_ref, kw_ref, v_ref, dec_ref, c_ref, qc_ref):
    T, bb, dh = q_ref.shape
    owner = lax.broadcasted_iota(jnp.int32, (T * bb, dh), 0) % bb
    q_blk = jnp.concatenate([q_ref[t] for t in range(T)], axis=0).astype(BF16)
    kw_blk = jnp.concatenate([kw_ref[t] for t in range(T)], axis=0).astype(BF16)
    v_blk = jnp.concatenate([v_ref[t] for t in range(T)], axis=0)
    c_olds = [c0_ref[bi, 0] for bi in range(bb)]
    reads = [_dot(q_blk, c_olds[bi].astype(BF16)) for bi in range(bb)]
    qc = jnp.zeros((T * bb, dh), F32)
    for bi in range(bb):
        qc = jnp.where(owner == bi, reads[bi], qc)
    for t in range(T):
        qc_ref[t] = qc[t * bb:(t + 1) * bb]
    for bi in range(bb):
        upd = _dot_tn(kw_blk, jnp.where(owner == bi, v_blk, 0.0).astype(BF16))
        c_ref[bi, 0] = dec_ref[bi:bi + 1, :] * c_olds[bi] + upd


def _mlstm_sample_post_kernel(wa_ref, wb_ref, qc_ref, og_ref, hm_ref):
    d = hm_ref.shape[1]
    dh = d // N_HEADS
    for hd in range(N_HEADS):
        hc = slice(hd * dh, (hd + 1) * dh)
        hh = wa_ref[:, hc] * qc_ref[:, hc] + wb_ref[:, hc]
        hm_ref[:, hc] = _head_norm(hh) * og_ref[:, hc]


def _mlstm_sample_post(wa, wb, qc, og):
    spec = _const_spec(wa.shape)
    return pl.pallas_call(
        _mlstm_sample_post_kernel,
        grid=(1,),
        in_specs=[spec, spec, spec, spec],
        out_specs=pl.BlockSpec(wa.shape, lambda i: (0, 0)),
        out_shape=jax.ShapeDtypeStruct(wa.shape, F32),
        compiler_params=_params(1),
        name="mlstm_sample_post",
    )(wa, wb, qc, og)


def _merge_kernel(x_ref, m1_ref, hm_ref, wmo_ref, wgm_ref, wo_ref, g_ref, b_ref, o_ref, *, alpha):
    tm = x_ref.shape[0]
    for r0 in range(0, tm, SUB_TILE):
        rows = slice(r0, min(r0 + SUB_TILE, tm))
        x = x_ref[rows, :]
        ym = _dot(hm_ref[rows, :].astype(BF16), wmo_ref[...])
        gate = _dot(x.astype(BF16), wgm_ref[...])
        merged = m1_ref[rows, :] + _sigmoid(gate) * ym
        r = alpha * x + _dot(merged.astype(BF16), wo_ref[...])
        o_ref[rows, :] = _layer_norm(r, g_ref[...], b_ref[...])


def _merge(x, m1, hm, wmo, wgm, wo, g, b, alpha):
    n, d = x.shape
    tm = min(TOKEN_TILE, n)
    tile = pl.BlockSpec((tm, d), lambda i: (i, 0))
    return pl.pallas_call(
        functools.partial(_merge_kernel, alpha=alpha),
        grid=(n // tm,),
        in_specs=[tile, tile, tile, _const_spec(wmo.shape), _gate_spec(d, 1),
                  _const_spec(wo.shape), _const_spec(g.shape), _const_spec(b.shape)],
        out_specs=tile,
        out_shape=jax.ShapeDtypeStruct((n, d), F32),
        compiler_params=_params(1),
        name="merge_ln1",
    )(x, m1, hm, wmo, wgm, wo, g, b)


def _ffn_kernel(x_ref, w1_ref, w2_ref, g_ref, b_ref, *rest, alpha, stream):
    if stream:
        c0_ref, q_ref, kw_ref, v_ref, dec_ref, o_ref, c_ref, qc_ref, hid_s = rest
        _stream_matrix_memory(c0_ref, q_ref, kw_ref, v_ref, dec_ref, c_ref, qc_ref)
    else:
        o_ref, hid_s = rest
    tm, dff = hid_s.shape
    sw = 1024 if dff % 1024 == 0 else dff
    for r0 in range(0, tm, SUB_TILE):
        rows = slice(r0, min(r0 + SUB_TILE, tm))
        x = x_ref[rows, :]
        xb = x.astype(BF16)
        for s in range(dff // sw):
            cs = slice(s * sw, (s + 1) * sw)
            hid = jnp.maximum(_dot(xb, w1_ref[:, cs]), 0.0)
            hid_s[rows, cs] = (hid * hid).astype(BF16)
        r = alpha * x + _dot(hid_s[rows, :], w2_ref[...])
        o_ref[rows, :] = _layer_norm(r, g_ref[...], b_ref[...])


def _ffn(x, w1, w2, g, b, alpha):
    n, d = x.shape
    tm = min(TOKEN_TILE, n)
    tile = pl.BlockSpec((tm, d), lambda i: (i, 0))
    return pl.pallas_call(
        functools.partial(_ffn_kernel, alpha=alpha, stream=False),
        grid=(n // tm,),
        in_specs=[tile, _const_spec(w1.shape), _const_spec(w2.shape), _const_spec(g.shape),
                  _const_spec(b.shape)],
        out_specs=tile,
        out_shape=jax.ShapeDtypeStruct((n, d), F32),
        scratch_shapes=[pltpu.VMEM((tm, w1.shape[1]), BF16)],
        compiler_params=_params(1),
        name="ffn_ln2",
    )(x, w1, w2, g, b)


def _ffn_with_stream(x, w1, w2, g, b, alpha, c0, q, kw, v, dec, n_steps):
    n, d = x.shape
    rows = q.shape[0]
    nb = rows // n_steps
    dh = d // N_HEADS
    bb = SAMPLE_BATCH_BLOCK
    n_grid = (nb // bb) * N_HEADS
    tm = n // n_grid
    assert tm * n_grid == n and tm % 8 == 0 and nb % bb == 0
    tile = pl.BlockSpec((tm, d), lambda s: (s, 0))
    cblock = pl.BlockSpec((bb, 1, dh, dh), lambda s: (s // N_HEADS, s % N_HEADS, 0, 0))
    tblock = pl.BlockSpec((n_steps, bb, dh), lambda s: (0, s // N_HEADS, s % N_HEADS))
    dblock = pl.BlockSpec((bb, dh), lambda s: (s // N_HEADS, s % N_HEADS))
    as_tbd = lambda a: a.reshape(n_steps, nb, d)
    x2, c_new, qc = pl.pallas_call(
        functools.partial(_ffn_kernel, alpha=alpha, stream=True),
        grid=(n_grid,),
        in_specs=[tile, _const_spec(w1.shape), _const_spec(w2.shape), _const_spec(g.shape),
                  _const_spec(b.shape), cblock, tblock, tblock, tblock, dblock],
        out_specs=[tile, cblock, tblock],
        out_shape=[jax.ShapeDtypeStruct((n, d), F32), jax.ShapeDtypeStruct(c0.shape, F32),
                   jax.ShapeDtypeStruct((n_steps, nb, d), F32)],
        scratch_shapes=[pltpu.VMEM((tm, w1.shape[1]), BF16)],
        compiler_params=_params(1),
        name="ffn_ln2_stream",
    )(x, w1, w2, g, b, c0, as_tbd(q), as_tbd(kw), as_tbd(v), dec)
    return x2, c_new, qc.reshape(rows, d)


def _w_in_prep_kernel(a_ref, g_ref, b_ref, wa_ref, wg_ref, wb_ref, *, n_a):
    j = pl.program_id(0)

    @pl.when(j == 0)
    def _():
        wg_ref[...] = g_ref[...]

    @pl.when(j < n_a)
    def _():
        wa_ref[...] = a_ref[...].T.astype(BF16)

    @pl.when(j >= n_a)
    def _():
        wb_ref[...] = b_ref[...].T.astype(BF16)


def _prepare_w_in(w_in_t):
    d = w_in_t.shape[1]
    n_a, n_b = 7, 2
    h2 = 2 * N_HEADS
    off_b = n_a * d + h2
    return pl.pallas_call(
        functools.partial(_w_in_prep_kernel, n_a=n_a),
        grid=(n_a + n_b,),
        in_specs=[pl.BlockSpec((d, d), lambda j: (jnp.minimum(j, n_a - 1), 0)),
                  pl.BlockSpec((pl.Element(h2), pl.Element(d)), lambda j: (n_a * d, 0)),
                  pl.BlockSpec((pl.Element(d), pl.Element(d)),
                               lambda j: (pl.multiple_of(off_b + jnp.maximum(j - n_a, 0) * d, 8),
                                          0))],
        out_specs=[pl.BlockSpec((d, d), lambda j: (0, jnp.minimum(j, n_a - 1))),
                   pl.BlockSpec((h2, d), lambda j: (0, 0)),
                   pl.BlockSpec((d, d), lambda j: (0, jnp.maximum(j - n_a, 0)))],
        out_shape=[jax.ShapeDtypeStruct((d, n_a * d), BF16),
                   jax.ShapeDtypeStruct((h2, d), F32),
                   jax.ShapeDtypeStruct((d, n_b * d), BF16)],
        compiler_params=_params(1),
        name="w_in_prep",
    )(w_in_t, w_in_t, w_in_t)


def _layer_weights(w_in, b_gate, conv_w, w_conv_out, mh_g, w_m_out, w_o, ln1_g, ln1_b,
                   w_ff1, w_ff2, ln2_g, ln2_b):
    d = w_in.shape[0]
    h2 = 2 * N_HEADS
    w_in_t = jnp.swapaxes(w_in, 0, 1)
    w_a, wg_t, w_b = _prepare_w_in(w_in_t)
    wgt_hi, wgt_lo = _split_bf16(wg_t)
    wgt = jnp.stack([jnp.concatenate([wgt_hi, wgt_lo], axis=0),
                     jnp.concatenate([wgt_hi, jnp.zeros_like(wgt_hi)], axis=0)])
    pad = ((0, 0), (0, GATE_PAD - h2))
    wgc = jnp.stack([jnp.pad(wgt_hi.T, pad), jnp.pad(wgt_lo.T, pad)])
    return dict(
        w_a=w_a, w_b=w_b,
        wgt=wgt, wgc=wgc,
        bgc=b_gate.reshape(h2, 1).astype(F32),
        bgr=jnp.pad(b_gate.reshape(1, h2).astype(F32), pad),
        cw=conv_w.astype(F32),
        mhg=mh_g.reshape(1, d).astype(F32),
        f32_weights=[w_conv_out, w_m_out, w_o, w_ff1, w_ff2],
        ln1_g=ln1_g.reshape(1, d), ln1_b=ln1_b.reshape(1, d),
        ln2_g=ln2_g.reshape(1, d), ln2_b=ln2_b.reshape(1, d))


def _layer(x, xs_tm, conv_tm, c0, n0, m0, p, alpha, n_steps):
    bsz, t, d = x.shape
    hm, c_p, n_p, m_p, (wco, wmo, wo, w1, w2) = _mlstm_prompt(
        x, p["w_a"], p["wgt"], p["bgc"], p["mhg"], p["f32_weights"])
    x1, conv_p = _conv_merge_prompt(x, hm, p["w_a"], p["cw"], wco, p["w_b"], wmo, wo,
                                    p["ln1_g"], p["ln1_b"], alpha)
    m1_s, conv_s = _conv_branch_sample(xs_tm, conv_tm, p["w_a"], p["cw"], wco, p["w_b"], n_steps)
    q, kw, v, og, dec, wa, wb, n_s, m_s = _mlstm_sample_pre(
        xs_tm, p["w_a"], p["wgc"], p["bgr"], p["mhg"], n0.reshape(n0.shape[0], d), m0, n_steps)
    x2, c_s, qc = _ffn_with_stream(x1, w1, w2, p["ln2_g"], p["ln2_b"], alpha,
                                   c0, q, kw, v, dec, n_steps)
    hm_s = _mlstm_sample_post(wa, wb, qc, og)
    x1_s = _merge(xs_tm, m1_s, hm_s, wmo, p["w_b"], wo, p["ln1_g"], p["ln1_b"], alpha)
    x2_s = _ffn(x1_s, w1, w2, p["ln2_g"], p["ln2_b"], alpha)
    return (x2.reshape(bsz, t, d), conv_p, c_p, n_p, m_p,
            x2_s, conv_s, c_s, n_s.reshape(n0.shape), m_s)


def kernel(x_prompt, x_sample, state_conv, state_C, state_n, state_m, w_in, b_gate, conv_w,
           w_conv_out, mh_g, w_m_out, w_o, ln1_g, ln1_b, w_ff1, w_ff2, ln2_g, ln2_b):
    depth = w_in.shape[0]
    alpha = (2.0 * depth) ** 0.25
    bsz, t, d = x_prompt.shape
    sb, st, _ = x_sample.shape
    assert t % TOKEN_TILE == 0 and t % MLSTM_TILE == 0 and MLSTM_TILE % CHUNK == 0
    assert d % N_HEADS == 0
    assert sb % SAMPLE_BATCH_BLOCK == 0 and st >= CONV_W - 1

    xp = x_prompt
    xs = jnp.transpose(x_sample, (1, 0, 2)).reshape(st * sb, d)
    outs = [[] for _ in range(8)]
    for l in range(depth):
        p = _layer_weights(w_in[l], b_gate[l], conv_w[l], w_conv_out[l], mh_g[l], w_m_out[l],
                           w_o[l], ln1_g[l], ln1_b[l], w_ff1[l], w_ff2[l], ln2_g[l], ln2_b[l])
        conv_tm = jnp.transpose(state_conv[l], (1, 0, 2)).reshape((CONV_W - 1) * sb, d)
        xp, cp, c_p, n_p, m_p, xs, cs_tm, c_s, n_s, m_s = _layer(
            xp, xs, conv_tm, state_C[l], state_n[l], state_m[l], p, alpha, st)
        cs = jnp.transpose(cs_tm.reshape(CONV_W - 1, sb, d), (1, 0, 2))
        for acc, val in zip(outs, (cp, cs, c_p, c_s, n_p, n_s, m_p, m_s)):
            acc.append(val)
    ys = jnp.transpose(xs.reshape(st, sb, d), (1, 0, 2))
    return (xp, ys) + tuple(jnp.stack(acc) for acc in outs)
```

```python
import functools

import jax
import jax.numpy as jnp
from jax import lax
from jax.experimental import pallas as pl
from jax.experimental.pallas import tpu as pltpu

F32 = jnp.float32
BF16 = jnp.bfloat16

LN_EPS = 1e-5
N_HEADS = 4
CHUNK = 128
CONV_W = 3
TOKEN_TILE = 1024
SUB_TILE = 512
MLSTM_TILE = 512
SAMPLE_BATCH_BLOCK = 16
LANES = 128
GATE_PAD = LANES
VMEM_LIMIT = 56 * 1024 * 1024


def _dot(a, b):
    return jnp.dot(a, b, preferred_element_type=F32)


def _dot_nt(a, b):
    return lax.dot_general(a, b, (((1,), (1,)), ((), ())), preferred_element_type=F32)


def _dot_tn(a, b):
    return lax.dot_general(a, b, (((0,), (0,)), ((), ())), preferred_element_type=F32)


def _sigmoid(x):
    return 1.0 / (1.0 + jnp.exp(-x))


def _log_sigmoid(x):
    return jnp.minimum(x, 0.0) - jnp.log1p(jnp.exp(-jnp.abs(x)))


def _split_bf16(x):
    hi = x.astype(BF16)
    lo = (x - hi.astype(F32)).astype(BF16)
    return hi, lo


def _layer_norm(r, g, b):
    mu = jnp.mean(r, axis=-1, keepdims=True)
    xc = r - mu
    var = jnp.mean(xc * xc, axis=-1, keepdims=True)
    return xc * lax.rsqrt(var + LN_EPS) * g + b


def _head_norm(h):
    mu = jnp.mean(h, axis=-1, keepdims=True)
    hc = h - mu
    return hc * lax.rsqrt(jnp.mean(hc * hc, axis=-1, keepdims=True) + LN_EPS)


def _const_spec(shape):
    zeros = (0,) * len(shape)
    return pl.BlockSpec(shape, lambda *_: zeros, pipeline_mode=pl.Buffered(1))


def _window_spec(block_shape, index):
    return pl.BlockSpec(block_shape, lambda *_: index, pipeline_mode=pl.Buffered(1))


def _bch_spec(d):
    return _window_spec((d, 3 * d), (0, 0))


def _qkvo_spec(d):
    return _window_spec((pl.Element(d), pl.Element(4 * d)), (0, 3 * d))


def _gate_spec(d, which):
    return _window_spec((d, d), (0, which))


def _params(n_axes):
    return pltpu.CompilerParams(dimension_semantics=("arbitrary",) * n_axes,
                                vmem_limit_bytes=VMEM_LIMIT)


def _conv_stripes(d):
    sw = 512 if d % 512 == 0 else d
    return [slice(s * sw, (s + 1) * sw) for s in range(d // sw)]


def _conv_merge_prompt_kernel(x_ref, hm_ref, wbch_ref, cw_ref, wco_ref, wgc_ref, wmo_ref, wgm_ref,
                              wo_ref, g_ref, b_ref, o_ref, cs_ref, u_s, a_s, *, alpha):
    tm, d = a_s.shape
    @pl.when(pl.program_id(1) == 0)
    def _():
        u_s[0:8, :] = jnp.zeros((8, d), F32)

    for r0 in range(0, tm, SUB_TILE):
        sub = min(SUB_TILE, tm - r0)
        x = x_ref[0, r0:r0 + sub, :]
        xb = x.astype(BF16)
        for cs in _conv_stripes(d):
            off = cs.start
            bg = _dot(xb, wbch_ref[:, off:cs.stop])
            cg = _dot(xb, wbch_ref[:, d + off:d + cs.stop])
            hc = _dot(xb, wbch_ref[:, 2 * d + off:2 * d + cs.stop])
            u = cg * hc
            u_s[8 + r0:8 + r0 + sub, cs] = u
            conv = (u_s[6 + r0:6 + r0 + sub, cs] * cw_ref[0:1, cs]
                    + u_s[7 + r0:7 + r0 + sub, cs] * cw_ref[1:2, cs] + u * cw_ref[2:3, cs])
            a_s[r0:r0 + sub, cs] = (bg * conv).astype(BF16)
        gate_c = _dot(xb, wgc_ref[...])
        gate_m = _dot(xb, wgm_ref[...])
        ym = _dot(hm_ref[r0:r0 + sub, :], wmo_ref[...])
        yc = _dot(a_s[r0:r0 + sub, :], wco_ref[...])
        merged = _sigmoid(gate_c) * yc + _sigmoid(gate_m) * ym
        r = alpha * x + _dot(merged.astype(BF16), wo_ref[...])
        o_ref[r0:r0 + sub, :] = _layer_norm(r, g_ref[...], b_ref[...])
    cs_ref[0] = u_s[tm + 6:tm + 8, :]
    u_s[0:8, :] = u_s[tm:tm + 8, :]


def _conv_sample_kernel(x_ref, st_ref, wbch_ref, cw_ref, wco_ref, wgc_ref, m1_ref, cs_ref, a_s,
                        *, n_steps):
    rows, d = a_s.shape
    nb = rows // n_steps
    xb = x_ref[...].astype(BF16)
    for cs in _conv_stripes(d):
        off = cs.start
        bg = _dot(xb, wbch_ref[:, off:cs.stop])
        cg = _dot(xb, wbch_ref[:, d + off:d + cs.stop])
        hc = _dot(xb, wbch_ref[:, 2 * d + off:2 * d + cs.stop])
        u = cg * hc
        up = [st_ref[j * nb:(j + 1) * nb, cs] for j in range(CONV_W - 1)]
        up += [u[t * nb:(t + 1) * nb] for t in range(n_steps)]
        for t in range(n_steps):
            conv = (up[t] * cw_ref[0:1, cs] + up[t + 1] * cw_ref[1:2, cs]
                    + up[t + 2] * cw_ref[2:3, cs])
            a_s[t * nb:(t + 1) * nb, cs] = (bg[t * nb:(t + 1) * nb] * conv).astype(BF16)
        for j in range(CONV_W - 1):
            cs_ref[j * nb:(j + 1) * nb, cs] = up[n_steps + j]
    yc = _dot(a_s[...], wco_ref[...])
    m1_ref[...] = _sigmoid(_dot(xb, wgc_ref[...])) * yc


def _conv_merge_prompt(x, hm, w_a, cw, wco, w_b, wmo, wo, g, b, alpha):
    bsz, t, d = x.shape
    tm = TOKEN_TILE
    flat = pl.BlockSpec((tm, d), lambda bi, j: (bi * (t // tm) + j, 0))
    return pl.pallas_call(
        functools.partial(_conv_merge_prompt_kernel, alpha=alpha),
        grid=(bsz, t // tm),
        in_specs=[pl.BlockSpec((1, tm, d), lambda bi, j: (bi, j, 0)), flat,
                  _bch_spec(d), _const_spec(cw.shape), _const_spec(wco.shape),
                  _gate_spec(d, 0), _const_spec(wmo.shape), _gate_spec(d, 1),
                  _const_spec(wo.shape), _const_spec(g.shape), _const_spec(b.shape)],
        out_specs=[flat, pl.BlockSpec((1, CONV_W - 1, d), lambda bi, j: (bi, 0, 0))],
        out_shape=[jax.ShapeDtypeStruct((bsz * t, d), F32),
                   jax.ShapeDtypeStruct((bsz, CONV_W - 1, d), F32)],
        scratch_shapes=[pltpu.VMEM((tm + 8, d), F32), pltpu.VMEM((tm, d), BF16)],
        compiler_params=_params(2),
        name="conv_merge_prompt",
    )(x, hm, w_a, cw, wco, w_b, wmo, w_b, wo, g, b)


def _conv_branch_sample(x_tm, st_tm, wbch, cw, wco, wgc, n_steps):
    rows, d = x_tm.shape
    return pl.pallas_call(
        functools.partial(_conv_sample_kernel, n_steps=n_steps),
        grid=(1,),
        in_specs=[_const_spec(x_tm.shape), _const_spec(st_tm.shape), _bch_spec(d),
                  _const_spec(cw.shape), _const_spec(wco.shape), _gate_spec(d, 0)],
        out_specs=[pl.BlockSpec((rows, d), lambda i: (0, 0)),
                   pl.BlockSpec(st_tm.shape, lambda i: (0, 0))],
        out_shape=[jax.ShapeDtypeStruct((rows, d), F32),
                   jax.ShapeDtypeStruct(st_tm.shape, F32)],
        scratch_shapes=[pltpu.VMEM((rows, d), BF16)],
        compiler_params=_params(1),
        name="conv_branch_sample",
    )(x_tm, st_tm, wbch, cw, wco, wgc)


def _project_qkvo(xh, wqkvo_ref, mhg_ref, q_s, k_s, v_s, og_s, d, dh):
    q_s[...] = _dot(xh, wqkvo_ref[:, 0:d]).astype(BF16).astype(q_s.dtype)
    k_s[...] = (_dot(xh, wqkvo_ref[:, d:2 * d]) * (dh ** -0.5)).astype(BF16).astype(k_s.dtype)
    v_s[...] = _dot(xh, wqkvo_ref[:, 2 * d:3 * d]).astype(BF16).astype(v_s.dtype)
    og_s[...] = mhg_ref[...] * _sigmoid(_dot(xh, wqkvo_ref[:, 3 * d:4 * d]))


def _rep(col, times):
    return col if times == 1 else jnp.concatenate([col] * times, axis=1)


def _mlstm_prompt_step(x_ref, wqkvo_ref, wgt_ref, bgc_ref, mhg_ref, hm_ref,
                       q_s, k_s, vx_s, og_s, e_s, colb_s, colm_s, cole_s, cst_s, mst_s,
                       *, cur, prev):
    _, tm, d = q_s.shape
    dh = d // N_HEADS
    L = CHUNK
    LANES = colb_s.shape[-1]
    n_chunks = tm // L
    wide = (dh + LANES) // LANES

    x = x_ref[...]
    xh, xl = _split_bf16(x)

    ga = _dot_nt(wgt_ref[0], xh)
    gb = _dot_nt(wgt_ref[1], xl)
    gt = ga[0:8] + ga[8:16] + gb[0:8] + bgc_ref[...]
    is_input_gate = lax.broadcasted_iota(jnp.int32, gt.shape, 0) < N_HEADS
    g = jnp.where(is_input_gate, gt, _log_sigmoid(gt))

    pos = lax.broadcasted_iota(jnp.int32, g.shape, 1) % L
    csum = g
    shift = 1
    while shift < L:
        csum = csum + jnp.where(pos >= shift, pltpu.roll(csum, shift, 1), 0.0)
        shift *= 2
    ba = jnp.concatenate([csum[N_HEADS:], g[:N_HEADS] - csum[N_HEADS:]], axis=0)

    ri = lax.broadcasted_iota(jnp.int32, (L, L), 0)
    ci = lax.broadcasted_iota(jnp.int32, (L, L), 1)
    causal = ri >= ci

    for c in range(n_chunks):
        slab = ba[:, c * L:(c + 1) * L]
        cols = slab.T
        for h in range(N_HEADS):
            b_rep = jnp.broadcast_to(cols[:, h:h + 1], (L, LANES))
            a_rep = jnp.broadcast_to(cols[:, N_HEADS + h:N_HEADS + h + 1], (L, LANES))
            a_row = slab[N_HEADS + h:N_HEADS + h + 1, :]
            dm = jnp.where(causal, _rep(b_rep, L // LANES) + a_row, -jnp.inf)
            m_loc = jnp.broadcast_to(jnp.max(dm, axis=1, keepdims=True), (L, LANES))
            e_s[cur, c, h] = jnp.exp(dm - _rep(m_loc, L // LANES))
            colb_s[cur, c, h] = b_rep
            colm_s[cur, c, h] = m_loc
            cole_s[cur, c, h] = jnp.exp(a_rep + b_rep[L - 1:L, :] - m_loc[L - 1:L, :])

    n_split = 2 if N_HEADS % 2 == 0 else 1
    piece = d // n_split

    def project(part, split):
        lo = split * piece
        cols = slice(lo, lo + piece)
        y = _dot(xh, wqkvo_ref[:, part * d + lo:part * d + lo + piece])
        if part == 0:
            q_s[cur, :, cols] = y.astype(BF16)
        elif part == 1:
            k_s[cur, :, cols] = (y * (dh ** -0.5)).astype(BF16)
        elif part == 2:
            v = y.astype(BF16)
            for h in range(lo // dh, (lo + piece) // dh):
                vx_s[cur, h, :, 0:dh] = v[:, h * dh - lo:(h + 1) * dh - lo]
                vx_s[cur, h, :, dh:] = jnp.ones((tm, LANES), BF16)
        else:
            og_s[cur, :, cols] = mhg_ref[:, cols] * _sigmoid(y)

    heads = [slice(h * dh, (h + 1) * dh) for h in range(N_HEADS)]

    def recur_ready(c):
        rows = slice(c * L, (c + 1) * L)
        qk = [_dot_nt(q_s[prev, rows, hc], k_s[prev, rows, hc]) for hc in heads]
        states = [cst_s[h] for h in range(N_HEADS)]
        qc = [_dot(q_s[prev, rows, hc], states[h].astype(BF16)) for h, hc in enumerate(heads)]
        ux = []
        for h, hc in enumerate(heads):
            kw = k_s[prev, rows, hc].astype(F32) * _rep(cole_s[prev, c, h], dh // LANES)
            ux.append(_dot_tn(kw.astype(BF16), vx_s[prev, h, rows, :]))
        return qk, states, qc, ux

    def recur_finish(c, qk, states, qc, ux):
        rows = slice(c * L, (c + 1) * L)
        svx = []
        for h in range(N_HEADS):
            s_loc = qk[h] * e_s[prev, c, h]
            svx.append(_dot(s_loc.astype(BF16), vx_s[prev, h, rows, :]))
        for h, hc in enumerate(heads):
            m_prev = mst_s[h]
            m_loc = colm_s[prev, c, h]
            inter = colb_s[prev, c, h] + m_prev
            m_t = jnp.maximum(inter, m_loc)
            w_inter = _rep(jnp.exp(inter - m_t), wide)
            w_loc = _rep(jnp.exp(m_loc - m_t), wide)
            nd = w_inter * qc[h] + w_loc * svx[h]
            inv = 1.0 / jnp.maximum(jnp.abs(nd[:, dh:]), jnp.exp(-m_t))
            hh = nd[:, 0:dh] * _rep(inv, dh // LANES)
            hm_ref[rows, hc] = (_head_norm(hh) * og_s[prev, rows, hc]).astype(BF16)
            m_new = m_t[L - 1:L, :]
            decay = _rep(jnp.exp(inter[L - 1:L, :] - m_new), wide)
            grow = _rep(jnp.exp(m_loc[L - 1:L, :] - m_new), wide)
            cst_s[h] = decay * states[h] + grow * ux[h]
            mst_s[h] = m_new

    pieces = [(part, split) for part in range(4) for split in range(n_split)]
    n_phases = 2 * n_chunks
    done = 0
    for c in range(n_chunks):
        for phase in (2 * c, 2 * c + 1):
            if phase % 2 == 0:
                partial = recur_ready(c)
            else:
                recur_finish(c, *partial)
            upto = (phase + 1) * len(pieces) // n_phases
            for part, split in pieces[done:upto]:
                project(part, split)
            done = upto


def _mlstm_prompt_kernel(*refs, tiles_per_seq, n_cast):
    x_ref, wqkvo_ref, wgt_ref, bgc_ref, mhg_ref = refs[:5]
    cast_src = refs[5:5 + n_cast]
    hm_ref, c_ref, n_ref, m_ref = refs[5 + n_cast:9 + n_cast]
    cast_dst = refs[9 + n_cast:9 + 2 * n_cast]
    q_s, k_s, vx_s, og_s, e_s, colb_s, colm_s, cole_s, cst_s, mst_s = refs[9 + 2 * n_cast:]
    dh = cst_s.shape[1]
    g_step = pl.program_id(0)
    for src, dst in zip(cast_src, cast_dst):
        dst[...] = src[...].astype(BF16)

    @pl.when(g_step == 0)
    def _():
        for ref in (q_s, k_s, vx_s, og_s, e_s, colb_s, colm_s, cole_s):
            ref[1] = jnp.zeros(ref.shape[1:], ref.dtype)

    @pl.when(jnp.logical_or(g_step == 0, (g_step - 1) % tiles_per_seq == 0))
    def _():
        cst_s[...] = jnp.zeros(cst_s.shape, F32)
        mst_s[...] = jnp.zeros(mst_s.shape, F32)

    for parity in (0, 1):
        @pl.when(g_step % 2 == parity)
        def _():
            _mlstm_prompt_step(x_ref, wqkvo_ref, wgt_ref, bgc_ref, mhg_ref, hm_ref,
                               q_s, k_s, vx_s, og_s, e_s, colb_s, colm_s, cole_s, cst_s, mst_s,
                               cur=parity, prev=1 - parity)

    @pl.when(jnp.logical_and(g_step >= 1, (g_step - 1) % tiles_per_seq == tiles_per_seq - 1))
    def _():
        for h in range(N_HEADS):
            state = cst_s[h]
            c_ref[0, h] = state[:, 0:dh]
            n_ref[0, h] = state[:, dh:].T[0:1, :]
            m_ref[0, h] = mst_s[h][:, 0:1]


def _mlstm_prompt(x, wqkvo, wgt, bgc, mhg, f32_weights):
    bsz, t, d = x.shape
    dh = d // N_HEADS
    tm = MLSTM_TILE
    nc = tm // CHUNK
    dx = dh + LANES
    tps = t // tm
    n_tiles = bsz * tps
    seq_block = lambda g: (jnp.maximum(g - 1, 0) // tps, 0, 0, 0)
    cast_specs = []
    for w in f32_weights:
        rows_per_step = w.shape[0] // n_tiles
        assert rows_per_step * n_tiles == w.shape[0] and rows_per_step % 16 == 0
        cast_specs.append(pl.BlockSpec((rows_per_step, w.shape[1]),
                                       lambda g: (jnp.minimum(g, n_tiles - 1), 0)))
    n_cast = len(f32_weights)
    hm, c, n, m, *casted = pl.pallas_call(
        functools.partial(_mlstm_prompt_kernel, tiles_per_seq=tps, n_cast=n_cast),
        grid=(n_tiles + 1,),
        in_specs=[pl.BlockSpec((tm, d), lambda g: (jnp.minimum(g, n_tiles - 1), 0)),
                  _qkvo_spec(d), _const_spec(wgt.shape), _const_spec(bgc.shape),
                  _const_spec(mhg.shape)] + cast_specs,
        out_specs=[pl.BlockSpec((tm, d), lambda g: (jnp.maximum(g - 1, 0), 0)),
                   pl.BlockSpec((1, N_HEADS, dh, dh), seq_block),
                   pl.BlockSpec((1, N_HEADS, 1, dh), seq_block),
                   pl.BlockSpec((1, N_HEADS, 1, 1), seq_block)] + cast_specs,
        out_shape=[jax.ShapeDtypeStruct((bsz * t, d), BF16),
                   jax.ShapeDtypeStruct((bsz, N_HEADS, dh, dh), F32),
                   jax.ShapeDtypeStruct((bsz, N_HEADS, 1, dh), F32),
                   jax.ShapeDtypeStruct((bsz, N_HEADS, 1, 1), F32)]
                  + [jax.ShapeDtypeStruct(w.shape, BF16) for w in f32_weights],
        scratch_shapes=[
            pltpu.VMEM((2, tm, d), BF16),
            pltpu.VMEM((2, tm, d), BF16),
            pltpu.VMEM((2, N_HEADS, tm, dx), BF16),
            pltpu.VMEM((2, tm, d), F32),
            pltpu.VMEM((2, nc, N_HEADS, CHUNK, CHUNK), F32),
            pltpu.VMEM((2, nc, N_HEADS, CHUNK, LANES), F32),
            pltpu.VMEM((2, nc, N_HEADS, CHUNK, LANES), F32),
            pltpu.VMEM((2, nc, N_HEADS, CHUNK, LANES), F32),
            pltpu.VMEM((N_HEADS, dh, dx), F32),
            pltpu.VMEM((N_HEADS, 1, LANES), F32)],
        compiler_params=_params(1),
        name="mlstm_prompt",
    )(x.reshape(bsz * t, d), wqkvo, wgt, bgc, mhg, *f32_weights)
    return hm, c, n.reshape(bsz, N_HEADS, dh), m.reshape(bsz, N_HEADS), casted


def _mlstm_sample_pre_kernel(x_ref, wqkvo_ref, wgc_ref, bgr_ref, mhg_ref, n0_ref, m0_ref,
                             q_ref, kw_ref, v_ref, og_ref, dec_ref, wa_ref, wb_ref, n_ref, m_ref,
                             *, n_steps):
    rows, d = q_ref.shape
    dh = d // N_HEADS
    nb = rows // n_steps
    T = n_steps
    slab = [slice(t * nb, (t + 1) * nb) for t in range(T)]

    x = x_ref[...]
    xh, xl = _split_bf16(x)
    _project_qkvo(xh, wqkvo_ref, mhg_ref, q_ref, kw_ref, v_ref, og_ref, d, dh)
    g = _dot(xh, wgc_ref[0]) + _dot(xl, wgc_ref[0]) + _dot(xh, wgc_ref[1]) + bgr_ref[...]
    is_input_gate = lax.broadcasted_iota(jnp.int32, g.shape, 1) < N_HEADS
    g = jnp.where(is_input_gate, g, _log_sigmoid(g))
    for hd in range(N_HEADS):
        hc = slice(hd * dh, (hd + 1) * dh)
        qf = [q_ref[slab[t], hc] for t in range(T)]
        kf = [kw_ref[slab[t], hc] for t in range(T)]
        vf = [v_ref[slab[t], hc] for t in range(T)]
        li = [g[slab[t], hd:hd + 1] for t in range(T)]
        lf = [g[slab[t], N_HEADS + hd:N_HEADS + hd + 1] for t in range(T)]
        m0 = m0_ref[:, hd:hd + 1]
        n0 = n0_ref[:, hc]
        b = [lf[0]]
        for t in range(1, T):
            b.append(b[t - 1] + lf[t])
        a = [li[t] - b[t] for t in range(T)]
        m_new = None
        for t in range(T):
            inter = b[t] + m0
            m_t = inter
            for s in range(t + 1):
                m_t = jnp.maximum(m_t, b[t] + a[s])
            w_inter = jnp.exp(inter - m_t)
            num = jnp.zeros((nb, dh), F32)
            den = w_inter * jnp.sum(qf[t] * n0, axis=1, keepdims=True)
            for s in range(t + 1):
                s_w = (jnp.sum(qf[t] * kf[s], axis=1, keepdims=True)
                       * jnp.exp(b[t] + a[s] - m_t))
                num = num + s_w * vf[s]
                den = den + s_w
            inv = 1.0 / jnp.maximum(jnp.abs(den), jnp.exp(-m_t))
            wa_ref[slab[t], hc] = jnp.broadcast_to(w_inter * inv, (nb, dh))
            wb_ref[slab[t], hc] = num * inv
            m_new = m_t
        b_last = b[T - 1]
        decay = jnp.exp(b_last + m0 - m_new)
        n_new = decay * n0
        for s in range(T):
            kw = kf[s] * jnp.exp(a[s] + b_last - m_new)
            kw_ref[slab[s], hc] = kw
            n_new = n_new + kw
        n_ref[:, hc] = n_new
        m_ref[:, hd:hd + 1] = m_new
        dec_ref[:, hc] = jnp.broadcast_to(decay, (nb, dh))


def _mlstm_sample_pre(x_tm, wqkvo, wgc, bgr, mhg, n0, m0, n_steps):
    rows, d = x_tm.shape
    nb = rows // n_steps
    whole = lambda shape: pl.BlockSpec(shape, lambda i: (0,) * len(shape))
    big = jax.ShapeDtypeStruct((rows, d), F32)
    small = jax.ShapeDtypeStruct((nb, d), F32)
    return pl.pallas_call(
        functools.partial(_mlstm_sample_pre_kernel, n_steps=n_steps),
        grid=(1,),
        in_specs=[_const_spec(x_tm.shape), _qkvo_spec(d), _const_spec(wgc.shape),
                  _const_spec(bgr.shape), _const_spec(mhg.shape), _const_spec(n0.shape),
                  _const_spec(m0.shape)],
        out_specs=[whole((rows, d))] * 4 + [whole((nb, d))] + [whole((rows, d))] * 2
                  + [whole((nb, d)), whole((nb, N_HEADS))],
        out_shape=[big, big, big, big, small, big, big, small,
                   jax.ShapeDtypeStruct((nb, N_HEADS), F32)],
        compiler_params=_params(1),
        name="mlstm_sample_pre",
    )(x_tm, wqkvo, wgc, bgr, mhg, n0, m0)


def _stream_matrix_memory(c0_ref, q_ref, kw_ref, v_ref, dec_ref, c_ref, qc_ref):
    T, bb, dh = q_ref.shape
    owner = lax.broadcasted_iota(jnp.int32, (T * bb, dh), 0) % bb
    q_blk = jnp.concatenate([q_ref[t] for t in range(T)], axis=0).astype(BF16)
    kw_blk = jnp.concatenate([kw_ref[t] for t in range(T)], axis=0).astype(BF16)
    v_blk = jnp.concatenate([v_ref[t] for t in range(T)], axis=0)
    c_olds = [c0_ref[bi, 0] for bi in range(bb)]
    reads = [_dot(q_blk, c_olds[bi].astype(BF16)) for bi in range(bb)]
    qc = jnp.zeros((T * bb, dh), F32)
    for bi in range(bb):
        qc = jnp.where(owner == bi, reads[bi], qc)
    for t in range(T):
        qc_ref[t] = qc[t * bb:(t + 1) * bb]
    for bi in range(bb):
        upd = _dot_tn(kw_blk, jnp.where(owner == bi, v_blk, 0.0).astype(BF16))
        c_ref[bi, 0] = dec_ref[bi:bi + 1, :] * c_olds[bi] + upd


def _mlstm_sample_post_kernel(wa_ref, wb_ref, qc_ref, og_ref, hm_ref):
    d = hm_ref.shape[1]
    dh = d // N_HEADS
    for hd in range(N_HEADS):
        hc = slice(hd * dh, (hd + 1) * dh)
        hh = wa_ref[:, hc] * qc_ref[:, hc] + wb_ref[:, hc]
        hm_ref[:, hc] = _head_norm(hh) * og_ref[:, hc]


def _mlstm_sample_post(wa, wb, qc, og):
    spec = _const_spec(wa.shape)
    return pl.pallas_call(
        _mlstm_sample_post_kernel,
        grid=(1,),
        in_specs=[spec, spec, spec, spec],
        out_specs=pl.BlockSpec(wa.shape, lambda i: (0, 0)),
        out_shape=jax.ShapeDtypeStruct(wa.shape, F32),
        compiler_params=_params(1),
        name="mlstm_sample_post",
    )(wa, wb, qc, og)


def _merge_kernel(x_ref, m1_ref, hm_ref, wmo_ref, wgm_ref, wo_ref, g_ref, b_ref, o_ref, *, alpha):
    tm = x_ref.shape[0]
    for r0 in range(0, tm, SUB_TILE):
        rows = slice(r0, min(r0 + SUB_TILE, tm))
        x = x_ref[rows, :]
        ym = _dot(hm_ref[rows, :].astype(BF16), wmo_ref[...])
        gate = _dot(x.astype(BF16), wgm_ref[...])
        merged = m1_ref[rows, :] + _sigmoid(gate) * ym
        r = alpha * x + _dot(merged.astype(BF16), wo_ref[...])
        o_ref[rows, :] = _layer_norm(r, g_ref[...], b_ref[...])


def _merge(x, m1, hm, wmo, wgm, wo, g, b, alpha):
    n, d = x.shape
    tm = min(TOKEN_TILE, n)
    tile = pl.BlockSpec((tm, d), lambda i: (i, 0))
    return pl.pallas_call(
        functools.partial(_merge_kernel, alpha=alpha),
        grid=(n // tm,),
        in_specs=[tile, tile, tile, _const_spec(wmo.shape), _gate_spec(d, 1),
                  _const_spec(wo.shape), _const_spec(g.shape), _const_spec(b.shape)],
        out_specs=tile,
        out_shape=jax.ShapeDtypeStruct((n, d), F32),
        compiler_params=_params(1),
        name="merge_ln1",
    )(x, m1, hm, wmo, wgm, wo, g, b)


def _ffn_hidden(xb, w1_ref, hid_s, rows):
    dff = hid_s.shape[1]
    sw = 1024 if dff % 1024 == 0 else dff
    for s in range(dff // sw):
        cs = slice(s * sw, (s + 1) * sw)
        hid = jnp.maximum(_dot(xb, w1_ref[:, cs]), 0.0)
        hid_s[rows, cs] = (hid * hid).astype(BF16)


def _ffn_kernel(x_ref, w1_ref, w2_ref, g_ref, b_ref, o_ref, hid_s, *, alpha):
    tm = hid_s.shape[0]
    for r0 in range(0, tm, SUB_TILE):
        rows = slice(r0, min(r0 + SUB_TILE, tm))
        x = x_ref[rows, :]
        _ffn_hidden(x.astype(BF16), w1_ref, hid_s, rows)
        r = alpha * x + _dot(hid_s[rows, :], w2_ref[...])
        o_ref[rows, :] = _layer_norm(r, g_ref[...], b_ref[...])


def _ffn_stream_kernel(x_ref, w1_ref, w2_ref, g_ref, b_ref, c0_ref, q_ref, kw_ref, v_ref,
                       dec_ref, o_ref, c_ref, qc_ref, hid_s, r_s, *, alpha):
    s = pl.program_id(0)
    last = pl.num_programs(0) - 1

    @pl.when(s == 0)
    def _():
        r_s[...] = jnp.zeros(r_s.shape, F32)

    @pl.when(s < last)
    def _():
        _stream_matrix_memory(c0_ref, q_ref, kw_ref, v_ref, dec_ref, c_ref, qc_ref)
        x = x_ref[...]
        _ffn_hidden(x.astype(BF16), w1_ref, hid_s, slice(None))
        o_ref[...] = _layer_norm(r_s[...], g_ref[...], b_ref[...])
        r_s[...] = alpha * x + _dot(hid_s[...], w2_ref[...])

    @pl.when(s == last)
    def _():
        o_ref[...] = _layer_norm(r_s[...], g_ref[...], b_ref[...])


def _ffn(x, w1, w2, g, b, alpha):
    n, d = x.shape
    tm = min(TOKEN_TILE, n)
    tile = pl.BlockSpec((tm, d), lambda i: (i, 0))
    return pl.pallas_call(
        functools.partial(_ffn_kernel, alpha=alpha),
        grid=(n // tm,),
        in_specs=[tile, _const_spec(w1.shape), _const_spec(w2.shape), _const_spec(g.shape),
                  _const_spec(b.shape)],
        out_specs=tile,
        out_shape=jax.ShapeDtypeStruct((n, d), F32),
        scratch_shapes=[pltpu.VMEM((tm, w1.shape[1]), BF16)],
        compiler_params=_params(1),
        name="ffn_ln2",
    )(x, w1, w2, g, b)


def _ffn_with_stream(x, w1, w2, g, b, alpha, c0, q, kw, v, dec, n_steps):
    n, d = x.shape
    rows = q.shape[0]
    nb = rows // n_steps
    dh = d // N_HEADS
    bb = SAMPLE_BATCH_BLOCK
    n_grid = (nb // bb) * N_HEADS
    tm = n // n_grid
    assert tm * n_grid == n and tm % 8 == 0 and nb % bb == 0
    cur = lambda s: jnp.minimum(s, n_grid - 1)
    x_tile = pl.BlockSpec((tm, d), lambda s: (cur(s), 0))
    o_tile = pl.BlockSpec((tm, d), lambda s: (jnp.maximum(s - 1, 0), 0))
    cblock = pl.BlockSpec((bb, 1, dh, dh),
                          lambda s: (cur(s) // N_HEADS, cur(s) % N_HEADS, 0, 0))
    tblock = pl.BlockSpec((n_steps, bb, dh), lambda s: (0, cur(s) // N_HEADS, cur(s) % N_HEADS))
    dblock = pl.BlockSpec((bb, dh), lambda s: (cur(s) // N_HEADS, cur(s) % N_HEADS))
    as_tbd = lambda a: a.reshape(n_steps, nb, d)
    x2, c_new, qc = pl.pallas_call(
        functools.partial(_ffn_stream_kernel, alpha=alpha),
        grid=(n_grid + 1,),
        in_specs=[x_tile, _const_spec(w1.shape), _const_spec(w2.shape), _const_spec(g.shape),
                  _const_spec(b.shape), cblock, tblock, tblock, tblock, dblock],
        out_specs=[o_tile, cblock, tblock],
        out_shape=[jax.ShapeDtypeStruct((n, d), F32), jax.ShapeDtypeStruct(c0.shape, F32),
                   jax.ShapeDtypeStruct((n_steps, nb, d), F32)],
        scratch_shapes=[pltpu.VMEM((tm, w1.shape[1]), BF16), pltpu.VMEM((tm, d), F32)],
        compiler_params=_params(1),
        name="ffn_ln2_stream",
    )(x, w1, w2, g, b, c0, as_tbd(q), as_tbd(kw), as_tbd(v), dec)
    return x2, c_new, qc.reshape(rows, d)


def _w_in_prep_kernel(a_ref, g_ref, b_ref, wa_ref, wg_ref, wb_ref, *, n_a):
    j = pl.program_id(0)

    @pl.when(j == 0)
    def _():
        wg_ref[...] = g_ref[...]

    @pl.when(j < n_a)
    def _():
        wa_ref[...] = a_ref[...].T.astype(BF16)

    @pl.when(j >= n_a)
    def _():
        wb_ref[...] = b_ref[...].T.astype(BF16)


def _prepare_w_in(w_in_t):
    d = w_in_t.shape[1]
    n_a, n_b = 7, 2
    h2 = 2 * N_HEADS
    off_b = n_a * d + h2
    return pl.pallas_call(
        functools.partial(_w_in_prep_kernel, n_a=n_a),
        grid=(n_a + n_b,),
        in_specs=[pl.BlockSpec((d, d), lambda j: (jnp.minimum(j, n_a - 1), 0)),
                  pl.BlockSpec((pl.Element(h2), pl.Element(d)), lambda j: (n_a * d, 0)),
                  pl.BlockSpec((pl.Element(d), pl.Element(d)),
                               lambda j: (pl.multiple_of(off_b + jnp.maximum(j - n_a, 0) * d, 8),
                                          0))],
        out_specs=[pl.BlockSpec((d, d), lambda j: (0, jnp.minimum(j, n_a - 1))),
                   pl.BlockSpec((h2, d), lambda j: (0, 0)),
                   pl.BlockSpec((d, d), lambda j: (0, jnp.maximum(j - n_a, 0)))],
        out_shape=[jax.ShapeDtypeStruct((d, n_a * d), BF16),
                   jax.ShapeDtypeStruct((h2, d), F32),
                   jax.ShapeDtypeStruct((d, n_b * d), BF16)],
        compiler_params=_params(1),
        name="w_in_prep",
    )(w_in_t, w_in_t, w_in_t)


def _layer_weights(w_in, b_gate, conv_w, w_conv_out, mh_g, w_m_out, w_o, ln1_g, ln1_b,
                   w_ff1, w_ff2, ln2_g, ln2_b):
    d = w_in.shape[0]
    h2 = 2 * N_HEADS
    w_in_t = jnp.swapaxes(w_in, 0, 1)
    w_a, wg_t, w_b = _prepare_w_in(w_in_t)
    wgt_hi, wgt_lo = _split_bf16(wg_t)
    wgt = jnp.stack([jnp.concatenate([wgt_hi, wgt_lo], axis=0),
                     jnp.concatenate([wgt_hi, jnp.zeros_like(wgt_hi)], axis=0)])
    pad = ((0, 0), (0, GATE_PAD - h2))
    wgc = jnp.stack([jnp.pad(wgt_hi.T, pad), jnp.pad(wgt_lo.T, pad)])
    return dict(
        w_a=w_a, w_b=w_b,
        wgt=wgt, wgc=wgc,
        bgc=b_gate.reshape(h2, 1).astype(F32),
        bgr=jnp.pad(b_gate.reshape(1, h2).astype(F32), pad),
        cw=conv_w.astype(F32),
        mhg=mh_g.reshape(1, d).astype(F32),
        f32_weights=[w_conv_out, w_m_out, w_o, w_ff1, w_ff2],
        ln1_g=ln1_g.reshape(1, d), ln1_b=ln1_b.reshape(1, d),
        ln2_g=ln2_g.reshape(1, d), ln2_b=ln2_b.reshape(1, d))


def _layer(x, xs_tm, conv_tm, c0, n0, m0, p, alpha, n_steps):
    bsz, t, d = x.shape
    hm, c_p, n_p, m_p, (wco, wmo, wo, w1, w2) = _mlstm_prompt(
        x, p["w_a"], p["wgt"], p["bgc"], p["mhg"], p["f32_weights"])
    x1, conv_p = _conv_merge_prompt(x, hm, p["w_a"], p["cw"], wco, p["w_b"], wmo, wo,
                                    p["ln1_g"], p["ln1_b"], alpha)
    m1_s, conv_s = _conv_branch_sample(xs_tm, conv_tm, p["w_a"], p["cw"], wco, p["w_b"], n_steps)
    q, kw, v, og, dec, wa, wb, n_s, m_s = _mlstm_sample_pre(
        xs_tm, p["w_a"], p["wgc"], p["bgr"], p["mhg"], n0.reshape(n0.shape[0], d), m0, n_steps)
    x2, c_s, qc = _ffn_with_stream(x1, w1, w2, p["ln2_g"], p["ln2_b"], alpha,
                                   c0, q, kw, v, dec, n_steps)
    hm_s = _mlstm_sample_post(wa, wb, qc, og)
    x1_s = _merge(xs_tm, m1_s, hm_s, wmo, p["w_b"], wo, p["ln1_g"], p["ln1_b"], alpha)
    x2_s = _ffn(x1_s, w1, w2, p["ln2_g"], p["ln2_b"], alpha)
    return (x2.reshape(bsz, t, d), conv_p, c_p, n_p, m_p,
            x2_s, conv_s, c_s, n_s.reshape(n0.shape), m_s)


def kernel(x_prompt, x_sample, state_conv, state_C, state_n, state_m, w_in, b_gate, conv_w,
           w_conv_out, mh_g, w_m_out, w_o, ln1_g, ln1_b, w_ff1, w_ff2, ln2_g, ln2_b):
    depth = w_in.shape[0]
    alpha = (2.0 * depth) ** 0.25
    bsz, t, d = x_prompt.shape
    sb, st, _ = x_sample.shape
    assert t % TOKEN_TILE == 0 and t % MLSTM_TILE == 0 and MLSTM_TILE % CHUNK == 0
    assert d % N_HEADS == 0
    assert sb % SAMPLE_BATCH_BLOCK == 0 and st >= CONV_W - 1

    xp = x_prompt
    xs = jnp.transpose(x_sample, (1, 0, 2)).reshape(st * sb, d)
    outs = [[] for _ in range(8)]
    for l in range(depth):
        p = _layer_weights(w_in[l], b_gate[l], conv_w[l], w_conv_out[l], mh_g[l], w_m_out[l],
                           w_o[l], ln1_g[l], ln1_b[l], w_ff1[l], w_ff2[l], ln2_g[l], ln2_b[l])
        conv_tm = jnp.transpose(state_conv[l], (1, 0, 2)).reshape((CONV_W - 1) * sb, d)
        xp, cp, c_p, n_p, m_p, xs, cs_tm, c_s, n_s, m_s = _layer(
            xp, xs, conv_tm, state_C[l], state_n[l], state_m[l], p, alpha, st)
        cs = jnp.transpose(cs_tm.reshape(CONV_W - 1, sb, d), (1, 0, 2))
        for acc, val in zip(outs, (cp, cs, c_p, c_s, n_p, n_s, m_p, m_s)):
            acc.append(val)
    ys = jnp.transpose(xs.reshape(st, sb, d), (1, 0, 2))
    return (xp, ys) + tuple(jnp.stack(acc) for acc in outs)
```

```python
import functools

import jax
import jax.numpy as jnp
from jax import lax
from jax.experimental import pallas as pl
from jax.experimental.pallas import tpu as pltpu

F32 = jnp.float32
BF16 = jnp.bfloat16

LN_EPS = 1e-5
N_HEADS = 4
CHUNK = 128
CONV_W = 3
TOKEN_TILE = 1024
SUB_TILE = 512
MLSTM_TILE = 512
SAMPLE_BATCH_BLOCK = 16
LANES = 128
GATE_PAD = LANES
VMEM_LIMIT = 56 * 1024 * 1024


def _dot(a, b):
    return jnp.dot(a, b, preferred_element_type=F32)


def _dot_nt(a, b):
    return lax.dot_general(a, b, (((1,), (1,)), ((), ())), preferred_element_type=F32)


def _dot_tn(a, b):
    return lax.dot_general(a, b, (((0,), (0,)), ((), ())), preferred_element_type=F32)


def _sigmoid(x):
    return 1.0 / (1.0 + jnp.exp(-x))


def _log_sigmoid(x):
    return jnp.minimum(x, 0.0) - jnp.log1p(jnp.exp(-jnp.abs(x)))


def _split_bf16(x):
    hi = x.astype(BF16)
    lo = (x - hi.astype(F32)).astype(BF16)
    return hi, lo


def _layer_norm(r, g, b):
    mu = jnp.mean(r, axis=-1, keepdims=True)
    xc = r - mu
    var = jnp.mean(xc * xc, axis=-1, keepdims=True)
    return xc * lax.rsqrt(var + LN_EPS) * g + b


def _head_norm(h):
    mu = jnp.mean(h, axis=-1, keepdims=True)
    hc = h - mu
    return hc * lax.rsqrt(jnp.mean(hc * hc, axis=-1, keepdims=True) + LN_EPS)


def _const_spec(shape):
    zeros = (0,) * len(shape)
    return pl.BlockSpec(shape, lambda *_: zeros, pipeline_mode=pl.Buffered(1))


def _window_spec(block_shape, index):
    return pl.BlockSpec(block_shape, lambda *_: index, pipeline_mode=pl.Buffered(1))


def _bch_spec(d):
    return _window_spec((d, 3 * d), (0, 0))


def _qkvo_spec(d):
    return _window_spec((pl.Element(d), pl.Element(4 * d)), (0, 3 * d))


def _gate_spec(d, which):
    return _window_spec((d, d), (0, which))


def _params(n_axes):
    return pltpu.CompilerParams(dimension_semantics=("arbitrary",) * n_axes,
                                vmem_limit_bytes=VMEM_LIMIT)


def _conv_stripes(d):
    sw = 512 if d % 512 == 0 else d
    return [slice(s * sw, (s + 1) * sw) for s in range(d // sw)]


def _conv_merge_prompt_kernel(x_ref, hm_ref, wbch_ref, cw_ref, wco_ref, wgc_ref, wmo_ref, wgm_ref,
                              wo_ref, g_ref, b_ref, o_ref, cs_ref, u_s, a_s, mg_s, *, alpha):
    tm, d = a_s.shape
    @pl.when(pl.program_id(1) == 0)
    def _():
        u_s[0:8, :] = jnp.zeros((8, d), F32)

    for r0 in range(0, tm, SUB_TILE):
        sub = min(SUB_TILE, tm - r0)
        x = x_ref[0, r0:r0 + sub, :]
        rows = slice(r0, r0 + sub)
        xb = x.astype(BF16)
        for cs in _conv_stripes(d):
            off = cs.start
            cg = _dot(xb, wbch_ref[:, d + off:d + cs.stop])
            hc = _dot(xb, wbch_ref[:, 2 * d + off:2 * d + cs.stop])
            u = cg * hc
            u_s[8 + r0:8 + r0 + sub, cs] = u
            conv = (u_s[6 + r0:6 + r0 + sub, cs] * cw_ref[0:1, cs]
                    + u_s[7 + r0:7 + r0 + sub, cs] * cw_ref[1:2, cs] + u * cw_ref[2:3, cs])
            bg = _dot(xb, wbch_ref[:, off:cs.stop])
            a_s[rows, cs] = (bg * conv).astype(BF16)
        hm = hm_ref[rows, :]
        for cs in _conv_stripes(d):
            gated_m = _sigmoid(_dot(xb, wgm_ref[:, cs])) * _dot(hm, wmo_ref[:, cs])
            gated_c = _sigmoid(_dot(xb, wgc_ref[:, cs])) * _dot(a_s[rows, :], wco_ref[:, cs])
            mg_s[rows, cs] = (gated_c + gated_m).astype(BF16)
        r = alpha * x + _dot(mg_s[rows, :], wo_ref[...])
        o_ref[rows, :] = _layer_norm(r, g_ref[...], b_ref[...])
    cs_ref[0] = u_s[tm + 6:tm + 8, :]
    u_s[0:8, :] = u_s[tm:tm + 8, :]


def _conv_sample_kernel(x_ref, st_ref, wbch_ref, cw_ref, wco_ref, wgc_ref, m1_ref, cs_ref, a_s,
                        *, n_steps):
    rows, d = a_s.shape
    nb = rows // n_steps
    xb = x_ref[...].astype(BF16)
    for cs in _conv_stripes(d):
        off = cs.start
        bg = _dot(xb, wbch_ref[:, off:cs.stop])
        cg = _dot(xb, wbch_ref[:, d + off:d + cs.stop])
        hc = _dot(xb, wbch_ref[:, 2 * d + off:2 * d + cs.stop])
        u = cg * hc
        up = [st_ref[j * nb:(j + 1) * nb, cs] for j in range(CONV_W - 1)]
        up += [u[t * nb:(t + 1) * nb] for t in range(n_steps)]
        for t in range(n_steps):
            conv = (up[t] * cw_ref[0:1, cs] + up[t + 1] * cw_ref[1:2, cs]
                    + up[t + 2] * cw_ref[2:3, cs])
            a_s[t * nb:(t + 1) * nb, cs] = (bg[t * nb:(t + 1) * nb] * conv).astype(BF16)
        for j in range(CONV_W - 1):
            cs_ref[j * nb:(j + 1) * nb, cs] = up[n_steps + j]
    yc = _dot(a_s[...], wco_ref[...])
    m1_ref[...] = _sigmoid(_dot(xb, wgc_ref[...])) * yc


def _conv_merge_prompt(x, hm, w_a, cw, wco, w_b, wmo, wo, g, b, alpha):
    bsz, t, d = x.shape
    tm = TOKEN_TILE
    flat = pl.BlockSpec((tm, d), lambda bi, j: (bi * (t // tm) + j, 0))
    return pl.pallas_call(
        functools.partial(_conv_merge_prompt_kernel, alpha=alpha),
        grid=(bsz, t // tm),
        in_specs=[pl.BlockSpec((1, tm, d), lambda bi, j: (bi, j, 0)), flat,
                  _bch_spec(d), _const_spec(cw.shape), _const_spec(wco.shape),
                  _gate_spec(d, 0), _const_spec(wmo.shape), _gate_spec(d, 1),
                  _const_spec(wo.shape), _const_spec(g.shape), _const_spec(b.shape)],
        out_specs=[flat, pl.BlockSpec((1, CONV_W - 1, d), lambda bi, j: (bi, 0, 0))],
        out_shape=[jax.ShapeDtypeStruct((bsz * t, d), F32),
                   jax.ShapeDtypeStruct((bsz, CONV_W - 1, d), F32)],
        scratch_shapes=[pltpu.VMEM((tm + 8, d), F32), pltpu.VMEM((tm, d), BF16),
                        pltpu.VMEM((tm, d), BF16)],
        compiler_params=_params(2),
        name="conv_merge_prompt",
    )(x, hm, w_a, cw, wco, w_b, wmo, w_b, wo, g, b)


def _conv_branch_sample(x_tm, st_tm, wbch, cw, wco, wgc, n_steps):
    rows, d = x_tm.shape
    return pl.pallas_call(
        functools.partial(_conv_sample_kernel, n_steps=n_steps),
        grid=(1,),
        in_specs=[_const_spec(x_tm.shape), _const_spec(st_tm.shape), _bch_spec(d),
                  _const_spec(cw.shape), _const_spec(wco.shape), _gate_spec(d, 0)],
        out_specs=[pl.BlockSpec((rows, d), lambda i: (0, 0)),
                   pl.BlockSpec(st_tm.shape, lambda i: (0, 0))],
        out_shape=[jax.ShapeDtypeStruct((rows, d), F32),
                   jax.ShapeDtypeStruct(st_tm.shape, F32)],
        scratch_shapes=[pltpu.VMEM((rows, d), BF16)],
        compiler_params=_params(1),
        name="conv_branch_sample",
    )(x_tm, st_tm, wbch, cw, wco, wgc)


def _project_qkvo(xh, wqkvo_ref, mhg_ref, q_s, k_s, v_s, og_s, d, dh):
    q_s[...] = _dot(xh, wqkvo_ref[:, 0:d]).astype(BF16).astype(q_s.dtype)
    k_s[...] = (_dot(xh, wqkvo_ref[:, d:2 * d]) * (dh ** -0.5)).astype(BF16).astype(k_s.dtype)
    v_s[...] = _dot(xh, wqkvo_ref[:, 2 * d:3 * d]).astype(BF16).astype(v_s.dtype)
    og_s[...] = mhg_ref[...] * _sigmoid(_dot(xh, wqkvo_ref[:, 3 * d:4 * d]))


def _rep(col, times):
    return col if times == 1 else jnp.concatenate([col] * times, axis=1)


def _mlstm_prompt_step(x_ref, wqkvo_ref, wgt_ref, bgc_ref, mhg_ref, hm_ref,
                       q_s, k_s, vx_s, og_s, e_s, colb_s, colm_s, cole_s, cst_s, mst_s,
                       *, cur, prev):
    _, tm, d = q_s.shape
    dh = d // N_HEADS
    L = CHUNK
    LANES = colb_s.shape[-1]
    n_chunks = tm // L
    wide = (dh + LANES) // LANES

    xh = x_ref[...].astype(BF16)

    h2 = 2 * N_HEADS
    ga = _dot_nt(wgt_ref[...], xh)
    gt = ga[0:h2] + ga[h2:2 * h2] + bgc_ref[...]
    is_input_gate = lax.broadcasted_iota(jnp.int32, gt.shape, 0) < N_HEADS
    g = jnp.where(is_input_gate, gt, _log_sigmoid(gt))

    pos = lax.broadcasted_iota(jnp.int32, g.shape, 1) % L
    csum = g
    shift = 1
    while shift < L:
        csum = csum + jnp.where(pos >= shift, pltpu.roll(csum, shift, 1), 0.0)
        shift *= 2
    ba = jnp.concatenate([csum[N_HEADS:], g[:N_HEADS] - csum[N_HEADS:]], axis=0)

    ri = lax.broadcasted_iota(jnp.int32, (L, L), 0)
    ci = lax.broadcasted_iota(jnp.int32, (L, L), 1)
    causal = ri >= ci

    for c in range(n_chunks):
        slab = ba[:, c * L:(c + 1) * L]
        cols = slab.T
        for h in range(N_HEADS):
            b_rep = jnp.broadcast_to(cols[:, h:h + 1], (L, LANES))
            a_rep = jnp.broadcast_to(cols[:, N_HEADS + h:N_HEADS + h + 1], (L, LANES))
            a_row = slab[N_HEADS + h:N_HEADS + h + 1, :]
            dm = jnp.where(causal, _rep(b_rep, L // LANES) + a_row, -jnp.inf)
            m_loc = jnp.broadcast_to(jnp.max(dm, axis=1, keepdims=True), (L, LANES))
            e_s[cur, c, h] = jnp.exp(dm - _rep(m_loc, L // LANES))
            colb_s[cur, c, h] = b_rep
            colm_s[cur, c, h] = m_loc
            cole_s[cur, c, h] = jnp.exp(a_rep + b_rep[L - 1:L, :] - m_loc[L - 1:L, :])

    n_split = 2 if N_HEADS % 2 == 0 else 1
    piece = d // n_split

    def project(part, split):
        lo = split * piece
        cols = slice(lo, lo + piece)
        y = _dot(xh, wqkvo_ref[:, part * d + lo:part * d + lo + piece])
        if part == 0:
            q_s[cur, :, cols] = y.astype(BF16)
        elif part == 1:
            k_s[cur, :, cols] = (y * (dh ** -0.5)).astype(BF16)
        elif part == 2:
            v = y.astype(BF16)
            for h in range(lo // dh, (lo + piece) // dh):
                vx_s[cur, h, :, 0:dh] = v[:, h * dh - lo:(h + 1) * dh - lo]
                vx_s[cur, h, :, dh:] = jnp.ones((tm, LANES), BF16)
        else:
            og_s[cur, :, cols] = mhg_ref[:, cols] * _sigmoid(y)

    heads = [slice(h * dh, (h + 1) * dh) for h in range(N_HEADS)]

    def recur_ready(c):
        rows = slice(c * L, (c + 1) * L)
        qk = [_dot_nt(q_s[prev, rows, hc], k_s[prev, rows, hc]) for hc in heads]
        states = [cst_s[h] for h in range(N_HEADS)]
        qc = [_dot(q_s[prev, rows, hc], states[h].astype(BF16)) for h, hc in enumerate(heads)]
        ux = []
        for h, hc in enumerate(heads):
            kw = k_s[prev, rows, hc].astype(F32) * _rep(cole_s[prev, c, h], dh // LANES)
            ux.append(_dot_tn(kw.astype(BF16), vx_s[prev, h, rows, :]))
        return qk, states, qc, ux

    def recur_finish(c, qk, states, qc, ux):
        rows = slice(c * L, (c + 1) * L)
        svx = []
        for h in range(N_HEADS):
            s_loc = qk[h] * e_s[prev, c, h]
            svx.append(_dot(s_loc.astype(BF16), vx_s[prev, h, rows, :]))
        for h, hc in enumerate(heads):
            m_prev = mst_s[h]
            m_loc = colm_s[prev, c, h]
            inter = colb_s[prev, c, h] + m_prev
            m_t = jnp.maximum(inter, m_loc)
            w_inter = _rep(jnp.exp(inter - m_t), wide)
            w_loc = _rep(jnp.exp(m_loc - m_t), wide)
            nd = w_inter * qc[h] + w_loc * svx[h]
            inv = 1.0 / jnp.maximum(jnp.abs(nd[:, dh:]), jnp.exp(-m_t))
            hh = nd[:, 0:dh] * _rep(inv, dh // LANES)
            hm_ref[rows, hc] = (_head_norm(hh) * og_s[prev, rows, hc]).astype(BF16)
            m_new = m_t[L - 1:L, :]
            decay = _rep(jnp.exp(inter[L - 1:L, :] - m_new), wide)
            grow = _rep(jnp.exp(m_loc[L - 1:L, :] - m_new), wide)
            cst_s[h] = decay * states[h] + grow * ux[h]
            mst_s[h] = m_new

    pieces = [(part, split) for part in range(4) for split in range(n_split)]
    n_phases = 2 * n_chunks
    done = 0
    for c in range(n_chunks):
        for phase in (2 * c, 2 * c + 1):
            if phase % 2 == 0:
                partial = recur_ready(c)
            else:
                recur_finish(c, *partial)
            upto = (phase + 1) * len(pieces) // n_phases
            for part, split in pieces[done:upto]:
                project(part, split)
            done = upto


def _mlstm_prompt_kernel(*refs, tiles_per_seq, n_cast):
    x_ref, wqkvo_ref, wgt_ref, bgc_ref, mhg_ref = refs[:5]
    cast_src = refs[5:5 + n_cast]
    hm_ref, c_ref, n_ref, m_ref = refs[5 + n_cast:9 + n_cast]
    cast_dst = refs[9 + n_cast:9 + 2 * n_cast]
    q_s, k_s, vx_s, og_s, e_s, colb_s, colm_s, cole_s, cst_s, mst_s = refs[9 + 2 * n_cast:]
    dh = cst_s.shape[1]
    g_step = pl.program_id(0)
    for src, dst in zip(cast_src, cast_dst):
        dst[...] = src[...].astype(BF16)

    @pl.when(g_step == 0)
    def _():
        for ref in (q_s, k_s, vx_s, og_s, e_s, colb_s, colm_s, cole_s):
            ref[1] = jnp.zeros(ref.shape[1:], ref.dtype)

    @pl.when(jnp.logical_or(g_step == 0, (g_step - 1) % tiles_per_seq == 0))
    def _():
        cst_s[...] = jnp.zeros(cst_s.shape, F32)
        mst_s[...] = jnp.zeros(mst_s.shape, F32)

    for parity in (0, 1):
        @pl.when(g_step % 2 == parity)
        def _():
            _mlstm_prompt_step(x_ref, wqkvo_ref, wgt_ref, bgc_ref, mhg_ref, hm_ref,
                               q_s, k_s, vx_s, og_s, e_s, colb_s, colm_s, cole_s, cst_s, mst_s,
                               cur=parity, prev=1 - parity)

    @pl.when(jnp.logical_and(g_step >= 1, (g_step - 1) % tiles_per_seq == tiles_per_seq - 1))
    def _():
        for h in range(N_HEADS):
            state = cst_s[h]
            c_ref[0, h] = state[:, 0:dh]
            n_ref[0, h] = state[:, dh:].T[0:1, :]
            m_ref[0, h] = mst_s[h][:, 0:1]


def _mlstm_prompt(x, wqkvo, wgt, bgc, mhg, f32_weights):
    bsz, t, d = x.shape
    dh = d // N_HEADS
    tm = MLSTM_TILE
    nc = tm // CHUNK
    dx = dh + LANES
    tps = t // tm
    n_tiles = bsz * tps
    seq_block = lambda g: (jnp.maximum(g - 1, 0) // tps, 0, 0, 0)
    cast_specs = []
    for w in f32_weights:
        rows_per_step = w.shape[0] // n_tiles
        assert rows_per_step * n_tiles == w.shape[0] and rows_per_step % 16 == 0
        cast_specs.append(pl.BlockSpec((rows_per_step, w.shape[1]),
                                       lambda g: (jnp.minimum(g, n_tiles - 1), 0)))
    n_cast = len(f32_weights)
    hm, c, n, m, *casted = pl.pallas_call(
        functools.partial(_mlstm_prompt_kernel, tiles_per_seq=tps, n_cast=n_cast),
        grid=(n_tiles + 1,),
        in_specs=[pl.BlockSpec((tm, d), lambda g: (jnp.minimum(g, n_tiles - 1), 0)),
                  _qkvo_spec(d), _const_spec(wgt.shape), _const_spec(bgc.shape),
                  _const_spec(mhg.shape)] + cast_specs,
        out_specs=[pl.BlockSpec((tm, d), lambda g: (jnp.maximum(g - 1, 0), 0)),
                   pl.BlockSpec((1, N_HEADS, dh, dh), seq_block),
                   pl.BlockSpec((1, N_HEADS, 1, dh), seq_block),
                   pl.BlockSpec((1, N_HEADS, 1, 1), seq_block)] + cast_specs,
        out_shape=[jax.ShapeDtypeStruct((bsz * t, d), BF16),
                   jax.ShapeDtypeStruct((bsz, N_HEADS, dh, dh), F32),
                   jax.ShapeDtypeStruct((bsz, N_HEADS, 1, dh), F32),
                   jax.ShapeDtypeStruct((bsz, N_HEADS, 1, 1), F32)]
                  + [jax.ShapeDtypeStruct(w.shape, BF16) for w in f32_weights],
        scratch_shapes=[
            pltpu.VMEM((2, tm, d), BF16),
            pltpu.VMEM((2, tm, d), BF16),
            pltpu.VMEM((2, N_HEADS, tm, dx), BF16),
            pltpu.VMEM((2, tm, d), F32),
            pltpu.VMEM((2, nc, N_HEADS, CHUNK, CHUNK), F32),
            pltpu.VMEM((2, nc, N_HEADS, CHUNK, LANES), F32),
            pltpu.VMEM((2, nc, N_HEADS, CHUNK, LANES), F32),
            pltpu.VMEM((2, nc, N_HEADS, CHUNK, LANES), F32),
            pltpu.VMEM((N_HEADS, dh, dx), F32),
            pltpu.VMEM((N_HEADS, 1, LANES), F32)],
        compiler_params=_params(1),
        name="mlstm_prompt",
    )(x.reshape(bsz * t, d), wqkvo, wgt, bgc, mhg, *f32_weights)
    return hm, c, n.reshape(bsz, N_HEADS, dh), m.reshape(bsz, N_HEADS), casted


def _mlstm_sample_pre_kernel(x_ref, wqkvo_ref, wgc_ref, bgr_ref, mhg_ref, n0_ref, m0_ref,
                             q_ref, kw_ref, v_ref, og_ref, dec_ref, wa_ref, wb_ref, n_ref, m_ref,
                             *, n_steps):
    rows, d = q_ref.shape
    dh = d // N_HEADS
    nb = rows // n_steps
    T = n_steps
    slab = [slice(t * nb, (t + 1) * nb) for t in range(T)]

    x = x_ref[...]
    xh, xl = _split_bf16(x)
    _project_qkvo(xh, wqkvo_ref, mhg_ref, q_ref, kw_ref, v_ref, og_ref, d, dh)
    g = _dot(xh, wgc_ref[0]) + _dot(xl, wgc_ref[0]) + _dot(xh, wgc_ref[1]) + bgr_ref[...]
    is_input_gate = lax.broadcasted_iota(jnp.int32, g.shape, 1) < N_HEADS
    g = jnp.where(is_input_gate, g, _log_sigmoid(g))
    for hd in range(N_HEADS):
        hc = slice(hd * dh, (hd + 1) * dh)
        qf = [q_ref[slab[t], hc] for t in range(T)]
        kf = [kw_ref[slab[t], hc] for t in range(T)]
        vf = [v_ref[slab[t], hc] for t in range(T)]
        li = [g[slab[t], hd:hd + 1] for t in range(T)]
        lf = [g[slab[t], N_HEADS + hd:N_HEADS + hd + 1] for t in range(T)]
        m0 = m0_ref[:, hd:hd + 1]
        n0 = n0_ref[:, hc]
        b = [lf[0]]
        for t in range(1, T):
            b.append(b[t - 1] + lf[t])
        a = [li[t] - b[t] for t in range(T)]
        m_new = None
        for t in range(T):
            inter = b[t] + m0
            m_t = inter
            for s in range(t + 1):
                m_t = jnp.maximum(m_t, b[t] + a[s])
            w_inter = jnp.exp(inter - m_t)
            num = jnp.zeros((nb, dh), F32)
            den = w_inter * jnp.sum(qf[t] * n0, axis=1, keepdims=True)
            for s in range(t + 1):
                s_w = (jnp.sum(qf[t] * kf[s], axis=1, keepdims=True)
                       * jnp.exp(b[t] + a[s] - m_t))
                num = num + s_w * vf[s]
                den = den + s_w
            inv = 1.0 / jnp.maximum(jnp.abs(den), jnp.exp(-m_t))
            wa_ref[slab[t], hc] = jnp.broadcast_to(w_inter * inv, (nb, dh))
            wb_ref[slab[t], hc] = num * inv
            m_new = m_t
        b_last = b[T - 1]
        decay = jnp.exp(b_last + m0 - m_new)
        n_new = decay * n0
        for s in range(T):
            kw = kf[s] * jnp.exp(a[s] + b_last - m_new)
            kw_ref[slab[s], hc] = kw
            n_new = n_new + kw
        n_ref[:, hc] = n_new
        m_ref[:, hd:hd + 1] = m_new
        dec_ref[:, hc] = jnp.broadcast_to(decay, (nb, dh))


def _mlstm_sample_pre(x_tm, wqkvo, wgc, bgr, mhg, n0, m0, n_steps):
    rows, d = x_tm.shape
    nb = rows // n_steps
    whole = lambda shape: pl.BlockSpec(shape, lambda i: (0,) * len(shape))
    big = jax.ShapeDtypeStruct((rows, d), F32)
    small = jax.ShapeDtypeStruct((nb, d), F32)
    return pl.pallas_call(
        functools.partial(_mlstm_sample_pre_kernel, n_steps=n_steps),
        grid=(1,),
        in_specs=[_const_spec(x_tm.shape), _qkvo_spec(d), _const_spec(wgc.shape),
                  _const_spec(bgr.shape), _const_spec(mhg.shape), _const_spec(n0.shape),
                  _const_spec(m0.shape)],
        out_specs=[whole((rows, d))] * 4 + [whole((nb, d))] + [whole((rows, d))] * 2
                  + [whole((nb, d)), whole((nb, N_HEADS))],
        out_shape=[big, big, big, big, small, big, big, small,
                   jax.ShapeDtypeStruct((nb, N_HEADS), F32)],
        compiler_params=_params(1),
        name="mlstm_sample_pre",
    )(x_tm, wqkvo, wgc, bgr, mhg, n0, m0)


def _stream_matrix_memory(c0_ref, q_ref, kw_ref, v_ref, dec_ref, c_ref, qc_ref):
    T, bb, dh = q_ref.shape
    owner = lax.broadcasted_iota(jnp.int32, (T * bb, dh), 0) % bb
    q_blk = jnp.concatenate([q_ref[t] for t in range(T)], axis=0).astype(BF16)
    kw_blk = jnp.concatenate([kw_ref[t] for t in range(T)], axis=0).astype(BF16)
    v_blk = jnp.concatenate([v_ref[t] for t in range(T)], axis=0)
    c_olds = [c0_ref[bi, 0] for bi in range(bb)]
    reads = [_dot(q_blk, c_olds[bi].astype(BF16)) for bi in range(bb)]
    qc = jnp.zeros((T * bb, dh), F32)
    for bi in range(bb):
        qc = jnp.where(owner == bi, reads[bi], qc)
    for t in range(T):
        qc_ref[t] = qc[t * bb:(t + 1) * bb]
    for bi in range(bb):
        upd = _dot_tn(kw_blk, jnp.where(owner == bi, v_blk, 0.0).astype(BF16))
        c_ref[bi, 0] = dec_ref[bi:bi + 1, :] * c_olds[bi] + upd


def _mlstm_sample_post_kernel(wa_ref, wb_ref, qc_ref, og_ref, hm_ref):
    d = hm_ref.shape[1]
    dh = d // N_HEADS
    for hd in range(N_HEADS):
        hc = slice(hd * dh, (hd + 1) * dh)
        hh = wa_ref[:, hc] * qc_ref[:, hc] + wb_ref[:, hc]
        hm_ref[:, hc] = _head_norm(hh) * og_ref[:, hc]


def _mlstm_sample_post(wa, wb, qc, og):
    spec = _const_spec(wa.shape)
    return pl.pallas_call(
        _mlstm_sample_post_kernel,
        grid=(1,),
        in_specs=[spec, spec, spec, spec],
        out_specs=pl.BlockSpec(wa.shape, lambda i: (0, 0)),
        out_shape=jax.ShapeDtypeStruct(wa.shape, F32),
        compiler_params=_params(1),
        name="mlstm_sample_post",
    )(wa, wb, qc, og)


def _merge_kernel(x_ref, m1_ref, hm_ref, wmo_ref, wgm_ref, wo_ref, g_ref, b_ref, o_ref, *, alpha):
    tm = x_ref.shape[0]
    for r0 in range(0, tm, SUB_TILE):
        rows = slice(r0, min(r0 + SUB_TILE, tm))
        x = x_ref[rows, :]
        ym = _dot(hm_ref[rows, :].astype(BF16), wmo_ref[...])
        gate = _dot(x.astype(BF16), wgm_ref[...])
        merged = m1_ref[rows, :] + _sigmoid(gate) * ym
        r = alpha * x + _dot(merged.astype(BF16), wo_ref[...])
        o_ref[rows, :] = _layer_norm(r, g_ref[...], b_ref[...])


def _merge(x, m1, hm, wmo, wgm, wo, g, b, alpha):
    n, d = x.shape
    tm = min(TOKEN_TILE, n)
    tile = pl.BlockSpec((tm, d), lambda i: (i, 0))
    return pl.pallas_call(
        functools.partial(_merge_kernel, alpha=alpha),
        grid=(n // tm,),
        in_specs=[tile, tile, tile, _const_spec(wmo.shape), _gate_spec(d, 1),
                  _const_spec(wo.shape), _const_spec(g.shape), _const_spec(b.shape)],
        out_specs=tile,
        out_shape=jax.ShapeDtypeStruct((n, d), F32),
        compiler_params=_params(1),
        name="merge_ln1",
    )(x, m1, hm, wmo, wgm, wo, g, b)


def _ffn_hidden(xb, w1_ref, hid_s, rows):
    dff = hid_s.shape[1]
    sw = 1024 if dff % 1024 == 0 else dff
    for s in range(dff // sw):
        cs = slice(s * sw, (s + 1) * sw)
        hid = jnp.maximum(_dot(xb, w1_ref[:, cs]), 0.0)
        hid_s[rows, cs] = (hid * hid).astype(BF16)


def _ffn_kernel(x_ref, w1_ref, w2_ref, g_ref, b_ref, o_ref, hid_s, *, alpha):
    tm = hid_s.shape[0]
    for r0 in range(0, tm, SUB_TILE):
        rows = slice(r0, min(r0 + SUB_TILE, tm))
        x = x_ref[rows, :]
        _ffn_hidden(x.astype(BF16), w1_ref, hid_s, rows)
        r = alpha * x + _dot(hid_s[rows, :], w2_ref[...])
        o_ref[rows, :] = _layer_norm(r, g_ref[...], b_ref[...])


def _ffn_stream_kernel(x_ref, w1_ref, w2_ref, g_ref, b_ref, c0_ref, q_ref, kw_ref, v_ref,
                       dec_ref, o_ref, c_ref, qc_ref, hid_s, r_s, *, alpha):
    s = pl.program_id(0)
    last = pl.num_programs(0) - 1

    @pl.when(s == 0)
    def _():
        r_s[...] = jnp.zeros(r_s.shape, F32)

    @pl.when(s < last)
    def _():
        _stream_matrix_memory(c0_ref, q_ref, kw_ref, v_ref, dec_ref, c_ref, qc_ref)
        x = x_ref[...]
        _ffn_hidden(x.astype(BF16), w1_ref, hid_s, slice(None))
        o_ref[...] = _layer_norm(r_s[...], g_ref[...], b_ref[...])
        r_s[...] = alpha * x + _dot(hid_s[...], w2_ref[...])

    @pl.when(s == last)
    def _():
        o_ref[...] = _layer_norm(r_s[...], g_ref[...], b_ref[...])


def _ffn(x, w1, w2, g, b, alpha):
    n, d = x.shape
    tm = min(TOKEN_TILE, n)
    tile = pl.BlockSpec((tm, d), lambda i: (i, 0))
    return pl.pallas_call(
        functools.partial(_ffn_kernel, alpha=alpha),
        grid=(n // tm,),
        in_specs=[tile, _const_spec(w1.shape), _const_spec(w2.shape), _const_spec(g.shape),
                  _const_spec(b.shape)],
        out_specs=tile,
        out_shape=jax.ShapeDtypeStruct((n, d), F32),
        scratch_shapes=[pltpu.VMEM((tm, w1.shape[1]), BF16)],
        compiler_params=_params(1),
        name="ffn_ln2",
    )(x, w1, w2, g, b)


def _ffn_with_stream(x, w1, w2, g, b, alpha, c0, q, kw, v, dec, n_steps):
    n, d = x.shape
    rows = q.shape[0]
    nb = rows // n_steps
    dh = d // N_HEADS
    bb = SAMPLE_BATCH_BLOCK
    n_grid = (nb // bb) * N_HEADS
    tm = n // n_grid
    assert tm * n_grid == n and tm % 8 == 0 and nb % bb == 0
    cur = lambda s: jnp.minimum(s, n_grid - 1)
    x_tile = pl.BlockSpec((tm, d), lambda s: (cur(s), 0))
    o_tile = pl.BlockSpec((tm, d), lambda s: (jnp.maximum(s - 1, 0), 0))
    cblock = pl.BlockSpec((bb, 1, dh, dh),
                          lambda s: (cur(s) // N_HEADS, cur(s) % N_HEADS, 0, 0))
    tblock = pl.BlockSpec((n_steps, bb, dh), lambda s: (0, cur(s) // N_HEADS, cur(s) % N_HEADS))
    dblock = pl.BlockSpec((bb, dh), lambda s: (cur(s) // N_HEADS, cur(s) % N_HEADS))
    as_tbd = lambda a: a.reshape(n_steps, nb, d)
    x2, c_new, qc = pl.pallas_call(
        functools.partial(_ffn_stream_kernel, alpha=alpha),
        grid=(n_grid + 1,),
        in_specs=[x_tile, _const_spec(w1.shape), _const_spec(w2.shape), _const_spec(g.shape),
                  _const_spec(b.shape), cblock, tblock, tblock, tblock, dblock],
        out_specs=[o_tile, cblock, tblock],
        out_shape=[jax.ShapeDtypeStruct((n, d), F32), jax.ShapeDtypeStruct(c0.shape, F32),
                   jax.ShapeDtypeStruct((n_steps, nb, d), F32)],
        scratch_shapes=[pltpu.VMEM((tm, w1.shape[1]), BF16), pltpu.VMEM((tm, d), F32)],
        compiler_params=_params(1),
        name="ffn_ln2_stream",
    )(x, w1, w2, g, b, c0, as_tbd(q), as_tbd(kw), as_tbd(v), dec)
    return x2, c_new, qc.reshape(rows, d)


def _w_in_prep_kernel(a_ref, g_ref, b_ref, wa_ref, wg_ref, wb_ref, *, n_a):
    j = pl.program_id(0)

    @pl.when(j == 0)
    def _():
        wg_ref[...] = g_ref[...]

    @pl.when(j < n_a)
    def _():
        wa_ref[...] = a_ref[...].T.astype(BF16)

    @pl.when(j >= n_a)
    def _():
        wb_ref[...] = b_ref[...].T.astype(BF16)


def _prepare_w_in(w_in_t):
    d = w_in_t.shape[1]
    n_a, n_b = 7, 2
    h2 = 2 * N_HEADS
    off_b = n_a * d + h2
    return pl.pallas_call(
        functools.partial(_w_in_prep_kernel, n_a=n_a),
        grid=(n_a + n_b,),
        in_specs=[pl.BlockSpec((d, d), lambda j: (jnp.minimum(j, n_a - 1), 0)),
                  pl.BlockSpec((pl.Element(h2), pl.Element(d)), lambda j: (n_a * d, 0)),
                  pl.BlockSpec((pl.Element(d), pl.Element(d)),
                               lambda j: (pl.multiple_of(off_b + jnp.maximum(j - n_a, 0) * d, 8),
                                          0))],
        out_specs=[pl.BlockSpec((d, d), lambda j: (0, jnp.minimum(j, n_a - 1))),
                   pl.BlockSpec((h2, d), lambda j: (0, 0)),
                   pl.BlockSpec((d, d), lambda j: (0, jnp.maximum(j - n_a, 0)))],
        out_shape=[jax.ShapeDtypeStruct((d, n_a * d), BF16),
                   jax.ShapeDtypeStruct((h2, d), F32),
                   jax.ShapeDtypeStruct((d, n_b * d), BF16)],
        compiler_params=_params(1),
        name="w_in_prep",
    )(w_in_t, w_in_t, w_in_t)


def _layer_weights(w_in, b_gate, conv_w, w_conv_out, mh_g, w_m_out, w_o, ln1_g, ln1_b,
                   w_ff1, w_ff2, ln2_g, ln2_b):
    d = w_in.shape[0]
    h2 = 2 * N_HEADS
    w_in_t = jnp.swapaxes(w_in, 0, 1)
    w_a, wg_t, w_b = _prepare_w_in(w_in_t)
    wgt_hi, wgt_lo = _split_bf16(wg_t)
    wgt = jnp.concatenate([wgt_hi, wgt_lo], axis=0)
    pad = ((0, 0), (0, GATE_PAD - h2))
    wgc = jnp.stack([jnp.pad(wgt_hi.T, pad), jnp.pad(wgt_lo.T, pad)])
    return dict(
        w_a=w_a, w_b=w_b,
        wgt=wgt, wgc=wgc,
        bgc=b_gate.reshape(h2, 1).astype(F32),
        bgr=jnp.pad(b_gate.reshape(1, h2).astype(F32), pad),
        cw=conv_w.astype(F32),
        mhg=mh_g.reshape(1, d).astype(F32),
        f32_weights=[w_conv_out, w_m_out, w_o, w_ff1, w_ff2],
        ln1_g=ln1_g.reshape(1, d), ln1_b=ln1_b.reshape(1, d),
        ln2_g=ln2_g.reshape(1, d), ln2_b=ln2_b.reshape(1, d))


def _layer(x, xs_tm, conv_tm, c0, n0, m0, p, alpha, n_steps):
    bsz, t, d = x.shape
    hm, c_p, n_p, m_p, (wco, wmo, wo, w1, w2) = _mlstm_prompt(
        x, p["w_a"], p["wgt"], p["bgc"], p["mhg"], p["f32_weights"])
    x1, conv_p = _conv_merge_prompt(x, hm, p["w_a"], p["cw"], wco, p["w_b"], wmo, wo,
                                    p["ln1_g"], p["ln1_b"], alpha)
    m1_s, conv_s = _conv_branch_sample(xs_tm, conv_tm, p["w_a"], p["cw"], wco, p["w_b"], n_steps)
    q, kw, v, og, dec, wa, wb, n_s, m_s = _mlstm_sample_pre(
        xs_tm, p["w_a"], p["wgc"], p["bgr"], p["mhg"], n0.reshape(n0.shape[0], d), m0, n_steps)
    x2, c_s, qc = _ffn_with_stream(x1, w1, w2, p["ln2_g"], p["ln2_b"], alpha,
                                   c0, q, kw, v, dec, n_steps)
    hm_s = _mlstm_sample_post(wa, wb, qc, og)
    x1_s = _merge(xs_tm, m1_s, hm_s, wmo, p["w_b"], wo, p["ln1_g"], p["ln1_b"], alpha)
    x2_s = _ffn(x1_s, w1, w2, p["ln2_g"], p["ln2_b"], alpha)
    return (x2.reshape(bsz, t, d), conv_p, c_p, n_p, m_p,
            x2_s, conv_s, c_s, n_s.reshape(n0.shape), m_s)


def kernel(x_prompt, x_sample, state_conv, state_C, state_n, state_m, w_in, b_gate, conv_w,
           w_conv_out, mh_g, w_m_out, w_o, ln1_g, ln1_b, w_ff1, w_ff2, ln2_g, ln2_b):
    depth = w_in.shape[0]
    alpha = (2.0 * depth) ** 0.25
    bsz, t, d = x_prompt.shape
    sb, st, _ = x_sample.shape
    assert t % TOKEN_TILE == 0 and t % MLSTM_TILE == 0 and MLSTM_TILE % CHUNK == 0
    assert d % N_HEADS == 0
    assert sb % SAMPLE_BATCH_BLOCK == 0 and st >= CONV_W - 1

    xp = x_prompt
    xs = jnp.transpose(x_sample, (1, 0, 2)).reshape(st * sb, d)
    outs = [[] for _ in range(8)]
    for l in range(depth):
        p = _layer_weights(w_in[l], b_gate[l], conv_w[l], w_conv_out[l], mh_g[l], w_m_out[l],
                           w_o[l], ln1_g[l], ln1_b[l], w_ff1[l], w_ff2[l], ln2_g[l], ln2_b[l])
        conv_tm = jnp.transpose(state_conv[l], (1, 0, 2)).reshape((CONV_W - 1) * sb, d)
        xp, cp, c_p, n_p, m_p, xs, cs_tm, c_s, n_s, m_s = _layer(
            xp, xs, conv_tm, state_C[l], state_n[l], state_m[l], p, alpha, st)
        cs = jnp.transpose(cs_tm.reshape(CONV_W - 1, sb, d), (1, 0, 2))
        for acc, val in zip(outs, (cp, cs, c_p, c_s, n_p, n_s, m_p, m_s)):
            acc.append(val)
    ys = jnp.transpose(xs.reshape(st, sb, d), (1, 0, 2))
    return (xp, ys) + tuple(jnp.stack(acc) for acc in outs)
```

```python
import functools

import jax
import jax.numpy as jnp
from jax import lax
from jax.experimental import pallas as pl
from jax.experimental.pallas import tpu as pltpu

F32 = jnp.float32
BF16 = jnp.bfloat16

LN_EPS = 1e-5
N_HEADS = 4
CHUNK = 128
CONV_W = 3
TOKEN_TILE = 1024
SUB_TILE = 512
MLSTM_TILE = 512
SAMPLE_BATCH_BLOCK = 16
LANES = 128
GATE_PAD = LANES
VMEM_LIMIT = 56 * 1024 * 1024


def _dot(a, b):
    return jnp.dot(a, b, preferred_element_type=F32)


def _dot_nt(a, b):
    return lax.dot_general(a, b, (((1,), (1,)), ((), ())), preferred_element_type=F32)


def _dot_tn(a, b):
    return lax.dot_general(a, b, (((0,), (0,)), ((), ())), preferred_element_type=F32)


def _sigmoid(x):
    return 1.0 / (1.0 + jnp.exp(-x))


def _log_sigmoid(x):
    return jnp.minimum(x, 0.0) - jnp.log1p(jnp.exp(-jnp.abs(x)))


def _split_bf16(x):
    hi = x.astype(BF16)
    lo = (x - hi.astype(F32)).astype(BF16)
    return hi, lo


def _layer_norm(r, g, b):
    mu = jnp.mean(r, axis=-1, keepdims=True)
    xc = r - mu
    var = jnp.mean(xc * xc, axis=-1, keepdims=True)
    return xc * lax.rsqrt(var + LN_EPS) * g + b


def _head_norm(h):
    mu = jnp.mean(h, axis=-1, keepdims=True)
    hc = h - mu
    return hc * lax.rsqrt(jnp.mean(hc * hc, axis=-1, keepdims=True) + LN_EPS)


def _const_spec(shape):
    zeros = (0,) * len(shape)
    return pl.BlockSpec(shape, lambda *_: zeros, pipeline_mode=pl.Buffered(1))


def _window_spec(block_shape, index):
    return pl.BlockSpec(block_shape, lambda *_: index, pipeline_mode=pl.Buffered(1))


def _bch_spec(d):
    return _window_spec((d, 3 * d), (0, 0))


def _qkvo_spec(d):
    return _window_spec((pl.Element(d), pl.Element(4 * d)), (0, 3 * d))


def _gate_spec(d, which):
    return _window_spec((d, d), (0, which))


def _params(n_axes):
    return pltpu.CompilerParams(dimension_semantics=("arbitrary",) * n_axes,
                                vmem_limit_bytes=VMEM_LIMIT)


def _conv_stripes(d):
    sw = 512 if d % 512 == 0 else d
    return [slice(s * sw, (s + 1) * sw) for s in range(d // sw)]


def _conv_merge_prompt_kernel(x_ref, hm_ref, wbch_ref, cw_ref, wco_ref, wgc_ref, wmo_ref, wgm_ref,
                              wo_ref, g_ref, b_ref, o_ref, cs_ref, u_s, a_s, mg_s, *, alpha):
    tm, d = a_s.shape
    @pl.when(pl.program_id(1) == 0)
    def _():
        u_s[0:8, :] = jnp.zeros((8, d), F32)

    for r0 in range(0, tm, SUB_TILE):
        sub = min(SUB_TILE, tm - r0)
        x = x_ref[0, r0:r0 + sub, :]
        rows = slice(r0, r0 + sub)
        xb = x.astype(BF16)
        for cs in _conv_stripes(d):
            off = cs.start
            cg = _dot(xb, wbch_ref[:, d + off:d + cs.stop])
            hc = _dot(xb, wbch_ref[:, 2 * d + off:2 * d + cs.stop])
            u = cg * hc
            u_s[8 + r0:8 + r0 + sub, cs] = u
            conv = (u_s[6 + r0:6 + r0 + sub, cs] * cw_ref[0:1, cs]
                    + u_s[7 + r0:7 + r0 + sub, cs] * cw_ref[1:2, cs] + u * cw_ref[2:3, cs])
            bg = _dot(xb, wbch_ref[:, off:cs.stop])
            a_s[rows, cs] = (bg * conv).astype(BF16)
        hm = hm_ref[rows, :]
        for cs in _conv_stripes(d):
            gated_m = _sigmoid(_dot(xb, wgm_ref[:, cs])) * _dot(hm, wmo_ref[:, cs])
            gated_c = _sigmoid(_dot(xb, wgc_ref[:, cs])) * _dot(a_s[rows, :], wco_ref[:, cs])
            mg_s[rows, cs] = (gated_c + gated_m).astype(BF16)
        r = alpha * x + _dot(mg_s[rows, :], wo_ref[...])
        o_ref[rows, :] = _layer_norm(r, g_ref[...], b_ref[...])
    cs_ref[0] = u_s[tm + 6:tm + 8, :]
    u_s[0:8, :] = u_s[tm:tm + 8, :]


def _conv_sample_kernel(x_ref, st_ref, wbch_ref, cw_ref, wco_ref, wgc_ref, m1_ref, cs_ref, a_s,
                        *, n_steps):
    rows, d = a_s.shape
    nb = rows // n_steps
    xb = x_ref[...].astype(BF16)
    for cs in _conv_stripes(d):
        off = cs.start
        bg = _dot(xb, wbch_ref[:, off:cs.stop])
        cg = _dot(xb, wbch_ref[:, d + off:d + cs.stop])
        hc = _dot(xb, wbch_ref[:, 2 * d + off:2 * d + cs.stop])
        u = cg * hc
        up = [st_ref[j * nb:(j + 1) * nb, cs] for j in range(CONV_W - 1)]
        up += [u[t * nb:(t + 1) * nb] for t in range(n_steps)]
        for t in range(n_steps):
            conv = (up[t] * cw_ref[0:1, cs] + up[t + 1] * cw_ref[1:2, cs]
                    + up[t + 2] * cw_ref[2:3, cs])
            a_s[t * nb:(t + 1) * nb, cs] = (bg[t * nb:(t + 1) * nb] * conv).astype(BF16)
        for j in range(CONV_W - 1):
            cs_ref[j * nb:(j + 1) * nb, cs] = up[n_steps + j]
    yc = _dot(a_s[...], wco_ref[...])
    m1_ref[...] = _sigmoid(_dot(xb, wgc_ref[...])) * yc


def _conv_merge_prompt(x, hm, w_a, cw, wco, w_b, wmo, wo, g, b, alpha):
    bsz, t, d = x.shape
    tm = TOKEN_TILE
    flat = pl.BlockSpec((tm, d), lambda bi, j: (bi * (t // tm) + j, 0))
    return pl.pallas_call(
        functools.partial(_conv_merge_prompt_kernel, alpha=alpha),
        grid=(bsz, t // tm),
        in_specs=[pl.BlockSpec((1, tm, d), lambda bi, j: (bi, j, 0)), flat,
                  _bch_spec(d), _const_spec(cw.shape), _const_spec(wco.shape),
                  _gate_spec(d, 0), _const_spec(wmo.shape), _gate_spec(d, 1),
                  _const_spec(wo.shape), _const_spec(g.shape), _const_spec(b.shape)],
        out_specs=[flat, pl.BlockSpec((1, CONV_W - 1, d), lambda bi, j: (bi, 0, 0))],
        out_shape=[jax.ShapeDtypeStruct((bsz * t, d), F32),
                   jax.ShapeDtypeStruct((bsz, CONV_W - 1, d), F32)],
        scratch_shapes=[pltpu.VMEM((tm + 8, d), F32), pltpu.VMEM((tm, d), BF16),
                        pltpu.VMEM((tm, d), BF16)],
        compiler_params=_params(2),
        name="conv_merge_prompt",
    )(x, hm, w_a, cw, wco, w_b, wmo, w_b, wo, g, b)


def _conv_branch_sample(x_tm, st_tm, wbch, cw, wco, wgc, n_steps):
    rows, d = x_tm.shape
    return pl.pallas_call(
        functools.partial(_conv_sample_kernel, n_steps=n_steps),
        grid=(1,),
        in_specs=[_const_spec(x_tm.shape), _const_spec(st_tm.shape), _bch_spec(d),
                  _const_spec(cw.shape), _const_spec(wco.shape), _gate_spec(d, 0)],
        out_specs=[pl.BlockSpec((rows, d), lambda i: (0, 0)),
                   pl.BlockSpec(st_tm.shape, lambda i: (0, 0))],
        out_shape=[jax.ShapeDtypeStruct((rows, d), F32),
                   jax.ShapeDtypeStruct(st_tm.shape, F32)],
        scratch_shapes=[pltpu.VMEM((rows, d), BF16)],
        compiler_params=_params(1),
        name="conv_branch_sample",
    )(x_tm, st_tm, wbch, cw, wco, wgc)


def _rep(col, times):
    return col if times == 1 else jnp.concatenate([col] * times, axis=1)


def _mlstm_prompt_step(x_ref, wqkvo_ref, wgt_ref, bgc_ref, mhg_ref, hm_ref,
                       q_s, k_s, vx_s, og_s, e_s, colb_s, colm_s, cole_s, cst_s, mst_s,
                       *, cur, prev):
    _, tm, d = q_s.shape
    dh = d // N_HEADS
    L = CHUNK
    LANES = colb_s.shape[-1]
    n_chunks = tm // L
    wide = (dh + LANES) // LANES

    xh = x_ref[...].astype(BF16)

    h2 = 2 * N_HEADS
    ga = _dot_nt(wgt_ref[...], xh)
    gt = ga[0:h2] + ga[h2:2 * h2] + bgc_ref[...]
    is_input_gate = lax.broadcasted_iota(jnp.int32, gt.shape, 0) < N_HEADS
    g = jnp.where(is_input_gate, gt, _log_sigmoid(gt))

    pos = lax.broadcasted_iota(jnp.int32, g.shape, 1) % L
    csum = g
    shift = 1
    while shift < L:
        csum = csum + jnp.where(pos >= shift, pltpu.roll(csum, shift, 1), 0.0)
        shift *= 2
    ba = jnp.concatenate([csum[N_HEADS:], g[:N_HEADS] - csum[N_HEADS:]], axis=0)

    ri = lax.broadcasted_iota(jnp.int32, (L, L), 0)
    ci = lax.broadcasted_iota(jnp.int32, (L, L), 1)
    causal = ri >= ci

    for c in range(n_chunks):
        slab = ba[:, c * L:(c + 1) * L]
        cols = slab.T
        for h in range(N_HEADS):
            b_rep = jnp.broadcast_to(cols[:, h:h + 1], (L, LANES))
            a_rep = jnp.broadcast_to(cols[:, N_HEADS + h:N_HEADS + h + 1], (L, LANES))
            a_row = slab[N_HEADS + h:N_HEADS + h + 1, :]
            dm = jnp.where(causal, _rep(b_rep, L // LANES) + a_row, -jnp.inf)
            m_loc = jnp.broadcast_to(jnp.max(dm, axis=1, keepdims=True), (L, LANES))
            e_s[cur, c, h] = jnp.exp(dm - _rep(m_loc, L // LANES))
            colb_s[cur, c, h] = b_rep
            colm_s[cur, c, h] = m_loc
            cole_s[cur, c, h] = jnp.exp(a_rep + b_rep[L - 1:L, :] - m_loc[L - 1:L, :])

    n_split = 2 if N_HEADS % 2 == 0 else 1
    piece = d // n_split

    def project(part, split):
        lo = split * piece
        cols = slice(lo, lo + piece)
        y = _dot(xh, wqkvo_ref[:, part * d + lo:part * d + lo + piece])
        if part == 0:
            q_s[cur, :, cols] = y.astype(BF16)
        elif part == 1:
            k_s[cur, :, cols] = (y * (dh ** -0.5)).astype(BF16)
        elif part == 2:
            v = y.astype(BF16)
            for h in range(lo // dh, (lo + piece) // dh):
                vx_s[cur, h, :, 0:dh] = v[:, h * dh - lo:(h + 1) * dh - lo]
                vx_s[cur, h, :, dh:] = jnp.ones((tm, LANES), BF16)
        else:
            og_s[cur, :, cols] = mhg_ref[:, cols] * _sigmoid(y)

    heads = [slice(h * dh, (h + 1) * dh) for h in range(N_HEADS)]

    def recur_ready(c):
        rows = slice(c * L, (c + 1) * L)
        qk = [_dot_nt(q_s[prev, rows, hc], k_s[prev, rows, hc]) for hc in heads]
        states = [cst_s[h] for h in range(N_HEADS)]
        qc = [_dot(q_s[prev, rows, hc], states[h].astype(BF16)) for h, hc in enumerate(heads)]
        ux = []
        for h, hc in enumerate(heads):
            kw = k_s[prev, rows, hc].astype(F32) * _rep(cole_s[prev, c, h], dh // LANES)
            ux.append(_dot_tn(kw.astype(BF16), vx_s[prev, h, rows, :]))
        return qk, states, qc, ux

    def recur_finish(c, qk, states, qc, ux):
        rows = slice(c * L, (c + 1) * L)
        svx = []
        for h in range(N_HEADS):
            s_loc = qk[h] * e_s[prev, c, h]
            svx.append(_dot(s_loc.astype(BF16), vx_s[prev, h, rows, :]))
        for h, hc in enumerate(heads):
            m_prev = mst_s[h]
            m_loc = colm_s[prev, c, h]
            inter = colb_s[prev, c, h] + m_prev
            m_t = jnp.maximum(inter, m_loc)
            w_inter = _rep(jnp.exp(inter - m_t), wide)
            w_loc = _rep(jnp.exp(m_loc - m_t), wide)
            nd = w_inter * qc[h] + w_loc * svx[h]
            inv = 1.0 / jnp.maximum(jnp.abs(nd[:, dh:]), jnp.exp(-m_t))
            hh = nd[:, 0:dh] * _rep(inv, dh // LANES)
            hm_ref[rows, hc] = (_head_norm(hh) * og_s[prev, rows, hc]).astype(BF16)
            m_new = m_t[L - 1:L, :]
            decay = _rep(jnp.exp(inter[L - 1:L, :] - m_new), wide)
            grow = _rep(jnp.exp(m_loc[L - 1:L, :] - m_new), wide)
            cst_s[h] = decay * states[h] + grow * ux[h]
            mst_s[h] = m_new

    pieces = [(part, split) for part in range(4) for split in range(n_split)]
    n_phases = 2 * n_chunks
    done = 0
    for c in range(n_chunks):
        for phase in (2 * c, 2 * c + 1):
            if phase % 2 == 0:
                partial = recur_ready(c)
            else:
                recur_finish(c, *partial)
            upto = (phase + 1) * len(pieces) // n_phases
            for part, split in pieces[done:upto]:
                project(part, split)
            done = upto


def _mlstm_prompt_kernel(*refs, tiles_per_seq, n_cast):
    x_ref, wqkvo_ref, wgt_ref, bgc_ref, mhg_ref = refs[:5]
    cast_src = refs[5:5 + n_cast]
    hm_ref, c_ref, n_ref, m_ref = refs[5 + n_cast:9 + n_cast]
    cast_dst = refs[9 + n_cast:9 + 2 * n_cast]
    q_s, k_s, vx_s, og_s, e_s, colb_s, colm_s, cole_s, cst_s, mst_s = refs[9 + 2 * n_cast:]
    dh = cst_s.shape[1]
    g_step = pl.program_id(0)
    for src, dst in zip(cast_src, cast_dst):
        dst[...] = src[...].astype(BF16)

    @pl.when(g_step == 0)
    def _():
        for ref in (q_s, k_s, vx_s, og_s, e_s, colb_s, colm_s, cole_s):
            ref[1] = jnp.zeros(ref.shape[1:], ref.dtype)

    @pl.when(jnp.logical_or(g_step == 0, (g_step - 1) % tiles_per_seq == 0))
    def _():
        cst_s[...] = jnp.zeros(cst_s.shape, F32)
        mst_s[...] = jnp.zeros(mst_s.shape, F32)

    for parity in (0, 1):
        @pl.when(g_step % 2 == parity)
        def _():
            _mlstm_prompt_step(x_ref, wqkvo_ref, wgt_ref, bgc_ref, mhg_ref, hm_ref,
                               q_s, k_s, vx_s, og_s, e_s, colb_s, colm_s, cole_s, cst_s, mst_s,
                               cur=parity, prev=1 - parity)

    @pl.when(jnp.logical_and(g_step >= 1, (g_step - 1) % tiles_per_seq == tiles_per_seq - 1))
    def _():
        for h in range(N_HEADS):
            state = cst_s[h]
            c_ref[0, h] = state[:, 0:dh]
            n_ref[0, h] = state[:, dh:].T[0:1, :]
            m_ref[0, h] = mst_s[h][:, 0:1]


def _mlstm_prompt(x, wqkvo, wgt, bgc, mhg, f32_weights):
    bsz, t, d = x.shape
    dh = d // N_HEADS
    tm = MLSTM_TILE
    nc = tm // CHUNK
    dx = dh + LANES
    tps = t // tm
    n_tiles = bsz * tps
    seq_block = lambda g: (jnp.maximum(g - 1, 0) // tps, 0, 0, 0)
    cast_specs = []
    for w in f32_weights:
        rows_per_step = w.shape[0] // n_tiles
        assert rows_per_step * n_tiles == w.shape[0] and rows_per_step % 16 == 0
        cast_specs.append(pl.BlockSpec((rows_per_step, w.shape[1]),
                                       lambda g: (jnp.minimum(g, n_tiles - 1), 0)))
    n_cast = len(f32_weights)
    hm, c, n, m, *casted = pl.pallas_call(
        functools.partial(_mlstm_prompt_kernel, tiles_per_seq=tps, n_cast=n_cast),
        grid=(n_tiles + 1,),
        in_specs=[pl.BlockSpec((tm, d), lambda g: (jnp.minimum(g, n_tiles - 1), 0)),
                  _qkvo_spec(d), _const_spec(wgt.shape), _const_spec(bgc.shape),
                  _const_spec(mhg.shape)] + cast_specs,
        out_specs=[pl.BlockSpec((tm, d), lambda g: (jnp.maximum(g - 1, 0), 0)),
                   pl.BlockSpec((1, N_HEADS, dh, dh), seq_block),
                   pl.BlockSpec((1, N_HEADS, 1, dh), seq_block),
                   pl.BlockSpec((1, N_HEADS, 1, 1), seq_block)] + cast_specs,
        out_shape=[jax.ShapeDtypeStruct((bsz * t, d), BF16),
                   jax.ShapeDtypeStruct((bsz, N_HEADS, dh, dh), F32),
                   jax.ShapeDtypeStruct((bsz, N_HEADS, 1, dh), F32),
                   jax.ShapeDtypeStruct((bsz, N_HEADS, 1, 1), F32)]
                  + [jax.ShapeDtypeStruct(w.shape, BF16) for w in f32_weights],
        scratch_shapes=[
            pltpu.VMEM((2, tm, d), BF16),
            pltpu.VMEM((2, tm, d), BF16),
            pltpu.VMEM((2, N_HEADS, tm, dx), BF16),
            pltpu.VMEM((2, tm, d), F32),
            pltpu.VMEM((2, nc, N_HEADS, CHUNK, CHUNK), F32),
            pltpu.VMEM((2, nc, N_HEADS, CHUNK, LANES), F32),
            pltpu.VMEM((2, nc, N_HEADS, CHUNK, LANES), F32),
            pltpu.VMEM((2, nc, N_HEADS, CHUNK, LANES), F32),
            pltpu.VMEM((N_HEADS, dh, dx), F32),
            pltpu.VMEM((N_HEADS, 1, LANES), F32)],
        compiler_params=_params(1),
        name="mlstm_prompt",
    )(x.reshape(bsz * t, d), wqkvo, wgt, bgc, mhg, *f32_weights)
    return hm, c, n.reshape(bsz, N_HEADS, dh), m.reshape(bsz, N_HEADS), casted


def _mlstm_sample_pre_kernel(x_ref, wq_ref, wk_ref, wv_ref, wo_ref, wgc_ref, bgr_ref, mhg_ref,
                             n0_ref, m0_ref,
                             q_ref, kw_ref, v_ref, og_ref, dec_ref, wa_ref, wb_ref, n_ref, m_ref,
                             g_s, *, n_steps):
    rows, dh = q_ref.shape
    nb = rows // n_steps
    T = n_steps
    hd = pl.program_id(0)
    slab = [slice(t * nb, (t + 1) * nb) for t in range(T)]
    head_lane = lax.broadcasted_iota(jnp.int32, m_ref.shape, 1)
    gate_lane = lax.broadcasted_iota(jnp.int32, (nb, g_s.shape[1]), 1)

    x = x_ref[...]
    xh = x.astype(BF16)

    @pl.when(hd == 0)
    def _():
        xl = (x - xh.astype(F32)).astype(BF16)
        g = _dot(xh, wgc_ref[0]) + _dot(xl, wgc_ref[0]) + _dot(xh, wgc_ref[1]) + bgr_ref[...]
        is_input_gate = lax.broadcasted_iota(jnp.int32, g.shape, 1) < N_HEADS
        g_s[...] = jnp.where(is_input_gate, g, _log_sigmoid(g))
        m_ref[...] = jnp.zeros(m_ref.shape, F32)

    def gate_column(t, col):
        return jnp.sum(jnp.where(gate_lane == col, g_s[slab[t], :], 0.0), axis=1, keepdims=True)

    q_all = _dot(xh, wq_ref[...]).astype(BF16).astype(F32)
    k_all = (_dot(xh, wk_ref[...]) * (dh ** -0.5)).astype(BF16).astype(F32)
    v_all = _dot(xh, wv_ref[...]).astype(BF16).astype(F32)
    og_ref[...] = mhg_ref[...] * _sigmoid(_dot(xh, wo_ref[...]))
    q_ref[...] = q_all
    v_ref[...] = v_all
    qf = [q_all[slab[t]] for t in range(T)]
    kf = [k_all[slab[t]] for t in range(T)]
    vf = [v_all[slab[t]] for t in range(T)]
    li = [gate_column(t, hd) for t in range(T)]
    lf = [gate_column(t, N_HEADS + hd) for t in range(T)]
    m0 = jnp.sum(jnp.where(head_lane == hd, m0_ref[...], 0.0), axis=1, keepdims=True)
    n0 = n0_ref[...]
    b = [lf[0]]
    for t in range(1, T):
        b.append(b[t - 1] + lf[t])
    a = [li[t] - b[t] for t in range(T)]
    m_new = None
    for t in range(T):
        inter = b[t] + m0
        m_t = inter
        for s in range(t + 1):
            m_t = jnp.maximum(m_t, b[t] + a[s])
        w_inter = jnp.exp(inter - m_t)
        num = jnp.zeros((nb, dh), F32)
        den = w_inter * jnp.sum(qf[t] * n0, axis=1, keepdims=True)
        for s in range(t + 1):
            s_w = (jnp.sum(qf[t] * kf[s], axis=1, keepdims=True)
                   * jnp.exp(b[t] + a[s] - m_t))
            num = num + s_w * vf[s]
            den = den + s_w
        inv = 1.0 / jnp.maximum(jnp.abs(den), jnp.exp(-m_t))
        wa_ref[slab[t], :] = jnp.broadcast_to(w_inter * inv, (nb, dh))
        wb_ref[slab[t], :] = num * inv
        m_new = m_t
    b_last = b[T - 1]
    decay = jnp.exp(b_last + m0 - m_new)
    n_new = decay * n0
    for s in range(T):
        kw = kf[s] * jnp.exp(a[s] + b_last - m_new)
        kw_ref[slab[s], :] = kw
        n_new = n_new + kw
    n_ref[...] = n_new
    m_ref[...] = jnp.where(head_lane == hd, m_new, m_ref[...])
    dec_ref[...] = jnp.broadcast_to(decay, (nb, dh))


def _mlstm_sample_pre(x_tm, wqkvo, wgc, bgr, mhg, n0, m0, n_steps):
    rows, d = x_tm.shape
    nb = rows // n_steps
    dh = d // N_HEADS
    head_w = lambda part: pl.BlockSpec((d, dh), lambda h: (0, (3 + part) * N_HEADS + h))
    head_cols = lambda n_rows: pl.BlockSpec((n_rows, dh), lambda h: (0, h))
    big = jax.ShapeDtypeStruct((rows, d), F32)
    small = jax.ShapeDtypeStruct((nb, d), F32)
    return pl.pallas_call(
        functools.partial(_mlstm_sample_pre_kernel, n_steps=n_steps),
        grid=(N_HEADS,),
        in_specs=[_const_spec(x_tm.shape), head_w(0), head_w(1), head_w(2), head_w(3),
                  _const_spec(wgc.shape), _const_spec(bgr.shape), head_cols(1), head_cols(nb),
                  _const_spec(m0.shape)],
        out_specs=[head_cols(rows)] * 4 + [head_cols(nb)] + [head_cols(rows)] * 2
                  + [head_cols(nb), pl.BlockSpec((nb, N_HEADS), lambda h: (0, 0))],
        out_shape=[big, big, big, big, small, big, big, small,
                   jax.ShapeDtypeStruct((nb, N_HEADS), F32)],
        scratch_shapes=[pltpu.VMEM((rows, GATE_PAD), F32)],
        compiler_params=_params(1),
        name="mlstm_sample_pre",
    )(x_tm, wqkvo, wqkvo, wqkvo, wqkvo, wgc, bgr, mhg, n0, m0)


def _stream_matrix_memory(c0_ref, q_ref, kw_ref, v_ref, dec_ref, c_ref, qc_ref):
    T, bb, dh = q_ref.shape
    owner = lax.broadcasted_iota(jnp.int32, (T * bb, dh), 0) % bb
    q_blk = jnp.concatenate([q_ref[t] for t in range(T)], axis=0).astype(BF16)
    kw_blk = jnp.concatenate([kw_ref[t] for t in range(T)], axis=0).astype(BF16)
    v_blk = jnp.concatenate([v_ref[t] for t in range(T)], axis=0)
    c_olds = [c0_ref[bi, 0] for bi in range(bb)]
    reads = [_dot(q_blk, c_olds[bi].astype(BF16)) for bi in range(bb)]
    qc = jnp.zeros((T * bb, dh), F32)
    for bi in range(bb):
        qc = jnp.where(owner == bi, reads[bi], qc)
    for t in range(T):
        qc_ref[t] = qc[t * bb:(t + 1) * bb]
    for bi in range(bb):
        upd = _dot_tn(kw_blk, jnp.where(owner == bi, v_blk, 0.0).astype(BF16))
        c_ref[bi, 0] = dec_ref[bi:bi + 1, :] * c_olds[bi] + upd


def _merge_sample_kernel(x_ref, m1_ref, wa_ref, wb_ref, qc_ref, og_ref, wmo_ref, wgm_ref, wo_ref,
                         g_ref, b_ref, o_ref, hm_s, *, alpha):
    d = x_ref.shape[1]
    dh = d // N_HEADS
    for hd in range(N_HEADS):
        hc = slice(hd * dh, (hd + 1) * dh)
        hh = wa_ref[:, hc] * qc_ref[:, hc] + wb_ref[:, hc]
        hm_s[:, hc] = (_head_norm(hh) * og_ref[:, hc]).astype(BF16)
    x = x_ref[...]
    gate = _dot(x.astype(BF16), wgm_ref[...])
    ym = _dot(hm_s[...], wmo_ref[...])
    merged = m1_ref[...] + _sigmoid(gate) * ym
    r = alpha * x + _dot(merged.astype(BF16), wo_ref[...])
    o_ref[...] = _layer_norm(r, g_ref[...], b_ref[...])


def _merge_sample(x, m1, wa, wb, qc, og, wmo, w_b, wo, g, b, alpha):
    n, d = x.shape
    whole = _const_spec((n, d))
    return pl.pallas_call(
        functools.partial(_merge_sample_kernel, alpha=alpha),
        grid=(1,),
        in_specs=[whole] * 6 + [_const_spec(wmo.shape), _gate_spec(d, 1), _const_spec(wo.shape),
                                _const_spec(g.shape), _const_spec(b.shape)],
        out_specs=pl.BlockSpec((n, d), lambda i: (0, 0)),
        out_shape=jax.ShapeDtypeStruct((n, d), F32),
        scratch_shapes=[pltpu.VMEM((n, d), BF16)],
        compiler_params=_params(1),
        name="merge_ln1_sample",
    )(x, m1, wa, wb, qc, og, wmo, w_b, wo, g, b)


def _ffn_hidden(xb, w1_ref, hid_s, rows):
    dff = hid_s.shape[1]
    sw = 1024 if dff % 1024 == 0 else dff
    for s in range(dff // sw):
        cs = slice(s * sw, (s + 1) * sw)
        hid = jnp.maximum(_dot(xb, w1_ref[:, cs]), 0.0)
        hid_s[rows, cs] = (hid * hid).astype(BF16)


def _ffn_small_kernel(x_ref, w1_ref, w2_ref, g_ref, b_ref, o_ref, acc_s, *, alpha):
    j = pl.program_id(0)
    x = x_ref[...]
    hid = jnp.maximum(_dot(x.astype(BF16), w1_ref[...]), 0.0)
    part = _dot((hid * hid).astype(BF16), w2_ref[...])

    @pl.when(j == 0)
    def _():
        acc_s[...] = alpha * x + part

    @pl.when(j > 0)
    def _():
        acc_s[...] += part

    @pl.when(j == pl.num_programs(0) - 1)
    def _():
        o_ref[...] = _layer_norm(acc_s[...], g_ref[...], b_ref[...])


def _ffn_stream_kernel(x_ref, w1_ref, w2_ref, g_ref, b_ref, c0_ref, q_ref, kw_ref, v_ref,
                       dec_ref, o_ref, c_ref, qc_ref, hid_s, r_s, *, alpha):
    s = pl.program_id(0)
    last = pl.num_programs(0) - 1

    @pl.when(s == 0)
    def _():
        r_s[...] = jnp.zeros(r_s.shape, F32)

    @pl.when(s < last)
    def _():
        _stream_matrix_memory(c0_ref, q_ref, kw_ref, v_ref, dec_ref, c_ref, qc_ref)
        x = x_ref[...]
        _ffn_hidden(x.astype(BF16), w1_ref, hid_s, slice(None))
        o_ref[...] = _layer_norm(r_s[...], g_ref[...], b_ref[...])
        r_s[...] = alpha * x + _dot(hid_s[...], w2_ref[...])

    @pl.when(s == last)
    def _():
        o_ref[...] = _layer_norm(r_s[...], g_ref[...], b_ref[...])


def _ffn_small(x, w1, w2, g, b, alpha):
    n, d = x.shape
    dff = w1.shape[1]
    sw = 1024 if dff % 1024 == 0 else dff
    return pl.pallas_call(
        functools.partial(_ffn_small_kernel, alpha=alpha),
        grid=(dff // sw,),
        in_specs=[_const_spec((n, d)), pl.BlockSpec((d, sw), lambda j: (0, j)),
                  pl.BlockSpec((sw, d), lambda j: (j, 0)), _const_spec(g.shape),
                  _const_spec(b.shape)],
        out_specs=pl.BlockSpec((n, d), lambda j: (0, 0)),
        out_shape=jax.ShapeDtypeStruct((n, d), F32),
        scratch_shapes=[pltpu.VMEM((n, d), F32)],
        compiler_params=_params(1),
        name="ffn_ln2_small",
    )(x, w1, w2, g, b)


def _ffn_with_stream(x, w1, w2, g, b, alpha, c0, q, kw, v, dec, n_steps):
    n, d = x.shape
    rows = q.shape[0]
    nb = rows // n_steps
    dh = d // N_HEADS
    bb = SAMPLE_BATCH_BLOCK
    n_grid = (nb // bb) * N_HEADS
    tm = n // n_grid
    assert tm * n_grid == n and tm % 8 == 0 and nb % bb == 0
    cur = lambda s: jnp.minimum(s, n_grid - 1)
    x_tile = pl.BlockSpec((tm, d), lambda s: (cur(s), 0))
    o_tile = pl.BlockSpec((tm, d), lambda s: (jnp.maximum(s - 1, 0), 0))
    cblock = pl.BlockSpec((bb, 1, dh, dh),
                          lambda s: (cur(s) // N_HEADS, cur(s) % N_HEADS, 0, 0))
    tblock = pl.BlockSpec((n_steps, bb, dh), lambda s: (0, cur(s) // N_HEADS, cur(s) % N_HEADS))
    dblock = pl.BlockSpec((bb, dh), lambda s: (cur(s) // N_HEADS, cur(s) % N_HEADS))
    as_tbd = lambda a: a.reshape(n_steps, nb, d)
    x2, c_new, qc = pl.pallas_call(
        functools.partial(_ffn_stream_kernel, alpha=alpha),
        grid=(n_grid + 1,),
        in_specs=[x_tile, _const_spec(w1.shape), _const_spec(w2.shape), _const_spec(g.shape),
                  _const_spec(b.shape), cblock, tblock, tblock, tblock, dblock],
        out_specs=[o_tile, cblock, tblock],
        out_shape=[jax.ShapeDtypeStruct((n, d), F32), jax.ShapeDtypeStruct(c0.shape, F32),
                   jax.ShapeDtypeStruct((n_steps, nb, d), F32)],
        scratch_shapes=[pltpu.VMEM((tm, w1.shape[1]), BF16), pltpu.VMEM((tm, d), F32)],
        compiler_params=_params(1),
        name="ffn_ln2_stream",
    )(x, w1, w2, g, b, c0, as_tbd(q), as_tbd(kw), as_tbd(v), dec)
    return x2, c_new, qc.reshape(rows, d)


def _w_in_prep_kernel(a_ref, g_ref, b_ref, wa_ref, wg_ref, wb_ref, *, n_a):
    j = pl.program_id(0)

    @pl.when(j == 0)
    def _():
        wg_ref[...] = g_ref[...]

    @pl.when(j < n_a)
    def _():
        wa_ref[...] = a_ref[...].T.astype(BF16)

    @pl.when(j >= n_a)
    def _():
        wb_ref[...] = b_ref[...].T.astype(BF16)


def _prepare_w_in(w_in_t):
    d = w_in_t.shape[1]
    n_a, n_b = 7, 2
    h2 = 2 * N_HEADS
    off_b = n_a * d + h2
    return pl.pallas_call(
        functools.partial(_w_in_prep_kernel, n_a=n_a),
        grid=(n_a + n_b,),
        in_specs=[pl.BlockSpec((d, d), lambda j: (jnp.minimum(j, n_a - 1), 0)),
                  pl.BlockSpec((pl.Element(h2), pl.Element(d)), lambda j: (n_a * d, 0)),
                  pl.BlockSpec((pl.Element(d), pl.Element(d)),
                               lambda j: (pl.multiple_of(off_b + jnp.maximum(j - n_a, 0) * d, 8),
                                          0))],
        out_specs=[pl.BlockSpec((d, d), lambda j: (0, jnp.minimum(j, n_a - 1))),
                   pl.BlockSpec((h2, d), lambda j: (0, 0)),
                   pl.BlockSpec((d, d), lambda j: (0, jnp.maximum(j - n_a, 0)))],
        out_shape=[jax.ShapeDtypeStruct((d, n_a * d), BF16),
                   jax.ShapeDtypeStruct((h2, d), F32),
                   jax.ShapeDtypeStruct((d, n_b * d), BF16)],
        compiler_params=_params(1),
        name="w_in_prep",
    )(w_in_t, w_in_t, w_in_t)


def _layer_weights(w_in, b_gate, conv_w, w_conv_out, mh_g, w_m_out, w_o, ln1_g, ln1_b,
                   w_ff1, w_ff2, ln2_g, ln2_b):
    d = w_in.shape[0]
    h2 = 2 * N_HEADS
    w_in_t = jnp.swapaxes(w_in, 0, 1)
    w_a, wg_t, w_b = _prepare_w_in(w_in_t)
    wgt_hi, wgt_lo = _split_bf16(wg_t)
    wgt = jnp.concatenate([wgt_hi, wgt_lo], axis=0)
    pad = ((0, 0), (0, GATE_PAD - h2))
    wgc = jnp.stack([jnp.pad(wgt_hi.T, pad), jnp.pad(wgt_lo.T, pad)])
    return dict(
        w_a=w_a, w_b=w_b,
        wgt=wgt, wgc=wgc,
        bgc=b_gate.reshape(h2, 1).astype(F32),
        bgr=jnp.pad(b_gate.reshape(1, h2).astype(F32), pad),
        cw=conv_w.astype(F32),
        mhg=mh_g.reshape(1, d).astype(F32),
        f32_weights=[w_conv_out, w_m_out, w_o, w_ff1, w_ff2],
        ln1_g=ln1_g.reshape(1, d), ln1_b=ln1_b.reshape(1, d),
        ln2_g=ln2_g.reshape(1, d), ln2_b=ln2_b.reshape(1, d))


def _layer(x, xs_tm, conv_tm, c0, n0, m0, p, alpha, n_steps):
    bsz, t, d = x.shape
    hm, c_p, n_p, m_p, (wco, wmo, wo, w1, w2) = _mlstm_prompt(
        x, p["w_a"], p["wgt"], p["bgc"], p["mhg"], p["f32_weights"])
    x1, conv_p = _conv_merge_prompt(x, hm, p["w_a"], p["cw"], wco, p["w_b"], wmo, wo,
                                    p["ln1_g"], p["ln1_b"], alpha)
    m1_s, conv_s = _conv_branch_sample(xs_tm, conv_tm, p["w_a"], p["cw"], wco, p["w_b"], n_steps)
    q, kw, v, og, dec, wa, wb, n_s, m_s = _mlstm_sample_pre(
        xs_tm, p["w_a"], p["wgc"], p["bgr"], p["mhg"], n0.reshape(n0.shape[0], d), m0, n_steps)
    x2, c_s, qc = _ffn_with_stream(x1, w1, w2, p["ln2_g"], p["ln2_b"], alpha,
                                   c0, q, kw, v, dec, n_steps)
    x1_s = _merge_sample(xs_tm, m1_s, wa, wb, qc, og, wmo, p["w_b"], wo, p["ln1_g"], p["ln1_b"],
                         alpha)
    x2_s = _ffn_small(x1_s, w1, w2, p["ln2_g"], p["ln2_b"], alpha)
    return (x2.reshape(bsz, t, d), conv_p, c_p, n_p, m_p,
            x2_s, conv_s, c_s, n_s.reshape(n0.shape), m_s)


def kernel(x_prompt, x_sample, state_conv, state_C, state_n, state_m, w_in, b_gate, conv_w,
           w_conv_out, mh_g, w_m_out, w_o, ln1_g, ln1_b, w_ff1, w_ff2, ln2_g, ln2_b):
    depth = w_in.shape[0]
    alpha = (2.0 * depth) ** 0.25
    bsz, t, d = x_prompt.shape
    sb, st, _ = x_sample.shape
    assert t % TOKEN_TILE == 0 and t % MLSTM_TILE == 0 and MLSTM_TILE % CHUNK == 0
    assert d % N_HEADS == 0
    assert sb % SAMPLE_BATCH_BLOCK == 0 and st >= CONV_W - 1

    xp = x_prompt
    xs = jnp.transpose(x_sample, (1, 0, 2)).reshape(st * sb, d)
    outs = [[] for _ in range(8)]
    for l in range(depth):
        p = _layer_weights(w_in[l], b_gate[l], conv_w[l], w_conv_out[l], mh_g[l], w_m_out[l],
                           w_o[l], ln1_g[l], ln1_b[l], w_ff1[l], w_ff2[l], ln2_g[l], ln2_b[l])
        conv_tm = jnp.transpose(state_conv[l], (1, 0, 2)).reshape((CONV_W - 1) * sb, d)
        xp, cp, c_p, n_p, m_p, xs, cs_tm, c_s, n_s, m_s = _layer(
            xp, xs, conv_tm, state_C[l], state_n[l], state_m[l], p, alpha, st)
        cs = jnp.transpose(cs_tm.reshape(CONV_W - 1, sb, d), (1, 0, 2))
        for acc, val in zip(outs, (cp, cs, c_p, c_s, n_p, n_s, m_p, m_s)):
            acc.append(val)
    ys = jnp.transpose(xs.reshape(st, sb, d), (1, 0, 2))
    return (xp, ys) + tuple(jnp.stack(acc) for acc in outs)
```

```python
import functools

import jax
import jax.numpy as jnp
from jax import lax
from jax.experimental import pallas as pl
from jax.experimental.pallas import tpu as pltpu

F32 = jnp.float32
BF16 = jnp.bfloat16

LN_EPS = 1e-5
N_HEADS = 4
CHUNK = 128
CONV_W = 3
TOKEN_TILE = 1024
SUB_TILE = 512
MLSTM_TILE = 512
SAMPLE_BATCH_BLOCK = 16
LANES = 128
GATE_PAD = LANES
VMEM_LIMIT = 56 * 1024 * 1024


def _dot(a, b):
    return jnp.dot(a, b, preferred_element_type=F32)


def _dot_nt(a, b):
    return lax.dot_general(a, b, (((1,), (1,)), ((), ())), preferred_element_type=F32)


def _dot_tn(a, b):
    return lax.dot_general(a, b, (((0,), (0,)), ((), ())), preferred_element_type=F32)


def _sigmoid(x):
    return 1.0 / (1.0 + jnp.exp(-x))


def _log_sigmoid(x):
    return jnp.minimum(x, 0.0) - jnp.log1p(jnp.exp(-jnp.abs(x)))


def _split_bf16(x):
    hi = x.astype(BF16)
    lo = (x - hi.astype(F32)).astype(BF16)
    return hi, lo


def _layer_norm(r, g, b):
    mu = jnp.mean(r, axis=-1, keepdims=True)
    xc = r - mu
    var = jnp.mean(xc * xc, axis=-1, keepdims=True)
    return xc * lax.rsqrt(var + LN_EPS) * g + b


def _head_norm(h):
    mu = jnp.mean(h, axis=-1, keepdims=True)
    hc = h - mu
    return hc * lax.rsqrt(jnp.mean(hc * hc, axis=-1, keepdims=True) + LN_EPS)


def _const_spec(shape):
    zeros = (0,) * len(shape)
    return pl.BlockSpec(shape, lambda *_: zeros, pipeline_mode=pl.Buffered(1))


def _window_spec(block_shape, index):
    return pl.BlockSpec(block_shape, lambda *_: index, pipeline_mode=pl.Buffered(1))


def _bch_spec(d):
    return _window_spec((d, 3 * d), (0, 0))


def _qkvo_spec(d):
    return _window_spec((pl.Element(d), pl.Element(4 * d)), (0, 3 * d))


def _gate_spec(d, which):
    return _window_spec((d, d), (0, which))


def _params(n_axes):
    return pltpu.CompilerParams(dimension_semantics=("arbitrary",) * n_axes,
                                vmem_limit_bytes=VMEM_LIMIT)


def _conv_stripes(d):
    sw = 512 if d % 512 == 0 else d
    return [slice(s * sw, (s + 1) * sw) for s in range(d // sw)]


def _conv_merge_prompt_kernel(x_ref, hm_ref, wbch_ref, cw_ref, wco_ref, wgc_ref, wmo_ref, wgm_ref,
                              wo_ref, g_ref, b_ref, o_ref, cs_ref, u_s, a_s, mg_s, *, alpha):
    tm, d = a_s.shape
    @pl.when(pl.program_id(1) == 0)
    def _():
        u_s[0:8, :] = jnp.zeros((8, d), F32)

    for r0 in range(0, tm, SUB_TILE):
        sub = min(SUB_TILE, tm - r0)
        x = x_ref[0, r0:r0 + sub, :]
        rows = slice(r0, r0 + sub)
        xb = x.astype(BF16)
        for cs in _conv_stripes(d):
            off = cs.start
            cg = _dot(xb, wbch_ref[:, d + off:d + cs.stop])
            hc = _dot(xb, wbch_ref[:, 2 * d + off:2 * d + cs.stop])
            u = cg * hc
            u_s[8 + r0:8 + r0 + sub, cs] = u
            conv = (u_s[6 + r0:6 + r0 + sub, cs] * cw_ref[0:1, cs]
                    + u_s[7 + r0:7 + r0 + sub, cs] * cw_ref[1:2, cs] + u * cw_ref[2:3, cs])
            bg = _dot(xb, wbch_ref[:, off:cs.stop])
            a_s[rows, cs] = (bg * conv).astype(BF16)
        hm = hm_ref[rows, :]
        for cs in _conv_stripes(d):
            gated_m = _sigmoid(_dot(xb, wgm_ref[:, cs])) * _dot(hm, wmo_ref[:, cs])
            gated_c = _sigmoid(_dot(xb, wgc_ref[:, cs])) * _dot(a_s[rows, :], wco_ref[:, cs])
            mg_s[rows, cs] = (gated_c + gated_m).astype(BF16)
        r = alpha * x + _dot(mg_s[rows, :], wo_ref[...])
        o_ref[rows, :] = _layer_norm(r, g_ref[...], b_ref[...])
    cs_ref[0] = u_s[tm + 6:tm + 8, :]
    u_s[0:8, :] = u_s[tm:tm + 8, :]


def _time_major(x_ref):
    return jnp.concatenate([x_ref[:, t, :] for t in range(x_ref.shape[1])], axis=0)


def _conv_sample_kernel(x_ref, st_ref, wbch_ref, cw_ref, wco_ref, wgc_ref, m1_ref, cs_ref, a_s):
    nb, n_steps, d = x_ref.shape
    xb = _time_major(x_ref).astype(BF16)
    for cs in _conv_stripes(d):
        off = cs.start
        bg = _dot(xb, wbch_ref[:, off:cs.stop])
        cg = _dot(xb, wbch_ref[:, d + off:d + cs.stop])
        hc = _dot(xb, wbch_ref[:, 2 * d + off:2 * d + cs.stop])
        u = cg * hc
        up = [st_ref[:, j, cs] for j in range(CONV_W - 1)]
        up += [u[t * nb:(t + 1) * nb] for t in range(n_steps)]
        for t in range(n_steps):
            conv = (up[t] * cw_ref[0:1, cs] + up[t + 1] * cw_ref[1:2, cs]
                    + up[t + 2] * cw_ref[2:3, cs])
            a_s[t * nb:(t + 1) * nb, cs] = (bg[t * nb:(t + 1) * nb] * conv).astype(BF16)
        for j in range(CONV_W - 1):
            cs_ref[:, j, cs] = up[n_steps + j]
    yc = _dot(a_s[...], wco_ref[...])
    m1_ref[...] = _sigmoid(_dot(xb, wgc_ref[...])) * yc


def _conv_merge_prompt(x, hm, w_a, cw, wco, w_b, wmo, wo, g, b, alpha):
    bsz, t, d = x.shape
    tm = TOKEN_TILE
    flat = pl.BlockSpec((tm, d), lambda bi, j: (bi * (t // tm) + j, 0))
    return pl.pallas_call(
        functools.partial(_conv_merge_prompt_kernel, alpha=alpha),
        grid=(bsz, t // tm),
        in_specs=[pl.BlockSpec((1, tm, d), lambda bi, j: (bi, j, 0)), flat,
                  _bch_spec(d), _const_spec(cw.shape), _const_spec(wco.shape),
                  _gate_spec(d, 0), _const_spec(wmo.shape), _gate_spec(d, 1),
                  _const_spec(wo.shape), _const_spec(g.shape), _const_spec(b.shape)],
        out_specs=[flat, pl.BlockSpec((1, CONV_W - 1, d), lambda bi, j: (bi, 0, 0))],
        out_shape=[jax.ShapeDtypeStruct((bsz * t, d), F32),
                   jax.ShapeDtypeStruct((bsz, CONV_W - 1, d), F32)],
        scratch_shapes=[pltpu.VMEM((tm + 8, d), F32), pltpu.VMEM((tm, d), BF16),
                        pltpu.VMEM((tm, d), BF16)],
        compiler_params=_params(2),
        name="conv_merge_prompt",
    )(x, hm, w_a, cw, wco, w_b, wmo, w_b, wo, g, b)


def _conv_branch_sample(xs, conv_buf, wbch, cw, wco, wgc):
    nb, n_steps, d = xs.shape
    rows = nb * n_steps
    return pl.pallas_call(
        _conv_sample_kernel,
        grid=(1,),
        in_specs=[_const_spec(xs.shape), _const_spec(conv_buf.shape), _bch_spec(d),
                  _const_spec(cw.shape), _const_spec(wco.shape), _gate_spec(d, 0)],
        out_specs=[pl.BlockSpec((rows, d), lambda i: (0, 0)),
                   pl.BlockSpec(conv_buf.shape, lambda i: (0, 0, 0))],
        out_shape=[jax.ShapeDtypeStruct((rows, d), F32),
                   jax.ShapeDtypeStruct(conv_buf.shape, F32)],
        scratch_shapes=[pltpu.VMEM((rows, d), BF16)],
        compiler_params=_params(1),
        name="conv_branch_sample",
    )(xs, conv_buf, wbch, cw, wco, wgc)


def _rep(col, times):
    return col if times == 1 else jnp.concatenate([col] * times, axis=1)


def _mlstm_prompt_step(x_ref, wqkvo_ref, wgt_ref, bgc_ref, mhg_ref, hm_ref,
                       q_s, k_s, vx_s, og_s, e_s, colb_s, colm_s, cole_s, cst_s, mst_s,
                       *, cur, prev):
    _, tm, d = q_s.shape
    dh = d // N_HEADS
    L = CHUNK
    LANES = colb_s.shape[-1]
    n_chunks = tm // L
    wide = (dh + LANES) // LANES

    xh = x_ref[...].astype(BF16)

    h2 = 2 * N_HEADS
    ga = _dot_nt(wgt_ref[...], xh)
    gt = ga[0:h2] + ga[h2:2 * h2] + bgc_ref[...]
    is_input_gate = lax.broadcasted_iota(jnp.int32, gt.shape, 0) < N_HEADS
    g = jnp.where(is_input_gate, gt, _log_sigmoid(gt))

    pos = lax.broadcasted_iota(jnp.int32, g.shape, 1) % L
    csum = g
    shift = 1
    while shift < L:
        csum = csum + jnp.where(pos >= shift, pltpu.roll(csum, shift, 1), 0.0)
        shift *= 2
    ba = jnp.concatenate([csum[N_HEADS:], g[:N_HEADS] - csum[N_HEADS:]], axis=0)

    ri = lax.broadcasted_iota(jnp.int32, (L, L), 0)
    ci = lax.broadcasted_iota(jnp.int32, (L, L), 1)
    causal = ri >= ci

    for c in range(n_chunks):
        slab = ba[:, c * L:(c + 1) * L]
        cols = slab.T
        for h in range(N_HEADS):
            b_rep = jnp.broadcast_to(cols[:, h:h + 1], (L, LANES))
            a_rep = jnp.broadcast_to(cols[:, N_HEADS + h:N_HEADS + h + 1], (L, LANES))
            a_row = slab[N_HEADS + h:N_HEADS + h + 1, :]
            dm = jnp.where(causal, _rep(b_rep, L // LANES) + a_row, -jnp.inf)
            m_loc = jnp.broadcast_to(jnp.max(dm, axis=1, keepdims=True), (L, LANES))
            e_s[cur, c, h] = jnp.exp(dm - _rep(m_loc, L // LANES))
            colb_s[cur, c, h] = b_rep
            colm_s[cur, c, h] = m_loc
            cole_s[cur, c, h] = jnp.exp(a_rep + b_rep[L - 1:L, :] - m_loc[L - 1:L, :])

    n_split = 2 if N_HEADS % 2 == 0 else 1
    piece = d // n_split

    def project(part, split):
        lo = split * piece
        cols = slice(lo, lo + piece)
        y = _dot(xh, wqkvo_ref[:, part * d + lo:part * d + lo + piece])
        if part == 0:
            q_s[cur, :, cols] = y.astype(BF16)
        elif part == 1:
            k_s[cur, :, cols] = (y * (dh ** -0.5)).astype(BF16)
        elif part == 2:
            v = y.astype(BF16)
            for h in range(lo // dh, (lo + piece) // dh):
                vx_s[cur, h, :, 0:dh] = v[:, h * dh - lo:(h + 1) * dh - lo]
                vx_s[cur, h, :, dh:] = jnp.ones((tm, LANES), BF16)
        else:
            og_s[cur, :, cols] = mhg_ref[:, cols] * _sigmoid(y)

    heads = [slice(h * dh, (h + 1) * dh) for h in range(N_HEADS)]

    def recur_ready(c):
        rows = slice(c * L, (c + 1) * L)
        qk = [_dot_nt(q_s[prev, rows, hc], k_s[prev, rows, hc]) for hc in heads]
        states = [cst_s[h] for h in range(N_HEADS)]
        qc = [_dot(q_s[prev, rows, hc], states[h].astype(BF16)) for h, hc in enumerate(heads)]
        ux = []
        for h, hc in enumerate(heads):
            kw = k_s[prev, rows, hc].astype(F32) * _rep(cole_s[prev, c, h], dh // LANES)
            ux.append(_dot_tn(kw.astype(BF16), vx_s[prev, h, rows, :]))
        return qk, states, qc, ux

    def recur_finish(c, qk, states, qc, ux):
        rows = slice(c * L, (c + 1) * L)
        svx = []
        for h in range(N_HEADS):
            s_loc = qk[h] * e_s[prev, c, h]
            svx.append(_dot(s_loc.astype(BF16), vx_s[prev, h, rows, :]))
        for h, hc in enumerate(heads):
            m_prev = mst_s[h]
            m_loc = colm_s[prev, c, h]
            inter = colb_s[prev, c, h] + m_prev
            m_t = jnp.maximum(inter, m_loc)
            w_inter = _rep(jnp.exp(inter - m_t), wide)
            w_loc = _rep(jnp.exp(m_loc - m_t), wide)
            nd = w_inter * qc[h] + w_loc * svx[h]
            inv = 1.0 / jnp.maximum(jnp.abs(nd[:, dh:]), jnp.exp(-m_t))
            hh = nd[:, 0:dh] * _rep(inv, dh // LANES)
            hm_ref[rows, hc] = (_head_norm(hh) * og_s[prev, rows, hc]).astype(BF16)
            m_new = m_t[L - 1:L, :]
            decay = _rep(jnp.exp(inter[L - 1:L, :] - m_new), wide)
            grow = _rep(jnp.exp(m_loc[L - 1:L, :] - m_new), wide)
            cst_s[h] = decay * states[h] + grow * ux[h]
            mst_s[h] = m_new

    pieces = [(part, split) for part in range(4) for split in range(n_split)]
    n_phases = 2 * n_chunks
    done = 0
    for c in range(n_chunks):
        for phase in (2 * c, 2 * c + 1):
            if phase % 2 == 0:
                partial = recur_ready(c)
            else:
                recur_finish(c, *partial)
            upto = (phase + 1) * len(pieces) // n_phases
            for part, split in pieces[done:upto]:
                project(part, split)
            done = upto


def _mlstm_prompt_kernel(*refs, tiles_per_seq, n_cast):
    x_ref, wqkvo_ref, wgt_ref, bgc_ref, mhg_ref = refs[:5]
    cast_src = refs[5:5 + n_cast]
    hm_ref, c_ref, n_ref, m_ref = refs[5 + n_cast:9 + n_cast]
    cast_dst = refs[9 + n_cast:9 + 2 * n_cast]
    q_s, k_s, vx_s, og_s, e_s, colb_s, colm_s, cole_s, cst_s, mst_s = refs[9 + 2 * n_cast:]
    dh = cst_s.shape[1]
    g_step = pl.program_id(0)
    for src, dst in zip(cast_src, cast_dst):
        dst[...] = src[...].astype(BF16)

    @pl.when(g_step == 0)
    def _():
        for ref in (q_s, k_s, vx_s, og_s, e_s, colb_s, colm_s, cole_s):
            ref[1] = jnp.zeros(ref.shape[1:], ref.dtype)

    @pl.when(jnp.logical_or(g_step == 0, (g_step - 1) % tiles_per_seq == 0))
    def _():
        cst_s[...] = jnp.zeros(cst_s.shape, F32)
        mst_s[...] = jnp.zeros(mst_s.shape, F32)

    for parity in (0, 1):
        @pl.when(g_step % 2 == parity)
        def _():
            _mlstm_prompt_step(x_ref, wqkvo_ref, wgt_ref, bgc_ref, mhg_ref, hm_ref,
                               q_s, k_s, vx_s, og_s, e_s, colb_s, colm_s, cole_s, cst_s, mst_s,
                               cur=parity, prev=1 - parity)

    @pl.when(jnp.logical_and(g_step >= 1, (g_step - 1) % tiles_per_seq == tiles_per_seq - 1))
    def _():
        for h in range(N_HEADS):
            state = cst_s[h]
            c_ref[0, h] = state[:, 0:dh]
            n_ref[0, h] = state[:, dh:].T[0:1, :]
            m_ref[0, h] = mst_s[h][:, 0:1]


def _mlstm_prompt(x, wqkvo, wgt, bgc, mhg, f32_weights):
    bsz, t, d = x.shape
    dh = d // N_HEADS
    tm = MLSTM_TILE
    nc = tm // CHUNK
    dx = dh + LANES
    tps = t // tm
    n_tiles = bsz * tps
    seq_block = lambda g: (jnp.maximum(g - 1, 0) // tps, 0, 0, 0)
    cast_specs = []
    for w in f32_weights:
        rows_per_step = w.shape[0] // n_tiles
        assert rows_per_step * n_tiles == w.shape[0] and rows_per_step % 16 == 0
        cast_specs.append(pl.BlockSpec((rows_per_step, w.shape[1]),
                                       lambda g: (jnp.minimum(g, n_tiles - 1), 0)))
    n_cast = len(f32_weights)
    hm, c, n, m, *casted = pl.pallas_call(
        functools.partial(_mlstm_prompt_kernel, tiles_per_seq=tps, n_cast=n_cast),
        grid=(n_tiles + 1,),
        in_specs=[pl.BlockSpec((tm, d), lambda g: (jnp.minimum(g, n_tiles - 1), 0)),
                  _qkvo_spec(d), _const_spec(wgt.shape), _const_spec(bgc.shape),
                  _const_spec(mhg.shape)] + cast_specs,
        out_specs=[pl.BlockSpec((tm, d), lambda g: (jnp.maximum(g - 1, 0), 0)),
                   pl.BlockSpec((1, N_HEADS, dh, dh), seq_block),
                   pl.BlockSpec((1, N_HEADS, 1, dh), seq_block),
                   pl.BlockSpec((1, N_HEADS, 1, 1), seq_block)] + cast_specs,
        out_shape=[jax.ShapeDtypeStruct((bsz * t, d), BF16),
                   jax.ShapeDtypeStruct((bsz, N_HEADS, dh, dh), F32),
                   jax.ShapeDtypeStruct((bsz, N_HEADS, 1, dh), F32),
                   jax.ShapeDtypeStruct((bsz, N_HEADS, 1, 1), F32)]
                  + [jax.ShapeDtypeStruct(w.shape, BF16) for w in f32_weights],
        scratch_shapes=[
            pltpu.VMEM((2, tm, d), BF16),
            pltpu.VMEM((2, tm, d), BF16),
            pltpu.VMEM((2, N_HEADS, tm, dx), BF16),
            pltpu.VMEM((2, tm, d), F32),
            pltpu.VMEM((2, nc, N_HEADS, CHUNK, CHUNK), F32),
            pltpu.VMEM((2, nc, N_HEADS, CHUNK, LANES), F32),
            pltpu.VMEM((2, nc, N_HEADS, CHUNK, LANES), F32),
            pltpu.VMEM((2, nc, N_HEADS, CHUNK, LANES), F32),
            pltpu.VMEM((N_HEADS, dh, dx), F32),
            pltpu.VMEM((N_HEADS, 1, LANES), F32)],
        compiler_params=_params(1),
        name="mlstm_prompt",
    )(x.reshape(bsz * t, d), wqkvo, wgt, bgc, mhg, *f32_weights)
    return hm, c, n.reshape(bsz, N_HEADS, dh), m.reshape(bsz, N_HEADS), casted


def _mlstm_sample_pre_kernel(x_ref, wq_ref, wk_ref, wv_ref, wo_ref, wgc_ref, bgr_ref, mhg_ref,
                             n0_ref, m0_ref,
                             q_ref, kw_ref, v_ref, og_ref, dec_ref, wa_ref, wb_ref, n_ref, m_ref,
                             g_s, *, n_steps):
    rows, dh = q_ref.shape
    nb = rows // n_steps
    T = n_steps
    hd = pl.program_id(0)
    slab = [slice(t * nb, (t + 1) * nb) for t in range(T)]
    head_lane = lax.broadcasted_iota(jnp.int32, m_ref.shape, 1)
    gate_lane = lax.broadcasted_iota(jnp.int32, (nb, g_s.shape[1]), 1)

    x = _time_major(x_ref)
    xh = x.astype(BF16)

    @pl.when(hd == 0)
    def _():
        xl = (x - xh.astype(F32)).astype(BF16)
        g = _dot(xh, wgc_ref[0]) + _dot(xl, wgc_ref[0]) + _dot(xh, wgc_ref[1]) + bgr_ref[...]
        is_input_gate = lax.broadcasted_iota(jnp.int32, g.shape, 1) < N_HEADS
        g_s[...] = jnp.where(is_input_gate, g, _log_sigmoid(g))
        m_ref[...] = jnp.zeros(m_ref.shape, F32)

    def gate_column(t, col):
        return jnp.sum(jnp.where(gate_lane == col, g_s[slab[t], :], 0.0), axis=1, keepdims=True)

    q_all = _dot(xh, wq_ref[...]).astype(BF16).astype(F32)
    k_all = (_dot(xh, wk_ref[...]) * (dh ** -0.5)).astype(BF16).astype(F32)
    v_all = _dot(xh, wv_ref[...]).astype(BF16).astype(F32)
    og_ref[...] = mhg_ref[...] * _sigmoid(_dot(xh, wo_ref[...]))
    q_ref[...] = q_all
    v_ref[...] = v_all
    qf = [q_all[slab[t]] for t in range(T)]
    kf = [k_all[slab[t]] for t in range(T)]
    vf = [v_all[slab[t]] for t in range(T)]
    li = [gate_column(t, hd) for t in range(T)]
    lf = [gate_column(t, N_HEADS + hd) for t in range(T)]
    m0 = jnp.sum(jnp.where(head_lane == hd, m0_ref[...], 0.0), axis=1, keepdims=True)
    n0 = n0_ref[...]
    b = [lf[0]]
    for t in range(1, T):
        b.append(b[t - 1] + lf[t])
    a = [li[t] - b[t] for t in range(T)]
    m_new = None
    for t in range(T):
        inter = b[t] + m0
        m_t = inter
        for s in range(t + 1):
            m_t = jnp.maximum(m_t, b[t] + a[s])
        w_inter = jnp.exp(inter - m_t)
        num = jnp.zeros((nb, dh), F32)
        den = w_inter * jnp.sum(qf[t] * n0, axis=1, keepdims=True)
        for s in range(t + 1):
            s_w = (jnp.sum(qf[t] * kf[s], axis=1, keepdims=True)
                   * jnp.exp(b[t] + a[s] - m_t))
            num = num + s_w * vf[s]
            den = den + s_w
        inv = 1.0 / jnp.maximum(jnp.abs(den), jnp.exp(-m_t))
        wa_ref[slab[t], :] = jnp.broadcast_to(w_inter * inv, (nb, dh))
        wb_ref[slab[t], :] = num * inv
        m_new = m_t
    b_last = b[T - 1]
    decay = jnp.exp(b_last + m0 - m_new)
    n_new = decay * n0
    for s in range(T):
        kw = kf[s] * jnp.exp(a[s] + b_last - m_new)
        kw_ref[slab[s], :] = kw
        n_new = n_new + kw
    n_ref[...] = n_new
    m_ref[...] = jnp.where(head_lane == hd, m_new, m_ref[...])
    dec_ref[...] = jnp.broadcast_to(decay, (nb, dh))


def _mlstm_sample_pre(xs, wqkvo, wgc, bgr, mhg, n0, m0):
    nb, n_steps, d = xs.shape
    rows = nb * n_steps
    dh = d // N_HEADS
    head_w = lambda part: pl.BlockSpec((d, dh), lambda h: (0, (3 + part) * N_HEADS + h))
    head_cols = lambda n_rows: pl.BlockSpec((n_rows, dh), lambda h: (0, h))
    big = jax.ShapeDtypeStruct((rows, d), F32)
    small = jax.ShapeDtypeStruct((nb, d), F32)
    return pl.pallas_call(
        functools.partial(_mlstm_sample_pre_kernel, n_steps=n_steps),
        grid=(N_HEADS,),
        in_specs=[_const_spec(xs.shape), head_w(0), head_w(1), head_w(2), head_w(3),
                  _const_spec(wgc.shape), _const_spec(bgr.shape), head_cols(1), head_cols(nb),
                  _const_spec(m0.shape)],
        out_specs=[head_cols(rows)] * 4 + [head_cols(nb)] + [head_cols(rows)] * 2
                  + [head_cols(nb), pl.BlockSpec((nb, N_HEADS), lambda h: (0, 0))],
        out_shape=[big, big, big, big, small, big, big, small,
                   jax.ShapeDtypeStruct((nb, N_HEADS), F32)],
        scratch_shapes=[pltpu.VMEM((rows, GATE_PAD), F32)],
        compiler_params=_params(1),
        name="mlstm_sample_pre",
    )(xs, wqkvo, wqkvo, wqkvo, wqkvo, wgc, bgr, mhg, n0, m0)


def _stream_matrix_memory(c0_ref, q_ref, kw_ref, v_ref, dec_ref, c_ref, qc_ref):
    T, bb, dh = q_ref.shape
    owner = lax.broadcasted_iota(jnp.int32, (T * bb, dh), 0) % bb
    q_blk = jnp.concatenate([q_ref[t] for t in range(T)], axis=0).astype(BF16)
    kw_blk = jnp.concatenate([kw_ref[t] for t in range(T)], axis=0).astype(BF16)
    v_blk = jnp.concatenate([v_ref[t] for t in range(T)], axis=0)
    c_olds = [c0_ref[bi, 0] for bi in range(bb)]
    reads = [_dot(q_blk, c_olds[bi].astype(BF16)) for bi in range(bb)]
    qc = jnp.zeros((T * bb, dh), F32)
    for bi in range(bb):
        qc = jnp.where(owner == bi, reads[bi], qc)
    for t in range(T):
        qc_ref[t] = qc[t * bb:(t + 1) * bb]
    for bi in range(bb):
        upd = _dot_tn(kw_blk, jnp.where(owner == bi, v_blk, 0.0).astype(BF16))
        c_ref[bi, 0] = dec_ref[bi:bi + 1, :] * c_olds[bi] + upd


def _merge_sample_kernel(x_ref, m1_ref, wa_ref, wb_ref, qc_ref, og_ref, wmo_ref, wgm_ref, wo_ref,
                         g_ref, b_ref, o_ref, hm_s, *, alpha):
    d = x_ref.shape[-1]
    dh = d // N_HEADS
    for hd in range(N_HEADS):
        hc = slice(hd * dh, (hd + 1) * dh)
        hh = wa_ref[:, hc] * qc_ref[:, hc] + wb_ref[:, hc]
        hm_s[:, hc] = (_head_norm(hh) * og_ref[:, hc]).astype(BF16)
    x = _time_major(x_ref)
    gate = _dot(x.astype(BF16), wgm_ref[...])
    ym = _dot(hm_s[...], wmo_ref[...])
    merged = m1_ref[...] + _sigmoid(gate) * ym
    r = alpha * x + _dot(merged.astype(BF16), wo_ref[...])
    o_ref[...] = _layer_norm(r, g_ref[...], b_ref[...])


def _merge_sample(xs, m1, wa, wb, qc, og, wmo, w_b, wo, g, b, alpha):
    n, d = m1.shape
    whole = _const_spec((n, d))
    return pl.pallas_call(
        functools.partial(_merge_sample_kernel, alpha=alpha),
        grid=(1,),
        in_specs=[_const_spec(xs.shape)] + [whole] * 5
                 + [_const_spec(wmo.shape), _gate_spec(d, 1), _const_spec(wo.shape),
                    _const_spec(g.shape), _const_spec(b.shape)],
        out_specs=pl.BlockSpec((n, d), lambda i: (0, 0)),
        out_shape=jax.ShapeDtypeStruct((n, d), F32),
        scratch_shapes=[pltpu.VMEM((n, d), BF16)],
        compiler_params=_params(1),
        name="merge_ln1_sample",
    )(xs, m1, wa, wb, qc, og, wmo, w_b, wo, g, b)


def _ffn_hidden(xb, w1_ref, hid_s, rows):
    dff = hid_s.shape[1]
    sw = 1024 if dff % 1024 == 0 else dff
    for s in range(dff // sw):
        cs = slice(s * sw, (s + 1) * sw)
        hid = jnp.maximum(_dot(xb, w1_ref[:, cs]), 0.0)
        hid_s[rows, cs] = (hid * hid).astype(BF16)


def _ffn_small_kernel(x_ref, w1_ref, w2_ref, g_ref, b_ref, o_ref, acc_s, *, alpha):
    j = pl.program_id(0)
    x = x_ref[...]
    hid = jnp.maximum(_dot(x.astype(BF16), w1_ref[...]), 0.0)
    part = _dot((hid * hid).astype(BF16), w2_ref[...])

    @pl.when(j == 0)
    def _():
        acc_s[...] = alpha * x + part

    @pl.when(j > 0)
    def _():
        acc_s[...] += part

    @pl.when(j == pl.num_programs(0) - 1)
    def _():
        y = _layer_norm(acc_s[...], g_ref[...], b_ref[...])
        nb, n_t, _ = o_ref.shape
        for t in range(n_t):
            o_ref[:, t, :] = y[t * nb:(t + 1) * nb]


def _ffn_stream_kernel(x_ref, w1_ref, w2_ref, g_ref, b_ref, c0_ref, q_ref, kw_ref, v_ref,
                       dec_ref, o_ref, c_ref, qc_ref, hid_s, r_s, *, alpha):
    s = pl.program_id(0)
    last = pl.num_programs(0) - 1

    @pl.when(s == 0)
    def _():
        r_s[...] = jnp.zeros(r_s.shape, F32)

    @pl.when(s < last)
    def _():
        _stream_matrix_memory(c0_ref, q_ref, kw_ref, v_ref, dec_ref, c_ref, qc_ref)
        x = x_ref[...]
        _ffn_hidden(x.astype(BF16), w1_ref, hid_s, slice(None))
        o_ref[...] = _layer_norm(r_s[...], g_ref[...], b_ref[...])
        r_s[...] = alpha * x + _dot(hid_s[...], w2_ref[...])

    @pl.when(s == last)
    def _():
        o_ref[...] = _layer_norm(r_s[...], g_ref[...], b_ref[...])


def _ffn_small(x, w1, w2, g, b, alpha, n_steps):
    n, d = x.shape
    dff = w1.shape[1]
    sw = 1024 if dff % 1024 == 0 else dff
    out_shape = (n // n_steps, n_steps, d)
    return pl.pallas_call(
        functools.partial(_ffn_small_kernel, alpha=alpha),
        grid=(dff // sw,),
        in_specs=[_const_spec((n, d)), pl.BlockSpec((d, sw), lambda j: (0, j)),
                  pl.BlockSpec((sw, d), lambda j: (j, 0)), _const_spec(g.shape),
                  _const_spec(b.shape)],
        out_specs=pl.BlockSpec(out_shape, lambda j: (0,) * len(out_shape)),
        out_shape=jax.ShapeDtypeStruct(out_shape, F32),
        scratch_shapes=[pltpu.VMEM((n, d), F32)],
        compiler_params=_params(1),
        name="ffn_ln2_small",
    )(x, w1, w2, g, b)


def _ffn_with_stream(x, w1, w2, g, b, alpha, c0, q, kw, v, dec, n_steps):
    n, d = x.shape
    rows = q.shape[0]
    nb = rows // n_steps
    dh = d // N_HEADS
    bb = SAMPLE_BATCH_BLOCK
    n_grid = (nb // bb) * N_HEADS
    tm = n // n_grid
    assert tm * n_grid == n and tm % 8 == 0 and nb % bb == 0
    cur = lambda s: jnp.minimum(s, n_grid - 1)
    x_tile = pl.BlockSpec((tm, d), lambda s: (cur(s), 0))
    o_tile = pl.BlockSpec((tm, d), lambda s: (jnp.maximum(s - 1, 0), 0))
    cblock = pl.BlockSpec((bb, 1, dh, dh),
                          lambda s: (cur(s) // N_HEADS, cur(s) % N_HEADS, 0, 0))
    tblock = pl.BlockSpec((n_steps, bb, dh), lambda s: (0, cur(s) // N_HEADS, cur(s) % N_HEADS))
    dblock = pl.BlockSpec((bb, dh), lambda s: (cur(s) // N_HEADS, cur(s) % N_HEADS))
    as_tbd = lambda a: a.reshape(n_steps, nb, d)
    x2, c_new, qc = pl.pallas_call(
        functools.partial(_ffn_stream_kernel, alpha=alpha),
        grid=(n_grid + 1,),
        in_specs=[x_tile, _const_spec(w1.shape), _const_spec(w2.shape), _const_spec(g.shape),
                  _const_spec(b.shape), cblock, tblock, tblock, tblock, dblock],
        out_specs=[o_tile, cblock, tblock],
        out_shape=[jax.ShapeDtypeStruct((n, d), F32), jax.ShapeDtypeStruct(c0.shape, F32),
                   jax.ShapeDtypeStruct((n_steps, nb, d), F32)],
        scratch_shapes=[pltpu.VMEM((tm, w1.shape[1]), BF16), pltpu.VMEM((tm, d), F32)],
        compiler_params=_params(1),
        name="ffn_ln2_stream",
    )(x, w1, w2, g, b, c0, as_tbd(q), as_tbd(kw), as_tbd(v), dec)
    return x2, c_new, qc.reshape(rows, d)


def _w_in_prep_kernel(a_ref, g_ref, b_ref, wa_ref, wg_ref, wb_ref, *, n_a):
    j = pl.program_id(0)

    @pl.when(j == 0)
    def _():
        wg_ref[...] = g_ref[...]

    @pl.when(j < n_a)
    def _():
        wa_ref[...] = a_ref[...].T.astype(BF16)

    @pl.when(j >= n_a)
    def _():
        wb_ref[...] = b_ref[...].T.astype(BF16)


def _prepare_w_in(w_in_t):
    d = w_in_t.shape[1]
    n_a, n_b = 7, 2
    h2 = 2 * N_HEADS
    off_b = n_a * d + h2
    return pl.pallas_call(
        functools.partial(_w_in_prep_kernel, n_a=n_a),
        grid=(n_a + n_b,),
        in_specs=[pl.BlockSpec((d, d), lambda j: (jnp.minimum(j, n_a - 1), 0)),
                  pl.BlockSpec((pl.Element(h2), pl.Element(d)), lambda j: (n_a * d, 0)),
                  pl.BlockSpec((pl.Element(d), pl.Element(d)),
                               lambda j: (pl.multiple_of(off_b + jnp.maximum(j - n_a, 0) * d, 8),
                                          0))],
        out_specs=[pl.BlockSpec((d, d), lambda j: (0, jnp.minimum(j, n_a - 1))),
                   pl.BlockSpec((h2, d), lambda j: (0, 0)),
                   pl.BlockSpec((d, d), lambda j: (0, jnp.maximum(j - n_a, 0)))],
        out_shape=[jax.ShapeDtypeStruct((d, n_a * d), BF16),
                   jax.ShapeDtypeStruct((h2, d), F32),
                   jax.ShapeDtypeStruct((d, n_b * d), BF16)],
        compiler_params=_params(1),
        name="w_in_prep",
    )(w_in_t, w_in_t, w_in_t)


def _layer_weights(w_in, b_gate, conv_w, w_conv_out, mh_g, w_m_out, w_o, ln1_g, ln1_b,
                   w_ff1, w_ff2, ln2_g, ln2_b):
    d = w_in.shape[0]
    h2 = 2 * N_HEADS
    w_in_t = jnp.swapaxes(w_in, 0, 1)
    w_a, wg_t, w_b = _prepare_w_in(w_in_t)
    wgt_hi, wgt_lo = _split_bf16(wg_t)
    wgt = jnp.concatenate([wgt_hi, wgt_lo], axis=0)
    pad = ((0, 0), (0, GATE_PAD - h2))
    wgc = jnp.stack([jnp.pad(wgt_hi.T, pad), jnp.pad(wgt_lo.T, pad)])
    return dict(
        w_a=w_a, w_b=w_b,
        wgt=wgt, wgc=wgc,
        bgc=b_gate.reshape(h2, 1).astype(F32),
        bgr=jnp.pad(b_gate.reshape(1, h2).astype(F32), pad),
        cw=conv_w.astype(F32),
        mhg=mh_g.reshape(1, d).astype(F32),
        f32_weights=[w_conv_out, w_m_out, w_o, w_ff1, w_ff2],
        ln1_g=ln1_g.reshape(1, d), ln1_b=ln1_b.reshape(1, d),
        ln2_g=ln2_g.reshape(1, d), ln2_b=ln2_b.reshape(1, d))


def _layer(x, xs, conv_buf, c0, n0, m0, p, alpha):
    bsz, t, d = x.shape
    n_steps = xs.shape[1]
    hm, c_p, n_p, m_p, (wco, wmo, wo, w1, w2) = _mlstm_prompt(
        x, p["w_a"], p["wgt"], p["bgc"], p["mhg"], p["f32_weights"])
    x1, conv_p = _conv_merge_prompt(x, hm, p["w_a"], p["cw"], wco, p["w_b"], wmo, wo,
                                    p["ln1_g"], p["ln1_b"], alpha)
    m1_s, conv_s = _conv_branch_sample(xs, conv_buf, p["w_a"], p["cw"], wco, p["w_b"])
    q, kw, v, og, dec, wa, wb, n_s, m_s = _mlstm_sample_pre(
        xs, p["w_a"], p["wgc"], p["bgr"], p["mhg"], n0.reshape(n0.shape[0], d), m0)
    x2, c_s, qc = _ffn_with_stream(x1, w1, w2, p["ln2_g"], p["ln2_b"], alpha,
                                   c0, q, kw, v, dec, n_steps)
    x1_s = _merge_sample(xs, m1_s, wa, wb, qc, og, wmo, p["w_b"], wo, p["ln1_g"], p["ln1_b"],
                         alpha)
    x2_s = _ffn_small(x1_s, w1, w2, p["ln2_g"], p["ln2_b"], alpha, n_steps)
    return (x2.reshape(bsz, t, d), conv_p, c_p, n_p, m_p,
            x2_s, conv_s, c_s, n_s.reshape(n0.shape), m_s)


def kernel(x_prompt, x_sample, state_conv, state_C, state_n, state_m, w_in, b_gate, conv_w,
           w_conv_out, mh_g, w_m_out, w_o, ln1_g, ln1_b, w_ff1, w_ff2, ln2_g, ln2_b):
    depth = w_in.shape[0]
    alpha = (2.0 * depth) ** 0.25
    bsz, t, d = x_prompt.shape
    sb, st, _ = x_sample.shape
    assert t % TOKEN_TILE == 0 and t % MLSTM_TILE == 0 and MLSTM_TILE % CHUNK == 0
    assert d % N_HEADS == 0
    assert sb % SAMPLE_BATCH_BLOCK == 0 and st >= CONV_W - 1

    xp, xs = x_prompt, x_sample
    outs = [[] for _ in range(8)]
    for l in range(depth):
        p = _layer_weights(w_in[l], b_gate[l], conv_w[l], w_conv_out[l], mh_g[l], w_m_out[l],
                           w_o[l], ln1_g[l], ln1_b[l], w_ff1[l], w_ff2[l], ln2_g[l], ln2_b[l])
        xp, cp, c_p, n_p, m_p, xs, cs, c_s, n_s, m_s = _layer(
            xp, xs, state_conv[l], state_C[l], state_n[l], state_m[l], p, alpha)
        for acc, val in zip(outs, (cp, cs, c_p, c_s, n_p, n_s, m_p, m_s)):
            acc.append(val)
    return (xp, xs) + tuple(jnp.stack(acc) for acc in outs)
```

```python
import functools

import jax
import jax.numpy as jnp
from jax import lax
from jax.experimental import pallas as pl
from jax.experimental.pallas import tpu as pltpu

F32 = jnp.float32
BF16 = jnp.bfloat16

LN_EPS = 1e-5
N_HEADS = 4
CHUNK = 128
CONV_W = 3
TOKEN_TILE = 1024
SUB_TILE = 512
MLSTM_TILE = 512
SAMPLE_BATCH_BLOCK = 16
LANES = 128
GATE_PAD = LANES
VMEM_LIMIT = 56 * 1024 * 1024


def _dot(a, b):
    return jnp.dot(a, b, preferred_element_type=F32)


def _dot_nt(a, b):
    return lax.dot_general(a, b, (((1,), (1,)), ((), ())), preferred_element_type=F32)


def _dot_tn(a, b):
    return lax.dot_general(a, b, (((0,), (0,)), ((), ())), preferred_element_type=F32)


def _sigmoid(x):
    return 1.0 / (1.0 + jnp.exp(-x))


def _log_sigmoid(x):
    return jnp.minimum(x, 0.0) - jnp.log1p(jnp.exp(-jnp.abs(x)))


def _split_bf16(x):
    hi = x.astype(BF16)
    lo = (x - hi.astype(F32)).astype(BF16)
    return hi, lo


def _layer_norm(r, g, b):
    mu = jnp.mean(r, axis=-1, keepdims=True)
    xc = r - mu
    var = jnp.mean(xc * xc, axis=-1, keepdims=True)
    return xc * lax.rsqrt(var + LN_EPS) * g + b


def _head_norm(h):
    mu = jnp.mean(h, axis=-1, keepdims=True)
    hc = h - mu
    return hc * lax.rsqrt(jnp.mean(hc * hc, axis=-1, keepdims=True) + LN_EPS)


def _const_spec(shape):
    zeros = (0,) * len(shape)
    return pl.BlockSpec(shape, lambda *_: zeros, pipeline_mode=pl.Buffered(1))


def _window_spec(block_shape, index):
    return pl.BlockSpec(block_shape, lambda *_: index, pipeline_mode=pl.Buffered(1))


def _bch_spec(d):
    return _const_spec((d, 3 * d))


def _qkvo_spec(d):
    return _const_spec((d, 4 * d))


def _gate_spec(d, which):
    return _window_spec((d, d), (0, which))


def _params(n_axes):
    return pltpu.CompilerParams(dimension_semantics=("arbitrary",) * n_axes,
                                vmem_limit_bytes=VMEM_LIMIT)


def _conv_stripes(d):
    sw = 512 if d % 512 == 0 else d
    return [slice(s * sw, (s + 1) * sw) for s in range(d // sw)]


def _conv_merge_prompt_kernel(x_ref, hm_ref, wbch_ref, cw_ref, wco_ref, wgc_ref, wmo_ref, wgm_ref,
                              wo_ref, g_ref, b_ref, o_ref, cs_ref, u_s, a_s, mg_s, *, alpha):
    tm, d = a_s.shape
    @pl.when(pl.program_id(1) == 0)
    def _():
        u_s[0:8, :] = jnp.zeros((8, d), F32)

    for r0 in range(0, tm, SUB_TILE):
        sub = min(SUB_TILE, tm - r0)
        x = x_ref[0, r0:r0 + sub, :]
        rows = slice(r0, r0 + sub)
        xb = x.astype(BF16)
        for cs in _conv_stripes(d):
            off = cs.start
            cg = _dot(xb, wbch_ref[:, d + off:d + cs.stop])
            hc = _dot(xb, wbch_ref[:, 2 * d + off:2 * d + cs.stop])
            u = cg * hc
            u_s[8 + r0:8 + r0 + sub, cs] = u
            conv = (u_s[6 + r0:6 + r0 + sub, cs] * cw_ref[0:1, cs]
                    + u_s[7 + r0:7 + r0 + sub, cs] * cw_ref[1:2, cs] + u * cw_ref[2:3, cs])
            bg = _dot(xb, wbch_ref[:, off:cs.stop])
            a_s[rows, cs] = (bg * conv).astype(BF16)
        hm = hm_ref[rows, :]
        for cs in _conv_stripes(d):
            gated_m = _sigmoid(_dot(xb, wgm_ref[:, cs])) * _dot(hm, wmo_ref[:, cs])
            gated_c = _sigmoid(_dot(xb, wgc_ref[:, cs])) * _dot(a_s[rows, :], wco_ref[:, cs])
            mg_s[rows, cs] = (gated_c + gated_m).astype(BF16)
        r = alpha * x + _dot(mg_s[rows, :], wo_ref[...])
        o_ref[rows, :] = _layer_norm(r, g_ref[...], b_ref[...])
    cs_ref[0] = u_s[tm + 6:tm + 8, :]
    u_s[0:8, :] = u_s[tm:tm + 8, :]


def _time_major(x_ref):
    return jnp.concatenate([x_ref[:, t, :] for t in range(x_ref.shape[1])], axis=0)


def _conv_sample_kernel(x_ref, st_ref, wbch_ref, cw_ref, wco_ref, wgc_ref, m1_ref, cs_ref, a_s):
    nb, n_steps, d = x_ref.shape
    xb = _time_major(x_ref).astype(BF16)
    for cs in _conv_stripes(d):
        off = cs.start
        bg = _dot(xb, wbch_ref[:, off:cs.stop])
        cg = _dot(xb, wbch_ref[:, d + off:d + cs.stop])
        hc = _dot(xb, wbch_ref[:, 2 * d + off:2 * d + cs.stop])
        u = cg * hc
        up = [st_ref[:, j, cs] for j in range(CONV_W - 1)]
        up += [u[t * nb:(t + 1) * nb] for t in range(n_steps)]
        for t in range(n_steps):
            conv = (up[t] * cw_ref[0:1, cs] + up[t + 1] * cw_ref[1:2, cs]
                    + up[t + 2] * cw_ref[2:3, cs])
            a_s[t * nb:(t + 1) * nb, cs] = (bg[t * nb:(t + 1) * nb] * conv).astype(BF16)
        for j in range(CONV_W - 1):
            cs_ref[:, j, cs] = up[n_steps + j]
    yc = _dot(a_s[...], wco_ref[...])
    m1_ref[...] = _sigmoid(_dot(xb, wgc_ref[...])) * yc


def _conv_merge_prompt(x, hm, w_a, cw, wco, w_b, wmo, wo, g, b, alpha):
    bsz, t, d = x.shape
    tm = TOKEN_TILE
    flat = pl.BlockSpec((tm, d), lambda bi, j: (bi * (t // tm) + j, 0))
    return pl.pallas_call(
        functools.partial(_conv_merge_prompt_kernel, alpha=alpha),
        grid=(bsz, t // tm),
        in_specs=[pl.BlockSpec((1, tm, d), lambda bi, j: (bi, j, 0)), flat,
                  _bch_spec(d), _const_spec(cw.shape), _const_spec(wco.shape),
                  _gate_spec(d, 0), _const_spec(wmo.shape), _gate_spec(d, 1),
                  _const_spec(wo.shape), _const_spec(g.shape), _const_spec(b.shape)],
        out_specs=[flat, pl.BlockSpec((1, CONV_W - 1, d), lambda bi, j: (bi, 0, 0))],
        out_shape=[jax.ShapeDtypeStruct((bsz * t, d), F32),
                   jax.ShapeDtypeStruct((bsz, CONV_W - 1, d), F32)],
        scratch_shapes=[pltpu.VMEM((tm + 8, d), F32), pltpu.VMEM((tm, d), BF16),
                        pltpu.VMEM((tm, d), BF16)],
        compiler_params=_params(2),
        name="conv_merge_prompt",
    )(x, hm, w_a, cw, wco, w_b, wmo, w_b, wo, g, b)


def _conv_branch_sample(xs, conv_buf, wbch, cw, wco, wgc):
    nb, n_steps, d = xs.shape
    rows = nb * n_steps
    return pl.pallas_call(
        _conv_sample_kernel,
        grid=(1,),
        in_specs=[_const_spec(xs.shape), _const_spec(conv_buf.shape), _bch_spec(d),
                  _const_spec(cw.shape), _const_spec(wco.shape), _gate_spec(d, 0)],
        out_specs=[pl.BlockSpec((rows, d), lambda i: (0, 0)),
                   pl.BlockSpec(conv_buf.shape, lambda i: (0, 0, 0))],
        out_shape=[jax.ShapeDtypeStruct((rows, d), F32),
                   jax.ShapeDtypeStruct(conv_buf.shape, F32)],
        scratch_shapes=[pltpu.VMEM((rows, d), BF16)],
        compiler_params=_params(1),
        name="conv_branch_sample",
    )(xs, conv_buf, wbch, cw, wco, wgc)


def _rep(col, times):
    return col if times == 1 else jnp.concatenate([col] * times, axis=1)


def _mlstm_prompt_step(x_ref, wqkvo_ref, wgt_ref, bgc_ref, mhg_ref, hm_ref,
                       q_s, k_s, vx_s, og_s, e_s, colb_s, colm_s, cole_s, cst_s, mst_s,
                       *, cur, prev):
    _, tm, d = q_s.shape
    dh = d // N_HEADS
    L = CHUNK
    LANES = colb_s.shape[-1]
    n_chunks = tm // L
    wide = (dh + LANES) // LANES

    xh = x_ref[...].astype(BF16)

    h2 = 2 * N_HEADS
    ga = _dot_nt(wgt_ref[...], xh)
    gt = ga[0:h2] + ga[h2:2 * h2] + bgc_ref[...]
    is_input_gate = lax.broadcasted_iota(jnp.int32, gt.shape, 0) < N_HEADS
    g = jnp.where(is_input_gate, gt, _log_sigmoid(gt))

    pos = lax.broadcasted_iota(jnp.int32, g.shape, 1) % L
    csum = g
    shift = 1
    while shift < L:
        csum = csum + jnp.where(pos >= shift, pltpu.roll(csum, shift, 1), 0.0)
        shift *= 2
    ba = jnp.concatenate([csum[N_HEADS:], g[:N_HEADS] - csum[N_HEADS:]], axis=0)

    ri = lax.broadcasted_iota(jnp.int32, (L, L), 0)
    ci = lax.broadcasted_iota(jnp.int32, (L, L), 1)
    causal = ri >= ci

    for c in range(n_chunks):
        slab = ba[:, c * L:(c + 1) * L]
        cols = slab.T
        for h in range(N_HEADS):
            b_rep = jnp.broadcast_to(cols[:, h:h + 1], (L, LANES))
            a_rep = jnp.broadcast_to(cols[:, N_HEADS + h:N_HEADS + h + 1], (L, LANES))
            a_row = slab[N_HEADS + h:N_HEADS + h + 1, :]
            dm = jnp.where(causal, _rep(b_rep, L // LANES) + a_row, -jnp.inf)
            m_loc = jnp.broadcast_to(jnp.max(dm, axis=1, keepdims=True), (L, LANES))
            e_s[cur, c, h] = jnp.exp(dm - _rep(m_loc, L // LANES))
            colb_s[cur, c, h] = b_rep
            colm_s[cur, c, h] = m_loc
            cole_s[cur, c, h] = jnp.exp(a_rep + b_rep[L - 1:L, :] - m_loc[L - 1:L, :])

    n_split = 2 if N_HEADS % 2 == 0 else 1
    piece = d // n_split

    def project(part, split):
        lo = split * piece
        cols = slice(lo, lo + piece)
        y = _dot(xh, wqkvo_ref[:, part * d + lo:part * d + lo + piece])
        if part == 0:
            q_s[cur, :, cols] = y.astype(BF16)
        elif part == 1:
            k_s[cur, :, cols] = (y * (dh ** -0.5)).astype(BF16)
        elif part == 2:
            v = y.astype(BF16)
            for h in range(lo // dh, (lo + piece) // dh):
                vx_s[cur, h, :, 0:dh] = v[:, h * dh - lo:(h + 1) * dh - lo]
                vx_s[cur, h, :, dh:] = jnp.ones((tm, LANES), BF16)
        else:
            og_s[cur, :, cols] = mhg_ref[:, cols] * _sigmoid(y)

    heads = [slice(h * dh, (h + 1) * dh) for h in range(N_HEADS)]

    def recur_ready(c):
        rows = slice(c * L, (c + 1) * L)
        qk = [_dot_nt(q_s[prev, rows, hc], k_s[prev, rows, hc]) for hc in heads]
        states = [cst_s[h] for h in range(N_HEADS)]
        qc = [_dot(q_s[prev, rows, hc], states[h].astype(BF16)) for h, hc in enumerate(heads)]
        ux = []
        for h, hc in enumerate(heads):
            kw = k_s[prev, rows, hc].astype(F32) * _rep(cole_s[prev, c, h], dh // LANES)
            ux.append(_dot_tn(kw.astype(BF16), vx_s[prev, h, rows, :]))
        return qk, states, qc, ux

    def recur_finish(c, qk, states, qc, ux):
        rows = slice(c * L, (c + 1) * L)
        svx = []
        for h in range(N_HEADS):
            s_loc = qk[h] * e_s[prev, c, h]
            svx.append(_dot(s_loc.astype(BF16), vx_s[prev, h, rows, :]))
        for h, hc in enumerate(heads):
            m_prev = mst_s[h]
            m_loc = colm_s[prev, c, h]
            inter = colb_s[prev, c, h] + m_prev
            m_t = jnp.maximum(inter, m_loc)
            w_inter = _rep(jnp.exp(inter - m_t), wide)
            w_loc = _rep(jnp.exp(m_loc - m_t), wide)
            nd = w_inter * qc[h] + w_loc * svx[h]
            inv = 1.0 / jnp.maximum(jnp.abs(nd[:, dh:]), jnp.exp(-m_t))
            hh = nd[:, 0:dh] * _rep(inv, dh // LANES)
            hm_ref[rows, hc] = (_head_norm(hh) * og_s[prev, rows, hc]).astype(BF16)
            m_new = m_t[L - 1:L, :]
            decay = _rep(jnp.exp(inter[L - 1:L, :] - m_new), wide)
            grow = _rep(jnp.exp(m_loc[L - 1:L, :] - m_new), wide)
            cst_s[h] = decay * states[h] + grow * ux[h]
            mst_s[h] = m_new

    pieces = [(part, split) for part in range(4) for split in range(n_split)]
    n_phases = 2 * n_chunks
    done = 0
    for c in range(n_chunks):
        for phase in (2 * c, 2 * c + 1):
            if phase % 2 == 0:
                partial = recur_ready(c)
            else:
                recur_finish(c, *partial)
            upto = (phase + 1) * len(pieces) // n_phases
            for part, split in pieces[done:upto]:
                project(part, split)
            done = upto


def _mlstm_prompt_kernel(*refs, tiles_per_seq, n_cast, n_xpose):
    n_side = n_cast + n_xpose
    x_ref, wqkvo_ref, wgt_ref, bgc_ref, mhg_ref = refs[:5]
    side_src = refs[5:5 + n_side]
    hm_ref, c_ref, n_ref, m_ref = refs[5 + n_side:9 + n_side]
    side_dst = refs[9 + n_side:9 + 2 * n_side]
    q_s, k_s, vx_s, og_s, e_s, colb_s, colm_s, cole_s, cst_s, mst_s = refs[9 + 2 * n_side:]
    dh = cst_s.shape[1]
    g_step = pl.program_id(0)

    def side_jobs():
        for i, (src, dst) in enumerate(zip(side_src, side_dst)):
            dst[...] = (src[...] if i < n_cast else src[...].T).astype(BF16)

    @pl.when(g_step == 0)
    def _():
        for ref in (q_s, k_s, vx_s, og_s, e_s, colb_s, colm_s, cole_s):
            ref[1] = jnp.zeros(ref.shape[1:], ref.dtype)

    @pl.when(jnp.logical_or(g_step == 0, (g_step - 1) % tiles_per_seq == 0))
    def _():
        cst_s[...] = jnp.zeros(cst_s.shape, F32)
        mst_s[...] = jnp.zeros(mst_s.shape, F32)

    for parity in (0, 1):
        @pl.when(g_step % 2 == parity)
        def _():
            side_jobs()
            _mlstm_prompt_step(x_ref, wqkvo_ref, wgt_ref, bgc_ref, mhg_ref, hm_ref,
                               q_s, k_s, vx_s, og_s, e_s, colb_s, colm_s, cole_s, cst_s, mst_s,
                               cur=parity, prev=1 - parity)

    @pl.when(jnp.logical_and(g_step >= 1, (g_step - 1) % tiles_per_seq == tiles_per_seq - 1))
    def _():
        for h in range(N_HEADS):
            state = cst_s[h]
            c_ref[0, h] = state[:, 0:dh]
            n_ref[0, h] = state[:, dh:].T[0:1, :]
            m_ref[0, h] = mst_s[h][:, 0:1]


XPOSE_ROWS = 256


def _mlstm_prompt(x, wqkvo, wgt, bgc, mhg, f32_weights, w_in_t):
    bsz, t, d = x.shape
    dh = d // N_HEADS
    tm = MLSTM_TILE
    nc = tm // CHUNK
    dx = dh + LANES
    tps = t // tm
    n_tiles = bsz * tps
    seq_block = lambda g: (jnp.maximum(g - 1, 0) // tps, 0, 0, 0)
    cast_specs = []
    for w in f32_weights:
        rows_per_step = w.shape[0] // n_tiles
        assert rows_per_step * n_tiles == w.shape[0] and rows_per_step % 16 == 0
        cast_specs.append(pl.BlockSpec((rows_per_step, w.shape[1]),
                                       lambda g: (jnp.minimum(g, n_tiles - 1), 0)))
    n_cast = len(f32_weights)
    r = XPOSE_ROWS
    n_bch, n_gate = 3 * d // r, 2 * d // r
    off_gate = 7 * d + 2 * N_HEADS
    assert n_bch + n_gate <= n_tiles + 1 and d % r == 0 and off_gate % 8 == 0
    bch_job = lambda g: jnp.minimum(g, n_bch - 1)
    gate_job = lambda g: jnp.clip(g - n_bch, 0, n_gate - 1)
    xpose_src = [pl.BlockSpec((r, d), lambda g: (bch_job(g), 0)),
                 pl.BlockSpec((pl.Element(r), pl.Element(d)),
                              lambda g: (pl.multiple_of(off_gate + gate_job(g) * r, 8), 0))]
    xpose_dst = [pl.BlockSpec((d, r), lambda g: (0, bch_job(g))),
                 pl.BlockSpec((d, r), lambda g: (0, gate_job(g)))]
    hm, c, n, m, *side = pl.pallas_call(
        functools.partial(_mlstm_prompt_kernel, tiles_per_seq=tps, n_cast=n_cast, n_xpose=2),
        grid=(n_tiles + 1,),
        in_specs=[pl.BlockSpec((tm, d), lambda g: (jnp.minimum(g, n_tiles - 1), 0)),
                  _qkvo_spec(d), _const_spec(wgt.shape), _const_spec(bgc.shape),
                  _const_spec(mhg.shape)] + cast_specs + xpose_src,
        out_specs=[pl.BlockSpec((tm, d), lambda g: (jnp.maximum(g - 1, 0), 0)),
                   pl.BlockSpec((1, N_HEADS, dh, dh), seq_block),
                   pl.BlockSpec((1, N_HEADS, 1, dh), seq_block),
                   pl.BlockSpec((1, N_HEADS, 1, 1), seq_block)] + cast_specs + xpose_dst,
        out_shape=[jax.ShapeDtypeStruct((bsz * t, d), BF16),
                   jax.ShapeDtypeStruct((bsz, N_HEADS, dh, dh), F32),
                   jax.ShapeDtypeStruct((bsz, N_HEADS, 1, dh), F32),
                   jax.ShapeDtypeStruct((bsz, N_HEADS, 1, 1), F32)]
                  + [jax.ShapeDtypeStruct(w.shape, BF16) for w in f32_weights]
                  + [jax.ShapeDtypeStruct((d, 3 * d), BF16),
                     jax.ShapeDtypeStruct((d, 2 * d), BF16)],
        scratch_shapes=[
            pltpu.VMEM((2, tm, d), BF16),
            pltpu.VMEM((2, tm, d), BF16),
            pltpu.VMEM((2, N_HEADS, tm, dx), BF16),
            pltpu.VMEM((2, tm, d), F32),
            pltpu.VMEM((2, nc, N_HEADS, CHUNK, CHUNK), F32),
            pltpu.VMEM((2, nc, N_HEADS, CHUNK, LANES), F32),
            pltpu.VMEM((2, nc, N_HEADS, CHUNK, LANES), F32),
            pltpu.VMEM((2, nc, N_HEADS, CHUNK, LANES), F32),
            pltpu.VMEM((N_HEADS, dh, dx), F32),
            pltpu.VMEM((N_HEADS, 1, LANES), F32)],
        compiler_params=_params(1),
        name="mlstm_prompt",
    )(x.reshape(bsz * t, d), wqkvo, wgt, bgc, mhg, *f32_weights, w_in_t, w_in_t)
    casted, (w_bch, w_gate) = side[:n_cast], side[n_cast:]
    return hm, c, n.reshape(bsz, N_HEADS, dh), m.reshape(bsz, N_HEADS), casted, w_bch, w_gate


def _mlstm_sample_pre_kernel(x_ref, wq_ref, wk_ref, wv_ref, wo_ref, wgc_ref, bgr_ref, mhg_ref,
                             n0_ref, m0_ref,
                             q_ref, kw_ref, v_ref, og_ref, dec_ref, wa_ref, wb_ref, n_ref, m_ref,
                             g_s, *, n_steps):
    rows, dh = q_ref.shape
    nb = rows // n_steps
    T = n_steps
    hd = pl.program_id(0)
    slab = [slice(t * nb, (t + 1) * nb) for t in range(T)]
    head_lane = lax.broadcasted_iota(jnp.int32, m_ref.shape, 1)
    gate_lane = lax.broadcasted_iota(jnp.int32, (nb, g_s.shape[1]), 1)

    x = _time_major(x_ref)
    xh = x.astype(BF16)

    @pl.when(hd == 0)
    def _():
        xl = (x - xh.astype(F32)).astype(BF16)
        g = _dot(xh, wgc_ref[0]) + _dot(xl, wgc_ref[0]) + _dot(xh, wgc_ref[1]) + bgr_ref[...]
        is_input_gate = lax.broadcasted_iota(jnp.int32, g.shape, 1) < N_HEADS
        g_s[...] = jnp.where(is_input_gate, g, _log_sigmoid(g))
        m_ref[...] = jnp.zeros(m_ref.shape, F32)

    def gate_column(t, col):
        return jnp.sum(jnp.where(gate_lane == col, g_s[slab[t], :], 0.0), axis=1, keepdims=True)

    q_all = _dot(xh, wq_ref[...]).astype(BF16).astype(F32)
    k_all = (_dot(xh, wk_ref[...]) * (dh ** -0.5)).astype(BF16).astype(F32)
    v_all = _dot(xh, wv_ref[...]).astype(BF16).astype(F32)
    og_ref[...] = mhg_ref[...] * _sigmoid(_dot(xh, wo_ref[...]))
    q_ref[...] = q_all
    v_ref[...] = v_all
    qf = [q_all[slab[t]] for t in range(T)]
    kf = [k_all[slab[t]] for t in range(T)]
    vf = [v_all[slab[t]] for t in range(T)]
    li = [gate_column(t, hd) for t in range(T)]
    lf = [gate_column(t, N_HEADS + hd) for t in range(T)]
    m0 = jnp.sum(jnp.where(head_lane == hd, m0_ref[...], 0.0), axis=1, keepdims=True)
    n0 = n0_ref[...]
    b = [lf[0]]
    for t in range(1, T):
        b.append(b[t - 1] + lf[t])
    a = [li[t] - b[t] for t in range(T)]
    m_new = None
    for t in range(T):
        inter = b[t] + m0
        m_t = inter
        for s in range(t + 1):
            m_t = jnp.maximum(m_t, b[t] + a[s])
        w_inter = jnp.exp(inter - m_t)
        num = jnp.zeros((nb, dh), F32)
        den = w_inter * jnp.sum(qf[t] * n0, axis=1, keepdims=True)
        for s in range(t + 1):
            s_w = (jnp.sum(qf[t] * kf[s], axis=1, keepdims=True)
                   * jnp.exp(b[t] + a[s] - m_t))
            num = num + s_w * vf[s]
            den = den + s_w
        inv = 1.0 / jnp.maximum(jnp.abs(den), jnp.exp(-m_t))
        wa_ref[slab[t], :] = jnp.broadcast_to(w_inter * inv, (nb, dh))
        wb_ref[slab[t], :] = num * inv
        m_new = m_t
    b_last = b[T - 1]
    decay = jnp.exp(b_last + m0 - m_new)
    n_new = decay * n0
    for s in range(T):
        kw = kf[s] * jnp.exp(a[s] + b_last - m_new)
        kw_ref[slab[s], :] = kw
        n_new = n_new + kw
    n_ref[...] = n_new
    m_ref[...] = jnp.where(head_lane == hd, m_new, m_ref[...])
    dec_ref[...] = jnp.broadcast_to(decay, (nb, dh))


def _mlstm_sample_pre(xs, wqkvo, wgc, bgr, mhg, n0, m0):
    nb, n_steps, d = xs.shape
    rows = nb * n_steps
    dh = d // N_HEADS
    head_w = lambda part: pl.BlockSpec((d, dh), lambda h: (0, part * N_HEADS + h))
    head_cols = lambda n_rows: pl.BlockSpec((n_rows, dh), lambda h: (0, h))
    big = jax.ShapeDtypeStruct((rows, d), F32)
    small = jax.ShapeDtypeStruct((nb, d), F32)
    return pl.pallas_call(
        functools.partial(_mlstm_sample_pre_kernel, n_steps=n_steps),
        grid=(N_HEADS,),
        in_specs=[_const_spec(xs.shape), head_w(0), head_w(1), head_w(2), head_w(3),
                  _const_spec(wgc.shape), _const_spec(bgr.shape), head_cols(1), head_cols(nb),
                  _const_spec(m0.shape)],
        out_specs=[head_cols(rows)] * 4 + [head_cols(nb)] + [head_cols(rows)] * 2
                  + [head_cols(nb), pl.BlockSpec((nb, N_HEADS), lambda h: (0, 0))],
        out_shape=[big, big, big, big, small, big, big, small,
                   jax.ShapeDtypeStruct((nb, N_HEADS), F32)],
        scratch_shapes=[pltpu.VMEM((rows, GATE_PAD), F32)],
        compiler_params=_params(1),
        name="mlstm_sample_pre",
    )(xs, wqkvo, wqkvo, wqkvo, wqkvo, wgc, bgr, mhg, n0, m0)


def _stream_matrix_memory(c0_ref, q_ref, kw_ref, v_ref, dec_ref, c_ref, qc_ref):
    T, bb, dh = q_ref.shape
    owner = lax.broadcasted_iota(jnp.int32, (T * bb, dh), 0) % bb
    q_blk = jnp.concatenate([q_ref[t] for t in range(T)], axis=0).astype(BF16)
    kw_blk = jnp.concatenate([kw_ref[t] for t in range(T)], axis=0).astype(BF16)
    v_blk = jnp.concatenate([v_ref[t] for t in range(T)], axis=0)
    c_olds = [c0_ref[bi, 0] for bi in range(bb)]
    reads = [_dot(q_blk, c_olds[bi].astype(BF16)) for bi in range(bb)]
    qc = jnp.zeros((T * bb, dh), F32)
    for bi in range(bb):
        qc = jnp.where(owner == bi, reads[bi], qc)
    for t in range(T):
        qc_ref[t] = qc[t * bb:(t + 1) * bb]
    for bi in range(bb):
        upd = _dot_tn(kw_blk, jnp.where(owner == bi, v_blk, 0.0).astype(BF16))
        c_ref[bi, 0] = dec_ref[bi:bi + 1, :] * c_olds[bi] + upd


def _merge_sample_kernel(x_ref, m1_ref, wa_ref, wb_ref, qc_ref, og_ref, wmo_ref, wgm_ref, wo_ref,
                         g_ref, b_ref, o_ref, hm_s, *, alpha):
    d = x_ref.shape[-1]
    dh = d // N_HEADS
    for hd in range(N_HEADS):
        hc = slice(hd * dh, (hd + 1) * dh)
        hh = wa_ref[:, hc] * qc_ref[:, hc] + wb_ref[:, hc]
        hm_s[:, hc] = (_head_norm(hh) * og_ref[:, hc]).astype(BF16)
    x = _time_major(x_ref)
    gate = _dot(x.astype(BF16), wgm_ref[...])
    ym = _dot(hm_s[...], wmo_ref[...])
    merged = m1_ref[...] + _sigmoid(gate) * ym
    r = alpha * x + _dot(merged.astype(BF16), wo_ref[...])
    o_ref[...] = _layer_norm(r, g_ref[...], b_ref[...])


def _merge_sample(xs, m1, wa, wb, qc, og, wmo, w_b, wo, g, b, alpha):
    n, d = m1.shape
    whole = _const_spec((n, d))
    return pl.pallas_call(
        functools.partial(_merge_sample_kernel, alpha=alpha),
        grid=(1,),
        in_specs=[_const_spec(xs.shape)] + [whole] * 5
                 + [_const_spec(wmo.shape), _gate_spec(d, 1), _const_spec(wo.shape),
                    _const_spec(g.shape), _const_spec(b.shape)],
        out_specs=pl.BlockSpec((n, d), lambda i: (0, 0)),
        out_shape=jax.ShapeDtypeStruct((n, d), F32),
        scratch_shapes=[pltpu.VMEM((n, d), BF16)],
        compiler_params=_params(1),
        name="merge_ln1_sample",
    )(xs, m1, wa, wb, qc, og, wmo, w_b, wo, g, b)


def _ffn_hidden(xb, w1_ref, hid_s, rows):
    dff = hid_s.shape[1]
    sw = 1024 if dff % 1024 == 0 else dff
    for s in range(dff // sw):
        cs = slice(s * sw, (s + 1) * sw)
        hid = jnp.maximum(_dot(xb, w1_ref[:, cs]), 0.0)
        hid_s[rows, cs] = (hid * hid).astype(BF16)


def _ffn_small_kernel(x_ref, w1_ref, w2_ref, g_ref, b_ref, o_ref, acc_s, *, alpha):
    j = pl.program_id(0)
    x = x_ref[...]
    hid = jnp.maximum(_dot(x.astype(BF16), w1_ref[...]), 0.0)
    part = _dot((hid * hid).astype(BF16), w2_ref[...])

    @pl.when(j == 0)
    def _():
        acc_s[...] = alpha * x + part

    @pl.when(j > 0)
    def _():
        acc_s[...] += part

    @pl.when(j == pl.num_programs(0) - 1)
    def _():
        y = _layer_norm(acc_s[...], g_ref[...], b_ref[...])
        nb, n_t, _ = o_ref.shape
        for t in range(n_t):
            o_ref[:, t, :] = y[t * nb:(t + 1) * nb]


def _ffn_stream_kernel(x_ref, w1_ref, w2_ref, g_ref, b_ref, c0_ref, q_ref, kw_ref, v_ref,
                       dec_ref, o_ref, c_ref, qc_ref, hid_s, r_s, *, alpha):
    s = pl.program_id(0)
    last = pl.num_programs(0) - 1

    @pl.when(s == 0)
    def _():
        r_s[...] = jnp.zeros(r_s.shape, F32)

    @pl.when(s < last)
    def _():
        _stream_matrix_memory(c0_ref, q_ref, kw_ref, v_ref, dec_ref, c_ref, qc_ref)
        x = x_ref[...]
        _ffn_hidden(x.astype(BF16), w1_ref, hid_s, slice(None))
        o_ref[...] = _layer_norm(r_s[...], g_ref[...], b_ref[...])
        r_s[...] = alpha * x + _dot(hid_s[...], w2_ref[...])

    @pl.when(s == last)
    def _():
        o_ref[...] = _layer_norm(r_s[...], g_ref[...], b_ref[...])


def _ffn_small(x, w1, w2, g, b, alpha, n_steps):
    n, d = x.shape
    dff = w1.shape[1]
    sw = 1024 if dff % 1024 == 0 else dff
    out_shape = (n // n_steps, n_steps, d)
    return pl.pallas_call(
        functools.partial(_ffn_small_kernel, alpha=alpha),
        grid=(dff // sw,),
        in_specs=[_const_spec((n, d)), pl.BlockSpec((d, sw), lambda j: (0, j)),
                  pl.BlockSpec((sw, d), lambda j: (j, 0)), _const_spec(g.shape),
                  _const_spec(b.shape)],
        out_specs=pl.BlockSpec(out_shape, lambda j: (0,) * len(out_shape)),
        out_shape=jax.ShapeDtypeStruct(out_shape, F32),
        scratch_shapes=[pltpu.VMEM((n, d), F32)],
        compiler_params=_params(1),
        name="ffn_ln2_small",
    )(x, w1, w2, g, b)


def _ffn_with_stream(x, w1, w2, g, b, alpha, c0, q, kw, v, dec, n_steps):
    n, d = x.shape
    rows = q.shape[0]
    nb = rows // n_steps
    dh = d // N_HEADS
    bb = SAMPLE_BATCH_BLOCK
    n_grid = (nb // bb) * N_HEADS
    tm = n // n_grid
    assert tm * n_grid == n and tm % 8 == 0 and nb % bb == 0
    cur = lambda s: jnp.minimum(s, n_grid - 1)
    x_tile = pl.BlockSpec((tm, d), lambda s: (cur(s), 0))
    o_tile = pl.BlockSpec((tm, d), lambda s: (jnp.maximum(s - 1, 0), 0))
    cblock = pl.BlockSpec((bb, 1, dh, dh),
                          lambda s: (cur(s) // N_HEADS, cur(s) % N_HEADS, 0, 0))
    tblock = pl.BlockSpec((n_steps, bb, dh), lambda s: (0, cur(s) // N_HEADS, cur(s) % N_HEADS))
    dblock = pl.BlockSpec((bb, dh), lambda s: (cur(s) // N_HEADS, cur(s) % N_HEADS))
    as_tbd = lambda a: a.reshape(n_steps, nb, d)
    x2, c_new, qc = pl.pallas_call(
        functools.partial(_ffn_stream_kernel, alpha=alpha),
        grid=(n_grid + 1,),
        in_specs=[x_tile, _const_spec(w1.shape), _const_spec(w2.shape), _const_spec(g.shape),
                  _const_spec(b.shape), cblock, tblock, tblock, tblock, dblock],
        out_specs=[o_tile, cblock, tblock],
        out_shape=[jax.ShapeDtypeStruct((n, d), F32), jax.ShapeDtypeStruct(c0.shape, F32),
                   jax.ShapeDtypeStruct((n_steps, nb, d), F32)],
        scratch_shapes=[pltpu.VMEM((tm, w1.shape[1]), BF16), pltpu.VMEM((tm, d), F32)],
        compiler_params=_params(1),
        name="ffn_ln2_stream",
    )(x, w1, w2, g, b, c0, as_tbd(q), as_tbd(kw), as_tbd(v), dec)
    return x2, c_new, qc.reshape(rows, d)


def _w_in_prep_kernel(a_ref, g_ref, wa_ref, wg_ref):
    @pl.when(pl.program_id(0) == 0)
    def _():
        wg_ref[...] = g_ref[...]

    wa_ref[...] = a_ref[...].T.astype(BF16)


def _prepare_w_in(w_in_t):
    d = w_in_t.shape[1]
    h2 = 2 * N_HEADS
    return pl.pallas_call(
        _w_in_prep_kernel,
        grid=(4,),
        in_specs=[pl.BlockSpec((d, d), lambda j: (3 + j, 0)),
                  pl.BlockSpec((pl.Element(h2), pl.Element(d)), lambda j: (7 * d, 0))],
        out_specs=[pl.BlockSpec((d, d), lambda j: (0, j)),
                   pl.BlockSpec((h2, d), lambda j: (0, 0))],
        out_shape=[jax.ShapeDtypeStruct((d, 4 * d), BF16),
                   jax.ShapeDtypeStruct((h2, d), F32)],
        compiler_params=_params(1),
        name="w_in_prep",
    )(w_in_t, w_in_t)


def _layer_weights(w_in, b_gate, conv_w, w_conv_out, mh_g, w_m_out, w_o, ln1_g, ln1_b,
                   w_ff1, w_ff2, ln2_g, ln2_b):
    d = w_in.shape[0]
    h2 = 2 * N_HEADS
    w_in_t = jnp.swapaxes(w_in, 0, 1)
    w_qkvo, wg_t = _prepare_w_in(w_in_t)
    wgt_hi, wgt_lo = _split_bf16(wg_t)
    wgt = jnp.concatenate([wgt_hi, wgt_lo], axis=0)
    pad = ((0, 0), (0, GATE_PAD - h2))
    wgc = jnp.stack([jnp.pad(wgt_hi.T, pad), jnp.pad(wgt_lo.T, pad)])
    return dict(
        w_in_t=w_in_t, w_qkvo=w_qkvo,
        wgt=wgt, wgc=wgc,
        bgc=b_gate.reshape(h2, 1).astype(F32),
        bgr=jnp.pad(b_gate.reshape(1, h2).astype(F32), pad),
        cw=conv_w.astype(F32),
        mhg=mh_g.reshape(1, d).astype(F32),
        f32_weights=[w_conv_out, w_m_out, w_o, w_ff1, w_ff2],
        ln1_g=ln1_g.reshape(1, d), ln1_b=ln1_b.reshape(1, d),
        ln2_g=ln2_g.reshape(1, d), ln2_b=ln2_b.reshape(1, d))


def _layer(x, xs, conv_buf, c0, n0, m0, p, alpha):
    bsz, t, d = x.shape
    n_steps = xs.shape[1]
    hm, c_p, n_p, m_p, (wco, wmo, wo, w1, w2), w_bch, w_gate = _mlstm_prompt(
        x, p["w_qkvo"], p["wgt"], p["bgc"], p["mhg"], p["f32_weights"], p["w_in_t"])
    x1, conv_p = _conv_merge_prompt(x, hm, w_bch, p["cw"], wco, w_gate, wmo, wo,
                                    p["ln1_g"], p["ln1_b"], alpha)
    m1_s, conv_s = _conv_branch_sample(xs, conv_buf, w_bch, p["cw"], wco, w_gate)
    q, kw, v, og, dec, wa, wb, n_s, m_s = _mlstm_sample_pre(
        xs, p["w_qkvo"], p["wgc"], p["bgr"], p["mhg"], n0.reshape(n0.shape[0], d), m0)
    x2, c_s, qc = _ffn_with_stream(x1, w1, w2, p["ln2_g"], p["ln2_b"], alpha,
                                   c0, q, kw, v, dec, n_steps)
    x1_s = _merge_sample(xs, m1_s, wa, wb, qc, og, wmo, w_gate, wo, p["ln1_g"], p["ln1_b"],
                         alpha)
    x2_s = _ffn_small(x1_s, w1, w2, p["ln2_g"], p["ln2_b"], alpha, n_steps)
    return (x2.reshape(bsz, t, d), conv_p, c_p, n_p, m_p,
            x2_s, conv_s, c_s, n_s.reshape(n0.shape), m_s)


def kernel(x_prompt, x_sample, state_conv, state_C, state_n, state_m, w_in, b_gate, conv_w,
           w_conv_out, mh_g, w_m_out, w_o, ln1_g, ln1_b, w_ff1, w_ff2, ln2_g, ln2_b):
    depth = w_in.shape[0]
    alpha = (2.0 * depth) ** 0.25
    bsz, t, d = x_prompt.shape
    sb, st, _ = x_sample.shape
    assert t % TOKEN_TILE == 0 and t % MLSTM_TILE == 0 and MLSTM_TILE % CHUNK == 0
    assert d % N_HEADS == 0
    assert sb % SAMPLE_BATCH_BLOCK == 0 and st >= CONV_W - 1

    xp, xs = x_prompt, x_sample
    outs = [[] for _ in range(8)]
    for l in range(depth):
        p = _layer_weights(w_in[l], b_gate[l], conv_w[l], w_conv_out[l], mh_g[l], w_m_out[l],
                           w_o[l], ln1_g[l], ln1_b[l], w_ff1[l], w_ff2[l], ln2_g[l], ln2_b[l])
        xp, cp, c_p, n_p, m_p, xs, cs, c_s, n_s, m_s = _layer(
            xp, xs, state_conv[l], state_C[l], state_n[l], state_m[l], p, alpha)
        for acc, val in zip(outs, (cp, cs, c_p, c_s, n_p, n_s, m_p, m_s)):
            acc.append(val)
    return (xp, xs) + tuple(jnp.stack(acc) for acc in outs)
```

```python
import functools

import jax
import jax.numpy as jnp
from jax import lax
from jax.experimental import pallas as pl
from jax.experimental.pallas import tpu as pltpu

F32 = jnp.float32
BF16 = jnp.bfloat16

LN_EPS = 1e-5
N_HEADS = 4
CHUNK = 128
CONV_W = 3
TOKEN_TILE = 1024
SUB_TILE = 512
MLSTM_TILE = 512
SAMPLE_BATCH_BLOCK = 16
LANES = 128
GATE_PAD = LANES
VMEM_LIMIT = 56 * 1024 * 1024


def _dot(a, b):
    return jnp.dot(a, b, preferred_element_type=F32)


def _dot_nt(a, b):
    return lax.dot_general(a, b, (((1,), (1,)), ((), ())), preferred_element_type=F32)


def _dot_tn(a, b):
    return lax.dot_general(a, b, (((0,), (0,)), ((), ())), preferred_element_type=F32)


def _sigmoid(x):
    return 1.0 / (1.0 + jnp.exp(-x))


def _log_sigmoid(x):
    return jnp.minimum(x, 0.0) - jnp.log1p(jnp.exp(-jnp.abs(x)))


def _split_bf16(x):
    hi = x.astype(BF16)
    lo = (x - hi.astype(F32)).astype(BF16)
    return hi, lo


def _layer_norm(r, g, b):
    mu = jnp.mean(r, axis=-1, keepdims=True)
    xc = r - mu
    var = jnp.mean(xc * xc, axis=-1, keepdims=True)
    return xc * lax.rsqrt(var + LN_EPS) * g + b


def _head_norm(h):
    mu = jnp.mean(h, axis=-1, keepdims=True)
    hc = h - mu
    return hc * lax.rsqrt(jnp.mean(hc * hc, axis=-1, keepdims=True) + LN_EPS)


def _const_spec(shape):
    zeros = (0,) * len(shape)
    return pl.BlockSpec(shape, lambda *_: zeros, pipeline_mode=pl.Buffered(1))


def _window_spec(block_shape, index):
    return pl.BlockSpec(block_shape, lambda *_: index, pipeline_mode=pl.Buffered(1))


def _bch_spec(d):
    return _const_spec((d, 3 * d))


def _qkvo_spec(d):
    return _const_spec((d, 4 * d))


def _gate_spec(d, which):
    return _window_spec((d, d), (0, which))


def _params(n_axes):
    return pltpu.CompilerParams(dimension_semantics=("arbitrary",) * n_axes,
                                vmem_limit_bytes=VMEM_LIMIT)


def _conv_stripes(d):
    sw = 256 if d % 256 == 0 else d
    return [slice(s * sw, (s + 1) * sw) for s in range(d // sw)]


def _conv_merge_prompt_kernel(x_ref, hm_ref, wbch_ref, cw_ref, wco_ref, wgc_ref, wmo_ref, wgm_ref,
                              wo_ref, g_ref, b_ref, w1_ref, w2_ref, o_ref, cs_ref, w1b_ref,
                              w2b_ref, u_s, a_s, mg_s, *, alpha):
    tm, d = a_s.shape
    @pl.when(pl.program_id(1) == 0)
    def _():
        u_s[0:8, :] = jnp.zeros((8, d), F32)

    w1b_ref[...] = w1_ref[...].astype(BF16)
    w2b_ref[...] = w2_ref[...].astype(BF16)

    for r0 in range(0, tm, SUB_TILE):
        sub = min(SUB_TILE, tm - r0)
        x = x_ref[0, r0:r0 + sub, :]
        rows = slice(r0, r0 + sub)
        xb = x.astype(BF16)
        for cs in _conv_stripes(d):
            off = cs.start
            cg = _dot(xb, wbch_ref[:, d + off:d + cs.stop])
            hc = _dot(xb, wbch_ref[:, 2 * d + off:2 * d + cs.stop])
            u = cg * hc
            u_s[8 + r0:8 + r0 + sub, cs] = u
            conv = (u_s[6 + r0:6 + r0 + sub, cs] * cw_ref[0:1, cs]
                    + u_s[7 + r0:7 + r0 + sub, cs] * cw_ref[1:2, cs] + u * cw_ref[2:3, cs])
            bg = _dot(xb, wbch_ref[:, off:cs.stop])
            a_s[rows, cs] = (bg * conv).astype(BF16)
        hm = hm_ref[rows, :]
        for cs in _conv_stripes(d):
            gated_m = _sigmoid(_dot(xb, wgm_ref[:, cs])) * _dot(hm, wmo_ref[:, cs])
            gated_c = _sigmoid(_dot(xb, wgc_ref[:, cs])) * _dot(a_s[rows, :], wco_ref[:, cs])
            mg_s[rows, cs] = (gated_c + gated_m).astype(BF16)
        r = alpha * x + _dot(mg_s[rows, :], wo_ref[...])
        o_ref[rows, :] = _layer_norm(r, g_ref[...], b_ref[...])
    cs_ref[0] = u_s[tm + 6:tm + 8, :]
    u_s[0:8, :] = u_s[tm:tm + 8, :]


def _time_major(x_ref):
    return jnp.concatenate([x_ref[:, t, :] for t in range(x_ref.shape[1])], axis=0)


def _conv_sample_kernel(x_ref, st_ref, wbch_ref, cw_ref, wco_ref, wgc_ref, m1_ref, cs_ref, a_s):
    nb, n_steps, d = x_ref.shape
    xb = _time_major(x_ref).astype(BF16)
    for cs in _conv_stripes(d):
        off = cs.start
        bg = _dot(xb, wbch_ref[:, off:cs.stop])
        cg = _dot(xb, wbch_ref[:, d + off:d + cs.stop])
        hc = _dot(xb, wbch_ref[:, 2 * d + off:2 * d + cs.stop])
        u = cg * hc
        up = [st_ref[:, j, cs] for j in range(CONV_W - 1)]
        up += [u[t * nb:(t + 1) * nb] for t in range(n_steps)]
        for t in range(n_steps):
            conv = (up[t] * cw_ref[0:1, cs] + up[t + 1] * cw_ref[1:2, cs]
                    + up[t + 2] * cw_ref[2:3, cs])
            a_s[t * nb:(t + 1) * nb, cs] = (bg[t * nb:(t + 1) * nb] * conv).astype(BF16)
        for j in range(CONV_W - 1):
            cs_ref[:, j, cs] = up[n_steps + j]
    yc = _dot(a_s[...], wco_ref[...])
    m1_ref[...] = _sigmoid(_dot(xb, wgc_ref[...])) * yc


def _conv_merge_prompt(x, hm, w_a, cw, wco, w_b, wmo, wo, g, b, alpha, w1, w2):
    bsz, t, d = x.shape
    tm = TOKEN_TILE
    tps = t // tm
    n_tiles = bsz * tps
    flat = pl.BlockSpec((tm, d), lambda bi, j: (bi * tps + j, 0))
    cast_specs = []
    for w in (w1, w2):
        rows_per_step = w.shape[0] // n_tiles
        assert rows_per_step * n_tiles == w.shape[0] and rows_per_step % 16 == 0
        cast_specs.append(pl.BlockSpec((rows_per_step, w.shape[1]),
                                       lambda bi, j: (bi * tps + j, 0)))
    return pl.pallas_call(
        functools.partial(_conv_merge_prompt_kernel, alpha=alpha),
        grid=(bsz, tps),
        in_specs=[pl.BlockSpec((1, tm, d), lambda bi, j: (bi, j, 0)), flat,
                  _bch_spec(d), _const_spec(cw.shape), _const_spec(wco.shape),
                  _gate_spec(d, 0), _const_spec(wmo.shape), _gate_spec(d, 1),
                  _const_spec(wo.shape), _const_spec(g.shape), _const_spec(b.shape)]
                 + cast_specs,
        out_specs=[flat, pl.BlockSpec((1, CONV_W - 1, d), lambda bi, j: (bi, 0, 0))]
                  + cast_specs,
        out_shape=[jax.ShapeDtypeStruct((bsz * t, d), F32),
                   jax.ShapeDtypeStruct((bsz, CONV_W - 1, d), F32),
                   jax.ShapeDtypeStruct(w1.shape, BF16), jax.ShapeDtypeStruct(w2.shape, BF16)],
        scratch_shapes=[pltpu.VMEM((tm + 8, d), F32), pltpu.VMEM((tm, d), BF16),
                        pltpu.VMEM((tm, d), BF16)],
        compiler_params=_params(2),
        name="conv_merge_prompt",
    )(x, hm, w_a, cw, wco, w_b, wmo, w_b, wo, g, b, w1, w2)


def _conv_branch_sample(xs, conv_buf, wbch, cw, wco, wgc):
    nb, n_steps, d = xs.shape
    rows = nb * n_steps
    return pl.pallas_call(
        _conv_sample_kernel,
        grid=(1,),
        in_specs=[_const_spec(xs.shape), _const_spec(conv_buf.shape), _bch_spec(d),
                  _const_spec(cw.shape), _const_spec(wco.shape), _gate_spec(d, 0)],
        out_specs=[pl.BlockSpec((rows, d), lambda i: (0, 0)),
                   pl.BlockSpec(conv_buf.shape, lambda i: (0, 0, 0))],
        out_shape=[jax.ShapeDtypeStruct((rows, d), F32),
                   jax.ShapeDtypeStruct(conv_buf.shape, F32)],
        scratch_shapes=[pltpu.VMEM((rows, d), BF16)],
        compiler_params=_params(1),
        name="conv_branch_sample",
    )(xs, conv_buf, wbch, cw, wco, wgc)


def _rep(col, times):
    return col if times == 1 else jnp.concatenate([col] * times, axis=1)


def _mlstm_prompt_step(x_ref, wqkvo_ref, wgt_ref, bgc_ref, mhg_ref, hm_ref,
                       q_s, k_s, v_s, og_s, e_s, colb_s, colm_s, cole_s, cst_s, nst_s, mst_s,
                       *, cur, prev):
    _, _, tm, dh = q_s.shape
    d = N_HEADS * dh
    L = CHUNK
    LANES = colb_s.shape[-1]
    n_chunks = tm // L
    wide = dh // LANES

    xh = x_ref[...].astype(BF16)

    h2 = 2 * N_HEADS
    ga = _dot_nt(wgt_ref[...], xh)
    gt = ga[0:h2] + ga[h2:2 * h2] + bgc_ref[...]
    is_input_gate = lax.broadcasted_iota(jnp.int32, gt.shape, 0) < N_HEADS
    g = jnp.where(is_input_gate, gt, _log_sigmoid(gt))

    pos = lax.broadcasted_iota(jnp.int32, g.shape, 1) % L
    csum = g
    shift = 1
    while shift < L:
        csum = csum + jnp.where(pos >= shift, pltpu.roll(csum, shift, 1), 0.0)
        shift *= 2
    ba = jnp.concatenate([csum[N_HEADS:], g[:N_HEADS] - csum[N_HEADS:]], axis=0)

    ri = lax.broadcasted_iota(jnp.int32, (L, L), 0)
    ci = lax.broadcasted_iota(jnp.int32, (L, L), 1)
    causal = ri >= ci

    for c in range(n_chunks):
        slab = ba[:, c * L:(c + 1) * L]
        cols = slab.T
        for h in range(N_HEADS):
            b_rep = jnp.broadcast_to(cols[:, h:h + 1], (L, LANES))
            a_rep = jnp.broadcast_to(cols[:, N_HEADS + h:N_HEADS + h + 1], (L, LANES))
            a_row = slab[N_HEADS + h:N_HEADS + h + 1, :]
            dm = jnp.where(causal, _rep(b_rep, L // LANES) + a_row, -jnp.inf)
            m_loc = jnp.broadcast_to(jnp.max(dm, axis=1, keepdims=True), (L, LANES))
            e_s[cur, c, h] = jnp.exp(dm - _rep(m_loc, L // LANES))
            colb_s[cur, c, h] = b_rep
            colm_s[cur, c, h] = m_loc
            cole_s[cur, c, h] = jnp.exp(a_rep + b_rep[L - 1:L, :] - m_loc[L - 1:L, :])

    n_split = 2 if N_HEADS % 2 == 0 else 1
    piece = d // n_split

    def project(part, split):
        lo = split * piece
        y = _dot(xh, wqkvo_ref[:, part * d + lo:part * d + lo + piece])
        if part == 1:
            y = y * (dh ** -0.5)
        elif part == 3:
            y = mhg_ref[:, lo:lo + piece] * _sigmoid(y)
        dst = (q_s, k_s, v_s, og_s)[part]
        for h in range(lo // dh, (lo + piece) // dh):
            dst[cur, h] = y[:, h * dh - lo:(h + 1) * dh - lo].astype(dst.dtype)

    heads = [slice(h * dh, (h + 1) * dh) for h in range(N_HEADS)]

    def lane_rep(col):
        return jnp.broadcast_to(col, (col.shape[0], LANES))

    def recur_ready(c):
        rows = slice(c * L, (c + 1) * L)
        qk = [_dot_nt(q_s[prev, h, rows, :], k_s[prev, h, rows, :]) for h in range(N_HEADS)]
        states = [cst_s[h] for h in range(N_HEADS)]
        qc = [_dot(q_s[prev, h, rows, :], states[h].astype(BF16)) for h, hc in enumerate(heads)]
        ux, ks, qn = [], [], []
        for h, hc in enumerate(heads):
            kw = k_s[prev, h, rows, :].astype(F32) * _rep(cole_s[prev, c, h], wide)
            ux.append(_dot_tn(kw.astype(BF16), v_s[prev, h, rows, :]))
            ks.append(jnp.sum(kw, axis=0, keepdims=True))
            qn.append(lane_rep(jnp.sum(q_s[prev, h, rows, :].astype(F32) * nst_s[h],
                                       axis=1, keepdims=True)))
        return qk, states, qc, ux, ks, qn

    def recur_finish(c, qk, states, qc, ux, ks, qn):
        rows = slice(c * L, (c + 1) * L)
        sv, rs = [], []
        for h, hc in enumerate(heads):
            s_loc = qk[h] * e_s[prev, c, h]
            sv.append(_dot(s_loc.astype(BF16), v_s[prev, h, rows, :]))
            rs.append(lane_rep(jnp.sum(s_loc, axis=1, keepdims=True)))
        for h, hc in enumerate(heads):
            m_prev = mst_s[h]
            m_loc = colm_s[prev, c, h]
            inter = colb_s[prev, c, h] + m_prev
            m_t = jnp.maximum(inter, m_loc)
            w_inter = jnp.exp(inter - m_t)
            w_loc = jnp.exp(m_loc - m_t)
            den = w_inter * qn[h] + w_loc * rs[h]
            inv = 1.0 / jnp.maximum(jnp.abs(den), jnp.exp(-m_t))
            hh = (_rep(w_inter * inv, wide) * qc[h] + _rep(w_loc * inv, wide) * sv[h])
            hm_ref[rows, hc] = (_head_norm(hh) * og_s[prev, h, rows, :]).astype(BF16)
            m_new = m_t[L - 1:L, :]
            decay = _rep(jnp.exp(inter[L - 1:L, :] - m_new), wide)
            grow = _rep(jnp.exp(m_loc[L - 1:L, :] - m_new), wide)
            cst_s[h] = decay * states[h] + grow * ux[h]
            nst_s[h] = decay * nst_s[h] + grow * ks[h]
            mst_s[h] = m_new

    pieces = [(part, split) for part in range(4) for split in range(n_split)]
    n_phases = 2 * n_chunks
    done = 0
    for c in range(n_chunks):
        for phase in (2 * c, 2 * c + 1):
            if phase % 2 == 0:
                partial = recur_ready(c)
            else:
                recur_finish(c, *partial)
            upto = (phase + 1) * len(pieces) // n_phases
            for part, split in pieces[done:upto]:
                project(part, split)
            done = upto


def _mlstm_prompt_kernel(*refs, tiles_per_seq, n_cast, n_xpose):
    n_side = n_cast + n_xpose
    x_ref, wqkvo_ref, wgt_ref, bgc_ref, mhg_ref = refs[:5]
    side_src = refs[5:5 + n_side]
    hm_ref, c_ref, n_ref, m_ref = refs[5 + n_side:9 + n_side]
    side_dst = refs[9 + n_side:9 + 2 * n_side]
    scratch = refs[9 + 2 * n_side:]
    q_s, k_s, v_s, og_s, e_s, colb_s, colm_s, cole_s, cst_s, nst_s, mst_s = scratch
    g_step = pl.program_id(0)

    def side_jobs():
        for i, (src, dst) in enumerate(zip(side_src, side_dst)):
            dst[...] = (src[...] if i < n_cast else src[...].T).astype(BF16)

    @pl.when(g_step == 0)
    def _():
        for ref in (q_s, k_s, v_s, og_s, e_s, colb_s, colm_s, cole_s):
            ref[1] = jnp.zeros(ref.shape[1:], ref.dtype)

    @pl.when(jnp.logical_or(g_step == 0, (g_step - 1) % tiles_per_seq == 0))
    def _():
        for ref in (cst_s, nst_s, mst_s):
            ref[...] = jnp.zeros(ref.shape, F32)

    for parity in (0, 1):
        @pl.when(g_step % 2 == parity)
        def _():
            side_jobs()
            _mlstm_prompt_step(x_ref, wqkvo_ref, wgt_ref, bgc_ref, mhg_ref, hm_ref, *scratch,
                               cur=parity, prev=1 - parity)

    @pl.when(jnp.logical_and(g_step >= 1, (g_step - 1) % tiles_per_seq == tiles_per_seq - 1))
    def _():
        for h in range(N_HEADS):
            c_ref[0, h] = cst_s[h]
            n_ref[0, h:h + 1, :] = nst_s[h]
            m_ref[0, h] = mst_s[h][:, 0:1]


XPOSE_ROWS = 256


def _mlstm_prompt(x, wqkvo, wgt, bgc, mhg, f32_weights, w_in_t):
    bsz, t, d = x.shape
    dh = d // N_HEADS
    tm = MLSTM_TILE
    nc = tm // CHUNK
    tps = t // tm
    n_tiles = bsz * tps
    seq_block = lambda g: (jnp.maximum(g - 1, 0) // tps, 0, 0, 0)
    cast_specs = []
    for w in f32_weights:
        rows_per_step = w.shape[0] // n_tiles
        assert rows_per_step * n_tiles == w.shape[0] and rows_per_step % 16 == 0
        cast_specs.append(pl.BlockSpec((rows_per_step, w.shape[1]),
                                       lambda g: (jnp.minimum(g, n_tiles - 1), 0)))
    n_cast = len(f32_weights)
    r = XPOSE_ROWS
    n_bch, n_gate = 3 * d // r, 2 * d // r
    off_gate = 7 * d + 2 * N_HEADS
    assert n_bch + n_gate <= n_tiles + 1 and d % r == 0 and off_gate % 8 == 0
    bch_job = lambda g: jnp.minimum(g, n_bch - 1)
    gate_job = lambda g: jnp.clip(g - n_bch, 0, n_gate - 1)
    xpose_src = [pl.BlockSpec((r, d), lambda g: (bch_job(g), 0)),
                 pl.BlockSpec((pl.Element(r), pl.Element(d)),
                              lambda g: (pl.multiple_of(off_gate + gate_job(g) * r, 8), 0))]
    xpose_dst = [pl.BlockSpec((d, r), lambda g: (0, bch_job(g))),
                 pl.BlockSpec((d, r), lambda g: (0, gate_job(g)))]
    hm, c, n, m, *side = pl.pallas_call(
        functools.partial(_mlstm_prompt_kernel, tiles_per_seq=tps, n_cast=n_cast, n_xpose=2),
        grid=(n_tiles + 1,),
        in_specs=[pl.BlockSpec((tm, d), lambda g: (jnp.minimum(g, n_tiles - 1), 0)),
                  _qkvo_spec(d), _const_spec(wgt.shape), _const_spec(bgc.shape),
                  _const_spec(mhg.shape)] + cast_specs + xpose_src,
        out_specs=[pl.BlockSpec((tm, d), lambda g: (jnp.maximum(g - 1, 0), 0)),
                   pl.BlockSpec((1, N_HEADS, dh, dh), seq_block),
                   pl.BlockSpec((1, N_HEADS, dh), lambda g: seq_block(g)[:3]),
                   pl.BlockSpec((1, N_HEADS, 1, 1), seq_block)] + cast_specs + xpose_dst,
        out_shape=[jax.ShapeDtypeStruct((bsz * t, d), BF16),
                   jax.ShapeDtypeStruct((bsz, N_HEADS, dh, dh), F32),
                   jax.ShapeDtypeStruct((bsz, N_HEADS, dh), F32),
                   jax.ShapeDtypeStruct((bsz, N_HEADS, 1, 1), F32)]
                  + [jax.ShapeDtypeStruct(w.shape, BF16) for w in f32_weights]
                  + [jax.ShapeDtypeStruct((d, 3 * d), BF16),
                     jax.ShapeDtypeStruct((d, 2 * d), BF16)],
        scratch_shapes=[
            pltpu.VMEM((2, N_HEADS, tm, dh), BF16),
            pltpu.VMEM((2, N_HEADS, tm, dh), BF16),
            pltpu.VMEM((2, N_HEADS, tm, dh), BF16),
            pltpu.VMEM((2, N_HEADS, tm, dh), F32),
            pltpu.VMEM((2, nc, N_HEADS, CHUNK, CHUNK), F32),
            pltpu.VMEM((2, nc, N_HEADS, CHUNK, LANES), F32),
            pltpu.VMEM((2, nc, N_HEADS, CHUNK, LANES), F32),
            pltpu.VMEM((2, nc, N_HEADS, CHUNK, LANES), F32),
            pltpu.VMEM((N_HEADS, dh, dh), F32),
            pltpu.VMEM((N_HEADS, 1, dh), F32),
            pltpu.VMEM((N_HEADS, 1, LANES), F32)],
        compiler_params=_params(1),
        name="mlstm_prompt",
    )(x.reshape(bsz * t, d), wqkvo, wgt, bgc, mhg, *f32_weights, w_in_t, w_in_t)
    casted, (w_bch, w_gate) = side[:n_cast], side[n_cast:]
    return hm, c, n, m.reshape(bsz, N_HEADS), casted, w_bch, w_gate


def _mlstm_sample_pre_kernel(x_ref, wq_ref, wk_ref, wv_ref, wo_ref, wgc_ref, bgr_ref, mhg_ref,
                             n0_ref, m0_ref,
                             q_ref, kw_ref, v_ref, og_ref, dec_ref, wa_ref, wb_ref, n_ref, m_ref,
                             g_s, *, n_steps):
    rows, dh = q_ref.shape
    nb = rows // n_steps
    T = n_steps
    hd = pl.program_id(0)
    slab = [slice(t * nb, (t + 1) * nb) for t in range(T)]
    head_lane = lax.broadcasted_iota(jnp.int32, m_ref.shape, 1)
    gate_lane = lax.broadcasted_iota(jnp.int32, (nb, g_s.shape[1]), 1)

    x = _time_major(x_ref)
    xh = x.astype(BF16)

    @pl.when(hd == 0)
    def _():
        xl = (x - xh.astype(F32)).astype(BF16)
        g = _dot(xh, wgc_ref[0]) + _dot(xl, wgc_ref[0]) + _dot(xh, wgc_ref[1]) + bgr_ref[...]
        is_input_gate = lax.broadcasted_iota(jnp.int32, g.shape, 1) < N_HEADS
        g_s[...] = jnp.where(is_input_gate, g, _log_sigmoid(g))
        m_ref[...] = jnp.zeros(m_ref.shape, F32)

    def gate_column(t, col):
        return jnp.sum(jnp.where(gate_lane == col, g_s[slab[t], :], 0.0), axis=1, keepdims=True)

    q_all = _dot(xh, wq_ref[...]).astype(BF16).astype(F32)
    k_all = (_dot(xh, wk_ref[...]) * (dh ** -0.5)).astype(BF16).astype(F32)
    v_all = _dot(xh, wv_ref[...]).astype(BF16).astype(F32)
    og_ref[...] = mhg_ref[...] * _sigmoid(_dot(xh, wo_ref[...]))
    q_ref[...] = q_all
    v_ref[...] = v_all
    qf = [q_all[slab[t]] for t in range(T)]
    kf = [k_all[slab[t]] for t in range(T)]
    vf = [v_all[slab[t]] for t in range(T)]
    li = [gate_column(t, hd) for t in range(T)]
    lf = [gate_column(t, N_HEADS + hd) for t in range(T)]
    m0 = jnp.sum(jnp.where(head_lane == hd, m0_ref[...], 0.0), axis=1, keepdims=True)
    n0 = n0_ref[:, hd, :]
    b = [lf[0]]
    for t in range(1, T):
        b.append(b[t - 1] + lf[t])
    a = [li[t] - b[t] for t in range(T)]
    m_new = None
    for t in range(T):
        inter = b[t] + m0
        m_t = inter
        for s in range(t + 1):
            m_t = jnp.maximum(m_t, b[t] + a[s])
        w_inter = jnp.exp(inter - m_t)
        num = jnp.zeros((nb, dh), F32)
        den = w_inter * jnp.sum(qf[t] * n0, axis=1, keepdims=True)
        for s in range(t + 1):
            s_w = (jnp.sum(qf[t] * kf[s], axis=1, keepdims=True)
                   * jnp.exp(b[t] + a[s] - m_t))
            num = num + s_w * vf[s]
            den = den + s_w
        inv = 1.0 / jnp.maximum(jnp.abs(den), jnp.exp(-m_t))
        wa_ref[slab[t], :] = jnp.broadcast_to(w_inter * inv, (nb, dh))
        wb_ref[slab[t], :] = num * inv
        m_new = m_t
    b_last = b[T - 1]
    decay = jnp.exp(b_last + m0 - m_new)
    n_new = decay * n0
    for s in range(T):
        kw = kf[s] * jnp.exp(a[s] + b_last - m_new)
        kw_ref[slab[s], :] = kw
        n_new = n_new + kw
    n_ref[:, hd, :] = n_new
    m_ref[...] = jnp.where(head_lane == hd, m_new, m_ref[...])
    dec_ref[...] = jnp.broadcast_to(decay, (nb, dh))


def _mlstm_sample_pre(xs, wqkvo, wgc, bgr, mhg, n0, m0):
    nb, n_steps, d = xs.shape
    rows = nb * n_steps
    dh = d // N_HEADS
    head_w = lambda part: pl.BlockSpec((d, dh), lambda h: (0, part * N_HEADS + h))
    head_cols = lambda n_rows: pl.BlockSpec((n_rows, dh), lambda h: (0, h))
    big = jax.ShapeDtypeStruct((rows, d), F32)
    small = jax.ShapeDtypeStruct((nb, d), F32)
    n_shape = (nb, N_HEADS, dh)
    return pl.pallas_call(
        functools.partial(_mlstm_sample_pre_kernel, n_steps=n_steps),
        grid=(N_HEADS,),
        in_specs=[_const_spec(xs.shape), head_w(0), head_w(1), head_w(2), head_w(3),
                  _const_spec(wgc.shape), _const_spec(bgr.shape), head_cols(1),
                  _const_spec(n_shape), _const_spec(m0.shape)],
        out_specs=[head_cols(rows)] * 4 + [head_cols(nb)] + [head_cols(rows)] * 2
                  + [pl.BlockSpec(n_shape, lambda h: (0, 0, 0)),
                     pl.BlockSpec((nb, N_HEADS), lambda h: (0, 0))],
        out_shape=[big, big, big, big, small, big, big, jax.ShapeDtypeStruct(n_shape, F32),
                   jax.ShapeDtypeStruct((nb, N_HEADS), F32)],
        scratch_shapes=[pltpu.VMEM((rows, GATE_PAD), F32)],
        compiler_params=_params(1),
        name="mlstm_sample_pre",
    )(xs, wqkvo, wqkvo, wqkvo, wqkvo, wgc, bgr, mhg, n0, m0)


def _stream_matrix_memory(c0_ref, q_ref, kw_ref, v_ref, dec_ref, c_ref, qc_ref):
    T, bb, dh = q_ref.shape
    owner = lax.broadcasted_iota(jnp.int32, (T * bb, dh), 0) % bb
    q_blk = jnp.concatenate([q_ref[t] for t in range(T)], axis=0).astype(BF16)
    kw_blk = jnp.concatenate([kw_ref[t] for t in range(T)], axis=0).astype(BF16)
    v_blk = jnp.concatenate([v_ref[t] for t in range(T)], axis=0)
    c_olds = [c0_ref[bi, 0] for bi in range(bb)]
    reads = [_dot(q_blk, c_olds[bi].astype(BF16)) for bi in range(bb)]
    qc = jnp.zeros((T * bb, dh), F32)
    for bi in range(bb):
        qc = jnp.where(owner == bi, reads[bi], qc)
    for t in range(T):
        qc_ref[t] = qc[t * bb:(t + 1) * bb]
    for bi in range(bb):
        upd = _dot_tn(kw_blk, jnp.where(owner == bi, v_blk, 0.0).astype(BF16))
        c_ref[bi, 0] = dec_ref[bi:bi + 1, :] * c_olds[bi] + upd


def _merge_sample_kernel(x_ref, m1_ref, wa_ref, wb_ref, qc_ref, og_ref, wmo_ref, wgm_ref, wo_ref,
                         g_ref, b_ref, o_ref, hm_s, *, alpha):
    d = x_ref.shape[-1]
    dh = d // N_HEADS
    for hd in range(N_HEADS):
        hc = slice(hd * dh, (hd + 1) * dh)
        hh = wa_ref[:, hc] * qc_ref[:, hc] + wb_ref[:, hc]
        hm_s[:, hc] = (_head_norm(hh) * og_ref[:, hc]).astype(BF16)
    x = _time_major(x_ref)
    gate = _dot(x.astype(BF16), wgm_ref[...])
    ym = _dot(hm_s[...], wmo_ref[...])
    merged = m1_ref[...] + _sigmoid(gate) * ym
    r = alpha * x + _dot(merged.astype(BF16), wo_ref[...])
    o_ref[...] = _layer_norm(r, g_ref[...], b_ref[...])


def _merge_sample(xs, m1, wa, wb, qc, og, wmo, w_b, wo, g, b, alpha):
    n, d = m1.shape
    whole = _const_spec((n, d))
    return pl.pallas_call(
        functools.partial(_merge_sample_kernel, alpha=alpha),
        grid=(1,),
        in_specs=[_const_spec(xs.shape)] + [whole] * 5
                 + [_const_spec(wmo.shape), _gate_spec(d, 1), _const_spec(wo.shape),
                    _const_spec(g.shape), _const_spec(b.shape)],
        out_specs=pl.BlockSpec((n, d), lambda i: (0, 0)),
        out_shape=jax.ShapeDtypeStruct((n, d), F32),
        scratch_shapes=[pltpu.VMEM((n, d), BF16)],
        compiler_params=_params(1),
        name="merge_ln1_sample",
    )(xs, m1, wa, wb, qc, og, wmo, w_b, wo, g, b)


def _ffn_hidden(xb, w1_ref, hid_s, rows):
    dff = hid_s.shape[1]
    sw = 1024 if dff % 1024 == 0 else dff
    for s in range(dff // sw):
        cs = slice(s * sw, (s + 1) * sw)
        hid = jnp.maximum(_dot(xb, w1_ref[:, cs]), 0.0)
        hid_s[rows, cs] = (hid * hid).astype(BF16)


def _ffn_small_kernel(x_ref, w1_ref, w2_ref, g_ref, b_ref, o_ref, acc_s, *, alpha):
    j = pl.program_id(0)
    x = x_ref[...]
    hid = jnp.maximum(_dot(x.astype(BF16), w1_ref[...]), 0.0)
    part = _dot((hid * hid).astype(BF16), w2_ref[...])

    @pl.when(j == 0)
    def _():
        acc_s[...] = alpha * x + part

    @pl.when(j > 0)
    def _():
        acc_s[...] += part

    @pl.when(j == pl.num_programs(0) - 1)
    def _():
        y = _layer_norm(acc_s[...], g_ref[...], b_ref[...])
        nb, n_t, _ = o_ref.shape
        for t in range(n_t):
            o_ref[:, t, :] = y[t * nb:(t + 1) * nb]


def _ffn_stream_kernel(x_ref, w1_ref, w2_ref, g_ref, b_ref, c0_ref, q_ref, kw_ref, v_ref,
                       dec_ref, o_ref, c_ref, qc_ref, hid_s, r_s, *, alpha):
    s = pl.program_id(0)
    last = pl.num_programs(0) - 1

    @pl.when(s == 0)
    def _():
        r_s[...] = jnp.zeros(r_s.shape, F32)

    @pl.when(s < last)
    def _():
        _stream_matrix_memory(c0_ref, q_ref, kw_ref, v_ref, dec_ref, c_ref, qc_ref)
        x = x_ref[...]
        _ffn_hidden(x.astype(BF16), w1_ref, hid_s, slice(None))
        o_ref[...] = _layer_norm(r_s[...], g_ref[...], b_ref[...])
        r_s[...] = alpha * x + _dot(hid_s[...], w2_ref[...])

    @pl.when(s == last)
    def _():
        o_ref[...] = _layer_norm(r_s[...], g_ref[...], b_ref[...])


def _ffn_small(x, w1, w2, g, b, alpha, n_steps):
    n, d = x.shape
    dff = w1.shape[1]
    sw = 1024 if dff % 1024 == 0 else dff
    out_shape = (n // n_steps, n_steps, d)
    return pl.pallas_call(
        functools.partial(_ffn_small_kernel, alpha=alpha),
        grid=(dff // sw,),
        in_specs=[_const_spec((n, d)), pl.BlockSpec((d, sw), lambda j: (0, j)),
                  pl.BlockSpec((sw, d), lambda j: (j, 0)), _const_spec(g.shape),
                  _const_spec(b.shape)],
        out_specs=pl.BlockSpec(out_shape, lambda j: (0,) * len(out_shape)),
        out_shape=jax.ShapeDtypeStruct(out_shape, F32),
        scratch_shapes=[pltpu.VMEM((n, d), F32)],
        compiler_params=_params(1),
        name="ffn_ln2_small",
    )(x, w1, w2, g, b)


def _ffn_with_stream(x, w1, w2, g, b, alpha, c0, q, kw, v, dec, n_steps):
    n, d = x.shape
    rows = q.shape[0]
    nb = rows // n_steps
    dh = d // N_HEADS
    bb = SAMPLE_BATCH_BLOCK
    n_grid = (nb // bb) * N_HEADS
    tm = n // n_grid
    assert tm * n_grid == n and tm % 8 == 0 and nb % bb == 0
    cur = lambda s: jnp.minimum(s, n_grid - 1)
    x_tile = pl.BlockSpec((tm, d), lambda s: (cur(s), 0))
    o_tile = pl.BlockSpec((tm, d), lambda s: (jnp.maximum(s - 1, 0), 0))
    cblock = pl.BlockSpec((bb, 1, dh, dh),
                          lambda s: (cur(s) // N_HEADS, cur(s) % N_HEADS, 0, 0))
    tblock = pl.BlockSpec((n_steps, bb, dh), lambda s: (0, cur(s) // N_HEADS, cur(s) % N_HEADS))
    dblock = pl.BlockSpec((bb, dh), lambda s: (cur(s) // N_HEADS, cur(s) % N_HEADS))
    as_tbd = lambda a: a.reshape(n_steps, nb, d)
    x2, c_new, qc = pl.pallas_call(
        functools.partial(_ffn_stream_kernel, alpha=alpha),
        grid=(n_grid + 1,),
        in_specs=[x_tile, _const_spec(w1.shape), _const_spec(w2.shape), _const_spec(g.shape),
                  _const_spec(b.shape), cblock, tblock, tblock, tblock, dblock],
        out_specs=[o_tile, cblock, tblock],
        out_shape=[jax.ShapeDtypeStruct((n, d), F32), jax.ShapeDtypeStruct(c0.shape, F32),
                   jax.ShapeDtypeStruct((n_steps, nb, d), F32)],
        scratch_shapes=[pltpu.VMEM((tm, w1.shape[1]), BF16), pltpu.VMEM((tm, d), F32)],
        compiler_params=_params(1),
        name="ffn_ln2_stream",
    )(x, w1, w2, g, b, c0, as_tbd(q), as_tbd(kw), as_tbd(v), dec)
    return x2, c_new, qc.reshape(rows, d)


def _w_in_prep_kernel(a_ref, g_ref, wa_ref, wg_ref):
    @pl.when(pl.program_id(0) == 0)
    def _():
        wg_ref[...] = g_ref[...]

    wa_ref[...] = a_ref[...].T.astype(BF16)


def _prepare_w_in(w_in_t):
    d = w_in_t.shape[1]
    h2 = 2 * N_HEADS
    return pl.pallas_call(
        _w_in_prep_kernel,
        grid=(4,),
        in_specs=[pl.BlockSpec((d, d), lambda j: (3 + j, 0)),
                  pl.BlockSpec((pl.Element(h2), pl.Element(d)), lambda j: (7 * d, 0))],
        out_specs=[pl.BlockSpec((d, d), lambda j: (0, j)),
                   pl.BlockSpec((h2, d), lambda j: (0, 0))],
        out_shape=[jax.ShapeDtypeStruct((d, 4 * d), BF16),
                   jax.ShapeDtypeStruct((h2, d), F32)],
        compiler_params=_params(1),
        name="w_in_prep",
    )(w_in_t, w_in_t)


def _layer_weights(w_in, b_gate, conv_w, w_conv_out, mh_g, w_m_out, w_o, ln1_g, ln1_b,
                   w_ff1, w_ff2, ln2_g, ln2_b):
    d = w_in.shape[0]
    h2 = 2 * N_HEADS
    w_in_t = jnp.swapaxes(w_in, 0, 1)
    w_qkvo, wg_t = _prepare_w_in(w_in_t)
    wgt_hi, wgt_lo = _split_bf16(wg_t)
    wgt = jnp.concatenate([wgt_hi, wgt_lo], axis=0)
    pad = ((0, 0), (0, GATE_PAD - h2))
    wgc = jnp.stack([jnp.pad(wgt_hi.T, pad), jnp.pad(wgt_lo.T, pad)])
    return dict(
        w_in_t=w_in_t, w_qkvo=w_qkvo,
        wgt=wgt, wgc=wgc,
        bgc=b_gate.reshape(h2, 1).astype(F32),
        bgr=jnp.pad(b_gate.reshape(1, h2).astype(F32), pad),
        cw=conv_w.astype(F32),
        mhg=mh_g.reshape(1, d).astype(F32),
        f32_weights=[w_conv_out, w_m_out, w_o],
        ffn_weights=(w_ff1, w_ff2),
        ln1_g=ln1_g.reshape(1, d), ln1_b=ln1_b.reshape(1, d),
        ln2_g=ln2_g.reshape(1, d), ln2_b=ln2_b.reshape(1, d))


def _layer(x, xs, conv_buf, c0, n0, m0, p, alpha):
    bsz, t, d = x.shape
    n_steps = xs.shape[1]
    hm, c_p, n_p, m_p, (wco, wmo, wo), w_bch, w_gate = _mlstm_prompt(
        x, p["w_qkvo"], p["wgt"], p["bgc"], p["mhg"], p["f32_weights"], p["w_in_t"])
    x1, conv_p, w1, w2 = _conv_merge_prompt(x, hm, w_bch, p["cw"], wco, w_gate, wmo, wo,
                                            p["ln1_g"], p["ln1_b"], alpha, *p["ffn_weights"])
    m1_s, conv_s = _conv_branch_sample(xs, conv_buf, w_bch, p["cw"], wco, w_gate)
    q, kw, v, og, dec, wa, wb, n_s, m_s = _mlstm_sample_pre(
        xs, p["w_qkvo"], p["wgc"], p["bgr"], p["mhg"], n0, m0)
    x2, c_s, qc = _ffn_with_stream(x1, w1, w2, p["ln2_g"], p["ln2_b"], alpha,
                                   c0, q, kw, v, dec, n_steps)
    x1_s = _merge_sample(xs, m1_s, wa, wb, qc, og, wmo, w_gate, wo, p["ln1_g"], p["ln1_b"],
                         alpha)
    x2_s = _ffn_small(x1_s, w1, w2, p["ln2_g"], p["ln2_b"], alpha, n_steps)
    return (x2.reshape(bsz, t, d), conv_p, c_p, n_p, m_p,
            x2_s, conv_s, c_s, n_s, m_s)


def kernel(x_prompt, x_sample, state_conv, state_C, state_n, state_m, w_in, b_gate, conv_w,
           w_conv_out, mh_g, w_m_out, w_o, ln1_g, ln1_b, w_ff1, w_ff2, ln2_g, ln2_b):
    depth = w_in.shape[0]
    alpha = (2.0 * depth) ** 0.25
    bsz, t, d = x_prompt.shape
    sb, st, _ = x_sample.shape
    assert t % TOKEN_TILE == 0 and t % MLSTM_TILE == 0 and MLSTM_TILE % CHUNK == 0
    assert d % N_HEADS == 0
    assert sb % SAMPLE_BATCH_BLOCK == 0 and st >= CONV_W - 1

    xp, xs = x_prompt, x_sample
    outs = [[] for _ in range(8)]
    for l in range(depth):
        p = _layer_weights(w_in[l], b_gate[l], conv_w[l], w_conv_out[l], mh_g[l], w_m_out[l],
                           w_o[l], ln1_g[l], ln1_b[l], w_ff1[l], w_ff2[l], ln2_g[l], ln2_b[l])
        xp, cp, c_p, n_p, m_p, xs, cs, c_s, n_s, m_s = _layer(
            xp, xs, state_conv[l], state_C[l], state_n[l], state_m[l], p, alpha)
        for acc, val in zip(outs, (cp, cs, c_p, c_s, n_p, n_s, m_p, m_s)):
            acc.append(val)
    return (xp, xs) + tuple(jnp.stack(acc) for acc in outs)
```

```python
import functools

import jax
import jax.numpy as jnp
from jax import lax
from jax.experimental import pallas as pl
from jax.experimental.pallas import tpu as pltpu

F32 = jnp.float32
BF16 = jnp.bfloat16

LN_EPS = 1e-5
N_HEADS = 4
CHUNK = 128
CONV_W = 3
TOKEN_TILE = 1024
SUB_TILE = 512
MLSTM_TILE = 512
SAMPLE_BATCH_BLOCK = 16
LANES = 128
GATE_PAD = LANES
VMEM_LIMIT = 56 * 1024 * 1024


def _dot(a, b):
    return jnp.dot(a, b, preferred_element_type=F32)


def _dot_nt(a, b):
    return lax.dot_general(a, b, (((1,), (1,)), ((), ())), preferred_element_type=F32)


def _dot_tn(a, b):
    return lax.dot_general(a, b, (((0,), (0,)), ((), ())), preferred_element_type=F32)


def _sigmoid(x):
    return 1.0 / (1.0 + jnp.exp(-x))


def _log_sigmoid(x):
    return jnp.minimum(x, 0.0) - jnp.log1p(jnp.exp(-jnp.abs(x)))


def _split_bf16(x):
    hi = x.astype(BF16)
    lo = (x - hi.astype(F32)).astype(BF16)
    return hi, lo


def _layer_norm(r, g, b):
    mu = jnp.mean(r, axis=-1, keepdims=True)
    xc = r - mu
    var = jnp.mean(xc * xc, axis=-1, keepdims=True)
    return xc * lax.rsqrt(var + LN_EPS) * g + b


def _head_norm(h):
    mu = jnp.mean(h, axis=-1, keepdims=True)
    hc = h - mu
    return hc * lax.rsqrt(jnp.mean(hc * hc, axis=-1, keepdims=True) + LN_EPS)


def _const_spec(shape):
    zeros = (0,) * len(shape)
    return pl.BlockSpec(shape, lambda *_: zeros, pipeline_mode=pl.Buffered(1))


def _window_spec(block_shape, index):
    return pl.BlockSpec(block_shape, lambda *_: index, pipeline_mode=pl.Buffered(1))


def _bch_spec(d):
    return _const_spec((d, 3 * d))


def _qkvo_spec(d):
    return _const_spec((d, 4 * d))


def _gate_spec(d, which):
    return _window_spec((d, d), (0, which))


def _params(n_axes):
    return pltpu.CompilerParams(dimension_semantics=("arbitrary",) * n_axes,
                                vmem_limit_bytes=VMEM_LIMIT)


def _conv_stripes(d):
    sw = 256 if d % 256 == 0 else d
    return [slice(s * sw, (s + 1) * sw) for s in range(d // sw)]


def _conv_merge_prompt_kernel(x_ref, hm_ref, wbch_ref, cw_ref, wco_ref, wgc_ref, wmo_ref, wgm_ref,
                              wo_ref, g_ref, b_ref, w1_ref, w2_ref, o_ref, cs_ref, w1b_ref,
                              w2b_ref, u_s, a_s, mg_s, *, alpha):
    tm, d = a_s.shape
    @pl.when(pl.program_id(1) == 0)
    def _():
        u_s[0:8, :] = jnp.zeros((8, d), F32)

    w1b_ref[...] = w1_ref[...].astype(BF16)
    w2b_ref[...] = w2_ref[...].astype(BF16)

    for r0 in range(0, tm, SUB_TILE):
        sub = min(SUB_TILE, tm - r0)
        x = x_ref[0, r0:r0 + sub, :]
        rows = slice(r0, r0 + sub)
        xb = x.astype(BF16)
        for cs in _conv_stripes(d):
            off = cs.start
            cg = _dot(xb, wbch_ref[:, d + off:d + cs.stop])
            hc = _dot(xb, wbch_ref[:, 2 * d + off:2 * d + cs.stop])
            u = cg * hc
            u_s[8 + r0:8 + r0 + sub, cs] = u
            conv = (u_s[6 + r0:6 + r0 + sub, cs] * cw_ref[0:1, cs]
                    + u_s[7 + r0:7 + r0 + sub, cs] * cw_ref[1:2, cs] + u * cw_ref[2:3, cs])
            bg = _dot(xb, wbch_ref[:, off:cs.stop])
            a_s[rows, cs] = (bg * conv).astype(BF16)
        hm = hm_ref[rows, :]
        for cs in _conv_stripes(d):
            gated_m = _sigmoid(_dot(xb, wgm_ref[:, cs])) * _dot(hm, wmo_ref[:, cs])
            gated_c = _sigmoid(_dot(xb, wgc_ref[:, cs])) * _dot(a_s[rows, :], wco_ref[:, cs])
            mg_s[rows, cs] = (gated_c + gated_m).astype(BF16)
        r = alpha * x + _dot(mg_s[rows, :], wo_ref[...])
        o_ref[rows, :] = _layer_norm(r, g_ref[...], b_ref[...])
    cs_ref[0] = u_s[tm + 6:tm + 8, :]
    u_s[0:8, :] = u_s[tm:tm + 8, :]


def _time_major(x_ref):
    return jnp.concatenate([x_ref[:, t, :] for t in range(x_ref.shape[1])], axis=0)


def _conv_sample_kernel(x_ref, st_ref, wbch_ref, cw_ref, wco_ref, wgc_ref, m1_ref, cs_ref, a_s):
    nb, n_steps, d = x_ref.shape
    xb = _time_major(x_ref).astype(BF16)
    for cs in _conv_stripes(d):
        off = cs.start
        bg = _dot(xb, wbch_ref[:, off:cs.stop])
        cg = _dot(xb, wbch_ref[:, d + off:d + cs.stop])
        hc = _dot(xb, wbch_ref[:, 2 * d + off:2 * d + cs.stop])
        u = cg * hc
        up = [st_ref[:, j, cs] for j in range(CONV_W - 1)]
        up += [u[t * nb:(t + 1) * nb] for t in range(n_steps)]
        for t in range(n_steps):
            conv = (up[t] * cw_ref[0:1, cs] + up[t + 1] * cw_ref[1:2, cs]
                    + up[t + 2] * cw_ref[2:3, cs])
            a_s[t * nb:(t + 1) * nb, cs] = (bg[t * nb:(t + 1) * nb] * conv).astype(BF16)
        for j in range(CONV_W - 1):
            cs_ref[:, j, cs] = up[n_steps + j]
    yc = _dot(a_s[...], wco_ref[...])
    m1_ref[...] = _sigmoid(_dot(xb, wgc_ref[...])) * yc


def _conv_merge_prompt(x, hm, w_a, cw, wco, w_b, wmo, wo, g, b, alpha, w1, w2):
    bsz, t, d = x.shape
    tm = TOKEN_TILE
    tps = t // tm
    n_tiles = bsz * tps
    flat = pl.BlockSpec((tm, d), lambda bi, j: (bi * tps + j, 0))
    cast_specs = []
    for w in (w1, w2):
        rows_per_step = w.shape[0] // n_tiles
        assert rows_per_step * n_tiles == w.shape[0] and rows_per_step % 16 == 0
        cast_specs.append(pl.BlockSpec((rows_per_step, w.shape[1]),
                                       lambda bi, j: (bi * tps + j, 0)))
    return pl.pallas_call(
        functools.partial(_conv_merge_prompt_kernel, alpha=alpha),
        grid=(bsz, tps),
        in_specs=[pl.BlockSpec((1, tm, d), lambda bi, j: (bi, j, 0)), flat,
                  _bch_spec(d), _const_spec(cw.shape), _const_spec(wco.shape),
                  _gate_spec(d, 0), _const_spec(wmo.shape), _gate_spec(d, 1),
                  _const_spec(wo.shape), _const_spec(g.shape), _const_spec(b.shape)]
                 + cast_specs,
        out_specs=[flat, pl.BlockSpec((1, CONV_W - 1, d), lambda bi, j: (bi, 0, 0))]
                  + cast_specs,
        out_shape=[jax.ShapeDtypeStruct((bsz * t, d), F32),
                   jax.ShapeDtypeStruct((bsz, CONV_W - 1, d), F32),
                   jax.ShapeDtypeStruct(w1.shape, BF16), jax.ShapeDtypeStruct(w2.shape, BF16)],
        scratch_shapes=[pltpu.VMEM((tm + 8, d), F32), pltpu.VMEM((tm, d), BF16),
                        pltpu.VMEM((tm, d), BF16)],
        compiler_params=_params(2),
        name="conv_merge_prompt",
    )(x, hm, w_a, cw, wco, w_b, wmo, w_b, wo, g, b, w1, w2)


def _conv_branch_sample(xs, conv_buf, wbch, cw, wco, wgc):
    nb, n_steps, d = xs.shape
    rows = nb * n_steps
    return pl.pallas_call(
        _conv_sample_kernel,
        grid=(1,),
        in_specs=[_const_spec(xs.shape), _const_spec(conv_buf.shape), _bch_spec(d),
                  _const_spec(cw.shape), _const_spec(wco.shape), _gate_spec(d, 0)],
        out_specs=[pl.BlockSpec((rows, d), lambda i: (0, 0)),
                   pl.BlockSpec(conv_buf.shape, lambda i: (0, 0, 0))],
        out_shape=[jax.ShapeDtypeStruct((rows, d), F32),
                   jax.ShapeDtypeStruct(conv_buf.shape, F32)],
        scratch_shapes=[pltpu.VMEM((rows, d), BF16)],
        compiler_params=_params(1),
        name="conv_branch_sample",
    )(xs, conv_buf, wbch, cw, wco, wgc)


def _rep(col, times):
    return col if times == 1 else jnp.concatenate([col] * times, axis=1)


def _mlstm_prompt_step(x_ref, wqkvo_ref, wgt_ref, bgc_ref, mhg_ref, hm_ref,
                       q_s, k_s, v_s, og_s, e_s, colb_s, colm_s, cole_s, cst_s, nst_s, mst_s,
                       *, cur, prev):
    _, _, tm, dh = q_s.shape
    d = N_HEADS * dh
    L = CHUNK
    LANES = colb_s.shape[-1]
    n_chunks = tm // L
    wide = dh // LANES

    xh = x_ref[...].astype(BF16)

    h2 = 2 * N_HEADS
    ga = _dot_nt(wgt_ref[...], xh)
    gt = ga[0:h2] + ga[h2:2 * h2] + bgc_ref[...]
    is_input_gate = lax.broadcasted_iota(jnp.int32, gt.shape, 0) < N_HEADS
    g = jnp.where(is_input_gate, gt, _log_sigmoid(gt))

    pos = lax.broadcasted_iota(jnp.int32, g.shape, 1) % L
    csum = g
    shift = 1
    while shift < L:
        csum = csum + jnp.where(pos >= shift, pltpu.roll(csum, shift, 1), 0.0)
        shift *= 2
    ba = jnp.concatenate([csum[N_HEADS:], g[:N_HEADS] - csum[N_HEADS:]], axis=0)

    ri = lax.broadcasted_iota(jnp.int32, (L, L), 0)
    ci = lax.broadcasted_iota(jnp.int32, (L, L), 1)
    causal = ri >= ci

    for c in range(n_chunks):
        slab = ba[:, c * L:(c + 1) * L]
        cols = slab.T
        for h in range(N_HEADS):
            b_rep = jnp.broadcast_to(cols[:, h:h + 1], (L, LANES))
            a_rep = jnp.broadcast_to(cols[:, N_HEADS + h:N_HEADS + h + 1], (L, LANES))
            a_row = slab[N_HEADS + h:N_HEADS + h + 1, :]
            dm = jnp.where(causal, _rep(b_rep, L // LANES) + a_row, -jnp.inf)
            m_loc = jnp.broadcast_to(jnp.max(dm, axis=1, keepdims=True), (L, LANES))
            e_s[cur, c, h] = jnp.exp(dm - _rep(m_loc, L // LANES))
            colb_s[cur, c, h] = b_rep
            colm_s[cur, c, h] = m_loc
            cole_s[cur, c, h] = jnp.exp(a_rep + b_rep[L - 1:L, :] - m_loc[L - 1:L, :])

    n_split = 2 if N_HEADS % 2 == 0 else 1
    piece = d // n_split

    def project(part, split):
        lo = split * piece
        y = _dot(xh, wqkvo_ref[:, part * d + lo:part * d + lo + piece])
        if part == 1:
            y = y * (dh ** -0.5)
        elif part == 3:
            y = mhg_ref[:, lo:lo + piece] * _sigmoid(y)
        dst = (q_s, k_s, v_s, og_s)[part]
        for h in range(lo // dh, (lo + piece) // dh):
            dst[cur, h] = y[:, h * dh - lo:(h + 1) * dh - lo].astype(dst.dtype)

    heads = [slice(h * dh, (h + 1) * dh) for h in range(N_HEADS)]

    def lane_rep(col):
        return jnp.broadcast_to(col, (col.shape[0], LANES))

    def recur_ready(c):
        rows = slice(c * L, (c + 1) * L)
        qk = [_dot_nt(q_s[prev, h, rows, :], k_s[prev, h, rows, :]) for h in range(N_HEADS)]
        states = [cst_s[h] for h in range(N_HEADS)]
        qc = [_dot(q_s[prev, h, rows, :], states[h].astype(BF16)) for h, hc in enumerate(heads)]
        ux, ks, qn = [], [], []
        for h, hc in enumerate(heads):
            kw = k_s[prev, h, rows, :].astype(F32) * _rep(cole_s[prev, c, h], wide)
            ux.append(_dot_tn(kw.astype(BF16), v_s[prev, h, rows, :]))
            ks.append(jnp.sum(kw, axis=0, keepdims=True))
            qn.append(lane_rep(jnp.sum(q_s[prev, h, rows, :].astype(F32) * nst_s[h],
                                       axis=1, keepdims=True)))
        return qk, states, qc, ux, ks, qn

    def recur_finish(c, qk, states, qc, ux, ks, qn):
        rows = slice(c * L, (c + 1) * L)
        sv, rs = [], []
        for h, hc in enumerate(heads):
            s_loc = qk[h] * e_s[prev, c, h]
            sv.append(_dot(s_loc.astype(BF16), v_s[prev, h, rows, :]))
            rs.append(lane_rep(jnp.sum(s_loc, axis=1, keepdims=True)))
        for h, hc in enumerate(heads):
            m_prev = mst_s[h]
            m_loc = colm_s[prev, c, h]
            inter = colb_s[prev, c, h] + m_prev
            m_t = jnp.maximum(inter, m_loc)
            w_inter = jnp.exp(inter - m_t)
            w_loc = jnp.exp(m_loc - m_t)
            den = w_inter * qn[h] + w_loc * rs[h]
            inv = 1.0 / jnp.maximum(jnp.abs(den), jnp.exp(-m_t))
            hh = (_rep(w_inter * inv, wide) * qc[h] + _rep(w_loc * inv, wide) * sv[h])
            hm_ref[rows, hc] = (_head_norm(hh) * og_s[prev, h, rows, :]).astype(BF16)
            m_new = m_t[L - 1:L, :]
            decay = _rep(jnp.exp(inter[L - 1:L, :] - m_new), wide)
            grow = _rep(jnp.exp(m_loc[L - 1:L, :] - m_new), wide)
            cst_s[h] = decay * states[h] + grow * ux[h]
            nst_s[h] = decay * nst_s[h] + grow * ks[h]
            mst_s[h] = m_new

    pieces = [(part, split) for part in range(4) for split in range(n_split)]
    n_phases = 2 * n_chunks
    done = 0
    for c in range(n_chunks):
        for phase in (2 * c, 2 * c + 1):
            if phase % 2 == 0:
                partial = recur_ready(c)
            else:
                recur_finish(c, *partial)
            upto = (phase + 1) * len(pieces) // n_phases
            for part, split in pieces[done:upto]:
                project(part, split)
            done = upto


def _mlstm_prompt_kernel(*refs, tiles_per_seq, n_cast, n_xpose):
    n_side = n_cast + n_xpose
    x_ref, wqkvo_ref, wgt_ref, bgc_ref, mhg_ref = refs[:5]
    side_src = refs[5:5 + n_side]
    hm_ref, c_ref, n_ref, m_ref = refs[5 + n_side:9 + n_side]
    side_dst = refs[9 + n_side:9 + 2 * n_side]
    scratch = refs[9 + 2 * n_side:]
    q_s, k_s, v_s, og_s, e_s, colb_s, colm_s, cole_s, cst_s, nst_s, mst_s = scratch
    g_step = pl.program_id(0)

    def side_jobs():
        for i, (src, dst) in enumerate(zip(side_src, side_dst)):
            dst[...] = (src[...] if i < n_cast else src[...].T).astype(BF16)

    @pl.when(g_step == 0)
    def _():
        for ref in (q_s, k_s, v_s, og_s, e_s, colb_s, colm_s, cole_s):
            ref[1] = jnp.zeros(ref.shape[1:], ref.dtype)

    @pl.when(jnp.logical_or(g_step == 0, (g_step - 1) % tiles_per_seq == 0))
    def _():
        for ref in (cst_s, nst_s, mst_s):
            ref[...] = jnp.zeros(ref.shape, F32)

    for parity in (0, 1):
        @pl.when(g_step % 2 == parity)
        def _():
            side_jobs()
            _mlstm_prompt_step(x_ref, wqkvo_ref, wgt_ref, bgc_ref, mhg_ref, hm_ref, *scratch,
                               cur=parity, prev=1 - parity)

    @pl.when(jnp.logical_and(g_step >= 1, (g_step - 1) % tiles_per_seq == tiles_per_seq - 1))
    def _():
        for h in range(N_HEADS):
            c_ref[0, h] = cst_s[h]
            n_ref[0, h:h + 1, :] = nst_s[h]
            m_ref[0, h] = mst_s[h][:, 0:1]


XPOSE_ROWS = 256


def _mlstm_prompt(x, wqkvo, wgt, bgc, mhg, f32_weights, w_in_t):
    bsz, t, d = x.shape
    dh = d // N_HEADS
    tm = MLSTM_TILE
    nc = tm // CHUNK
    tps = t // tm
    n_tiles = bsz * tps
    seq_block = lambda g: (jnp.maximum(g - 1, 0) // tps, 0, 0, 0)
    cast_specs = []
    for w in f32_weights:
        rows_per_step = w.shape[0] // n_tiles
        assert rows_per_step * n_tiles == w.shape[0] and rows_per_step % 16 == 0
        cast_specs.append(pl.BlockSpec((rows_per_step, w.shape[1]),
                                       lambda g: (jnp.minimum(g, n_tiles - 1), 0)))
    n_cast = len(f32_weights)
    r = XPOSE_ROWS
    n_bch, n_gate = 3 * d // r, 2 * d // r
    off_gate = 7 * d + 2 * N_HEADS
    assert n_bch + n_gate <= n_tiles + 1 and d % r == 0 and off_gate % 8 == 0
    bch_job = lambda g: jnp.minimum(g, n_bch - 1)
    gate_job = lambda g: jnp.clip(g - n_bch, 0, n_gate - 1)
    xpose_src = [pl.BlockSpec((r, d), lambda g: (bch_job(g), 0)),
                 pl.BlockSpec((pl.Element(r), pl.Element(d)),
                              lambda g: (pl.multiple_of(off_gate + gate_job(g) * r, 8), 0))]
    xpose_dst = [pl.BlockSpec((d, r), lambda g: (0, bch_job(g))),
                 pl.BlockSpec((d, r), lambda g: (0, gate_job(g)))]
    hm, c, n, m, *side = pl.pallas_call(
        functools.partial(_mlstm_prompt_kernel, tiles_per_seq=tps, n_cast=n_cast, n_xpose=2),
        grid=(n_tiles + 1,),
        in_specs=[pl.BlockSpec((tm, d), lambda g: (jnp.minimum(g, n_tiles - 1), 0)),
                  _qkvo_spec(d), _const_spec(wgt.shape), _const_spec(bgc.shape),
                  _const_spec(mhg.shape)] + cast_specs + xpose_src,
        out_specs=[pl.BlockSpec((tm, d), lambda g: (jnp.maximum(g - 1, 0), 0)),
                   pl.BlockSpec((1, N_HEADS, dh, dh), seq_block),
                   pl.BlockSpec((1, N_HEADS, dh), lambda g: seq_block(g)[:3]),
                   pl.BlockSpec((1, N_HEADS, 1, 1), seq_block)] + cast_specs + xpose_dst,
        out_shape=[jax.ShapeDtypeStruct((bsz * t, d), BF16),
                   jax.ShapeDtypeStruct((bsz, N_HEADS, dh, dh), F32),
                   jax.ShapeDtypeStruct((bsz, N_HEADS, dh), F32),
                   jax.ShapeDtypeStruct((bsz, N_HEADS, 1, 1), F32)]
                  + [jax.ShapeDtypeStruct(w.shape, BF16) for w in f32_weights]
                  + [jax.ShapeDtypeStruct((d, 3 * d), BF16),
                     jax.ShapeDtypeStruct((d, 2 * d), BF16)],
        scratch_shapes=[
            pltpu.VMEM((2, N_HEADS, tm, dh), BF16),
            pltpu.VMEM((2, N_HEADS, tm, dh), BF16),
            pltpu.VMEM((2, N_HEADS, tm, dh), BF16),
            pltpu.VMEM((2, N_HEADS, tm, dh), F32),
            pltpu.VMEM((2, nc, N_HEADS, CHUNK, CHUNK), F32),
            pltpu.VMEM((2, nc, N_HEADS, CHUNK, LANES), F32),
            pltpu.VMEM((2, nc, N_HEADS, CHUNK, LANES), F32),
            pltpu.VMEM((2, nc, N_HEADS, CHUNK, LANES), F32),
            pltpu.VMEM((N_HEADS, dh, dh), F32),
            pltpu.VMEM((N_HEADS, 1, dh), F32),
            pltpu.VMEM((N_HEADS, 1, LANES), F32)],
        compiler_params=_params(1),
        name="mlstm_prompt",
    )(x.reshape(bsz * t, d), wqkvo, wgt, bgc, mhg, *f32_weights, w_in_t, w_in_t)
    casted, (w_bch, w_gate) = side[:n_cast], side[n_cast:]
    return hm, c, n, m.reshape(bsz, N_HEADS), casted, w_bch, w_gate


def _mlstm_sample_pre_kernel(x_ref, wq_ref, wk_ref, wv_ref, wo_ref, wgc_ref, bgr_ref, mhg_ref,
                             n0_ref, m0_ref,
                             q_ref, kw_ref, v_ref, og_ref, dec_ref, wa_ref, wb_ref, n_ref, m_ref,
                             g_s, *, n_steps):
    rows, dh = q_ref.shape
    nb = rows // n_steps
    T = n_steps
    hd = pl.program_id(0)
    slab = [slice(t * nb, (t + 1) * nb) for t in range(T)]
    head_lane = lax.broadcasted_iota(jnp.int32, m_ref.shape, 1)
    gate_lane = lax.broadcasted_iota(jnp.int32, (nb, g_s.shape[1]), 1)

    x = _time_major(x_ref)
    xh = x.astype(BF16)

    @pl.when(hd == 0)
    def _():
        xl = (x - xh.astype(F32)).astype(BF16)
        g = _dot(xh, wgc_ref[0]) + _dot(xl, wgc_ref[0]) + _dot(xh, wgc_ref[1]) + bgr_ref[...]
        is_input_gate = lax.broadcasted_iota(jnp.int32, g.shape, 1) < N_HEADS
        g_s[...] = jnp.where(is_input_gate, g, _log_sigmoid(g))
        m_ref[...] = jnp.zeros(m_ref.shape, F32)

    def gate_column(t, col):
        return jnp.sum(jnp.where(gate_lane == col, g_s[slab[t], :], 0.0), axis=1, keepdims=True)

    q_all = _dot(xh, wq_ref[...]).astype(BF16).astype(F32)
    k_all = (_dot(xh, wk_ref[...]) * (dh ** -0.5)).astype(BF16).astype(F32)
    v_all = _dot(xh, wv_ref[...]).astype(BF16).astype(F32)
    og_ref[...] = mhg_ref[...] * _sigmoid(_dot(xh, wo_ref[...]))
    q_ref[...] = q_all
    v_ref[...] = v_all
    qf = [q_all[slab[t]] for t in range(T)]
    kf = [k_all[slab[t]] for t in range(T)]
    vf = [v_all[slab[t]] for t in range(T)]
    li = [gate_column(t, hd) for t in range(T)]
    lf = [gate_column(t, N_HEADS + hd) for t in range(T)]
    m0 = jnp.sum(jnp.where(head_lane == hd, m0_ref[...], 0.0), axis=1, keepdims=True)
    n0 = n0_ref[:, hd, :]
    b = [lf[0]]
    for t in range(1, T):
        b.append(b[t - 1] + lf[t])
    a = [li[t] - b[t] for t in range(T)]
    m_new = None
    for t in range(T):
        inter = b[t] + m0
        m_t = inter
        for s in range(t + 1):
            m_t = jnp.maximum(m_t, b[t] + a[s])
        w_inter = jnp.exp(inter - m_t)
        num = jnp.zeros((nb, dh), F32)
        den = w_inter * jnp.sum(qf[t] * n0, axis=1, keepdims=True)
        for s in range(t + 1):
            s_w = (jnp.sum(qf[t] * kf[s], axis=1, keepdims=True)
                   * jnp.exp(b[t] + a[s] - m_t))
            num = num + s_w * vf[s]
            den = den + s_w
        inv = 1.0 / jnp.maximum(jnp.abs(den), jnp.exp(-m_t))
        wa_ref[slab[t], :] = jnp.broadcast_to(w_inter * inv, (nb, dh))
        wb_ref[slab[t], :] = num * inv
        m_new = m_t
    b_last = b[T - 1]
    decay = jnp.exp(b_last + m0 - m_new)
    n_new = decay * n0
    for s in range(T):
        kw = kf[s] * jnp.exp(a[s] + b_last - m_new)
        kw_ref[slab[s], :] = kw
        n_new = n_new + kw
    n_ref[:, hd, :] = n_new
    m_ref[...] = jnp.where(head_lane == hd, m_new, m_ref[...])
    dec_ref[...] = jnp.broadcast_to(decay, (nb, dh))


def _mlstm_sample_pre(xs, wqkvo, wgc, bgr, mhg, n0, m0):
    nb, n_steps, d = xs.shape
    rows = nb * n_steps
    dh = d // N_HEADS
    head_w = lambda part: pl.BlockSpec((d, dh), lambda h: (0, part * N_HEADS + h))
    head_cols = lambda n_rows: pl.BlockSpec((n_rows, dh), lambda h: (0, h))
    big = jax.ShapeDtypeStruct((rows, d), F32)
    small = jax.ShapeDtypeStruct((nb, d), F32)
    n_shape = (nb, N_HEADS, dh)
    return pl.pallas_call(
        functools.partial(_mlstm_sample_pre_kernel, n_steps=n_steps),
        grid=(N_HEADS,),
        in_specs=[_const_spec(xs.shape), head_w(0), head_w(1), head_w(2), head_w(3),
                  _const_spec(wgc.shape), _const_spec(bgr.shape), head_cols(1),
                  _const_spec(n_shape), _const_spec(m0.shape)],
        out_specs=[head_cols(rows)] * 4 + [head_cols(nb)] + [head_cols(rows)] * 2
                  + [pl.BlockSpec(n_shape, lambda h: (0, 0, 0)),
                     pl.BlockSpec((nb, N_HEADS), lambda h: (0, 0))],
        out_shape=[big, big, big, big, small, big, big, jax.ShapeDtypeStruct(n_shape, F32),
                   jax.ShapeDtypeStruct((nb, N_HEADS), F32)],
        scratch_shapes=[pltpu.VMEM((rows, GATE_PAD), F32)],
        compiler_params=_params(1),
        name="mlstm_sample_pre",
    )(xs, wqkvo, wqkvo, wqkvo, wqkvo, wgc, bgr, mhg, n0, m0)


def _stream_matrix_memory(c0_ref, q_ref, kw_ref, v_ref, dec_ref, c_ref, qc_ref):
    T, bb, dh = q_ref.shape
    owner = lax.broadcasted_iota(jnp.int32, (T * bb, dh), 0) % bb
    q_blk = jnp.concatenate([q_ref[t] for t in range(T)], axis=0).astype(BF16)
    kw_blk = jnp.concatenate([kw_ref[t] for t in range(T)], axis=0).astype(BF16)
    v_blk = jnp.concatenate([v_ref[t] for t in range(T)], axis=0)
    c_olds = [c0_ref[bi, 0] for bi in range(bb)]
    reads = [_dot(q_blk, c_olds[bi].astype(BF16)) for bi in range(bb)]
    qc = jnp.zeros((T * bb, dh), F32)
    for bi in range(bb):
        qc = jnp.where(owner == bi, reads[bi], qc)
    for t in range(T):
        qc_ref[t] = qc[t * bb:(t + 1) * bb]
    for bi in range(bb):
        upd = _dot_tn(kw_blk, jnp.where(owner == bi, v_blk, 0.0).astype(BF16))
        c_ref[bi, 0] = dec_ref[bi:bi + 1, :] * c_olds[bi] + upd


def _merge_sample_kernel(x_ref, m1_ref, wa_ref, wb_ref, qc_ref, og_ref, wmo_ref, wgm_ref, wo_ref,
                         g_ref, b_ref, o_ref, hm_s, *, alpha):
    d = x_ref.shape[-1]
    dh = d // N_HEADS
    for hd in range(N_HEADS):
        hc = slice(hd * dh, (hd + 1) * dh)
        hh = wa_ref[:, hc] * qc_ref[:, hc] + wb_ref[:, hc]
        hm_s[:, hc] = (_head_norm(hh) * og_ref[:, hc]).astype(BF16)
    x = _time_major(x_ref)
    gate = _dot(x.astype(BF16), wgm_ref[...])
    ym = _dot(hm_s[...], wmo_ref[...])
    merged = m1_ref[...] + _sigmoid(gate) * ym
    r = alpha * x + _dot(merged.astype(BF16), wo_ref[...])
    o_ref[...] = _layer_norm(r, g_ref[...], b_ref[...])


def _merge_sample(xs, m1, wa, wb, qc, og, wmo, w_b, wo, g, b, alpha):
    n, d = m1.shape
    whole = _const_spec((n, d))
    return pl.pallas_call(
        functools.partial(_merge_sample_kernel, alpha=alpha),
        grid=(1,),
        in_specs=[_const_spec(xs.shape)] + [whole] * 5
                 + [_const_spec(wmo.shape), _gate_spec(d, 1), _const_spec(wo.shape),
                    _const_spec(g.shape), _const_spec(b.shape)],
        out_specs=pl.BlockSpec((n, d), lambda i: (0, 0)),
        out_shape=jax.ShapeDtypeStruct((n, d), F32),
        scratch_shapes=[pltpu.VMEM((n, d), BF16)],
        compiler_params=_params(1),
        name="merge_ln1_sample",
    )(xs, m1, wa, wb, qc, og, wmo, w_b, wo, g, b)


def _ffn_stripe(dff):
    return 1024 if dff % 1024 == 0 else dff


def _ffn_hidden(xb, w1_ref, hid_s, rows, before_stripe=None):
    dff = hid_s.shape[1]
    sw = _ffn_stripe(dff)
    for s in range(dff // sw):
        cs = slice(s * sw, (s + 1) * sw)
        if before_stripe is not None:
            before_stripe(s)
        hid = jnp.maximum(_dot(xb, w1_ref[:, cs]), 0.0)
        hid_s[rows, cs] = (hid * hid).astype(BF16)


def _ffn_small_kernel(x_ref, w1_ref, w2_ref, g_ref, b_ref, o_ref, acc_s, *, alpha):
    j = pl.program_id(0)
    x = x_ref[...]
    hid = jnp.maximum(_dot(x.astype(BF16), w1_ref[...]), 0.0)
    part = _dot((hid * hid).astype(BF16), w2_ref[...])

    @pl.when(j == 0)
    def _():
        acc_s[...] = alpha * x + part

    @pl.when(j > 0)
    def _():
        acc_s[...] += part

    @pl.when(j == pl.num_programs(0) - 1)
    def _():
        y = _layer_norm(acc_s[...], g_ref[...], b_ref[...])
        nb, n_t, _ = o_ref.shape
        for t in range(n_t):
            o_ref[:, t, :] = y[t * nb:(t + 1) * nb]


def _ffn_stream_kernel(x_ref, w1_hbm, w2_hbm, g_ref, b_ref, c0_ref, q_ref, kw_ref, v_ref,
                       dec_ref, o_ref, c_ref, qc_ref, w1_s, w2_s, sem, hid_s, r_s, *, alpha):
    s = pl.program_id(0)
    last = pl.num_programs(0) - 1
    sw = _ffn_stripe(w1_s.shape[1])
    n_w1 = w1_s.shape[1] // sw

    def w1_copy(k):
        cs = pl.ds(k * sw, sw)
        return pltpu.make_async_copy(w1_hbm.at[:, cs], w1_s.at[:, cs], sem.at[k])

    w2_copy = pltpu.make_async_copy(w2_hbm, w2_s, sem.at[n_w1])

    def step(first):
        _stream_matrix_memory(c0_ref, q_ref, kw_ref, v_ref, dec_ref, c_ref, qc_ref)
        x = x_ref[...]
        _ffn_hidden(x.astype(BF16), w1_s, hid_s, slice(None),
                    before_stripe=(lambda k: w1_copy(k).wait()) if first else None)
        o_ref[...] = _layer_norm(r_s[...], g_ref[...], b_ref[...])
        if first:
            w2_copy.wait()
        r_s[...] = alpha * x + _dot(hid_s[...], w2_s[...])

    @pl.when(s == 0)
    def _():
        for k in range(n_w1):
            w1_copy(k).start()
        w2_copy.start()
        r_s[...] = jnp.zeros(r_s.shape, F32)
        step(True)

    @pl.when((s > 0) & (s < last))
    def _():
        step(False)

    @pl.when(s == last)
    def _():
        o_ref[...] = _layer_norm(r_s[...], g_ref[...], b_ref[...])


def _ffn_small(x, w1, w2, g, b, alpha, n_steps):
    n, d = x.shape
    dff = w1.shape[1]
    sw = 1024 if dff % 1024 == 0 else dff
    out_shape = (n // n_steps, n_steps, d)
    return pl.pallas_call(
        functools.partial(_ffn_small_kernel, alpha=alpha),
        grid=(dff // sw,),
        in_specs=[_const_spec((n, d)), pl.BlockSpec((d, sw), lambda j: (0, j)),
                  pl.BlockSpec((sw, d), lambda j: (j, 0)), _const_spec(g.shape),
                  _const_spec(b.shape)],
        out_specs=pl.BlockSpec(out_shape, lambda j: (0,) * len(out_shape)),
        out_shape=jax.ShapeDtypeStruct(out_shape, F32),
        scratch_shapes=[pltpu.VMEM((n, d), F32)],
        compiler_params=_params(1),
        name="ffn_ln2_small",
    )(x, w1, w2, g, b)


def _ffn_with_stream(x, w1, w2, g, b, alpha, c0, q, kw, v, dec, n_steps):
    n, d = x.shape
    rows = q.shape[0]
    nb = rows // n_steps
    dh = d // N_HEADS
    bb = SAMPLE_BATCH_BLOCK
    n_grid = (nb // bb) * N_HEADS
    tm = n // n_grid
    assert tm * n_grid == n and tm % 8 == 0 and nb % bb == 0
    cur = lambda s: jnp.minimum(s, n_grid - 1)
    x_tile = pl.BlockSpec((tm, d), lambda s: (cur(s), 0))
    o_tile = pl.BlockSpec((tm, d), lambda s: (jnp.maximum(s - 1, 0), 0))
    cblock = pl.BlockSpec((bb, 1, dh, dh),
                          lambda s: (cur(s) // N_HEADS, cur(s) % N_HEADS, 0, 0))
    tblock = pl.BlockSpec((n_steps, bb, dh), lambda s: (0, cur(s) // N_HEADS, cur(s) % N_HEADS))
    dblock = pl.BlockSpec((bb, dh), lambda s: (cur(s) // N_HEADS, cur(s) % N_HEADS))
    as_tbd = lambda a: a.reshape(n_steps, nb, d)
    x2, c_new, qc = pl.pallas_call(
        functools.partial(_ffn_stream_kernel, alpha=alpha),
        grid=(n_grid + 1,),
        in_specs=[x_tile, pl.BlockSpec(memory_space=pl.ANY), pl.BlockSpec(memory_space=pl.ANY),
                  _const_spec(g.shape), _const_spec(b.shape), cblock, tblock, tblock, tblock,
                  dblock],
        out_specs=[o_tile, cblock, tblock],
        out_shape=[jax.ShapeDtypeStruct((n, d), F32), jax.ShapeDtypeStruct(c0.shape, F32),
                   jax.ShapeDtypeStruct((n_steps, nb, d), F32)],
        scratch_shapes=[pltpu.VMEM(w1.shape, BF16), pltpu.VMEM(w2.shape, BF16),
                        pltpu.SemaphoreType.DMA((w1.shape[1] // _ffn_stripe(w1.shape[1]) + 1,)),
                        pltpu.VMEM((tm, w1.shape[1]), BF16), pltpu.VMEM((tm, d), F32)],
        compiler_params=_params(1),
        name="ffn_ln2_stream",
    )(x, w1, w2, g, b, c0, as_tbd(q), as_tbd(kw), as_tbd(v), dec)
    return x2, c_new, qc.reshape(rows, d)


def _w_in_prep_kernel(a_ref, g_ref, wa_ref, wg_ref):
    @pl.when(pl.program_id(0) == 0)
    def _():
        wg_ref[...] = g_ref[...]

    wa_ref[...] = a_ref[...].T.astype(BF16)


def _prepare_w_in(w_in_t):
    d = w_in_t.shape[1]
    h2 = 2 * N_HEADS
    return pl.pallas_call(
        _w_in_prep_kernel,
        grid=(4,),
        in_specs=[pl.BlockSpec((d, d), lambda j: (3 + j, 0)),
                  pl.BlockSpec((pl.Element(h2), pl.Element(d)), lambda j: (7 * d, 0))],
        out_specs=[pl.BlockSpec((d, d), lambda j: (0, j)),
                   pl.BlockSpec((h2, d), lambda j: (0, 0))],
        out_shape=[jax.ShapeDtypeStruct((d, 4 * d), BF16),
                   jax.ShapeDtypeStruct((h2, d), F32)],
        compiler_params=_params(1),
        name="w_in_prep",
    )(w_in_t, w_in_t)


def _layer_weights(w_in, b_gate, conv_w, w_conv_out, mh_g, w_m_out, w_o, ln1_g, ln1_b,
                   w_ff1, w_ff2, ln2_g, ln2_b):
    d = w_in.shape[0]
    h2 = 2 * N_HEADS
    w_in_t = jnp.swapaxes(w_in, 0, 1)
    w_qkvo, wg_t = _prepare_w_in(w_in_t)
    wgt_hi, wgt_lo = _split_bf16(wg_t)
    wgt = jnp.concatenate([wgt_hi, wgt_lo], axis=0)
    pad = ((0, 0), (0, GATE_PAD - h2))
    wgc = jnp.stack([jnp.pad(wgt_hi.T, pad), jnp.pad(wgt_lo.T, pad)])
    return dict(
        w_in_t=w_in_t, w_qkvo=w_qkvo,
        wgt=wgt, wgc=wgc,
        bgc=b_gate.reshape(h2, 1).astype(F32),
        bgr=jnp.pad(b_gate.reshape(1, h2).astype(F32), pad),
        cw=conv_w.astype(F32),
        mhg=mh_g.reshape(1, d).astype(F32),
        f32_weights=[w_conv_out, w_m_out, w_o],
        ffn_weights=(w_ff1, w_ff2),
        ln1_g=ln1_g.reshape(1, d), ln1_b=ln1_b.reshape(1, d),
        ln2_g=ln2_g.reshape(1, d), ln2_b=ln2_b.reshape(1, d))


def _layer(x, xs, conv_buf, c0, n0, m0, p, alpha):
    bsz, t, d = x.shape
    n_steps = xs.shape[1]
    hm, c_p, n_p, m_p, (wco, wmo, wo), w_bch, w_gate = _mlstm_prompt(
        x, p["w_qkvo"], p["wgt"], p["bgc"], p["mhg"], p["f32_weights"], p["w_in_t"])
    x1, conv_p, w1, w2 = _conv_merge_prompt(x, hm, w_bch, p["cw"], wco, w_gate, wmo, wo,
                                            p["ln1_g"], p["ln1_b"], alpha, *p["ffn_weights"])
    m1_s, conv_s = _conv_branch_sample(xs, conv_buf, w_bch, p["cw"], wco, w_gate)
    q, kw, v, og, dec, wa, wb, n_s, m_s = _mlstm_sample_pre(
        xs, p["w_qkvo"], p["wgc"], p["bgr"], p["mhg"], n0, m0)
    x2, c_s, qc = _ffn_with_stream(x1, w1, w2, p["ln2_g"], p["ln2_b"], alpha,
                                   c0, q, kw, v, dec, n_steps)
    x1_s = _merge_sample(xs, m1_s, wa, wb, qc, og, wmo, w_gate, wo, p["ln1_g"], p["ln1_b"],
                         alpha)
    x2_s = _ffn_small(x1_s, w1, w2, p["ln2_g"], p["ln2_b"], alpha, n_steps)
    return (x2.reshape(bsz, t, d), conv_p, c_p, n_p, m_p,
            x2_s, conv_s, c_s, n_s, m_s)


def kernel(x_prompt, x_sample, state_conv, state_C, state_n, state_m, w_in, b_gate, conv_w,
           w_conv_out, mh_g, w_m_out, w_o, ln1_g, ln1_b, w_ff1, w_ff2, ln2_g, ln2_b):
    depth = w_in.shape[0]
    alpha = (2.0 * depth) ** 0.25
    bsz, t, d = x_prompt.shape
    sb, st, _ = x_sample.shape
    assert t % TOKEN_TILE == 0 and t % MLSTM_TILE == 0 and MLSTM_TILE % CHUNK == 0
    assert d % N_HEADS == 0
    assert sb % SAMPLE_BATCH_BLOCK == 0 and st >= CONV_W - 1

    xp, xs = x_prompt, x_sample
    outs = [[] for _ in range(8)]
    for l in range(depth):
        p = _layer_weights(w_in[l], b_gate[l], conv_w[l], w_conv_out[l], mh_g[l], w_m_out[l],
                           w_o[l], ln1_g[l], ln1_b[l], w_ff1[l], w_ff2[l], ln2_g[l], ln2_b[l])
        xp, cp, c_p, n_p, m_p, xs, cs, c_s, n_s, m_s = _layer(
            xp, xs, state_conv[l], state_C[l], state_n[l], state_m[l], p, alpha)
        for acc, val in zip(outs, (cp, cs, c_p, c_s, n_p, n_s, m_p, m_s)):
            acc.append(val)
    return (xp, xs) + tuple(jnp.stack(acc) for acc in outs)
```

```python
import functools

import jax
import jax.numpy as jnp
from jax import lax
from jax.experimental import pallas as pl
from jax.experimental.pallas import tpu as pltpu

F32 = jnp.float32
BF16 = jnp.bfloat16

LN_EPS = 1e-5
N_HEADS = 4
CHUNK = 128
CONV_W = 3
TOKEN_TILE = 1024
SUB_TILE = 512
MLSTM_TILE = 512
SAMPLE_BATCH_BLOCK = 16
LANES = 128
GATE_PAD = LANES
VMEM_LIMIT = 56 * 1024 * 1024


def _dot(a, b):
    return jnp.dot(a, b, preferred_element_type=F32)


def _dot_nt(a, b):
    return lax.dot_general(a, b, (((1,), (1,)), ((), ())), preferred_element_type=F32)


def _dot_tn(a, b):
    return lax.dot_general(a, b, (((0,), (0,)), ((), ())), preferred_element_type=F32)


def _sigmoid(x):
    return 1.0 / (1.0 + jnp.exp(-x))


def _log_sigmoid(x):
    return jnp.minimum(x, 0.0) - jnp.log1p(jnp.exp(-jnp.abs(x)))


def _split_bf16(x):
    hi = x.astype(BF16)
    lo = (x - hi.astype(F32)).astype(BF16)
    return hi, lo


def _layer_norm(r, g, b):
    mu = jnp.mean(r, axis=-1, keepdims=True)
    xc = r - mu
    var = jnp.mean(xc * xc, axis=-1, keepdims=True)
    return xc * lax.rsqrt(var + LN_EPS) * g + b


def _head_norm(h):
    mu = jnp.mean(h, axis=-1, keepdims=True)
    hc = h - mu
    return hc * lax.rsqrt(jnp.mean(hc * hc, axis=-1, keepdims=True) + LN_EPS)


def _const_spec(shape):
    zeros = (0,) * len(shape)
    return pl.BlockSpec(shape, lambda *_: zeros, pipeline_mode=pl.Buffered(1))


def _window_spec(block_shape, index):
    return pl.BlockSpec(block_shape, lambda *_: index, pipeline_mode=pl.Buffered(1))


def _bch_spec(d):
    return _const_spec((d, 3 * d))


def _qkvo_spec(d):
    return _const_spec((d, 4 * d))


def _gate_spec(d, which):
    return _window_spec((d, d), (0, which))


def _params(n_axes):
    return pltpu.CompilerParams(dimension_semantics=("arbitrary",) * n_axes,
                                vmem_limit_bytes=VMEM_LIMIT)


def _conv_stripes(d):
    sw = 256 if d % 256 == 0 else d
    return [slice(s * sw, (s + 1) * sw) for s in range(d // sw)]


def _conv_merge_prompt_kernel(x_ref, hm_ref, wbch_hbm, cw_ref, wco_hbm, wg_hbm, wmo_hbm, wo_hbm,
                              g_ref, b_ref, w1_ref, w2_ref, o_ref, cs_ref, w1b_ref, w2b_ref,
                              wbch_s, wco_s, wg_s, wmo_s, wo_s, sem, u_s, a_s, mg_s, *, alpha):
    tm, d = a_s.shape
    stripes = _conv_stripes(d)

    def bch_copies(i):
        cs = stripes[i]
        width = cs.stop - cs.start
        return [pltpu.make_async_copy(wbch_hbm.at[:, pl.ds(part * d + cs.start, width)],
                                      wbch_s.at[:, pl.ds(part * d + cs.start, width)],
                                      sem.at[3 * i + part]) for part in (1, 2, 0)]

    n_bch = 3 * len(stripes)
    merge_copies = [pltpu.make_async_copy(src, dst, sem.at[n_bch + i]) for i, (src, dst) in
                    enumerate(((wg_hbm, wg_s), (wmo_hbm, wmo_s), (wco_hbm, wco_s)))]
    wo_copy = pltpu.make_async_copy(wo_hbm, wo_s, sem.at[n_bch + 3])

    def body(first):
        w1b_ref[...] = w1_ref[...].astype(BF16)
        w2b_ref[...] = w2_ref[...].astype(BF16)
        for r0 in range(0, tm, SUB_TILE):
            wait_here = first and r0 == 0
            sub = min(SUB_TILE, tm - r0)
            x = x_ref[0, r0:r0 + sub, :]
            rows = slice(r0, r0 + sub)
            xb = x.astype(BF16)
            for i, cs in enumerate(stripes):
                off = cs.start
                if wait_here:
                    for copy in bch_copies(i):
                        copy.wait()
                cg = _dot(xb, wbch_s[:, d + off:d + cs.stop])
                hc = _dot(xb, wbch_s[:, 2 * d + off:2 * d + cs.stop])
                u = cg * hc
                u_s[8 + r0:8 + r0 + sub, cs] = u
                conv = (u_s[6 + r0:6 + r0 + sub, cs] * cw_ref[0:1, cs]
                        + u_s[7 + r0:7 + r0 + sub, cs] * cw_ref[1:2, cs] + u * cw_ref[2:3, cs])
                bg = _dot(xb, wbch_s[:, off:cs.stop])
                a_s[rows, cs] = (bg * conv).astype(BF16)
            hm = hm_ref[rows, :]
            if wait_here:
                for copy in merge_copies:
                    copy.wait()
            for cs in stripes:
                gm_cols = slice(d + cs.start, d + cs.stop)
                gated_m = _sigmoid(_dot(xb, wg_s[:, gm_cols])) * _dot(hm, wmo_s[:, cs])
                gated_c = _sigmoid(_dot(xb, wg_s[:, cs])) * _dot(a_s[rows, :], wco_s[:, cs])
                mg_s[rows, cs] = (gated_c + gated_m).astype(BF16)
            if wait_here:
                wo_copy.wait()
            r = alpha * x + _dot(mg_s[rows, :], wo_s[...])
            o_ref[rows, :] = _layer_norm(r, g_ref[...], b_ref[...])
        cs_ref[0] = u_s[tm + 6:tm + 8, :]
        u_s[0:8, :] = u_s[tm:tm + 8, :]

    @pl.when(pl.program_id(1) == 0)
    def _():
        u_s[0:8, :] = jnp.zeros((8, d), F32)

    first_step = (pl.program_id(0) == 0) & (pl.program_id(1) == 0)

    @pl.when(first_step)
    def _():
        for i in range(len(stripes)):
            for copy in bch_copies(i):
                copy.start()
        for copy in merge_copies + [wo_copy]:
            copy.start()
        body(True)

    @pl.when(jnp.logical_not(first_step))
    def _():
        body(False)


def _time_major(x_ref):
    return jnp.concatenate([x_ref[:, t, :] for t in range(x_ref.shape[1])], axis=0)


def _conv_sample_kernel(x_ref, st_ref, wbch_ref, cw_ref, wco_ref, wgc_ref, m1_ref, cs_ref, a_s):
    nb, n_steps, d = x_ref.shape
    xb = _time_major(x_ref).astype(BF16)
    for cs in _conv_stripes(d):
        off = cs.start
        bg = _dot(xb, wbch_ref[:, off:cs.stop])
        cg = _dot(xb, wbch_ref[:, d + off:d + cs.stop])
        hc = _dot(xb, wbch_ref[:, 2 * d + off:2 * d + cs.stop])
        u = cg * hc
        up = [st_ref[:, j, cs] for j in range(CONV_W - 1)]
        up += [u[t * nb:(t + 1) * nb] for t in range(n_steps)]
        for t in range(n_steps):
            conv = (up[t] * cw_ref[0:1, cs] + up[t + 1] * cw_ref[1:2, cs]
                    + up[t + 2] * cw_ref[2:3, cs])
            a_s[t * nb:(t + 1) * nb, cs] = (bg[t * nb:(t + 1) * nb] * conv).astype(BF16)
        for j in range(CONV_W - 1):
            cs_ref[:, j, cs] = up[n_steps + j]
    yc = _dot(a_s[...], wco_ref[...])
    m1_ref[...] = _sigmoid(_dot(xb, wgc_ref[...])) * yc


def _conv_merge_prompt(x, hm, w_a, cw, wco, w_b, wmo, wo, g, b, alpha, w1, w2):
    bsz, t, d = x.shape
    tm = TOKEN_TILE
    tps = t // tm
    n_tiles = bsz * tps
    flat = pl.BlockSpec((tm, d), lambda bi, j: (bi * tps + j, 0))
    in_hbm = pl.BlockSpec(memory_space=pl.ANY)
    cast_specs = []
    for w in (w1, w2):
        rows_per_step = w.shape[0] // n_tiles
        assert rows_per_step * n_tiles == w.shape[0] and rows_per_step % 16 == 0
        cast_specs.append(pl.BlockSpec((rows_per_step, w.shape[1]),
                                       lambda bi, j: (bi * tps + j, 0)))
    return pl.pallas_call(
        functools.partial(_conv_merge_prompt_kernel, alpha=alpha),
        grid=(bsz, tps),
        in_specs=[pl.BlockSpec((1, tm, d), lambda bi, j: (bi, j, 0)), flat,
                  in_hbm, _const_spec(cw.shape), in_hbm, in_hbm, in_hbm, in_hbm,
                  _const_spec(g.shape), _const_spec(b.shape)]
                 + cast_specs,
        out_specs=[flat, pl.BlockSpec((1, CONV_W - 1, d), lambda bi, j: (bi, 0, 0))]
                  + cast_specs,
        out_shape=[jax.ShapeDtypeStruct((bsz * t, d), F32),
                   jax.ShapeDtypeStruct((bsz, CONV_W - 1, d), F32),
                   jax.ShapeDtypeStruct(w1.shape, BF16), jax.ShapeDtypeStruct(w2.shape, BF16)],
        scratch_shapes=[pltpu.VMEM(w.shape, BF16) for w in (w_a, wco, w_b, wmo, wo)]
                       + [pltpu.SemaphoreType.DMA((3 * len(_conv_stripes(d)) + 4,)),
                          pltpu.VMEM((tm + 8, d), F32), pltpu.VMEM((tm, d), BF16),
                          pltpu.VMEM((tm, d), BF16)],
        compiler_params=_params(2),
        name="conv_merge_prompt",
    )(x, hm, w_a, cw, wco, w_b, wmo, wo, g, b, w1, w2)


def _conv_branch_sample(xs, conv_buf, wbch, cw, wco, wgc):
    nb, n_steps, d = xs.shape
    rows = nb * n_steps
    return pl.pallas_call(
        _conv_sample_kernel,
        grid=(1,),
        in_specs=[_const_spec(xs.shape), _const_spec(conv_buf.shape), _bch_spec(d),
                  _const_spec(cw.shape), _const_spec(wco.shape), _gate_spec(d, 0)],
        out_specs=[pl.BlockSpec((rows, d), lambda i: (0, 0)),
                   pl.BlockSpec(conv_buf.shape, lambda i: (0, 0, 0))],
        out_shape=[jax.ShapeDtypeStruct((rows, d), F32),
                   jax.ShapeDtypeStruct(conv_buf.shape, F32)],
        scratch_shapes=[pltpu.VMEM((rows, d), BF16)],
        compiler_params=_params(1),
        name="conv_branch_sample",
    )(xs, conv_buf, wbch, cw, wco, wgc)


def _rep(col, times):
    return col if times == 1 else jnp.concatenate([col] * times, axis=1)


def _mlstm_prompt_step(x_ref, wqkvo_ref, wgt_ref, bgc_ref, mhg_ref, hm_ref,
                       q_s, k_s, v_s, og_s, e_s, colb_s, colm_s, cole_s, cst_s, nst_s, mst_s,
                       *, cur, prev):
    _, _, tm, dh = q_s.shape
    d = N_HEADS * dh
    L = CHUNK
    LANES = colb_s.shape[-1]
    n_chunks = tm // L
    wide = dh // LANES

    xh = x_ref[...].astype(BF16)

    h2 = 2 * N_HEADS
    ga = _dot_nt(wgt_ref[...], xh)
    gt = ga[0:h2] + ga[h2:2 * h2] + bgc_ref[...]
    is_input_gate = lax.broadcasted_iota(jnp.int32, gt.shape, 0) < N_HEADS
    g = jnp.where(is_input_gate, gt, _log_sigmoid(gt))

    pos = lax.broadcasted_iota(jnp.int32, g.shape, 1) % L
    csum = g
    shift = 1
    while shift < L:
        csum = csum + jnp.where(pos >= shift, pltpu.roll(csum, shift, 1), 0.0)
        shift *= 2
    ba = jnp.concatenate([csum[N_HEADS:], g[:N_HEADS] - csum[N_HEADS:]], axis=0)

    ri = lax.broadcasted_iota(jnp.int32, (L, L), 0)
    ci = lax.broadcasted_iota(jnp.int32, (L, L), 1)
    causal = ri >= ci

    for c in range(n_chunks):
        slab = ba[:, c * L:(c + 1) * L]
        cols = slab.T
        for h in range(N_HEADS):
            b_rep = jnp.broadcast_to(cols[:, h:h + 1], (L, LANES))
            a_rep = jnp.broadcast_to(cols[:, N_HEADS + h:N_HEADS + h + 1], (L, LANES))
            a_row = slab[N_HEADS + h:N_HEADS + h + 1, :]
            dm = jnp.where(causal, _rep(b_rep, L // LANES) + a_row, -jnp.inf)
            m_loc = jnp.broadcast_to(jnp.max(dm, axis=1, keepdims=True), (L, LANES))
            e_s[cur, c, h] = jnp.exp(dm - _rep(m_loc, L // LANES))
            colb_s[cur, c, h] = b_rep
            colm_s[cur, c, h] = m_loc
            cole_s[cur, c, h] = jnp.exp(a_rep + b_rep[L - 1:L, :] - m_loc[L - 1:L, :])

    n_split = 2 if N_HEADS % 2 == 0 else 1
    piece = d // n_split

    def project(part, split):
        lo = split * piece
        y = _dot(xh, wqkvo_ref[:, part * d + lo:part * d + lo + piece])
        if part == 1:
            y = y * (dh ** -0.5)
        elif part == 3:
            y = mhg_ref[:, lo:lo + piece] * _sigmoid(y)
        dst = (q_s, k_s, v_s, og_s)[part]
        for h in range(lo // dh, (lo + piece) // dh):
            dst[cur, h] = y[:, h * dh - lo:(h + 1) * dh - lo].astype(dst.dtype)

    heads = [slice(h * dh, (h + 1) * dh) for h in range(N_HEADS)]

    def lane_rep(col):
        return jnp.broadcast_to(col, (col.shape[0], LANES))

    def recur_ready(c):
        rows = slice(c * L, (c + 1) * L)
        qk = [_dot_nt(q_s[prev, h, rows, :], k_s[prev, h, rows, :]) for h in range(N_HEADS)]
        states = [cst_s[h] for h in range(N_HEADS)]
        qc = [_dot(q_s[prev, h, rows, :], states[h].astype(BF16)) for h, hc in enumerate(heads)]
        ux, ks, qn = [], [], []
        for h, hc in enumerate(heads):
            kw = k_s[prev, h, rows, :].astype(F32) * _rep(cole_s[prev, c, h], wide)
            ux.append(_dot_tn(kw.astype(BF16), v_s[prev, h, rows, :]))
            ks.append(jnp.sum(kw, axis=0, keepdims=True))
            qn.append(lane_rep(jnp.sum(q_s[prev, h, rows, :].astype(F32) * nst_s[h],
                                       axis=1, keepdims=True)))
        return qk, states, qc, ux, ks, qn

    def recur_finish(c, qk, states, qc, ux, ks, qn):
        rows = slice(c * L, (c + 1) * L)
        sv, rs = [], []
        for h, hc in enumerate(heads):
            s_loc = qk[h] * e_s[prev, c, h]
            sv.append(_dot(s_loc.astype(BF16), v_s[prev, h, rows, :]))
            rs.append(lane_rep(jnp.sum(s_loc, axis=1, keepdims=True)))
        for h, hc in enumerate(heads):
            m_prev = mst_s[h]
            m_loc = colm_s[prev, c, h]
            inter = colb_s[prev, c, h] + m_prev
            m_t = jnp.maximum(inter, m_loc)
            w_inter = jnp.exp(inter - m_t)
            w_loc = jnp.exp(m_loc - m_t)
            den = w_inter * qn[h] + w_loc * rs[h]
            inv = 1.0 / jnp.maximum(jnp.abs(den), jnp.exp(-m_t))
            hh = (_rep(w_inter * inv, wide) * qc[h] + _rep(w_loc * inv, wide) * sv[h])
            hm_ref[rows, hc] = (_head_norm(hh) * og_s[prev, h, rows, :]).astype(BF16)
            m_new = m_t[L - 1:L, :]
            decay = _rep(jnp.exp(inter[L - 1:L, :] - m_new), wide)
            grow = _rep(jnp.exp(m_loc[L - 1:L, :] - m_new), wide)
            cst_s[h] = decay * states[h] + grow * ux[h]
            nst_s[h] = decay * nst_s[h] + grow * ks[h]
            mst_s[h] = m_new

    pieces = [(part, split) for part in range(4) for split in range(n_split)]
    n_phases = 2 * n_chunks
    done = 0
    for c in range(n_chunks):
        for phase in (2 * c, 2 * c + 1):
            if phase % 2 == 0:
                partial = recur_ready(c)
            else:
                recur_finish(c, *partial)
            upto = (phase + 1) * len(pieces) // n_phases
            for part, split in pieces[done:upto]:
                project(part, split)
            done = upto


def _mlstm_prompt_kernel(*refs, tiles_per_seq, n_cast, n_xpose):
    n_side = n_cast + n_xpose
    x_ref, wqkvo_ref, wgt_ref, bgc_ref, mhg_ref = refs[:5]
    side_src = refs[5:5 + n_side]
    hm_ref, c_ref, n_ref, m_ref = refs[5 + n_side:9 + n_side]
    side_dst = refs[9 + n_side:9 + 2 * n_side]
    scratch = refs[9 + 2 * n_side:]
    q_s, k_s, v_s, og_s, e_s, colb_s, colm_s, cole_s, cst_s, nst_s, mst_s = scratch
    g_step = pl.program_id(0)

    def side_jobs():
        for i, (src, dst) in enumerate(zip(side_src, side_dst)):
            dst[...] = (src[...] if i < n_cast else src[...].T).astype(BF16)

    @pl.when(g_step == 0)
    def _():
        for ref in (q_s, k_s, v_s, og_s, e_s, colb_s, colm_s, cole_s):
            ref[1] = jnp.zeros(ref.shape[1:], ref.dtype)

    @pl.when(jnp.logical_or(g_step == 0, (g_step - 1) % tiles_per_seq == 0))
    def _():
        for ref in (cst_s, nst_s, mst_s):
            ref[...] = jnp.zeros(ref.shape, F32)

    for parity in (0, 1):
        @pl.when(g_step % 2 == parity)
        def _():
            side_jobs()
            _mlstm_prompt_step(x_ref, wqkvo_ref, wgt_ref, bgc_ref, mhg_ref, hm_ref, *scratch,
                               cur=parity, prev=1 - parity)

    @pl.when(jnp.logical_and(g_step >= 1, (g_step - 1) % tiles_per_seq == tiles_per_seq - 1))
    def _():
        for h in range(N_HEADS):
            c_ref[0, h] = cst_s[h]
            n_ref[0, h:h + 1, :] = nst_s[h]
            m_ref[0, h] = mst_s[h][:, 0:1]


XPOSE_ROWS = 256


def _mlstm_prompt(x, wqkvo, wgt, bgc, mhg, f32_weights, w_in_t):
    bsz, t, d = x.shape
    dh = d // N_HEADS
    tm = MLSTM_TILE
    nc = tm // CHUNK
    tps = t // tm
    n_tiles = bsz * tps
    seq_block = lambda g: (jnp.maximum(g - 1, 0) // tps, 0, 0, 0)
    cast_specs = []
    for w in f32_weights:
        rows_per_step = w.shape[0] // n_tiles
        assert rows_per_step * n_tiles == w.shape[0] and rows_per_step % 16 == 0
        cast_specs.append(pl.BlockSpec((rows_per_step, w.shape[1]),
                                       lambda g: (jnp.minimum(g, n_tiles - 1), 0)))
    n_cast = len(f32_weights)
    r = XPOSE_ROWS
    n_bch, n_gate = 3 * d // r, 2 * d // r
    off_gate = 7 * d + 2 * N_HEADS
    assert n_bch + n_gate <= n_tiles + 1 and d % r == 0 and off_gate % 8 == 0
    bch_job = lambda g: jnp.minimum(g, n_bch - 1)
    gate_job = lambda g: jnp.clip(g - n_bch, 0, n_gate - 1)
    xpose_src = [pl.BlockSpec((r, d), lambda g: (bch_job(g), 0)),
                 pl.BlockSpec((pl.Element(r), pl.Element(d)),
                              lambda g: (pl.multiple_of(off_gate + gate_job(g) * r, 8), 0))]
    xpose_dst = [pl.BlockSpec((d, r), lambda g: (0, bch_job(g))),
                 pl.BlockSpec((d, r), lambda g: (0, gate_job(g)))]
    hm, c, n, m, *side = pl.pallas_call(
        functools.partial(_mlstm_prompt_kernel, tiles_per_seq=tps, n_cast=n_cast, n_xpose=2),
        grid=(n_tiles + 1,),
        in_specs=[pl.BlockSpec((tm, d), lambda g: (jnp.minimum(g, n_tiles - 1), 0)),
                  _qkvo_spec(d), _const_spec(wgt.shape), _const_spec(bgc.shape),
                  _const_spec(mhg.shape)] + cast_specs + xpose_src,
        out_specs=[pl.BlockSpec((tm, d), lambda g: (jnp.maximum(g - 1, 0), 0)),
                   pl.BlockSpec((1, N_HEADS, dh, dh), seq_block),
                   pl.BlockSpec((1, N_HEADS, dh), lambda g: seq_block(g)[:3]),
                   pl.BlockSpec((1, N_HEADS, 1, 1), seq_block)] + cast_specs + xpose_dst,
        out_shape=[jax.ShapeDtypeStruct((bsz * t, d), BF16),
                   jax.ShapeDtypeStruct((bsz, N_HEADS, dh, dh), F32),
                   jax.ShapeDtypeStruct((bsz, N_HEADS, dh), F32),
                   jax.ShapeDtypeStruct((bsz, N_HEADS, 1, 1), F32)]
                  + [jax.ShapeDtypeStruct(w.shape, BF16) for w in f32_weights]
                  + [jax.ShapeDtypeStruct((d, 3 * d), BF16),
                     jax.ShapeDtypeStruct((d, 2 * d), BF16)],
        scratch_shapes=[
            pltpu.VMEM((2, N_HEADS, tm, dh), BF16),
            pltpu.VMEM((2, N_HEADS, tm, dh), BF16),
            pltpu.VMEM((2, N_HEADS, tm, dh), BF16),
            pltpu.VMEM((2, N_HEADS, tm, dh), F32),
            pltpu.VMEM((2, nc, N_HEADS, CHUNK, CHUNK), F32),
            pltpu.VMEM((2, nc, N_HEADS, CHUNK, LANES), F32),
            pltpu.VMEM((2, nc, N_HEADS, CHUNK, LANES), F32),
            pltpu.VMEM((2, nc, N_HEADS, CHUNK, LANES), F32),
            pltpu.VMEM((N_HEADS, dh, dh), F32),
            pltpu.VMEM((N_HEADS, 1, dh), F32),
            pltpu.VMEM((N_HEADS, 1, LANES), F32)],
        compiler_params=_params(1),
        name="mlstm_prompt",
    )(x.reshape(bsz * t, d), wqkvo, wgt, bgc, mhg, *f32_weights, w_in_t, w_in_t)
    casted, (w_bch, w_gate) = side[:n_cast], side[n_cast:]
    return hm, c, n, m.reshape(bsz, N_HEADS), casted, w_bch, w_gate


def _mlstm_sample_pre_kernel(x_ref, wq_ref, wk_ref, wv_ref, wo_ref, wgc_ref, bgr_ref, mhg_ref,
                             n0_ref, m0_ref,
                             q_ref, kw_ref, v_ref, og_ref, dec_ref, wa_ref, wb_ref, n_ref, m_ref,
                             g_s, *, n_steps):
    rows, dh = q_ref.shape
    nb = rows // n_steps
    T = n_steps
    hd = pl.program_id(0)
    slab = [slice(t * nb, (t + 1) * nb) for t in range(T)]
    head_lane = lax.broadcasted_iota(jnp.int32, m_ref.shape, 1)
    gate_lane = lax.broadcasted_iota(jnp.int32, (nb, g_s.shape[1]), 1)

    x = _time_major(x_ref)
    xh = x.astype(BF16)

    @pl.when(hd == 0)
    def _():
        xl = (x - xh.astype(F32)).astype(BF16)
        g = _dot(xh, wgc_ref[0]) + _dot(xl, wgc_ref[0]) + _dot(xh, wgc_ref[1]) + bgr_ref[...]
        is_input_gate = lax.broadcasted_iota(jnp.int32, g.shape, 1) < N_HEADS
        g_s[...] = jnp.where(is_input_gate, g, _log_sigmoid(g))
        m_ref[...] = jnp.zeros(m_ref.shape, F32)

    def gate_column(t, col):
        return jnp.sum(jnp.where(gate_lane == col, g_s[slab[t], :], 0.0), axis=1, keepdims=True)

    q_all = _dot(xh, wq_ref[...]).astype(BF16).astype(F32)
    k_all = (_dot(xh, wk_ref[...]) * (dh ** -0.5)).astype(BF16).astype(F32)
    v_all = _dot(xh, wv_ref[...]).astype(BF16).astype(F32)
    og_ref[...] = mhg_ref[...] * _sigmoid(_dot(xh, wo_ref[...]))
    q_ref[...] = q_all
    v_ref[...] = v_all
    qf = [q_all[slab[t]] for t in range(T)]
    kf = [k_all[slab[t]] for t in range(T)]
    vf = [v_all[slab[t]] for t in range(T)]
    li = [gate_column(t, hd) for t in range(T)]
    lf = [gate_column(t, N_HEADS + hd) for t in range(T)]
    m0 = jnp.sum(jnp.where(head_lane == hd, m0_ref[...], 0.0), axis=1, keepdims=True)
    n0 = n0_ref[:, hd, :]
    b = [lf[0]]
    for t in range(1, T):
        b.append(b[t - 1] + lf[t])
    a = [li[t] - b[t] for t in range(T)]
    m_new = None
    for t in range(T):
        inter = b[t] + m0
        m_t = inter
        for s in range(t + 1):
            m_t = jnp.maximum(m_t, b[t] + a[s])
        w_inter = jnp.exp(inter - m_t)
        num = jnp.zeros((nb, dh), F32)
        den = w_inter * jnp.sum(qf[t] * n0, axis=1, keepdims=True)
        for s in range(t + 1):
            s_w = (jnp.sum(qf[t] * kf[s], axis=1, keepdims=True)
                   * jnp.exp(b[t] + a[s] - m_t))
            num = num + s_w * vf[s]
            den = den + s_w
        inv = 1.0 / jnp.maximum(jnp.abs(den), jnp.exp(-m_t))
        wa_ref[slab[t], :] = jnp.broadcast_to(w_inter * inv, (nb, dh))
        wb_ref[slab[t], :] = num * inv
        m_new = m_t
    b_last = b[T - 1]
    decay = jnp.exp(b_last + m0 - m_new)
    n_new = decay * n0
    for s in range(T):
        kw = kf[s] * jnp.exp(a[s] + b_last - m_new)
        kw_ref[slab[s], :] = kw
        n_new = n_new + kw
    n_ref[:, hd, :] = n_new
    m_ref[...] = jnp.where(head_lane == hd, m_new, m_ref[...])
    dec_ref[...] = jnp.broadcast_to(decay, (nb, dh))


def _mlstm_sample_pre(xs, wqkvo, wgc, bgr, mhg, n0, m0):
    nb, n_steps, d = xs.shape
    rows = nb * n_steps
    dh = d // N_HEADS
    head_w = lambda part: pl.BlockSpec((d, dh), lambda h: (0, part * N_HEADS + h))
    head_cols = lambda n_rows: pl.BlockSpec((n_rows, dh), lambda h: (0, h))
    big = jax.ShapeDtypeStruct((rows, d), F32)
    small = jax.ShapeDtypeStruct((nb, d), F32)
    n_shape = (nb, N_HEADS, dh)
    return pl.pallas_call(
        functools.partial(_mlstm_sample_pre_kernel, n_steps=n_steps),
        grid=(N_HEADS,),
        in_specs=[_const_spec(xs.shape), head_w(0), head_w(1), head_w(2), head_w(3),
                  _const_spec(wgc.shape), _const_spec(bgr.shape), head_cols(1),
                  _const_spec(n_shape), _const_spec(m0.shape)],
        out_specs=[head_cols(rows)] * 4 + [head_cols(nb)] + [head_cols(rows)] * 2
                  + [pl.BlockSpec(n_shape, lambda h: (0, 0, 0)),
                     pl.BlockSpec((nb, N_HEADS), lambda h: (0, 0))],
        out_shape=[big, big, big, big, small, big, big, jax.ShapeDtypeStruct(n_shape, F32),
                   jax.ShapeDtypeStruct((nb, N_HEADS), F32)],
        scratch_shapes=[pltpu.VMEM((rows, GATE_PAD), F32)],
        compiler_params=_params(1),
        name="mlstm_sample_pre",
    )(xs, wqkvo, wqkvo, wqkvo, wqkvo, wgc, bgr, mhg, n0, m0)


def _stream_matrix_memory(c0_ref, q_ref, kw_ref, v_ref, dec_ref, c_ref, qc_ref):
    T, bb, dh = q_ref.shape
    owner = lax.broadcasted_iota(jnp.int32, (T * bb, dh), 0) % bb
    q_blk = jnp.concatenate([q_ref[t] for t in range(T)], axis=0).astype(BF16)
    kw_blk = jnp.concatenate([kw_ref[t] for t in range(T)], axis=0).astype(BF16)
    v_blk = jnp.concatenate([v_ref[t] for t in range(T)], axis=0)
    c_olds = [c0_ref[bi, 0] for bi in range(bb)]
    reads = [_dot(q_blk, c_olds[bi].astype(BF16)) for bi in range(bb)]
    qc = jnp.zeros((T * bb, dh), F32)
    for bi in range(bb):
        qc = jnp.where(owner == bi, reads[bi], qc)
    for t in range(T):
        qc_ref[t] = qc[t * bb:(t + 1) * bb]
    for bi in range(bb):
        upd = _dot_tn(kw_blk, jnp.where(owner == bi, v_blk, 0.0).astype(BF16))
        c_ref[bi, 0] = dec_ref[bi:bi + 1, :] * c_olds[bi] + upd


def _merge_sample_kernel(x_ref, m1_ref, wa_ref, wb_ref, qc_ref, og_ref, wmo_ref, wgm_ref, wo_ref,
                         g_ref, b_ref, o_ref, hm_s, *, alpha):
    d = x_ref.shape[-1]
    dh = d // N_HEADS
    for hd in range(N_HEADS):
        hc = slice(hd * dh, (hd + 1) * dh)
        hh = wa_ref[:, hc] * qc_ref[:, hc] + wb_ref[:, hc]
        hm_s[:, hc] = (_head_norm(hh) * og_ref[:, hc]).astype(BF16)
    x = _time_major(x_ref)
    gate = _dot(x.astype(BF16), wgm_ref[...])
    ym = _dot(hm_s[...], wmo_ref[...])
    merged = m1_ref[...] + _sigmoid(gate) * ym
    r = alpha * x + _dot(merged.astype(BF16), wo_ref[...])
    o_ref[...] = _layer_norm(r, g_ref[...], b_ref[...])


def _merge_sample(xs, m1, wa, wb, qc, og, wmo, w_b, wo, g, b, alpha):
    n, d = m1.shape
    whole = _const_spec((n, d))
    return pl.pallas_call(
        functools.partial(_merge_sample_kernel, alpha=alpha),
        grid=(1,),
        in_specs=[_const_spec(xs.shape)] + [whole] * 5
                 + [_const_spec(wmo.shape), _gate_spec(d, 1), _const_spec(wo.shape),
                    _const_spec(g.shape), _const_spec(b.shape)],
        out_specs=pl.BlockSpec((n, d), lambda i: (0, 0)),
        out_shape=jax.ShapeDtypeStruct((n, d), F32),
        scratch_shapes=[pltpu.VMEM((n, d), BF16)],
        compiler_params=_params(1),
        name="merge_ln1_sample",
    )(xs, m1, wa, wb, qc, og, wmo, w_b, wo, g, b)


def _ffn_stripe(dff):
    return 1024 if dff % 1024 == 0 else dff


def _ffn_hidden(xb, w1_ref, hid_s, rows, before_stripe=None):
    dff = hid_s.shape[1]
    sw = _ffn_stripe(dff)
    for s in range(dff // sw):
        cs = slice(s * sw, (s + 1) * sw)
        if before_stripe is not None:
            before_stripe(s)
        hid = jnp.maximum(_dot(xb, w1_ref[:, cs]), 0.0)
        hid_s[rows, cs] = (hid * hid).astype(BF16)


def _ffn_small_kernel(x_ref, w1_ref, w2_ref, g_ref, b_ref, o_ref, acc_s, *, alpha):
    j = pl.program_id(0)
    x = x_ref[...]
    hid = jnp.maximum(_dot(x.astype(BF16), w1_ref[...]), 0.0)
    part = _dot((hid * hid).astype(BF16), w2_ref[...])

    @pl.when(j == 0)
    def _():
        acc_s[...] = alpha * x + part

    @pl.when(j > 0)
    def _():
        acc_s[...] += part

    @pl.when(j == pl.num_programs(0) - 1)
    def _():
        y = _layer_norm(acc_s[...], g_ref[...], b_ref[...])
        nb, n_t, _ = o_ref.shape
        for t in range(n_t):
            o_ref[:, t, :] = y[t * nb:(t + 1) * nb]


def _ffn_stream_kernel(x_ref, w1_hbm, w2_hbm, g_ref, b_ref, c0_ref, q_ref, kw_ref, v_ref,
                       dec_ref, o_ref, c_ref, qc_ref, w1_s, w2_s, sem, hid_s, r_s, *, alpha):
    s = pl.program_id(0)
    last = pl.num_programs(0) - 1
    sw = _ffn_stripe(w1_s.shape[1])
    n_w1 = w1_s.shape[1] // sw

    def w1_copy(k):
        cs = pl.ds(k * sw, sw)
        return pltpu.make_async_copy(w1_hbm.at[:, cs], w1_s.at[:, cs], sem.at[k])

    w2_copy = pltpu.make_async_copy(w2_hbm, w2_s, sem.at[n_w1])

    def step(first):
        _stream_matrix_memory(c0_ref, q_ref, kw_ref, v_ref, dec_ref, c_ref, qc_ref)
        x = x_ref[...]
        _ffn_hidden(x.astype(BF16), w1_s, hid_s, slice(None),
                    before_stripe=(lambda k: w1_copy(k).wait()) if first else None)
        o_ref[...] = _layer_norm(r_s[...], g_ref[...], b_ref[...])
        if first:
            w2_copy.wait()
        r_s[...] = alpha * x + _dot(hid_s[...], w2_s[...])

    @pl.when(s == 0)
    def _():
        for k in range(n_w1):
            w1_copy(k).start()
        w2_copy.start()
        r_s[...] = jnp.zeros(r_s.shape, F32)
        step(True)

    @pl.when((s > 0) & (s < last))
    def _():
        step(False)

    @pl.when(s == last)
    def _():
        o_ref[...] = _layer_norm(r_s[...], g_ref[...], b_ref[...])


def _ffn_small(x, w1, w2, g, b, alpha, n_steps):
    n, d = x.shape
    dff = w1.shape[1]
    sw = 1024 if dff % 1024 == 0 else dff
    out_shape = (n // n_steps, n_steps, d)
    return pl.pallas_call(
        functools.partial(_ffn_small_kernel, alpha=alpha),
        grid=(dff // sw,),
        in_specs=[_const_spec((n, d)), pl.BlockSpec((d, sw), lambda j: (0, j)),
                  pl.BlockSpec((sw, d), lambda j: (j, 0)), _const_spec(g.shape),
                  _const_spec(b.shape)],
        out_specs=pl.BlockSpec(out_shape, lambda j: (0,) * len(out_shape)),
        out_shape=jax.ShapeDtypeStruct(out_shape, F32),
        scratch_shapes=[pltpu.VMEM((n, d), F32)],
        compiler_params=_params(1),
        name="ffn_ln2_small",
    )(x, w1, w2, g, b)


def _ffn_with_stream(x, w1, w2, g, b, alpha, c0, q, kw, v, dec, n_steps):
    n, d = x.shape
    rows = q.shape[0]
    nb = rows // n_steps
    dh = d // N_HEADS
    bb = SAMPLE_BATCH_BLOCK
    n_grid = (nb // bb) * N_HEADS
    tm = n // n_grid
    assert tm * n_grid == n and tm % 8 == 0 and nb % bb == 0
    cur = lambda s: jnp.minimum(s, n_grid - 1)
    x_tile = pl.BlockSpec((tm, d), lambda s: (cur(s), 0))
    o_tile = pl.BlockSpec((tm, d), lambda s: (jnp.maximum(s - 1, 0), 0))
    cblock = pl.BlockSpec((bb, 1, dh, dh),
                          lambda s: (cur(s) // N_HEADS, cur(s) % N_HEADS, 0, 0))
    tblock = pl.BlockSpec((n_steps, bb, dh), lambda s: (0, cur(s) // N_HEADS, cur(s) % N_HEADS))
    dblock = pl.BlockSpec((bb, dh), lambda s: (cur(s) // N_HEADS, cur(s) % N_HEADS))
    as_tbd = lambda a: a.reshape(n_steps, nb, d)
    x2, c_new, qc = pl.pallas_call(
        functools.partial(_ffn_stream_kernel, alpha=alpha),
        grid=(n_grid + 1,),
        in_specs=[x_tile, pl.BlockSpec(memory_space=pl.ANY), pl.BlockSpec(memory_space=pl.ANY),
                  _const_spec(g.shape), _const_spec(b.shape), cblock, tblock, tblock, tblock,
                  dblock],
        out_specs=[o_tile, cblock, tblock],
        out_shape=[jax.ShapeDtypeStruct((n, d), F32), jax.ShapeDtypeStruct(c0.shape, F32),
                   jax.ShapeDtypeStruct((n_steps, nb, d), F32)],
        scratch_shapes=[pltpu.VMEM(w1.shape, BF16), pltpu.VMEM(w2.shape, BF16),
                        pltpu.SemaphoreType.DMA((w1.shape[1] // _ffn_stripe(w1.shape[1]) + 1,)),
                        pltpu.VMEM((tm, w1.shape[1]), BF16), pltpu.VMEM((tm, d), F32)],
        compiler_params=_params(1),
        name="ffn_ln2_stream",
    )(x, w1, w2, g, b, c0, as_tbd(q), as_tbd(kw), as_tbd(v), dec)
    return x2, c_new, qc.reshape(rows, d)


def _w_in_prep_kernel(a_ref, g_ref, wa_ref, wg_ref):
    @pl.when(pl.program_id(0) == 0)
    def _():
        wg_ref[...] = g_ref[...]

    wa_ref[...] = a_ref[...].T.astype(BF16)


def _prepare_w_in(w_in_t):
    d = w_in_t.shape[1]
    h2 = 2 * N_HEADS
    return pl.pallas_call(
        _w_in_prep_kernel,
        grid=(4,),
        in_specs=[pl.BlockSpec((d, d), lambda j: (3 + j, 0)),
                  pl.BlockSpec((pl.Element(h2), pl.Element(d)), lambda j: (7 * d, 0))],
        out_specs=[pl.BlockSpec((d, d), lambda j: (0, j)),
                   pl.BlockSpec((h2, d), lambda j: (0, 0))],
        out_shape=[jax.ShapeDtypeStruct((d, 4 * d), BF16),
                   jax.ShapeDtypeStruct((h2, d), F32)],
        compiler_params=_params(1),
        name="w_in_prep",
    )(w_in_t, w_in_t)


def _layer_weights(w_in, b_gate, conv_w, w_conv_out, mh_g, w_m_out, w_o, ln1_g, ln1_b,
                   w_ff1, w_ff2, ln2_g, ln2_b):
    d = w_in.shape[0]
    h2 = 2 * N_HEADS
    w_in_t = jnp.swapaxes(w_in, 0, 1)
    w_qkvo, wg_t = _prepare_w_in(w_in_t)
    wgt_hi, wgt_lo = _split_bf16(wg_t)
    wgt = jnp.concatenate([wgt_hi, wgt_lo], axis=0)
    pad = ((0, 0), (0, GATE_PAD - h2))
    wgc = jnp.stack([jnp.pad(wgt_hi.T, pad), jnp.pad(wgt_lo.T, pad)])
    return dict(
        w_in_t=w_in_t, w_qkvo=w_qkvo,
        wgt=wgt, wgc=wgc,
        bgc=b_gate.reshape(h2, 1).astype(F32),
        bgr=jnp.pad(b_gate.reshape(1, h2).astype(F32), pad),
        cw=conv_w.astype(F32),
        mhg=mh_g.reshape(1, d).astype(F32),
        f32_weights=[w_conv_out, w_m_out, w_o],
        ffn_weights=(w_ff1, w_ff2),
        ln1_g=ln1_g.reshape(1, d), ln1_b=ln1_b.reshape(1, d),
        ln2_g=ln2_g.reshape(1, d), ln2_b=ln2_b.reshape(1, d))


def _layer(x, xs, conv_buf, c0, n0, m0, p, alpha):
    bsz, t, d = x.shape
    n_steps = xs.shape[1]
    hm, c_p, n_p, m_p, (wco, wmo, wo), w_bch, w_gate = _mlstm_prompt(
        x, p["w_qkvo"], p["wgt"], p["bgc"], p["mhg"], p["f32_weights"], p["w_in_t"])
    x1, conv_p, w1, w2 = _conv_merge_prompt(x, hm, w_bch, p["cw"], wco, w_gate, wmo, wo,
                                            p["ln1_g"], p["ln1_b"], alpha, *p["ffn_weights"])
    m1_s, conv_s = _conv_branch_sample(xs, conv_buf, w_bch, p["cw"], wco, w_gate)
    q, kw, v, og, dec, wa, wb, n_s, m_s = _mlstm_sample_pre(
        xs, p["w_qkvo"], p["wgc"], p["bgr"], p["mhg"], n0, m0)
    x2, c_s, qc = _ffn_with_stream(x1, w1, w2, p["ln2_g"], p["ln2_b"], alpha,
                                   c0, q, kw, v, dec, n_steps)
    x1_s = _merge_sample(xs, m1_s, wa, wb, qc, og, wmo, w_gate, wo, p["ln1_g"], p["ln1_b"],
                         alpha)
    x2_s = _ffn_small(x1_s, w1, w2, p["ln2_g"], p["ln2_b"], alpha, n_steps)
    return (x2.reshape(bsz, t, d), conv_p, c_p, n_p, m_p,
            x2_s, conv_s, c_s, n_s, m_s)


def kernel(x_prompt, x_sample, state_conv, state_C, state_n, state_m, w_in, b_gate, conv_w,
           w_conv_out, mh_g, w_m_out, w_o, ln1_g, ln1_b, w_ff1, w_ff2, ln2_g, ln2_b):
    depth = w_in.shape[0]
    alpha = (2.0 * depth) ** 0.25
    bsz, t, d = x_prompt.shape
    sb, st, _ = x_sample.shape
    assert t % TOKEN_TILE == 0 and t % MLSTM_TILE == 0 and MLSTM_TILE % CHUNK == 0
    assert d % N_HEADS == 0
    assert sb % SAMPLE_BATCH_BLOCK == 0 and st >= CONV_W - 1

    xp, xs = x_prompt, x_sample
    outs = [[] for _ in range(8)]
    for l in range(depth):
        p = _layer_weights(w_in[l], b_gate[l], conv_w[l], w_conv_out[l], mh_g[l], w_m_out[l],
                           w_o[l], ln1_g[l], ln1_b[l], w_ff1[l], w_ff2[l], ln2_g[l], ln2_b[l])
        xp, cp, c_p, n_p, m_p, xs, cs, c_s, n_s, m_s = _layer(
            xp, xs, state_conv[l], state_C[l], state_n[l], state_m[l], p, alpha)
        for acc, val in zip(outs, (cp, cs, c_p, c_s, n_p, n_s, m_p, m_s)):
            acc.append(val)
    return (xp, xs) + tuple(jnp.stack(acc) for acc in outs)
```

```python
import functools

import jax
import jax.numpy as jnp
from jax import lax
from jax.experimental import pallas as pl
from jax.experimental.pallas import tpu as pltpu

F32 = jnp.float32
BF16 = jnp.bfloat16

LN_EPS = 1e-5
N_HEADS = 4
CHUNK = 128
CONV_W = 3
TOKEN_TILE = 1024
SUB_TILE = 512
MLSTM_TILE = 512
SAMPLE_BATCH_BLOCK = 16
LANES = 128
GATE_PAD = LANES
VMEM_LIMIT = 56 * 1024 * 1024


def _dot(a, b):
    return jnp.dot(a, b, preferred_element_type=F32)


def _dot_nt(a, b):
    return lax.dot_general(a, b, (((1,), (1,)), ((), ())), preferred_element_type=F32)


def _dot_tn(a, b):
    return lax.dot_general(a, b, (((0,), (0,)), ((), ())), preferred_element_type=F32)


def _sigmoid(x):
    return 1.0 / (1.0 + jnp.exp(-x))


def _log_sigmoid(x):
    return jnp.minimum(x, 0.0) - jnp.log1p(jnp.exp(-jnp.abs(x)))


def _split_bf16(x):
    hi = x.astype(BF16)
    lo = (x - hi.astype(F32)).astype(BF16)
    return hi, lo


def _layer_norm(r, g, b):
    mu = jnp.mean(r, axis=-1, keepdims=True)
    xc = r - mu
    var = jnp.mean(xc * xc, axis=-1, keepdims=True)
    return xc * lax.rsqrt(var + LN_EPS) * g + b


def _head_norm(h):
    mu = jnp.mean(h, axis=-1, keepdims=True)
    hc = h - mu
    return hc * lax.rsqrt(jnp.mean(hc * hc, axis=-1, keepdims=True) + LN_EPS)


def _const_spec(shape):
    zeros = (0,) * len(shape)
    return pl.BlockSpec(shape, lambda *_: zeros, pipeline_mode=pl.Buffered(1))


def _window_spec(block_shape, index):
    return pl.BlockSpec(block_shape, lambda *_: index, pipeline_mode=pl.Buffered(1))


def _bch_spec(d):
    return _const_spec((d, 3 * d))


def _qkvo_spec(d):
    return _const_spec((d, 4 * d))


def _gate_spec(d, which):
    return _window_spec((d, d), (0, which))


def _params(n_axes):
    return pltpu.CompilerParams(dimension_semantics=("arbitrary",) * n_axes,
                                vmem_limit_bytes=VMEM_LIMIT)


def _conv_stripes(d):
    sw = 256 if d % 256 == 0 else d
    return [slice(s * sw, (s + 1) * sw) for s in range(d // sw)]


def _conv_merge_prompt_kernel(x_ref, hm_ref, wbch_ref, cw_ref, wco_ref, wgc_ref, wmo_ref, wgm_ref,
                              wo_ref, g_ref, b_ref, w1_ref, w2_ref, o_ref, cs_ref, w1b_ref,
                              w2b_ref, u_s, a_s, mg_s, *, alpha):
    tm, d = a_s.shape
    @pl.when(pl.program_id(1) == 0)
    def _():
        u_s[0:8, :] = jnp.zeros((8, d), F32)

    w1b_ref[...] = w1_ref[...].astype(BF16)
    w2b_ref[...] = w2_ref[...].astype(BF16)

    for r0 in range(0, tm, SUB_TILE):
        sub = min(SUB_TILE, tm - r0)
        x = x_ref[0, r0:r0 + sub, :]
        rows = slice(r0, r0 + sub)
        xb = x.astype(BF16)
        for cs in _conv_stripes(d):
            off = cs.start
            cg = _dot(xb, wbch_ref[:, d + off:d + cs.stop])
            hc = _dot(xb, wbch_ref[:, 2 * d + off:2 * d + cs.stop])
            u = cg * hc
            u_s[8 + r0:8 + r0 + sub, cs] = u
            conv = (u_s[6 + r0:6 + r0 + sub, cs] * cw_ref[0:1, cs]
                    + u_s[7 + r0:7 + r0 + sub, cs] * cw_ref[1:2, cs] + u * cw_ref[2:3, cs])
            bg = _dot(xb, wbch_ref[:, off:cs.stop])
            a_s[rows, cs] = (bg * conv).astype(BF16)
        hm = hm_ref[rows, :]
        for cs in _conv_stripes(d):
            gated_m = _sigmoid(_dot(xb, wgm_ref[:, cs])) * _dot(hm, wmo_ref[:, cs])
            gated_c = _sigmoid(_dot(xb, wgc_ref[:, cs])) * _dot(a_s[rows, :], wco_ref[:, cs])
            mg_s[rows, cs] = (gated_c + gated_m).astype(BF16)
        r = alpha * x + _dot(mg_s[rows, :], wo_ref[...])
        o_ref[rows, :] = _layer_norm(r, g_ref[...], b_ref[...])
    cs_ref[0] = u_s[tm + 6:tm + 8, :]
    u_s[0:8, :] = u_s[tm:tm + 8, :]


def _time_major(x_ref):
    return jnp.concatenate([x_ref[:, t, :] for t in range(x_ref.shape[1])], axis=0)


def _conv_sample_kernel(x_ref, st_ref, wbch_ref, cw_ref, wco_ref, wgc_ref, m1_ref, cs_ref, a_s):
    nb, n_steps, d = x_ref.shape
    xb = _time_major(x_ref).astype(BF16)
    for cs in _conv_stripes(d):
        off = cs.start
        bg = _dot(xb, wbch_ref[:, off:cs.stop])
        cg = _dot(xb, wbch_ref[:, d + off:d + cs.stop])
        hc = _dot(xb, wbch_ref[:, 2 * d + off:2 * d + cs.stop])
        u = cg * hc
        up = [st_ref[:, j, cs] for j in range(CONV_W - 1)]
        up += [u[t * nb:(t + 1) * nb] for t in range(n_steps)]
        for t in range(n_steps):
            conv = (up[t] * cw_ref[0:1, cs] + up[t + 1] * cw_ref[1:2, cs]
                    + up[t + 2] * cw_ref[2:3, cs])
            a_s[t * nb:(t + 1) * nb, cs] = (bg[t * nb:(t + 1) * nb] * conv).astype(BF16)
        for j in range(CONV_W - 1):
            cs_ref[:, j, cs] = up[n_steps + j]
    yc = _dot(a_s[...], wco_ref[...])
    m1_ref[...] = _sigmoid(_dot(xb, wgc_ref[...])) * yc


def _conv_merge_prompt(x, hm, w_a, cw, wco, w_b, wmo, wo, g, b, alpha, w1, w2):
    bsz, t, d = x.shape
    tm = TOKEN_TILE
    tps = t // tm
    n_tiles = bsz * tps
    flat = pl.BlockSpec((tm, d), lambda bi, j: (bi * tps + j, 0))
    cast_specs = []
    for w in (w1, w2):
        rows_per_step = w.shape[0] // n_tiles
        assert rows_per_step * n_tiles == w.shape[0] and rows_per_step % 16 == 0
        cast_specs.append(pl.BlockSpec((rows_per_step, w.shape[1]),
                                       lambda bi, j: (bi * tps + j, 0)))
    return pl.pallas_call(
        functools.partial(_conv_merge_prompt_kernel, alpha=alpha),
        grid=(bsz, tps),
        in_specs=[pl.BlockSpec((1, tm, d), lambda bi, j: (bi, j, 0)), flat,
                  _bch_spec(d), _const_spec(cw.shape), _const_spec(wco.shape),
                  _gate_spec(d, 0), _const_spec(wmo.shape), _gate_spec(d, 1),
                  _const_spec(wo.shape), _const_spec(g.shape), _const_spec(b.shape)]
                 + cast_specs,
        out_specs=[flat, pl.BlockSpec((1, CONV_W - 1, d), lambda bi, j: (bi, 0, 0))]
                  + cast_specs,
        out_shape=[jax.ShapeDtypeStruct((bsz * t, d), F32),
                   jax.ShapeDtypeStruct((bsz, CONV_W - 1, d), F32),
                   jax.ShapeDtypeStruct(w1.shape, BF16), jax.ShapeDtypeStruct(w2.shape, BF16)],
        scratch_shapes=[pltpu.VMEM((tm + 8, d), F32), pltpu.VMEM((tm, d), BF16),
                        pltpu.VMEM((tm, d), BF16)],
        compiler_params=_params(2),
        name="conv_merge_prompt",
    )(x, hm, w_a, cw, wco, w_b, wmo, w_b, wo, g, b, w1, w2)


def _conv_branch_sample(xs, conv_buf, wbch, cw, wco, wgc):
    nb, n_steps, d = xs.shape
    rows = nb * n_steps
    return pl.pallas_call(
        _conv_sample_kernel,
        grid=(1,),
        in_specs=[_const_spec(xs.shape), _const_spec(conv_buf.shape), _bch_spec(d),
                  _const_spec(cw.shape), _const_spec(wco.shape), _gate_spec(d, 0)],
        out_specs=[pl.BlockSpec((rows, d), lambda i: (0, 0)),
                   pl.BlockSpec(conv_buf.shape, lambda i: (0, 0, 0))],
        out_shape=[jax.ShapeDtypeStruct((rows, d), F32),
                   jax.ShapeDtypeStruct(conv_buf.shape, F32)],
        scratch_shapes=[pltpu.VMEM((rows, d), BF16)],
        compiler_params=_params(1),
        name="conv_branch_sample",
    )(xs, conv_buf, wbch, cw, wco, wgc)


def _rep(col, times):
    return col if times == 1 else jnp.concatenate([col] * times, axis=1)


def _mlstm_prompt_step(x_ref, wqkvo_ref, wgt_ref, bgc_ref, mhg_ref, hm_ref,
                       q_s, k_s, v_s, og_s, e_s, colb_s, colm_s, cole_s, cst_s, nst_s, mst_s,
                       *, cur, prev):
    _, _, tm, dh = q_s.shape
    d = N_HEADS * dh
    L = CHUNK
    LANES = colb_s.shape[-1]
    n_chunks = tm // L
    wide = dh // LANES

    xh = x_ref[...].astype(BF16)

    h2 = 2 * N_HEADS
    ga = _dot_nt(wgt_ref[...], xh)
    gt = ga[0:h2] + ga[h2:2 * h2] + bgc_ref[...]
    is_input_gate = lax.broadcasted_iota(jnp.int32, gt.shape, 0) < N_HEADS
    g = jnp.where(is_input_gate, gt, _log_sigmoid(gt))

    pos = lax.broadcasted_iota(jnp.int32, g.shape, 1) % L
    csum = g
    shift = 1
    while shift < L:
        csum = csum + jnp.where(pos >= shift, pltpu.roll(csum, shift, 1), 0.0)
        shift *= 2
    ba = jnp.concatenate([csum[N_HEADS:], g[:N_HEADS] - csum[N_HEADS:]], axis=0)

    ri = lax.broadcasted_iota(jnp.int32, (L, L), 0)
    ci = lax.broadcasted_iota(jnp.int32, (L, L), 1)
    causal = ri >= ci

    for c in range(n_chunks):
        slab = ba[:, c * L:(c + 1) * L]
        cols = slab.T
        for h in range(N_HEADS):
            b_rep = jnp.broadcast_to(cols[:, h:h + 1], (L, LANES))
            a_rep = jnp.broadcast_to(cols[:, N_HEADS + h:N_HEADS + h + 1], (L, LANES))
            a_row = slab[N_HEADS + h:N_HEADS + h + 1, :]
            dm = jnp.where(causal, _rep(b_rep, L // LANES) + a_row, -jnp.inf)
            m_loc = jnp.broadcast_to(jnp.max(dm, axis=1, keepdims=True), (L, LANES))
            e_s[cur, c, h] = jnp.exp(dm - _rep(m_loc, L // LANES))
            colb_s[cur, c, h] = b_rep
            colm_s[cur, c, h] = m_loc
            cole_s[cur, c, h] = jnp.exp(a_rep + b_rep[L - 1:L, :] - m_loc[L - 1:L, :])

    n_split = N_HEADS
    piece = d // n_split

    def project(part, split):
        lo = split * piece
        y = _dot(xh, wqkvo_ref[:, part * d + lo:part * d + lo + piece])
        if part == 1:
            y = y * (dh ** -0.5)
        elif part == 3:
            y = mhg_ref[:, lo:lo + piece] * _sigmoid(y)
        dst = (q_s, k_s, v_s, og_s)[part]
        for h in range(lo // dh, (lo + piece) // dh):
            dst[cur, h] = y[:, h * dh - lo:(h + 1) * dh - lo].astype(dst.dtype)

    heads = [slice(h * dh, (h + 1) * dh) for h in range(N_HEADS)]

    def lane_rep(col):
        return jnp.broadcast_to(col, (col.shape[0], LANES))

    def recur_ready(c):
        rows = slice(c * L, (c + 1) * L)
        qk = [_dot_nt(q_s[prev, h, rows, :], k_s[prev, h, rows, :]) for h in range(N_HEADS)]
        states = [cst_s[h] for h in range(N_HEADS)]
        qc = [_dot(q_s[prev, h, rows, :], states[h].astype(BF16)) for h, hc in enumerate(heads)]
        ux, ks, qn = [], [], []
        for h, hc in enumerate(heads):
            kw = k_s[prev, h, rows, :].astype(F32) * _rep(cole_s[prev, c, h], wide)
            ux.append(_dot_tn(kw.astype(BF16), v_s[prev, h, rows, :]))
            ks.append(jnp.sum(kw, axis=0, keepdims=True))
            qn.append(lane_rep(jnp.sum(q_s[prev, h, rows, :].astype(F32) * nst_s[h],
                                       axis=1, keepdims=True)))
        return qk, states, qc, ux, ks, qn

    def recur_finish(c, qk, states, qc, ux, ks, qn):
        rows = slice(c * L, (c + 1) * L)
        sv, rs = [], []
        for h, hc in enumerate(heads):
            s_loc = qk[h] * e_s[prev, c, h]
            sv.append(_dot(s_loc.astype(BF16), v_s[prev, h, rows, :]))
            rs.append(lane_rep(jnp.sum(s_loc, axis=1, keepdims=True)))
        for h, hc in enumerate(heads):
            m_prev = mst_s[h]
            m_loc = colm_s[prev, c, h]
            inter = colb_s[prev, c, h] + m_prev
            m_t = jnp.maximum(inter, m_loc)
            w_inter = jnp.exp(inter - m_t)
            w_loc = jnp.exp(m_loc - m_t)
            den = w_inter * qn[h] + w_loc * rs[h]
            inv = 1.0 / jnp.maximum(jnp.abs(den), jnp.exp(-m_t))
            hh = (_rep(w_inter * inv, wide) * qc[h] + _rep(w_loc * inv, wide) * sv[h])
            hm_ref[rows, hc] = (_head_norm(hh) * og_s[prev, h, rows, :]).astype(BF16)
            m_new = m_t[L - 1:L, :]
            decay = _rep(jnp.exp(inter[L - 1:L, :] - m_new), wide)
            grow = _rep(jnp.exp(m_loc[L - 1:L, :] - m_new), wide)
            cst_s[h] = decay * states[h] + grow * ux[h]
            nst_s[h] = decay * nst_s[h] + grow * ks[h]
            mst_s[h] = m_new

    pieces = [(part, split) for part in range(4) for split in range(n_split)]
    n_phases = 2 * n_chunks
    done = 0
    for c in range(n_chunks):
        for phase in (2 * c, 2 * c + 1):
            if phase % 2 == 0:
                partial = recur_ready(c)
            else:
                recur_finish(c, *partial)
            upto = (phase + 1) * len(pieces) // n_phases
            for part, split in pieces[done:upto]:
                project(part, split)
            done = upto


def _mlstm_prompt_kernel(*refs, tiles_per_seq, n_cast, n_xpose):
    n_side = n_cast + n_xpose
    x_ref, wqkvo_ref, wgt_ref, bgc_ref, mhg_ref = refs[:5]
    side_src = refs[5:5 + n_side]
    hm_ref, c_ref, n_ref, m_ref = refs[5 + n_side:9 + n_side]
    side_dst = refs[9 + n_side:9 + 2 * n_side]
    scratch = refs[9 + 2 * n_side:]
    q_s, k_s, v_s, og_s, e_s, colb_s, colm_s, cole_s, cst_s, nst_s, mst_s = scratch
    g_step = pl.program_id(0)

    def side_jobs():
        for i, (src, dst) in enumerate(zip(side_src, side_dst)):
            dst[...] = (src[...] if i < n_cast else src[...].T).astype(BF16)

    @pl.when(g_step == 0)
    def _():
        for ref in (q_s, k_s, v_s, og_s, e_s, colb_s, colm_s, cole_s):
            ref[1] = jnp.zeros(ref.shape[1:], ref.dtype)

    @pl.when(jnp.logical_or(g_step == 0, (g_step - 1) % tiles_per_seq == 0))
    def _():
        for ref in (cst_s, nst_s, mst_s):
            ref[...] = jnp.zeros(ref.shape, F32)

    for parity in (0, 1):
        @pl.when(g_step % 2 == parity)
        def _():
            side_jobs()
            _mlstm_prompt_step(x_ref, wqkvo_ref, wgt_ref, bgc_ref, mhg_ref, hm_ref, *scratch,
                               cur=parity, prev=1 - parity)

    @pl.when(jnp.logical_and(g_step >= 1, (g_step - 1) % tiles_per_seq == tiles_per_seq - 1))
    def _():
        for h in range(N_HEADS):
            c_ref[0, h] = cst_s[h]
            n_ref[0, h:h + 1, :] = nst_s[h]
            m_ref[0, h] = mst_s[h][:, 0:1]


XPOSE_ROWS = 256


def _mlstm_prompt(x, wqkvo, wgt, bgc, mhg, f32_weights, w_in_t):
    bsz, t, d = x.shape
    dh = d // N_HEADS
    tm = MLSTM_TILE
    nc = tm // CHUNK
    tps = t // tm
    n_tiles = bsz * tps
    seq_block = lambda g: (jnp.maximum(g - 1, 0) // tps, 0, 0, 0)
    cast_specs = []
    for w in f32_weights:
        rows_per_step = w.shape[0] // n_tiles
        assert rows_per_step * n_tiles == w.shape[0] and rows_per_step % 16 == 0
        cast_specs.append(pl.BlockSpec((rows_per_step, w.shape[1]),
                                       lambda g: (jnp.minimum(g, n_tiles - 1), 0)))
    n_cast = len(f32_weights)
    r = XPOSE_ROWS
    n_bch, n_gate = 3 * d // r, 2 * d // r
    off_gate = 7 * d + 2 * N_HEADS
    assert n_bch + n_gate <= n_tiles + 1 and d % r == 0 and off_gate % 8 == 0
    bch_job = lambda g: jnp.minimum(g, n_bch - 1)
    gate_job = lambda g: jnp.clip(g - n_bch, 0, n_gate - 1)
    xpose_src = [pl.BlockSpec((r, d), lambda g: (bch_job(g), 0)),
                 pl.BlockSpec((pl.Element(r), pl.Element(d)),
                              lambda g: (pl.multiple_of(off_gate + gate_job(g) * r, 8), 0))]
    xpose_dst = [pl.BlockSpec((d, r), lambda g: (0, bch_job(g))),
                 pl.BlockSpec((d, r), lambda g: (0, gate_job(g)))]
    hm, c, n, m, *side = pl.pallas_call(
        functools.partial(_mlstm_prompt_kernel, tiles_per_seq=tps, n_cast=n_cast, n_xpose=2),
        grid=(n_tiles + 1,),
        in_specs=[pl.BlockSpec((tm, d), lambda g: (jnp.minimum(g, n_tiles - 1), 0)),
                  _qkvo_spec(d), _const_spec(wgt.shape), _const_spec(bgc.shape),
                  _const_spec(mhg.shape)] + cast_specs + xpose_src,
        out_specs=[pl.BlockSpec((tm, d), lambda g: (jnp.maximum(g - 1, 0), 0)),
                   pl.BlockSpec((1, N_HEADS, dh, dh), seq_block),
                   pl.BlockSpec((1, N_HEADS, dh), lambda g: seq_block(g)[:3]),
                   pl.BlockSpec((1, N_HEADS, 1, 1), seq_block)] + cast_specs + xpose_dst,
        out_shape=[jax.ShapeDtypeStruct((bsz * t, d), BF16),
                   jax.ShapeDtypeStruct((bsz, N_HEADS, dh, dh), F32),
                   jax.ShapeDtypeStruct((bsz, N_HEADS, dh), F32),
                   jax.ShapeDtypeStruct((bsz, N_HEADS, 1, 1), F32)]
                  + [jax.ShapeDtypeStruct(w.shape, BF16) for w in f32_weights]
                  + [jax.ShapeDtypeStruct((d, 3 * d), BF16),
                     jax.ShapeDtypeStruct((d, 2 * d), BF16)],
        scratch_shapes=[
            pltpu.VMEM((2, N_HEADS, tm, dh), BF16),
            pltpu.VMEM((2, N_HEADS, tm, dh), BF16),
            pltpu.VMEM((2, N_HEADS, tm, dh), BF16),
            pltpu.VMEM((2, N_HEADS, tm, dh), F32),
            pltpu.VMEM((2, nc, N_HEADS, CHUNK, CHUNK), F32),
            pltpu.VMEM((2, nc, N_HEADS, CHUNK, LANES), F32),
            pltpu.VMEM((2, nc, N_HEADS, CHUNK, LANES), F32),
            pltpu.VMEM((2, nc, N_HEADS, CHUNK, LANES), F32),
            pltpu.VMEM((N_HEADS, dh, dh), F32),
            pltpu.VMEM((N_HEADS, 1, dh), F32),
            pltpu.VMEM((N_HEADS, 1, LANES), F32)],
        compiler_params=_params(1),
        name="mlstm_prompt",
    )(x.reshape(bsz * t, d), wqkvo, wgt, bgc, mhg, *f32_weights, w_in_t, w_in_t)
    casted, (w_bch, w_gate) = side[:n_cast], side[n_cast:]
    return hm, c, n, m.reshape(bsz, N_HEADS), casted, w_bch, w_gate


def _mlstm_sample_pre_kernel(x_ref, wq_ref, wk_ref, wv_ref, wo_ref, wgc_ref, bgr_ref, mhg_ref,
                             n0_ref, m0_ref,
                             q_ref, kw_ref, v_ref, og_ref, dec_ref, wa_ref, wb_ref, n_ref, m_ref,
                             g_s, *, n_steps):
    rows, dh = q_ref.shape
    nb = rows // n_steps
    T = n_steps
    hd = pl.program_id(0)
    slab = [slice(t * nb, (t + 1) * nb) for t in range(T)]
    head_lane = lax.broadcasted_iota(jnp.int32, m_ref.shape, 1)
    gate_lane = lax.broadcasted_iota(jnp.int32, (nb, g_s.shape[1]), 1)

    x = _time_major(x_ref)
    xh = x.astype(BF16)

    @pl.when(hd == 0)
    def _():
        xl = (x - xh.astype(F32)).astype(BF16)
        g = _dot(xh, wgc_ref[0]) + _dot(xl, wgc_ref[0]) + _dot(xh, wgc_ref[1]) + bgr_ref[...]
        is_input_gate = lax.broadcasted_iota(jnp.int32, g.shape, 1) < N_HEADS
        g_s[...] = jnp.where(is_input_gate, g, _log_sigmoid(g))
        m_ref[...] = jnp.zeros(m_ref.shape, F32)

    def gate_column(t, col):
        return jnp.sum(jnp.where(gate_lane == col, g_s[slab[t], :], 0.0), axis=1, keepdims=True)

    q_all = _dot(xh, wq_ref[...]).astype(BF16).astype(F32)
    k_all = (_dot(xh, wk_ref[...]) * (dh ** -0.5)).astype(BF16).astype(F32)
    v_all = _dot(xh, wv_ref[...]).astype(BF16).astype(F32)
    og_ref[...] = mhg_ref[...] * _sigmoid(_dot(xh, wo_ref[...]))
    q_ref[...] = q_all
    v_ref[...] = v_all
    qf = [q_all[slab[t]] for t in range(T)]
    kf = [k_all[slab[t]] for t in range(T)]
    vf = [v_all[slab[t]] for t in range(T)]
    li = [gate_column(t, hd) for t in range(T)]
    lf = [gate_column(t, N_HEADS + hd) for t in range(T)]
    m0 = jnp.sum(jnp.where(head_lane == hd, m0_ref[...], 0.0), axis=1, keepdims=True)
    n0 = n0_ref[:, hd, :]
    b = [lf[0]]
    for t in range(1, T):
        b.append(b[t - 1] + lf[t])
    a = [li[t] - b[t] for t in range(T)]
    m_new = None
    for t in range(T):
        inter = b[t] + m0
        m_t = inter
        for s in range(t + 1):
            m_t = jnp.maximum(m_t, b[t] + a[s])
        w_inter = jnp.exp(inter - m_t)
        num = jnp.zeros((nb, dh), F32)
        den = w_inter * jnp.sum(qf[t] * n0, axis=1, keepdims=True)
        for s in range(t + 1):
            s_w = (jnp.sum(qf[t] * kf[s], axis=1, keepdims=True)
                   * jnp.exp(b[t] + a[s] - m_t))
            num = num + s_w * vf[s]
            den = den + s_w
        inv = 1.0 / jnp.maximum(jnp.abs(den), jnp.exp(-m_t))
        wa_ref[slab[t], :] = jnp.broadcast_to(w_inter * inv, (nb, dh))
        wb_ref[slab[t], :] = num * inv
        m_new = m_t
    b_last = b[T - 1]
    decay = jnp.exp(b_last + m0 - m_new)
    n_new = decay * n0
    for s in range(T):
        kw = kf[s] * jnp.exp(a[s] + b_last - m_new)
        kw_ref[slab[s], :] = kw
        n_new = n_new + kw
    n_ref[:, hd, :] = n_new
    m_ref[...] = jnp.where(head_lane == hd, m_new, m_ref[...])
    dec_ref[...] = jnp.broadcast_to(decay, (nb, dh))


def _mlstm_sample_pre(xs, wqkvo, wgc, bgr, mhg, n0, m0):
    nb, n_steps, d = xs.shape
    rows = nb * n_steps
    dh = d // N_HEADS
    head_w = lambda part: pl.BlockSpec((d, dh), lambda h: (0, part * N_HEADS + h))
    head_cols = lambda n_rows: pl.BlockSpec((n_rows, dh), lambda h: (0, h))
    big = jax.ShapeDtypeStruct((rows, d), F32)
    small = jax.ShapeDtypeStruct((nb, d), F32)
    n_shape = (nb, N_HEADS, dh)
    return pl.pallas_call(
        functools.partial(_mlstm_sample_pre_kernel, n_steps=n_steps),
        grid=(N_HEADS,),
        in_specs=[_const_spec(xs.shape), head_w(0), head_w(1), head_w(2), head_w(3),
                  _const_spec(wgc.shape), _const_spec(bgr.shape), head_cols(1),
                  _const_spec(n_shape), _const_spec(m0.shape)],
        out_specs=[head_cols(rows)] * 4 + [head_cols(nb)] + [head_cols(rows)] * 2
                  + [pl.BlockSpec(n_shape, lambda h: (0, 0, 0)),
                     pl.BlockSpec((nb, N_HEADS), lambda h: (0, 0))],
        out_shape=[big, big, big, big, small, big, big, jax.ShapeDtypeStruct(n_shape, F32),
                   jax.ShapeDtypeStruct((nb, N_HEADS), F32)],
        scratch_shapes=[pltpu.VMEM((rows, GATE_PAD), F32)],
        compiler_params=_params(1),
        name="mlstm_sample_pre",
    )(xs, wqkvo, wqkvo, wqkvo, wqkvo, wgc, bgr, mhg, n0, m0)


def _stream_matrix_memory(c0_ref, q_ref, kw_ref, v_ref, dec_ref, c_ref, qc_ref):
    T, bb, dh = q_ref.shape
    owner = lax.broadcasted_iota(jnp.int32, (T * bb, dh), 0) % bb
    q_blk = jnp.concatenate([q_ref[t] for t in range(T)], axis=0).astype(BF16)
    kw_blk = jnp.concatenate([kw_ref[t] for t in range(T)], axis=0).astype(BF16)
    v_blk = jnp.concatenate([v_ref[t] for t in range(T)], axis=0)
    c_olds = [c0_ref[bi, 0] for bi in range(bb)]
    reads = [_dot(q_blk, c_olds[bi].astype(BF16)) for bi in range(bb)]
    qc = jnp.zeros((T * bb, dh), F32)
    for bi in range(bb):
        qc = jnp.where(owner == bi, reads[bi], qc)
    for t in range(T):
        qc_ref[t] = qc[t * bb:(t + 1) * bb]
    for bi in range(bb):
        upd = _dot_tn(kw_blk, jnp.where(owner == bi, v_blk, 0.0).astype(BF16))
        c_ref[bi, 0] = dec_ref[bi:bi + 1, :] * c_olds[bi] + upd


def _merge_sample_kernel(x_ref, m1_ref, wa_ref, wb_ref, qc_ref, og_ref, wmo_ref, wgm_ref, wo_ref,
                         g_ref, b_ref, o_ref, hm_s, *, alpha):
    d = x_ref.shape[-1]
    dh = d // N_HEADS
    for hd in range(N_HEADS):
        hc = slice(hd * dh, (hd + 1) * dh)
        hh = wa_ref[:, hc] * qc_ref[:, hc] + wb_ref[:, hc]
        hm_s[:, hc] = (_head_norm(hh) * og_ref[:, hc]).astype(BF16)
    x = _time_major(x_ref)
    gate = _dot(x.astype(BF16), wgm_ref[...])
    ym = _dot(hm_s[...], wmo_ref[...])
    merged = m1_ref[...] + _sigmoid(gate) * ym
    r = alpha * x + _dot(merged.astype(BF16), wo_ref[...])
    o_ref[...] = _layer_norm(r, g_ref[...], b_ref[...])


def _merge_sample(xs, m1, wa, wb, qc, og, wmo, w_b, wo, g, b, alpha):
    n, d = m1.shape
    whole = _const_spec((n, d))
    return pl.pallas_call(
        functools.partial(_merge_sample_kernel, alpha=alpha),
        grid=(1,),
        in_specs=[_const_spec(xs.shape)] + [whole] * 5
                 + [_const_spec(wmo.shape), _gate_spec(d, 1), _const_spec(wo.shape),
                    _const_spec(g.shape), _const_spec(b.shape)],
        out_specs=pl.BlockSpec((n, d), lambda i: (0, 0)),
        out_shape=jax.ShapeDtypeStruct((n, d), F32),
        scratch_shapes=[pltpu.VMEM((n, d), BF16)],
        compiler_params=_params(1),
        name="merge_ln1_sample",
    )(xs, m1, wa, wb, qc, og, wmo, w_b, wo, g, b)


def _ffn_stripe(dff):
    return 1024 if dff % 1024 == 0 else dff


def _ffn_hidden(xb, w1_ref, hid_s, rows, before_stripe=None):
    dff = hid_s.shape[1]
    sw = _ffn_stripe(dff)
    for s in range(dff // sw):
        cs = slice(s * sw, (s + 1) * sw)
        if before_stripe is not None:
            before_stripe(s)
        hid = jnp.maximum(_dot(xb, w1_ref[:, cs]), 0.0)
        hid_s[rows, cs] = (hid * hid).astype(BF16)


def _ffn_small_kernel(x_ref, w1_ref, w2_ref, g_ref, b_ref, o_ref, acc_s, *, alpha):
    j = pl.program_id(0)
    x = x_ref[...]
    hid = jnp.maximum(_dot(x.astype(BF16), w1_ref[...]), 0.0)
    part = _dot((hid * hid).astype(BF16), w2_ref[...])

    @pl.when(j == 0)
    def _():
        acc_s[...] = alpha * x + part

    @pl.when(j > 0)
    def _():
        acc_s[...] += part

    @pl.when(j == pl.num_programs(0) - 1)
    def _():
        y = _layer_norm(acc_s[...], g_ref[...], b_ref[...])
        nb, n_t, _ = o_ref.shape
        for t in range(n_t):
            o_ref[:, t, :] = y[t * nb:(t + 1) * nb]


def _ffn_stream_kernel(x_ref, w1_hbm, w2_hbm, g_ref, b_ref, c0_ref, q_ref, kw_ref, v_ref,
                       dec_ref, o_ref, c_ref, qc_ref, w1_s, w2_s, sem, hid_s, r_s, *, alpha):
    s = pl.program_id(0)
    last = pl.num_programs(0) - 1
    sw = _ffn_stripe(w1_s.shape[1])
    n_w1 = w1_s.shape[1] // sw

    def w1_copy(k):
        cs = pl.ds(k * sw, sw)
        return pltpu.make_async_copy(w1_hbm.at[:, cs], w1_s.at[:, cs], sem.at[k])

    w2_copy = pltpu.make_async_copy(w2_hbm, w2_s, sem.at[n_w1])

    def step(first):
        _stream_matrix_memory(c0_ref, q_ref, kw_ref, v_ref, dec_ref, c_ref, qc_ref)
        x = x_ref[...]
        _ffn_hidden(x.astype(BF16), w1_s, hid_s, slice(None),
                    before_stripe=(lambda k: w1_copy(k).wait()) if first else None)
        o_ref[...] = _layer_norm(r_s[...], g_ref[...], b_ref[...])
        if first:
            w2_copy.wait()
        r_s[...] = alpha * x + _dot(hid_s[...], w2_s[...])

    @pl.when(s == 0)
    def _():
        for k in range(n_w1):
            w1_copy(k).start()
        w2_copy.start()
        r_s[...] = jnp.zeros(r_s.shape, F32)
        step(True)

    @pl.when((s > 0) & (s < last))
    def _():
        step(False)

    @pl.when(s == last)
    def _():
        o_ref[...] = _layer_norm(r_s[...], g_ref[...], b_ref[...])


def _ffn_small(x, w1, w2, g, b, alpha, n_steps):
    n, d = x.shape
    dff = w1.shape[1]
    sw = 1024 if dff % 1024 == 0 else dff
    out_shape = (n // n_steps, n_steps, d)
    return pl.pallas_call(
        functools.partial(_ffn_small_kernel, alpha=alpha),
        grid=(dff // sw,),
        in_specs=[_const_spec((n, d)), pl.BlockSpec((d, sw), lambda j: (0, j)),
                  pl.BlockSpec((sw, d), lambda j: (j, 0)), _const_spec(g.shape),
                  _const_spec(b.shape)],
        out_specs=pl.BlockSpec(out_shape, lambda j: (0,) * len(out_shape)),
        out_shape=jax.ShapeDtypeStruct(out_shape, F32),
        scratch_shapes=[pltpu.VMEM((n, d), F32)],
        compiler_params=_params(1),
        name="ffn_ln2_small",
    )(x, w1, w2, g, b)


def _ffn_with_stream(x, w1, w2, g, b, alpha, c0, q, kw, v, dec, n_steps):
    n, d = x.shape
    rows = q.shape[0]
    nb = rows // n_steps
    dh = d // N_HEADS
    bb = SAMPLE_BATCH_BLOCK
    n_grid = (nb // bb) * N_HEADS
    tm = n // n_grid
    assert tm * n_grid == n and tm % 8 == 0 and nb % bb == 0
    cur = lambda s: jnp.minimum(s, n_grid - 1)
    x_tile = pl.BlockSpec((tm, d), lambda s: (cur(s), 0))
    o_tile = pl.BlockSpec((tm, d), lambda s: (jnp.maximum(s - 1, 0), 0))
    cblock = pl.BlockSpec((bb, 1, dh, dh),
                          lambda s: (cur(s) // N_HEADS, cur(s) % N_HEADS, 0, 0))
    tblock = pl.BlockSpec((n_steps, bb, dh), lambda s: (0, cur(s) // N_HEADS, cur(s) % N_HEADS))
    dblock = pl.BlockSpec((bb, dh), lambda s: (cur(s) // N_HEADS, cur(s) % N_HEADS))
    as_tbd = lambda a: a.reshape(n_steps, nb, d)
    x2, c_new, qc = pl.pallas_call(
        functools.partial(_ffn_stream_kernel, alpha=alpha),
        grid=(n_grid + 1,),
        in_specs=[x_tile, pl.BlockSpec(memory_space=pl.ANY), pl.BlockSpec(memory_space=pl.ANY),
                  _const_spec(g.shape), _const_spec(b.shape), cblock, tblock, tblock, tblock,
                  dblock],
        out_specs=[o_tile, cblock, tblock],
        out_shape=[jax.ShapeDtypeStruct((n, d), F32), jax.ShapeDtypeStruct(c0.shape, F32),
                   jax.ShapeDtypeStruct((n_steps, nb, d), F32)],
        scratch_shapes=[pltpu.VMEM(w1.shape, BF16), pltpu.VMEM(w2.shape, BF16),
                        pltpu.SemaphoreType.DMA((w1.shape[1] // _ffn_stripe(w1.shape[1]) + 1,)),
                        pltpu.VMEM((tm, w1.shape[1]), BF16), pltpu.VMEM((tm, d), F32)],
        compiler_params=_params(1),
        name="ffn_ln2_stream",
    )(x, w1, w2, g, b, c0, as_tbd(q), as_tbd(kw), as_tbd(v), dec)
    return x2, c_new, qc.reshape(rows, d)


def _w_in_prep_kernel(a_ref, g_ref, wa_ref, wg_ref):
    @pl.when(pl.program_id(0) == 0)
    def _():
        wg_ref[...] = g_ref[...]

    wa_ref[...] = a_ref[...].T.astype(BF16)


def _prepare_w_in(w_in_t):
    d = w_in_t.shape[1]
    h2 = 2 * N_HEADS
    return pl.pallas_call(
        _w_in_prep_kernel,
        grid=(4,),
        in_specs=[pl.BlockSpec((d, d), lambda j: (3 + j, 0)),
                  pl.BlockSpec((pl.Element(h2), pl.Element(d)), lambda j: (7 * d, 0))],
        out_specs=[pl.BlockSpec((d, d), lambda j: (0, j)),
                   pl.BlockSpec((h2, d), lambda j: (0, 0))],
        out_shape=[jax.ShapeDtypeStruct((d, 4 * d), BF16),
                   jax.ShapeDtypeStruct((h2, d), F32)],
        compiler_params=_params(1),
        name="w_in_prep",
    )(w_in_t, w_in_t)


def _layer_weights(w_in, b_gate, conv_w, w_conv_out, mh_g, w_m_out, w_o, ln1_g, ln1_b,
                   w_ff1, w_ff2, ln2_g, ln2_b):
    d = w_in.shape[0]
    h2 = 2 * N_HEADS
    w_in_t = jnp.swapaxes(w_in, 0, 1)
    w_qkvo, wg_t = _prepare_w_in(w_in_t)
    wgt_hi, wgt_lo = _split_bf16(wg_t)
    wgt = jnp.concatenate([wgt_hi, wgt_lo], axis=0)
    pad = ((0, 0), (0, GATE_PAD - h2))
    wgc = jnp.stack([jnp.pad(wgt_hi.T, pad), jnp.pad(wgt_lo.T, pad)])
    return dict(
        w_in_t=w_in_t, w_qkvo=w_qkvo,
        wgt=wgt, wgc=wgc,
        bgc=b_gate.reshape(h2, 1).astype(F32),
        bgr=jnp.pad(b_gate.reshape(1, h2).astype(F32), pad),
        cw=conv_w.astype(F32),
        mhg=mh_g.reshape(1, d).astype(F32),
        f32_weights=[w_conv_out, w_m_out, w_o],
        ffn_weights=(w_ff1, w_ff2),
        ln1_g=ln1_g.reshape(1, d), ln1_b=ln1_b.reshape(1, d),
        ln2_g=ln2_g.reshape(1, d), ln2_b=ln2_b.reshape(1, d))


def _layer(x, xs, conv_buf, c0, n0, m0, p, alpha):
    bsz, t, d = x.shape
    n_steps = xs.shape[1]
    hm, c_p, n_p, m_p, (wco, wmo, wo), w_bch, w_gate = _mlstm_prompt(
        x, p["w_qkvo"], p["wgt"], p["bgc"], p["mhg"], p["f32_weights"], p["w_in_t"])
    x1, conv_p, w1, w2 = _conv_merge_prompt(x, hm, w_bch, p["cw"], wco, w_gate, wmo, wo,
                                            p["ln1_g"], p["ln1_b"], alpha, *p["ffn_weights"])
    m1_s, conv_s = _conv_branch_sample(xs, conv_buf, w_bch, p["cw"], wco, w_gate)
    q, kw, v, og, dec, wa, wb, n_s, m_s = _mlstm_sample_pre(
        xs, p["w_qkvo"], p["wgc"], p["bgr"], p["mhg"], n0, m0)
    x2, c_s, qc = _ffn_with_stream(x1, w1, w2, p["ln2_g"], p["ln2_b"], alpha,
                                   c0, q, kw, v, dec, n_steps)
    x1_s = _merge_sample(xs, m1_s, wa, wb, qc, og, wmo, w_gate, wo, p["ln1_g"], p["ln1_b"],
                         alpha)
    x2_s = _ffn_small(x1_s, w1, w2, p["ln2_g"], p["ln2_b"], alpha, n_steps)
    return (x2.reshape(bsz, t, d), conv_p, c_p, n_p, m_p,
            x2_s, conv_s, c_s, n_s, m_s)


def kernel(x_prompt, x_sample, state_conv, state_C, state_n, state_m, w_in, b_gate, conv_w,
           w_conv_out, mh_g, w_m_out, w_o, ln1_g, ln1_b, w_ff1, w_ff2, ln2_g, ln2_b):
    depth = w_in.shape[0]
    alpha = (2.0 * depth) ** 0.25
    bsz, t, d = x_prompt.shape
    sb, st, _ = x_sample.shape
    assert t % TOKEN_TILE == 0 and t % MLSTM_TILE == 0 and MLSTM_TILE % CHUNK == 0
    assert d % N_HEADS == 0
    assert sb % SAMPLE_BATCH_BLOCK == 0 and st >= CONV_W - 1

    xp, xs = x_prompt, x_sample
    outs = [[] for _ in range(8)]
    for l in range(depth):
        p = _layer_weights(w_in[l], b_gate[l], conv_w[l], w_conv_out[l], mh_g[l], w_m_out[l],
                           w_o[l], ln1_g[l], ln1_b[l], w_ff1[l], w_ff2[l], ln2_g[l], ln2_b[l])
        xp, cp, c_p, n_p, m_p, xs, cs, c_s, n_s, m_s = _layer(
            xp, xs, state_conv[l], state_C[l], state_n[l], state_m[l], p, alpha)
        for acc, val in zip(outs, (cp, cs, c_p, c_s, n_p, n_s, m_p, m_s)):
            acc.append(val)
    return (xp, xs) + tuple(jnp.stack(acc) for acc in outs)
```
